```python
import math
import jax, jax.numpy as jnp
from jax import lax
import numpy as np

D_MODEL = 2048
BATCH = 4
SEQ = 2048
DEPTH = 2

HEAD_DIM = 128
A_HEADS = 4
A_WIDTH = A_HEADS * HEAD_DIM
A_BLOCK = 256
A_TOPK = 3
A_QCHUNK = 32
RPE_BUCKETS = 32
RPE_MAX_DIST = 128
B_HEADS = 4
B_DK = 128
B_DV = 256
B_KWIDTH = B_HEADS * B_DK
B_VWIDTH = B_HEADS * B_DV
B_LOWRANK = 16
B_GATE_NORM = 16.0
B_CHUNK = 64
C_GROUPS = 4
C_GROUP_DIM = 128
C_WIDTH = C_GROUPS * C_GROUP_DIM
C_CHUNK = 128
N_BRANCH = 3
D_FF = 4 * D_MODEL
EPS = 1e-6
NEG_INF = -1e30
IN_WIDTHS = (A_WIDTH, A_WIDTH, A_WIDTH, B_KWIDTH, B_KWIDTH, B_VWIDTH, B_VWIDTH, B_LOWRANK, C_WIDTH, C_WIDTH, D_MODEL, D_MODEL, D_MODEL)
IN_WIDTH = sum(IN_WIDTHS)

kernel_name = "hybrid_moba_gla_gmlp_block"


def rms_norm(x, g):
    xf = x.astype(jnp.float32)
    y = xf * lax.rsqrt(jnp.mean(xf * xf, axis=-1, keepdims=True) + EPS)
    return (y * g.astype(jnp.float32)).astype(x.dtype)


def layer_norm(x, g, b):
    xf = x.astype(jnp.float32)
    mu = jnp.mean(xf, axis=-1, keepdims=True)
    xc = xf - mu
    y = xc * lax.rsqrt(jnp.mean(xc * xc, axis=-1, keepdims=True) + EPS)
    return (y * g.astype(jnp.float32) + b.astype(jnp.float32)).astype(x.dtype)


def rpe_bucket(dist):
    n = jnp.maximum(dist, 0)
    max_exact = RPE_BUCKETS // 2
    nf = jnp.maximum(n, 1).astype(jnp.float32)
    large = max_exact + (jnp.log(nf / max_exact) / math.log(RPE_MAX_DIST / max_exact)
                         * (RPE_BUCKETS - max_exact)).astype(jnp.int32)
    large = jnp.minimum(large, RPE_BUCKETS - 1)
    return jnp.where(n < max_exact, n, large)


def moba_attention(q, k, v, rpe_table):
    bsz, seq, nh, hd = q.shape
    nb = -(-seq // A_BLOCK)
    sp = nb * A_BLOCK
    pad = ((0, 0), (0, sp - seq), (0, 0), (0, 0))
    q, k, v = (jnp.pad(t, pad).transpose(0, 2, 1, 3) for t in (q, k, v))
    kb = k.reshape(bsz, nh, nb, A_BLOCK, hd)
    vb = v.reshape(bsz, nh, nb, A_BLOCK, hd)
    kmean = jnp.mean(kb.astype(jnp.float32), axis=3)
    qblk = jnp.arange(sp, dtype=jnp.int32) // A_BLOCK
    past = jnp.arange(nb, dtype=jnp.int32)[None, :] < qblk[:, None]
    gate = jnp.einsum('bhtd,bhnd->bhtn', q.astype(jnp.float32), kmean)
    gate = jnp.where(past, gate, -jnp.inf)
    ksel = min(A_TOPK, nb)
    _, idx = lax.top_k(gate, ksel)
    idx = idx.astype(jnp.int32)
    valid = idx < qblk[None, None, :, None]
    nqc = sp // A_QCHUNK

    def to_chunks(t):
        return jnp.moveaxis(t.reshape(bsz, nh, nqc, A_QCHUNK, *t.shape[3:]), 2, 0)

    bt = rpe_table.T.astype(jnp.float32)
    b_ix = jnp.arange(bsz)[:, None, None, None]
    h_ix = jnp.arange(nh)[None, :, None, None]
    scale = hd ** -0.5
    offs = jnp.arange(A_BLOCK, dtype=jnp.int32)

    def chunk_attend(args):
        qc, ic, vc, c = args
        t = c * A_QCHUNK + jnp.arange(A_QCHUNK, dtype=jnp.int32)
        own = (c * A_QCHUNK) // A_BLOCK
        k_own = lax.dynamic_index_in_dim(kb, own, axis=2, keepdims=False)
        v_own = lax.dynamic_index_in_dim(vb, own, axis=2, keepdims=False)
        k_sel = kb[b_ix, h_ix, ic]
        v_sel = vb[b_ix, h_ix, ic]
        kpos_sel = ic[..., None] * A_BLOCK + offs
        s_sel = jnp.einsum('bhqd,bhqnkd->bhqnk', qc, k_sel).astype(jnp.float32) * scale
        s_sel = s_sel + bt[h_ix[..., None], rpe_bucket(t[:, None, None] - kpos_sel)]
        s_sel = jnp.where(vc[..., None], s_sel, NEG_INF).reshape(bsz, nh, A_QCHUNK, ksel * A_BLOCK)
        kpos_own = own * A_BLOCK + offs
        s_own = jnp.einsum('bhqd,bhkd->bhqk', qc, k_own).astype(jnp.float32) * scale
        s_own = s_own + bt[:, rpe_bucket(t[:, None] - kpos_own[None, :])][None]
        s_own = jnp.where(kpos_own[None, :] <= t[:, None], s_own, NEG_INF)
        p = jax.nn.softmax(jnp.concatenate([s_sel, s_own], axis=-1), axis=-1).astype(v.dtype)
        p_sel = p[..., :ksel * A_BLOCK].reshape(bsz, nh, A_QCHUNK, ksel, A_BLOCK)
        p_own = p[..., ksel * A_BLOCK:]
        return (jnp.einsum('bhqnk,bhqnkd->bhqd', p_sel, v_sel)
                + jnp.einsum('bhqk,bhkd->bhqd', p_own, v_own))

    out = lax.map(chunk_attend, (to_chunks(q), to_chunks(idx), to_chunks(valid),
                                 jnp.arange(nqc, dtype=jnp.int32)))
    out = jnp.moveaxis(out, 0, 2).reshape(bsz, nh, sp, hd).transpose(0, 2, 1, 3)[:, :seq]
    return out.reshape(bsz, seq, nh * hd)


def gla_chunked(q, k, v, log_a):
    bsz, seq, nh, dk = q.shape
    dv = v.shape[-1]
    nc = seq // B_CHUNK

    def to_chunks(t):
        return t.astype(jnp.float32).reshape(bsz, nc, B_CHUNK, nh, t.shape[-1]).transpose(1, 0, 3, 2, 4)

    qc, kc, vc, gc = to_chunks(q), to_chunks(k), to_chunks(v), to_chunks(log_a)
    bcum = jnp.cumsum(gc, axis=3)
    q_dec = qc * jnp.exp(bcum) * dk ** -0.5
    k_inv = kc * jnp.exp(-bcum)
    b_last = bcum[:, :, :, -1:, :]
    k_end = kc * jnp.exp(b_last - bcum)
    causal = jnp.tril(jnp.ones((B_CHUNK, B_CHUNK), dtype=bool))
    a_intra = jnp.where(causal, jnp.einsum('nbhtd,nbhsd->nbhts', q_dec, k_inv), 0.0)
    o_intra = jnp.einsum('nbhts,nbhsv->nbhtv', a_intra, vc)
    states = jnp.einsum('nbhsd,nbhsv->nbhdv', k_end, vc)
    decay = jnp.exp(b_last[:, :, :, 0, :])

    def step(s_prev, inp):
        d, s_c = inp
        return d[..., None] * s_prev + s_c, s_prev

    _, s_before = lax.scan(step, jnp.zeros((bsz, nh, dk, dv), jnp.float32), (decay, states))
    o = o_intra + jnp.einsum('nbhtd,nbhdv->nbhtv', q_dec, s_before)
    return o.transpose(1, 0, 3, 2, 4).reshape(bsz, seq, nh, dv).astype(v.dtype)


def spatial_gating(u, v, ln_g, ln_b, w_s, b_s):
    bsz, seq, _ = u.shape
    u = jax.nn.gelu(u)
    v = layer_norm(jax.nn.gelu(v), ln_g, ln_b)
    nch = seq // C_CHUNK
    vg = v.reshape(bsz, nch, C_CHUNK, C_GROUPS, C_GROUP_DIM)
    w = jnp.where(jnp.tril(jnp.ones((C_CHUNK, C_CHUNK), dtype=bool)), w_s, 0.0)
    mixed = jnp.einsum('gts,bnsgc->bntgc', w, vg) + b_s.T[:, :, None]
    return u * mixed.reshape(bsz, seq, C_WIDTH)


def hybrid_layer(x, rpe_table, n1, w_in, qn, kn, lr_w, lr_b, on_g, ln_g, ln_b, sg_w, sg_b,
                 wa, wb, wc, w_o, n2, w1, w2):
    bsz, seq, _ = x.shape
    xn = rms_norm(x, n1)
    proj = xn @ w_in
    splits = np.cumsum(IN_WIDTHS[:-1]).tolist()
    qa, ka, va, qb, kb, vb, rb, lrb, uc, vc, ga, gb, gc = jnp.split(proj, splits, axis=-1)
    qa = rms_norm(qa.reshape(bsz, seq, A_HEADS, HEAD_DIM), qn)
    ka = rms_norm(ka.reshape(bsz, seq, A_HEADS, HEAD_DIM), kn)
    va = va.reshape(bsz, seq, A_HEADS, HEAD_DIM)
    y_a = moba_attention(qa, ka, va, rpe_table)
    log_a = jax.nn.log_sigmoid((lrb @ lr_w + lr_b).astype(jnp.float32)) / B_GATE_NORM
    o_b = gla_chunked(qb.reshape(bsz, seq, B_HEADS, B_DK), kb.reshape(bsz, seq, B_HEADS, B_DK),
                      vb.reshape(bsz, seq, B_HEADS, B_DV), log_a.reshape(bsz, seq, B_HEADS, B_DK))
    y_b = rms_norm(o_b, on_g).reshape(bsz, seq, B_VWIDTH) * jax.nn.silu(rb)
    y_c = spatial_gating(uc, vc, ln_g, ln_b, sg_w, sg_b)
    merged = (jax.nn.sigmoid(ga) * (y_a @ wa) + jax.nn.sigmoid(gb) * (y_b @ wb)
              + jax.nn.sigmoid(gc) * (y_c @ wc))
    h = x + merged @ w_o
    hn = rms_norm(h, n2)
    return h + jnp.square(jax.nn.relu(hn @ w1)) @ w2


def setup_inputs(seed: int = 0) -> dict:
    key = jax.random.key(seed)
    ks = jax.random.split(key, 24)
    L = DEPTH

    def nrm(k, shape, scale):
        return jax.random.normal(k, shape, jnp.float32) * scale

    return {
        "x": nrm(ks[0], (BATCH, SEQ, D_MODEL), 1.0),
        "rpe_table": nrm(ks[1], (RPE_BUCKETS, A_HEADS), 0.5),
        "norm1_g": 1.0 + nrm(ks[2], (L, D_MODEL), 0.02),
        "w_in": nrm(ks[3], (L, D_MODEL, IN_WIDTH), D_MODEL ** -0.5),
        "q_norm_g": 1.0 + nrm(ks[4], (L, HEAD_DIM), 0.02),
        "k_norm_g": 1.0 + nrm(ks[5], (L, HEAD_DIM), 0.02),
        "gla_lr_w": nrm(ks[6], (L, B_LOWRANK, B_KWIDTH), B_LOWRANK ** -0.5),
        "gla_lr_b": nrm(ks[7], (L, B_KWIDTH), 0.1),
        "gla_out_g": 1.0 + nrm(ks[8], (L, B_DV), 0.02),
        "sg_ln_g": 1.0 + nrm(ks[9], (L, C_WIDTH), 0.02),
        "sg_ln_b": nrm(ks[10], (L, C_WIDTH), 0.02),
        "sg_w": nrm(ks[11], (L, C_GROUPS, C_CHUNK, C_CHUNK), C_CHUNK ** -0.5),
        "sg_b": 1.0 + nrm(ks[12], (L, C_GROUPS, C_CHUNK), 0.02),
        "w_br_a": nrm(ks[13], (L, A_WIDTH, D_MODEL), A_WIDTH ** -0.5),
        "w_br_b": nrm(ks[14], (L, B_VWIDTH, D_MODEL), B_VWIDTH ** -0.5),
        "w_br_c": nrm(ks[15], (L, C_WIDTH, D_MODEL), C_WIDTH ** -0.5),
        "w_o": nrm(ks[16], (L, D_MODEL, D_MODEL), D_MODEL ** -0.5),
        "norm2_g": 1.0 + nrm(ks[17], (L, D_MODEL), 0.02),
        "w_ff1": nrm(ks[18], (L, D_MODEL, D_FF), D_MODEL ** -0.5),
        "w_ff2": nrm(ks[19], (L, D_FF, D_MODEL), D_FF ** -0.5),
    }


def reference(x, rpe_table, norm1_g, w_in, q_norm_g, k_norm_g, gla_lr_w, gla_lr_b, gla_out_g,
              sg_ln_g, sg_ln_b, sg_w, sg_b, w_br_a, w_br_b, w_br_c, w_o, norm2_g, w_ff1, w_ff2):
    h = x
    for l in range(DEPTH):
        h = hybrid_layer(h, rpe_table, norm1_g[l], w_in[l], q_norm_g[l], k_norm_g[l],
                         gla_lr_w[l], gla_lr_b[l], gla_out_g[l], sg_ln_g[l], sg_ln_b[l],
                         sg_w[l], sg_b[l], w_br_a[l], w_br_b[l], w_br_c[l], w_o[l],
                         norm2_g[l], w_ff1[l], w_ff2[l])
    return h
```

```python
import functools
import math

import jax
import jax.numpy as jnp
from jax import lax
from jax.experimental import pallas as pl
from jax.experimental.pallas import tpu as pltpu

D_MODEL = 2048
DEPTH = 2
HEAD_DIM = 128
A_HEADS = 4
A_WIDTH = A_HEADS * HEAD_DIM
A_BLOCK = 256
A_TOPK = 3
RPE_BUCKETS = 32
RPE_MAX_DIST = 128
B_HEADS = 4
B_DK = 128
B_DV = 256
B_KWIDTH = B_HEADS * B_DK
B_VWIDTH = B_HEADS * B_DV
B_LOWRANK = 16
B_GATE_NORM = 16.0
B_CHUNK = 64
C_GROUPS = 4
C_GROUP_DIM = 128
C_WIDTH = C_GROUPS * C_GROUP_DIM
C_CHUNK = 128
D_FF = 4 * D_MODEL
EPS = 1e-6
NEG_INF = -1e30

LANES = 128
VMEM_LIMIT = 56 * 1024 * 1024

F32 = jnp.float32
BF16 = jnp.bfloat16

AB_WIDTH = 3 * A_WIDTH + 2 * B_KWIDTH + 2 * B_VWIDTH
C_PROJ_WIDTH = 2 * C_WIDTH + LANES
G_WIDTH = 3 * D_MODEL


def _rpe_thresholds():
    max_exact = RPE_BUCKETS // 2

    def bucket(n):
        if n < max_exact:
            return n
        v = math.log(n / max_exact) / math.log(RPE_MAX_DIST / max_exact) * (RPE_BUCKETS - max_exact)
        return min(max_exact + int(v), RPE_BUCKETS - 1)

    table = [bucket(n) for n in range(4 * RPE_MAX_DIST)]
    return [min(n for n in range(len(table)) if table[n] >= b)
            for b in range(max_exact + 1, RPE_BUCKETS)]


RPE_THRESHOLDS = _rpe_thresholds()
assert RPE_THRESHOLDS[-1] <= A_BLOCK


def _cparams(*sem):
    return pltpu.CompilerParams(dimension_semantics=sem, vmem_limit_bytes=VMEM_LIMIT)


def _rpe_bias_kernel(tbl_ref, bias_ref, far_ref):
    h = pl.program_id(0)
    t = lax.broadcasted_iota(jnp.int32, (A_BLOCK, 2 * A_BLOCK), 0)
    c = lax.broadcasted_iota(jnp.int32, (A_BLOCK, 2 * A_BLOCK), 1)
    n = jnp.maximum(t - c + A_BLOCK, 0)
    max_exact = RPE_BUCKETS // 2
    large = jnp.full(n.shape, max_exact, jnp.int32)
    for thr in RPE_THRESHOLDS:
        large = large + (n >= thr).astype(jnp.int32)
    bucket = jnp.where(n < max_exact, n, large)
    bias = jnp.zeros(n.shape, F32)
    for b in range(RPE_BUCKETS):
        bias = jnp.where(bucket == b, tbl_ref[b, h], bias)
    bias_ref[0] = bias
    far_ref[0] = jnp.full((8, LANES), tbl_ref[RPE_BUCKETS - 1, h], F32)


def rpe_bias(rpe_table):
    return pl.pallas_call(
        _rpe_bias_kernel,
        grid=(A_HEADS,),
        in_specs=[pl.BlockSpec(memory_space=pltpu.SMEM)],
        out_specs=[pl.BlockSpec((1, A_BLOCK, 2 * A_BLOCK), lambda h: (h, 0, 0)),
                   pl.BlockSpec((1, 8, LANES), lambda h: (h, 0, 0))],
        out_shape=[jax.ShapeDtypeStruct((A_HEADS, A_BLOCK, 2 * A_BLOCK), F32),
                   jax.ShapeDtypeStruct((A_HEADS, 8, LANES), F32)],
        compiler_params=_cparams("arbitrary"),
        name="rpe_bias",
    )(rpe_table)


def _norm_matmul_kernel(x_ref, g_ref, w_ref, o_ref, xn_ref, *, row_chunk):
    @pl.when(pl.program_id(1) == 0)
    def _():
        def body(r, carry):
            rows = pl.ds(pl.multiple_of(r * row_chunk, row_chunk), row_chunk)
            x = x_ref[rows, :]
            ms = jnp.mean(x * x, axis=-1, keepdims=True)
            xn_ref[rows, :] = (x * lax.rsqrt(ms + EPS) * g_ref[...]).astype(BF16)
            return carry
        lax.fori_loop(0, x_ref.shape[0] // row_chunk, body, 0)

    o_ref[...] = jnp.dot(xn_ref[...], w_ref[...], preferred_element_type=F32).astype(o_ref.dtype)


def norm_matmul(x, g, w, *, tm, tn, out_dtype, name):
    m, k = x.shape
    n = w.shape[1]
    assert m % tm == 0 and n % tn == 0
    return pl.pallas_call(
        functools.partial(_norm_matmul_kernel, row_chunk=128),
        grid=(m // tm, n // tn),
        in_specs=[pl.BlockSpec((tm, k), lambda i, j: (i, 0)),
                  pl.BlockSpec((1, k), lambda i, j: (0, 0)),
                  pl.BlockSpec((k, tn), lambda i, j: (0, j))],
        out_specs=pl.BlockSpec((tm, tn), lambda i, j: (i, j)),
        out_shape=jax.ShapeDtypeStruct((m, n), out_dtype),
        scratch_shapes=[pltpu.VMEM((tm, k), BF16)],
        compiler_params=_cparams("parallel", "arbitrary"),
        name=name,
    )(x, g.reshape(1, k), w)


def _head_rms(x, g):
    ms = jnp.mean(x * x, axis=-1, keepdims=True)
    return x * lax.rsqrt(ms + EPS) * g


def _moba_kernel(q_ref, k_ref, v_ref, qg_ref, kg_ref, bias_ref, far_ref, o_ref,
                 qn_ref, qb_ref, kb_ref, vb_ref, kmean_ref):
    seq = q_ref.shape[0]
    nb = seq // A_BLOCK
    qn = _head_rms(q_ref[...], qg_ref[...])
    qn_ref[...] = qn
    qb_ref[...] = qn.astype(BF16)
    kn = _head_rms(k_ref[...], kg_ref[...])
    kb_ref[...] = kn.astype(BF16)
    vb_ref[...] = v_ref[...].astype(BF16)
    for j in range(nb):
        kmean_ref[j:j + 1, :] = jnp.mean(kn[j * A_BLOCK:(j + 1) * A_BLOCK], axis=0, keepdims=True)
    kmean = kmean_ref[...]

    scale = HEAD_DIM ** -0.5
    bias_prev = bias_ref[0, :, :A_BLOCK]
    bias_own = bias_ref[0, :, A_BLOCK:]
    far = far_ref[0, 0:1, 0:1]
    tq = lax.broadcasted_iota(jnp.int32, (A_BLOCK, A_BLOCK), 0)
    tk = lax.broadcasted_iota(jnp.int32, (A_BLOCK, A_BLOCK), 1)
    causal = tk <= tq
    col = lax.broadcasted_iota(jnp.int32, (A_BLOCK, nb), 1)

    for i in range(nb):
        rows = slice(i * A_BLOCK, (i + 1) * A_BLOCK)
        nk = (i + 1) * A_BLOCK
        s = lax.dot_general(qb_ref[rows, :], kb_ref[:nk, :], (((1,), (1,)), ((), ())),
                            preferred_element_type=F32) * scale
        if i > A_TOPK:
            g = lax.dot_general(qn_ref[rows, :], kmean, (((1,), (1,)), ((), ())),
                                precision=lax.Precision.HIGHEST,
                                preferred_element_type=F32)
            rank = jnp.zeros((A_BLOCK, nb), jnp.int32)
            for jp in range(i):
                gj = g[:, jp:jp + 1]
                beats = (gj > g) | ((gj == g) & (jp < col))
                rank = rank + beats.astype(jnp.int32)
            sel = jnp.where((col < i) & (rank < A_TOPK), 1.0, 0.0)
        parts = []
        for j in range(i + 1):
            sj = s[:, j * A_BLOCK:(j + 1) * A_BLOCK]
            if j == i:
                sj = jnp.where(causal, sj + bias_own, NEG_INF)
            else:
                sj = sj + (bias_prev if j == i - 1 else far)
                if i > A_TOPK:
                    sj = jnp.where(sel[:, j:j + 1] > 0.5, sj, NEG_INF)
            parts.append(sj)
        s = parts[0] if len(parts) == 1 else jnp.concatenate(parts, axis=-1)
        m = jnp.max(s, axis=-1, keepdims=True)
        p = jnp.exp(s - m)
        l = jnp.sum(p, axis=-1, keepdims=True)
        o = jnp.dot(p.astype(BF16), vb_ref[:nk, :], preferred_element_type=F32)
        o_ref[rows, :] = (o / l).astype(o_ref.dtype)


def moba(proj_ab, qg, kg, bias, far, *, bsz, seq):
    hd = HEAD_DIM
    return pl.pallas_call(
        _moba_kernel,
        grid=(bsz, A_HEADS),
        in_specs=[pl.BlockSpec((seq, hd), lambda b, h: (b, h)),
                  pl.BlockSpec((seq, hd), lambda b, h: (b, A_HEADS + h)),
                  pl.BlockSpec((seq, hd), lambda b, h: (b, 2 * A_HEADS + h)),
                  pl.BlockSpec((1, hd), lambda b, h: (0, 0)),
                  pl.BlockSpec((1, hd), lambda b, h: (0, 0)),
                  pl.BlockSpec((1, A_BLOCK, 2 * A_BLOCK), lambda b, h: (h, 0, 0)),
                  pl.BlockSpec((1, 8, LANES), lambda b, h: (h, 0, 0))],
        out_specs=pl.BlockSpec((seq, hd), lambda b, h: (b, h)),
        out_shape=jax.ShapeDtypeStruct((bsz * seq, A_WIDTH), BF16),
        scratch_shapes=[pltpu.VMEM((seq, hd), F32),
                        pltpu.VMEM((seq, hd), BF16),
                        pltpu.VMEM((seq, hd), BF16),
                        pltpu.VMEM((seq, hd), BF16),
                        pltpu.VMEM((seq // A_BLOCK, hd), F32)],
        compiler_params=_cparams("parallel", "parallel"),
        name="moba",
    )(proj_ab, proj_ab, proj_ab, qg.reshape(1, hd), kg.reshape(1, hd), bias, far)


def _gla_kernel(q_ref, k_ref, v_ref, r_ref, lr_ref, lrw_ref, lrb_ref, og_ref, o_ref,
                la_ref, st_ref):
    seq = q_ref.shape[0]
    ck = B_CHUNK
    z = jnp.dot(lr_ref[...].astype(BF16), lrw_ref[...].astype(BF16),
                preferred_element_type=F32) + lrb_ref[...]
    la_ref[...] = -(jnp.maximum(-z, 0.0) + jnp.log1p(jnp.exp(-jnp.abs(z)))) / B_GATE_NORM
    st_ref[...] = jnp.zeros_like(st_ref)

    ti = lax.broadcasted_iota(jnp.int32, (ck, ck), 0)
    si = lax.broadcasted_iota(jnp.int32, (ck, ck), 1)
    causal = si <= ti
    tril = jnp.where(causal, 1.0, 0.0).astype(F32)
    qscale = B_DK ** -0.5

    def body(c, carry):
        rows = pl.ds(pl.multiple_of(c * ck, ck), ck)
        la = la_ref[rows, :]
        bc = jnp.dot(tril, la, precision=lax.Precision.HIGHEST, preferred_element_type=F32)
        bl = bc[ck - 1:ck, :]
        q = q_ref[rows, :]
        k = k_ref[rows, :]
        v = v_ref[rows, :].astype(BF16)
        qd = (q * jnp.exp(bc) * qscale).astype(BF16)
        ki = (k * jnp.exp(-bc)).astype(BF16)
        ke = (k * jnp.exp(bl - bc)).astype(BF16)
        a = lax.dot_general(qd, ki, (((1,), (1,)), ((), ())), preferred_element_type=F32)
        a = jnp.where(causal, a, 0.0).astype(BF16)
        st = st_ref[...]
        o = (jnp.dot(a, v, preferred_element_type=F32)
             + lax.dot_general(qd, st.astype(BF16), (((1,), (1,)), ((), ())),
                               preferred_element_type=F32))
        st_ref[...] = st * jnp.exp(bl) + lax.dot_general(
            v, ke, (((0,), (0,)), ((), ())), preferred_element_type=F32)
        ms = jnp.mean(o * o, axis=-1, keepdims=True)
        y = o * lax.rsqrt(ms + EPS) * og_ref[...]
        r = r_ref[rows, :]
        o_ref[rows, :] = (y * (r * jax.nn.sigmoid(r))).astype(o_ref.dtype)
        return carry

    lax.fori_loop(0, seq // ck, body, 0)


def gla(proj_ab, proj_c, lr_w, lr_b, og, *, bsz, seq):
    q0 = 3 * A_WIDTH // B_DK
    k0 = q0 + B_HEADS
    v0 = (3 * A_WIDTH + 2 * B_KWIDTH) // B_DV
    r0 = v0 + B_HEADS
    lr0 = 2 * C_WIDTH // LANES
    return pl.pallas_call(
        _gla_kernel,
        grid=(bsz, B_HEADS),
        in_specs=[pl.BlockSpec((seq, B_DK), lambda b, h: (b, q0 + h)),
                  pl.BlockSpec((seq, B_DK), lambda b, h: (b, k0 + h)),
                  pl.BlockSpec((seq, B_DV), lambda b, h: (b, v0 + h)),
                  pl.BlockSpec((seq, B_DV), lambda b, h: (b, r0 + h)),
                  pl.BlockSpec((seq, LANES), lambda b, h: (b, lr0)),
                  pl.BlockSpec((LANES, B_DK), lambda b, h: (0, h)),
                  pl.BlockSpec((1, B_DK), lambda b, h: (0, h)),
                  pl.BlockSpec((1, B_DV), lambda b, h: (0, 0))],
        out_specs=pl.BlockSpec((seq, B_DV), lambda b, h: (b, h)),
        out_shape=jax.ShapeDtypeStruct((bsz * seq, B_VWIDTH), BF16),
        scratch_shapes=[pltpu.VMEM((seq, B_DK), F32),
                        pltpu.VMEM((B_DV, B_DK), F32)],
        compiler_params=_cparams("parallel", "parallel"),
        name="gla",
    )(proj_ab, proj_ab, proj_ab, proj_ab, proj_c, lr_w, lr_b.reshape(1, B_KWIDTH),
      og.reshape(1, B_DV))


def _gelu(x):
    c = math.sqrt(2.0 / math.pi)
    return 0.5 * x * (1.0 + jnp.tanh(c * (x + 0.044715 * (x * x * x))))


def _sgu_kernel(u_ref, v_ref, lg_ref, lb_ref, w_ref, b_ref, o_ref):
    rows = u_ref.shape[0]
    ti = lax.broadcasted_iota(jnp.int32, (C_CHUNK, C_CHUNK), 0)
    si = lax.broadcasted_iota(jnp.int32, (C_CHUNK, C_CHUNK), 1)
    causal = si <= ti
    ws = [jnp.where(causal, w_ref[g], 0.0).astype(BF16) for g in range(C_GROUPS)]
    for n in range(rows // C_CHUNK):
        r = slice(n * C_CHUNK, (n + 1) * C_CHUNK)
        v = _gelu(v_ref[r, :])
        mu = jnp.mean(v, axis=-1, keepdims=True)
        vc = v - mu
        vn = vc * lax.rsqrt(jnp.mean(vc * vc, axis=-1, keepdims=True) + EPS)
        vn = (vn * lg_ref[...] + lb_ref[...]).astype(BF16)
        u = _gelu(u_ref[r, :])
        for g in range(C_GROUPS):
            cols = slice(g * C_GROUP_DIM, (g + 1) * C_GROUP_DIM)
            mixed = jnp.dot(ws[g], vn[:, cols], preferred_element_type=F32) + b_ref[g]
            o_ref[r, cols] = (u[:, cols] * mixed).astype(o_ref.dtype)


def sgu(proj_c, ln_g, ln_b, w_s, b_s, *, tm):
    m = proj_c.shape[0]
    return pl.pallas_call(
        _sgu_kernel,
        grid=(m // tm,),
        in_specs=[pl.BlockSpec((tm, C_WIDTH), lambda i: (i, 0)),
                  pl.BlockSpec((tm, C_WIDTH), lambda i: (i, 1)),
                  pl.BlockSpec((1, C_WIDTH), lambda i: (0, 0)),
                  pl.BlockSpec((1, C_WIDTH), lambda i: (0, 0)),
                  pl.BlockSpec((C_GROUPS, C_CHUNK, C_CHUNK), lambda i: (0, 0, 0)),
                  pl.BlockSpec((C_GROUPS, C_CHUNK, 1), lambda i: (0, 0, 0))],
        out_specs=pl.BlockSpec((tm, C_WIDTH), lambda i: (i, 0)),
        out_shape=jax.ShapeDtypeStruct((m, C_WIDTH), BF16),
        compiler_params=_cparams("parallel"),
        name="sgu",
    )(proj_c, proj_c, ln_g.reshape(1, C_WIDTH), ln_b.reshape(1, C_WIDTH), w_s,
      b_s.reshape(C_GROUPS, C_CHUNK, 1))


def _merge_kernel(ya_ref, yb_ref, yc_ref, ga_ref, gb_ref, gc_ref, x_ref,
                  wa_ref, wb_ref, wc_ref, wo_ref, o_ref):
    merged = (jax.nn.sigmoid(ga_ref[...]) * jnp.dot(ya_ref[...], wa_ref[...], preferred_element_type=F32)
              + jax.nn.sigmoid(gb_ref[...]) * jnp.dot(yb_ref[...], wb_ref[...], preferred_element_type=F32)
              + jax.nn.sigmoid(gc_ref[...]) * jnp.dot(yc_ref[...], wc_ref[...], preferred_element_type=F32))
    o_ref[...] = x_ref[...] + jnp.dot(merged.astype(BF16), wo_ref[...], preferred_element_type=F32)


def merge(ya, yb, yc, proj_g, x, wa, wb, wc, wo, *, tm):
    m, d = x.shape
    resident = functools.partial(pl.BlockSpec, pipeline_mode=pl.Buffered(1))
    return pl.pallas_call(
        _merge_kernel,
        grid=(m // tm,),
        in_specs=[pl.BlockSpec((tm, A_WIDTH), lambda i: (i, 0)),
                  pl.BlockSpec((tm, B_VWIDTH), lambda i: (i, 0)),
                  pl.BlockSpec((tm, C_WIDTH), lambda i: (i, 0)),
                  pl.BlockSpec((tm, d), lambda i: (i, 0)),
                  pl.BlockSpec((tm, d), lambda i: (i, 1)),
                  pl.BlockSpec((tm, d), lambda i: (i, 2)),
                  pl.BlockSpec((tm, d), lambda i: (i, 0)),
                  resident((A_WIDTH, d), lambda i: (0, 0)),
                  resident((B_VWIDTH, d), lambda i: (0, 0)),
                  resident((C_WIDTH, d), lambda i: (0, 0)),
                  resident((d, d), lambda i: (0, 0))],
        out_specs=pl.BlockSpec((tm, d), lambda i: (i, 0)),
        out_shape=jax.ShapeDtypeStruct((m, d), F32),
        compiler_params=_cparams("parallel"),
        name="merge",
    )(ya, yb, yc, proj_g, proj_g, proj_g, x, wa, wb, wc, wo)


def _mlp_kernel(h_ref, g_ref, w1_ref, w2_ref, o_ref, hn_ref, *, row_chunk):
    @pl.when(pl.program_id(1) == 0)
    def _():
        def body(r, carry):
            rows = pl.ds(pl.multiple_of(r * row_chunk, row_chunk), row_chunk)
            h = h_ref[rows, :]
            ms = jnp.mean(h * h, axis=-1, keepdims=True)
            hn_ref[rows, :] = (h * lax.rsqrt(ms + EPS) * g_ref[...]).astype(BF16)
            o_ref[rows, :] = h
            return carry
        lax.fori_loop(0, h_ref.shape[0] // row_chunk, body, 0)

    a = jnp.dot(hn_ref[...], w1_ref[...], preferred_element_type=F32)
    a = jnp.square(jnp.maximum(a, 0.0)).astype(BF16)
    o_ref[...] += jnp.dot(a, w2_ref[...], preferred_element_type=F32)


def mlp(h, g, w1, w2, *, tm, tf):
    m, d = h.shape
    f = w1.shape[1]
    return pl.pallas_call(
        functools.partial(_mlp_kernel, row_chunk=128),
        grid=(m // tm, f // tf),
        in_specs=[pl.BlockSpec((tm, d), lambda i, j: (i, 0), pipeline_mode=pl.Buffered(1)),
                  pl.BlockSpec((1, d), lambda i, j: (0, 0)),
                  pl.BlockSpec((d, tf), lambda i, j: (0, j)),
                  pl.BlockSpec((tf, d), lambda i, j: (j, 0))],
        out_specs=pl.BlockSpec((tm, d), lambda i, j: (i, 0)),
        out_shape=jax.ShapeDtypeStruct((m, d), F32),
        scratch_shapes=[pltpu.VMEM((tm, d), BF16)],
        compiler_params=_cparams("parallel", "arbitrary"),
        name="mlp",
    )(h, g.reshape(1, d), w1, w2)


def _split_w_in(w):
    a_end = AB_WIDTH
    lr_end = a_end + B_LOWRANK
    c_end = lr_end + 2 * C_WIDTH
    w_ab = w[:, :a_end].astype(BF16)
    w_lr = jnp.pad(w[:, a_end:lr_end], ((0, 0), (0, LANES - B_LOWRANK)))
    w_c = jnp.concatenate([w[:, lr_end:c_end], w_lr], axis=1).astype(BF16)
    w_g = w[:, c_end:].astype(BF16)
    return w_ab, w_c, w_g


def kernel(x, rpe_table, norm1_g, w_in, q_norm_g, k_norm_g, gla_lr_w, gla_lr_b, gla_out_g,
           sg_ln_g, sg_ln_b, sg_w, sg_b, w_br_a, w_br_b, w_br_c, w_o, norm2_g, w_ff1, w_ff2):
    bsz, seq, d = x.shape
    assert seq % A_BLOCK == 0 and seq % B_CHUNK == 0 and seq % C_CHUNK == 0
    h = x.reshape(bsz * seq, d)
    bias, far = rpe_bias(rpe_table)
    for l in range(DEPTH):
        w_ab, w_c, w_g = _split_w_in(w_in[l])
        proj_ab = norm_matmul(h, norm1_g[l], w_ab, tm=1024, tn=1152, out_dtype=F32, name="proj_ab")
        proj_c = norm_matmul(h, norm1_g[l], w_c, tm=1024, tn=1152, out_dtype=F32, name="proj_c")
        proj_g = norm_matmul(h, norm1_g[l], w_g, tm=1024, tn=1024, out_dtype=F32, name="proj_g")
        y_a = moba(proj_ab, q_norm_g[l], k_norm_g[l], bias, far, bsz=bsz, seq=seq)
        lr_w = jnp.pad(gla_lr_w[l], ((0, LANES - B_LOWRANK), (0, 0)))
        y_b = gla(proj_ab, proj_c, lr_w, gla_lr_b[l], gla_out_g[l], bsz=bsz, seq=seq)
        y_c = sgu(proj_c, sg_ln_g[l], sg_ln_b[l], sg_w[l], sg_b[l], tm=1024)
        h = merge(y_a, y_b, y_c, proj_g, h,
                  w_br_a[l].astype(BF16), w_br_b[l].astype(BF16), w_br_c[l].astype(BF16),
                  w_o[l].astype(BF16), tm=256)
        h = mlp(h, norm2_g[l], w_ff1[l].astype(BF16), w_ff2[l].astype(BF16), tm=1024, tf=512)
    return h.reshape(bsz, seq, d)
```

```python
import functools
import math

import jax
import jax.numpy as jnp
from jax import lax
from jax.experimental import pallas as pl
from jax.experimental.pallas import tpu as pltpu

D_MODEL = 2048
DEPTH = 2
HEAD_DIM = 128
A_HEADS = 4
A_WIDTH = A_HEADS * HEAD_DIM
A_BLOCK = 256
A_TOPK = 3
RPE_BUCKETS = 32
RPE_MAX_DIST = 128
B_HEADS = 4
B_DK = 128
B_DV = 256
B_KWIDTH = B_HEADS * B_DK
B_VWIDTH = B_HEADS * B_DV
B_LOWRANK = 16
B_GATE_NORM = 16.0
B_CHUNK = 64
C_GROUPS = 4
C_GROUP_DIM = 128
C_WIDTH = C_GROUPS * C_GROUP_DIM
C_CHUNK = 128
D_FF = 4 * D_MODEL
EPS = 1e-6
NEG_INF = -1e30

LANES = 128
VMEM_LIMIT = 56 * 1024 * 1024

F32 = jnp.float32
BF16 = jnp.bfloat16

AB_WIDTH = 3 * A_WIDTH + 2 * B_KWIDTH + 2 * B_VWIDTH
G_WIDTH = 3 * D_MODEL
AB0 = G_WIDTH
C0 = AB0 + AB_WIDTH
LR0 = C0 + 2 * C_WIDTH
PROJ_TN = 1536
PROJ_WIDTH = -(-(LR0 + LANES) // PROJ_TN) * PROJ_TN


def _rpe_thresholds():
    max_exact = RPE_BUCKETS // 2

    def bucket(n):
        if n < max_exact:
            return n
        v = math.log(n / max_exact) / math.log(RPE_MAX_DIST / max_exact) * (RPE_BUCKETS - max_exact)
        return min(max_exact + int(v), RPE_BUCKETS - 1)

    table = [bucket(n) for n in range(4 * RPE_MAX_DIST)]
    return [min(n for n in range(len(table)) if table[n] >= b)
            for b in range(max_exact + 1, RPE_BUCKETS)]


RPE_THRESHOLDS = _rpe_thresholds()
assert RPE_THRESHOLDS[-1] <= A_BLOCK


def _cparams(*sem):
    return pltpu.CompilerParams(dimension_semantics=sem, vmem_limit_bytes=VMEM_LIMIT)


def _rpe_bias_kernel(tbl_ref, bias_ref, far_ref):
    h = pl.program_id(0)
    t = lax.broadcasted_iota(jnp.int32, (A_BLOCK, 2 * A_BLOCK), 0)
    c = lax.broadcasted_iota(jnp.int32, (A_BLOCK, 2 * A_BLOCK), 1)
    n = jnp.maximum(t - c + A_BLOCK, 0)
    max_exact = RPE_BUCKETS // 2
    large = jnp.full(n.shape, max_exact, jnp.int32)
    for thr in RPE_THRESHOLDS:
        large = large + (n >= thr).astype(jnp.int32)
    bucket = jnp.where(n < max_exact, n, large)
    bias = jnp.zeros(n.shape, F32)
    for b in range(RPE_BUCKETS):
        bias = jnp.where(bucket == b, tbl_ref[b, h], bias)
    bias_ref[0] = bias
    far_ref[0] = jnp.full((8, LANES), tbl_ref[RPE_BUCKETS - 1, h], F32)


def rpe_bias(rpe_table):
    return pl.pallas_call(
        _rpe_bias_kernel,
        grid=(A_HEADS,),
        in_specs=[pl.BlockSpec(memory_space=pltpu.SMEM)],
        out_specs=[pl.BlockSpec((1, A_BLOCK, 2 * A_BLOCK), lambda h: (h, 0, 0)),
                   pl.BlockSpec((1, 8, LANES), lambda h: (h, 0, 0))],
        out_shape=[jax.ShapeDtypeStruct((A_HEADS, A_BLOCK, 2 * A_BLOCK), F32),
                   jax.ShapeDtypeStruct((A_HEADS, 8, LANES), F32)],
        compiler_params=_cparams("arbitrary"),
        name="rpe_bias",
    )(rpe_table)


def _norm_matmul_kernel(x_ref, g_ref, w_ref, o_ref, xn_ref, *, row_chunk):
    @pl.when(pl.program_id(1) == 0)
    def _():
        def body(r, carry):
            rows = pl.ds(pl.multiple_of(r * row_chunk, row_chunk), row_chunk)
            x = x_ref[rows, :]
            ms = jnp.mean(x * x, axis=-1, keepdims=True)
            xn_ref[rows, :] = (x * lax.rsqrt(ms + EPS) * g_ref[...]).astype(BF16)
            return carry
        lax.fori_loop(0, x_ref.shape[0] // row_chunk, body, 0)

    o_ref[...] = jnp.dot(xn_ref[...], w_ref[...], preferred_element_type=F32).astype(o_ref.dtype)


def norm_matmul(x, g, w, layer, *, tm, tn, out_dtype, name):
    m, k = x.shape
    n = w.shape[2]
    assert m % tm == 0 and n % tn == 0
    return pl.pallas_call(
        functools.partial(_norm_matmul_kernel, row_chunk=128),
        grid=(m // tm, n // tn),
        in_specs=[pl.BlockSpec((tm, k), lambda i, j: (i, 0)),
                  pl.BlockSpec((1, k), lambda i, j: (0, 0)),
                  pl.BlockSpec((None, k, tn), lambda i, j: (layer, 0, j))],
        out_specs=pl.BlockSpec((tm, tn), lambda i, j: (i, j)),
        out_shape=jax.ShapeDtypeStruct((m, n), out_dtype),
        scratch_shapes=[pltpu.VMEM((tm, k), BF16)],
        compiler_params=_cparams("parallel", "arbitrary"),
        name=name,
    )(x, g.reshape(1, k), w)


def _head_rms(x, g):
    ms = jnp.mean(x * x, axis=-1, keepdims=True)
    return x * lax.rsqrt(ms + EPS) * g


def _moba_kernel(q_ref, k_ref, v_ref, qg_ref, kg_ref, bias_ref, far_ref, o_ref,
                 qn_ref, qb_ref, kb_ref, vb_ref, kmean_ref):
    seq = q_ref.shape[0]
    nb = seq // A_BLOCK
    qn = _head_rms(q_ref[...], qg_ref[...])
    qn_ref[...] = qn
    qb_ref[...] = qn.astype(BF16)
    kn = _head_rms(k_ref[...], kg_ref[...])
    kb_ref[...] = kn.astype(BF16)
    vb_ref[...] = v_ref[...].astype(BF16)
    for j in range(nb):
        kmean_ref[j:j + 1, :] = jnp.mean(kn[j * A_BLOCK:(j + 1) * A_BLOCK], axis=0, keepdims=True)
    kmean = kmean_ref[...]

    scale = HEAD_DIM ** -0.5
    bias_prev = bias_ref[0, :, :A_BLOCK]
    bias_own = bias_ref[0, :, A_BLOCK:]
    far = far_ref[0, 0:1, 0:1]
    tq = lax.broadcasted_iota(jnp.int32, (A_BLOCK, A_BLOCK), 0)
    tk = lax.broadcasted_iota(jnp.int32, (A_BLOCK, A_BLOCK), 1)
    causal = tk <= tq
    col = lax.broadcasted_iota(jnp.int32, (A_BLOCK, nb), 1)

    for i in range(nb):
        rows = slice(i * A_BLOCK, (i + 1) * A_BLOCK)
        nk = (i + 1) * A_BLOCK
        s = lax.dot_general(qb_ref[rows, :], kb_ref[:nk, :], (((1,), (1,)), ((), ())),
                            preferred_element_type=F32) * scale
        if i > A_TOPK:
            g = lax.dot_general(qn_ref[rows, :], kmean, (((1,), (1,)), ((), ())),
                                precision=lax.Precision.HIGHEST,
                                preferred_element_type=F32)
            rank = jnp.zeros((A_BLOCK, nb), jnp.int32)
            for jp in range(i):
                gj = g[:, jp:jp + 1]
                beats = (gj > g) | ((gj == g) & (jp < col))
                rank = rank + beats.astype(jnp.int32)
            sel = jnp.where((col < i) & (rank < A_TOPK), 1.0, 0.0)
        parts = []
        for j in range(i + 1):
            sj = s[:, j * A_BLOCK:(j + 1) * A_BLOCK]
            if j == i:
                sj = jnp.where(causal, sj + bias_own, NEG_INF)
            else:
                sj = sj + (bias_prev if j == i - 1 else far)
                if i > A_TOPK:
                    sj = jnp.where(sel[:, j:j + 1] > 0.5, sj, NEG_INF)
            parts.append(sj)
        s = parts[0] if len(parts) == 1 else jnp.concatenate(parts, axis=-1)
        m = jnp.max(s, axis=-1, keepdims=True)
        p = jnp.exp(s - m)
        l = jnp.sum(p, axis=-1, keepdims=True)
        o = jnp.dot(p.astype(BF16), vb_ref[:nk, :], preferred_element_type=F32)
        o_ref[rows, :] = (o / l).astype(o_ref.dtype)


def moba(proj, qg, kg, bias, far, *, bsz, seq):
    hd = HEAD_DIM
    q0 = AB0 // hd
    return pl.pallas_call(
        _moba_kernel,
        grid=(bsz, A_HEADS),
        in_specs=[pl.BlockSpec((seq, hd), lambda b, h: (b, q0 + h)),
                  pl.BlockSpec((seq, hd), lambda b, h: (b, q0 + A_HEADS + h)),
                  pl.BlockSpec((seq, hd), lambda b, h: (b, q0 + 2 * A_HEADS + h)),
                  pl.BlockSpec((1, hd), lambda b, h: (0, 0)),
                  pl.BlockSpec((1, hd), lambda b, h: (0, 0)),
                  pl.BlockSpec((1, A_BLOCK, 2 * A_BLOCK), lambda b, h: (h, 0, 0)),
                  pl.BlockSpec((1, 8, LANES), lambda b, h: (h, 0, 0))],
        out_specs=pl.BlockSpec((seq, hd), lambda b, h: (b, h)),
        out_shape=jax.ShapeDtypeStruct((bsz * seq, A_WIDTH), BF16),
        scratch_shapes=[pltpu.VMEM((seq, hd), F32),
                        pltpu.VMEM((seq, hd), BF16),
                        pltpu.VMEM((seq, hd), BF16),
                        pltpu.VMEM((seq, hd), BF16),
                        pltpu.VMEM((seq // A_BLOCK, hd), F32)],
        compiler_params=_cparams("parallel", "parallel"),
        name="moba",
    )(proj, proj, proj, qg.reshape(1, hd), kg.reshape(1, hd), bias, far)


def _gla_kernel(q_ref, k_ref, v_ref, r_ref, lr_ref, lrw_ref, lrb_ref, og_ref, o_ref,
                qd_ref, ki_ref, ke_ref, dec_ref):
    seq = q_ref.shape[0]
    ck = B_CHUNK
    grp = 4 * ck
    qscale = B_DK ** -0.5

    ti = lax.broadcasted_iota(jnp.int32, (grp, grp), 0)
    si = lax.broadcasted_iota(jnp.int32, (grp, grp), 1)
    tril_grp = jnp.where((si <= ti) & (si // ck == ti // ck), 1.0, 0.0).astype(BF16)
    lrw = lrw_ref[...].astype(BF16)
    for g in range(seq // grp):
        rows = slice(g * grp, (g + 1) * grp)
        z = jnp.dot(lr_ref[rows, :].astype(BF16), lrw, preferred_element_type=F32) + lrb_ref[...]
        la = -(jnp.maximum(-z, 0.0) + jnp.log1p(jnp.exp(-jnp.abs(z)))) / B_GATE_NORM
        hi = la.astype(BF16)
        r1 = la - hi.astype(F32)
        mid = r1.astype(BF16)
        lo = (r1 - mid.astype(F32)).astype(BF16)
        parts = jnp.dot(tril_grp, jnp.concatenate([hi, mid, lo], axis=1), preferred_element_type=F32)
        bc = parts[:, :B_DK] + parts[:, B_DK:2 * B_DK] + parts[:, 2 * B_DK:]
        bc3 = bc.reshape(grp // ck, ck, B_DK)
        bl3 = bc3[:, ck - 1:ck, :]
        q = q_ref[rows, :]
        k3 = k_ref[rows, :].reshape(grp // ck, ck, B_DK)
        qd_ref[rows, :] = (q * jnp.exp(bc) * qscale).astype(BF16)
        ki_ref[rows, :] = (k3 * jnp.exp(-bc3)).reshape(grp, B_DK).astype(BF16)
        ke_ref[rows, :] = (k3 * jnp.exp(bl3 - bc3)).reshape(grp, B_DK).astype(BF16)
        dec_ref[g * (grp // ck):(g + 1) * (grp // ck), :] = jnp.exp(bl3).reshape(grp // ck, B_DK)

    tc = lax.broadcasted_iota(jnp.int32, (ck, ck), 0)
    sc = lax.broadcasted_iota(jnp.int32, (ck, ck), 1)
    causal = sc <= tc

    def body(c, st):
        rows = pl.ds(pl.multiple_of(c * ck, ck), ck)
        qd = qd_ref[rows, :]
        v = v_ref[rows, :].astype(BF16)
        a = lax.dot_general(qd, ki_ref[rows, :], (((1,), (1,)), ((), ())), preferred_element_type=F32)
        a = jnp.where(causal, a, 0.0).astype(BF16)
        o = (jnp.dot(a, v, preferred_element_type=F32)
             + lax.dot_general(qd, st.astype(BF16), (((1,), (1,)), ((), ())),
                               preferred_element_type=F32))
        st = st * dec_ref[pl.ds(c, 1), :] + lax.dot_general(
            v, ke_ref[rows, :], (((0,), (0,)), ((), ())), preferred_element_type=F32)
        ms = jnp.mean(o * o, axis=-1, keepdims=True)
        y = o * lax.rsqrt(ms + EPS) * og_ref[...]
        r = r_ref[rows, :]
        o_ref[rows, :] = (y * (r * jax.nn.sigmoid(r))).astype(o_ref.dtype)
        return st

    lax.fori_loop(0, seq // ck, body, jnp.zeros((B_DV, B_DK), F32), unroll=4)


def gla(proj, lr_w, lr_b, og, *, bsz, seq):
    q0 = (AB0 + 3 * A_WIDTH) // B_DK
    k0 = q0 + B_HEADS
    v0 = (AB0 + 3 * A_WIDTH + 2 * B_KWIDTH) // B_DV
    r0 = v0 + B_HEADS
    lr0 = LR0 // LANES
    return pl.pallas_call(
        _gla_kernel,
        grid=(bsz, B_HEADS),
        in_specs=[pl.BlockSpec((seq, B_DK), lambda b, h: (b, q0 + h)),
                  pl.BlockSpec((seq, B_DK), lambda b, h: (b, k0 + h)),
                  pl.BlockSpec((seq, B_DV), lambda b, h: (b, v0 + h)),
                  pl.BlockSpec((seq, B_DV), lambda b, h: (b, r0 + h)),
                  pl.BlockSpec((seq, LANES), lambda b, h: (b, lr0)),
                  pl.BlockSpec((LANES, B_DK), lambda b, h: (0, h)),
                  pl.BlockSpec((1, B_DK), lambda b, h: (0, h)),
                  pl.BlockSpec((1, B_DV), lambda b, h: (0, 0))],
        out_specs=pl.BlockSpec((seq, B_DV), lambda b, h: (b, h)),
        out_shape=jax.ShapeDtypeStruct((bsz * seq, B_VWIDTH), BF16),
        scratch_shapes=[pltpu.VMEM((seq, B_DK), BF16),
                        pltpu.VMEM((seq, B_DK), BF16),
                        pltpu.VMEM((seq, B_DK), BF16),
                        pltpu.VMEM((seq // B_CHUNK, B_DK), F32)],
        compiler_params=_cparams("parallel", "parallel"),
        name="gla",
    )(proj, proj, proj, proj, proj, lr_w, lr_b.reshape(1, B_KWIDTH), og.reshape(1, B_DV))


def _gelu(x):
    c = math.sqrt(2.0 / math.pi)
    return 0.5 * x * (1.0 + jnp.tanh(c * (x + 0.044715 * (x * x * x))))


def _sgu_kernel(u_ref, v_ref, lg_ref, lb_ref, w_ref, b_ref, o_ref):
    rows = u_ref.shape[0]
    ti = lax.broadcasted_iota(jnp.int32, (C_CHUNK, C_CHUNK), 0)
    si = lax.broadcasted_iota(jnp.int32, (C_CHUNK, C_CHUNK), 1)
    causal = si <= ti
    ws = [jnp.where(causal, w_ref[g], 0.0).astype(BF16) for g in range(C_GROUPS)]
    for n in range(rows // C_CHUNK):
        r = slice(n * C_CHUNK, (n + 1) * C_CHUNK)
        v = _gelu(v_ref[r, :])
        mu = jnp.mean(v, axis=-1, keepdims=True)
        vc = v - mu
        vn = vc * lax.rsqrt(jnp.mean(vc * vc, axis=-1, keepdims=True) + EPS)
        vn = (vn * lg_ref[...] + lb_ref[...]).astype(BF16)
        u = _gelu(u_ref[r, :])
        for g in range(C_GROUPS):
            cols = slice(g * C_GROUP_DIM, (g + 1) * C_GROUP_DIM)
            mixed = jnp.dot(ws[g], vn[:, cols], preferred_element_type=F32) + b_ref[g]
            o_ref[r, cols] = (u[:, cols] * mixed).astype(o_ref.dtype)


def sgu(proj, ln_g, ln_b, w_s, b_s, *, tm):
    m = proj.shape[0]
    u0 = C0 // C_WIDTH
    return pl.pallas_call(
        _sgu_kernel,
        grid=(m // tm,),
        in_specs=[pl.BlockSpec((tm, C_WIDTH), lambda i: (i, u0)),
                  pl.BlockSpec((tm, C_WIDTH), lambda i: (i, u0 + 1)),
                  pl.BlockSpec((1, C_WIDTH), lambda i: (0, 0)),
                  pl.BlockSpec((1, C_WIDTH), lambda i: (0, 0)),
                  pl.BlockSpec((C_GROUPS, C_CHUNK, C_CHUNK), lambda i: (0, 0, 0)),
                  pl.BlockSpec((C_GROUPS, C_CHUNK, 1), lambda i: (0, 0, 0))],
        out_specs=pl.BlockSpec((tm, C_WIDTH), lambda i: (i, 0)),
        out_shape=jax.ShapeDtypeStruct((m, C_WIDTH), BF16),
        compiler_params=_cparams("parallel"),
        name="sgu",
    )(proj, proj, ln_g.reshape(1, C_WIDTH), ln_b.reshape(1, C_WIDTH), w_s,
      b_s.reshape(C_GROUPS, C_CHUNK, 1))


def _merge_kernel(ya_ref, yb_ref, yc_ref, ga_ref, gb_ref, gc_ref, x_ref,
                  wa_ref, wb_ref, wc_ref, wo_ref, o_ref):
    merged = (jax.nn.sigmoid(ga_ref[...]) * jnp.dot(ya_ref[...], wa_ref[...], preferred_element_type=F32)
              + jax.nn.sigmoid(gb_ref[...]) * jnp.dot(yb_ref[...], wb_ref[...], preferred_element_type=F32)
              + jax.nn.sigmoid(gc_ref[...]) * jnp.dot(yc_ref[...], wc_ref[...], preferred_element_type=F32))
    o_ref[...] = x_ref[...] + jnp.dot(merged.astype(BF16), wo_ref[...], preferred_element_type=F32)


def merge(ya, yb, yc, proj, x, wa, wb, wc, wo, layer, *, tm):
    m, d = x.shape
    resident = functools.partial(pl.BlockSpec, pipeline_mode=pl.Buffered(1))
    return pl.pallas_call(
        _merge_kernel,
        grid=(m // tm,),
        in_specs=[pl.BlockSpec((tm, A_WIDTH), lambda i: (i, 0)),
                  pl.BlockSpec((tm, B_VWIDTH), lambda i: (i, 0)),
                  pl.BlockSpec((tm, C_WIDTH), lambda i: (i, 0)),
                  pl.BlockSpec((tm, d), lambda i: (i, 0)),
                  pl.BlockSpec((tm, d), lambda i: (i, 1)),
                  pl.BlockSpec((tm, d), lambda i: (i, 2)),
                  pl.BlockSpec((tm, d), lambda i: (i, 0)),
                  resident((None, A_WIDTH, d), lambda i: (layer, 0, 0)),
                  resident((None, B_VWIDTH, d), lambda i: (layer, 0, 0)),
                  resident((None, C_WIDTH, d), lambda i: (layer, 0, 0)),
                  resident((None, d, d), lambda i: (layer, 0, 0))],
        out_specs=pl.BlockSpec((tm, d), lambda i: (i, 0)),
        out_shape=jax.ShapeDtypeStruct((m, d), F32),
        compiler_params=_cparams("parallel"),
        name="merge",
    )(ya, yb, yc, proj, proj, proj, x, wa, wb, wc, wo)


def _mlp_kernel(h_ref, g_ref, w1_ref, w2_ref, o_ref, hn_ref, *, row_chunk):
    @pl.when(pl.program_id(1) == 0)
    def _():
        def body(r, carry):
            rows = pl.ds(pl.multiple_of(r * row_chunk, row_chunk), row_chunk)
            h = h_ref[rows, :]
            ms = jnp.mean(h * h, axis=-1, keepdims=True)
            hn_ref[rows, :] = (h * lax.rsqrt(ms + EPS) * g_ref[...]).astype(BF16)
            o_ref[rows, :] = h
            return carry
        lax.fori_loop(0, h_ref.shape[0] // row_chunk, body, 0)

    a = jnp.dot(hn_ref[...], w1_ref[...].astype(BF16), preferred_element_type=F32)
    a = jnp.square(jnp.maximum(a, 0.0)).astype(BF16)
    o_ref[...] += jnp.dot(a, w2_ref[...].astype(BF16), preferred_element_type=F32)


def mlp(h, g, w1, w2, layer, *, tm, tf):
    m, d = h.shape
    f = w1.shape[2]
    return pl.pallas_call(
        functools.partial(_mlp_kernel, row_chunk=128),
        grid=(m // tm, f // tf),
        in_specs=[pl.BlockSpec((tm, d), lambda i, j: (i, 0), pipeline_mode=pl.Buffered(1)),
                  pl.BlockSpec((1, d), lambda i, j: (0, 0)),
                  pl.BlockSpec((None, d, tf), lambda i, j: (layer, 0, j)),
                  pl.BlockSpec((None, tf, d), lambda i, j: (layer, j, 0))],
        out_specs=pl.BlockSpec((tm, d), lambda i, j: (i, 0)),
        out_shape=jax.ShapeDtypeStruct((m, d), F32),
        scratch_shapes=[pltpu.VMEM((tm, d), BF16)],
        compiler_params=_cparams("parallel", "arbitrary"),
        name="mlp",
    )(h, g.reshape(1, d), w1, w2)


def _regroup_w_in(w):
    lr_end = AB_WIDTH + B_LOWRANK
    c_end = lr_end + 2 * C_WIDTH
    w = jnp.concatenate([w[..., c_end:], w[..., :AB_WIDTH], w[..., lr_end:c_end],
                         w[..., AB_WIDTH:lr_end]], axis=-1)
    return jnp.pad(w, ((0, 0), (0, 0), (0, PROJ_WIDTH - w.shape[-1]))).astype(BF16)


def kernel(x, rpe_table, norm1_g, w_in, q_norm_g, k_norm_g, gla_lr_w, gla_lr_b, gla_out_g,
           sg_ln_g, sg_ln_b, sg_w, sg_b, w_br_a, w_br_b, w_br_c, w_o, norm2_g, w_ff1, w_ff2):
    bsz, seq, d = x.shape
    assert seq % A_BLOCK == 0 and seq % B_CHUNK == 0 and seq % C_CHUNK == 0
    h = x.reshape(bsz * seq, d)
    bias, far = rpe_bias(rpe_table)
    w_proj = _regroup_w_in(w_in)
    wa, wb, wc, wo = (w.astype(BF16) for w in (w_br_a, w_br_b, w_br_c, w_o))
    lr_w = jnp.pad(gla_lr_w, ((0, 0), (0, LANES - B_LOWRANK), (0, 0)))
    for l in range(DEPTH):
        proj = norm_matmul(h, norm1_g[l], w_proj, l, tm=1024, tn=PROJ_TN, out_dtype=F32, name="proj")
        y_a = moba(proj, q_norm_g[l], k_norm_g[l], bias, far, bsz=bsz, seq=seq)
        y_b = gla(proj, lr_w[l], gla_lr_b[l], gla_out_g[l], bsz=bsz, seq=seq)
        y_c = sgu(proj, sg_ln_g[l], sg_ln_b[l], sg_w[l], sg_b[l], tm=1024)
        h = merge(y_a, y_b, y_c, proj, h, wa, wb, wc, wo, l, tm=256)
        h = mlp(h, norm2_g[l], w_ff1, w_ff2, l, tm=1024, tf=512)
    return h.reshape(bsz, seq, d)
```

```python
import functools
import math

import jax
import jax.numpy as jnp
from jax import lax
from jax.experimental import pallas as pl
from jax.experimental.pallas import tpu as pltpu

D_MODEL = 2048
DEPTH = 2
HEAD_DIM = 128
A_HEADS = 4
A_WIDTH = A_HEADS * HEAD_DIM
A_BLOCK = 256
A_TOPK = 3
RPE_BUCKETS = 32
RPE_MAX_DIST = 128
B_HEADS = 4
B_DK = 128
B_DV = 256
B_KWIDTH = B_HEADS * B_DK
B_VWIDTH = B_HEADS * B_DV
B_LOWRANK = 16
B_GATE_NORM = 16.0
B_CHUNK = 64
C_GROUPS = 4
C_GROUP_DIM = 128
C_WIDTH = C_GROUPS * C_GROUP_DIM
C_CHUNK = 128
D_FF = 4 * D_MODEL
EPS = 1e-6
NEG_INF = -1e30
LOG2E = 1.4426950408889634

LANES = 128
BF16_ROWS = 16
VMEM_LIMIT = 56 * 1024 * 1024

F32 = jnp.float32
BF16 = jnp.bfloat16

PROJ_TN = 1536
AB_WIDTH = 3 * A_WIDTH + 2 * B_KWIDTH + 2 * B_VWIDTH
G_WIDTH = 3 * D_MODEL
TAIL_WIDTH = B_LOWRANK + 2 * C_WIDTH + G_WIDTH
TAIL_PAD = PROJ_TN - (B_LOWRANK + 2 * C_WIDTH)
AB0 = G_WIDTH
T0 = AB0 + AB_WIDTH
LR_COL = T0 + TAIL_PAD
U_COL = LR_COL + B_LOWRANK
PROJ_WIDTH = T0 + PROJ_TN
N_AB_TILES = AB_WIDTH // PROJ_TN
N_G_TILES = G_WIDTH // PROJ_TN
assert AB_WIDTH % PROJ_TN == 0 and G_WIDTH % PROJ_TN == 0 and U_COL % C_WIDTH == 0
LR_LANE = LR_COL % LANES


def _rpe_thresholds():
    max_exact = RPE_BUCKETS // 2

    def bucket(n):
        if n < max_exact:
            return n
        v = math.log(n / max_exact) / math.log(RPE_MAX_DIST / max_exact) * (RPE_BUCKETS - max_exact)
        return min(max_exact + int(v), RPE_BUCKETS - 1)

    table = [bucket(n) for n in range(4 * RPE_MAX_DIST)]
    return [min(n for n in range(len(table)) if table[n] >= b)
            for b in range(max_exact + 1, RPE_BUCKETS)]


RPE_THRESHOLDS = _rpe_thresholds()
assert RPE_THRESHOLDS[-1] <= A_BLOCK


def _cparams(*sem):
    return pltpu.CompilerParams(dimension_semantics=sem, vmem_limit_bytes=VMEM_LIMIT)


def _rpe_bias_kernel(tbl_ref, bias_ref):
    h = pl.program_id(0)
    t = lax.broadcasted_iota(jnp.int32, (A_BLOCK, 2 * A_BLOCK), 0)
    c = lax.broadcasted_iota(jnp.int32, (A_BLOCK, 2 * A_BLOCK), 1)
    n = jnp.maximum(t - c + A_BLOCK, 0)
    max_exact = RPE_BUCKETS // 2
    large = jnp.full(n.shape, max_exact, jnp.int32)
    for thr in RPE_THRESHOLDS:
        large = large + (n >= thr).astype(jnp.int32)
    bucket = jnp.where(n < max_exact, n, large)
    bias = jnp.zeros(n.shape, F32)
    for b in range(RPE_BUCKETS):
        bias = jnp.where(bucket == b, tbl_ref[b, h], bias)
    bias = (bias - tbl_ref[RPE_BUCKETS - 1, h]) * LOG2E
    bias_ref[0] = jnp.where(c - A_BLOCK > t, NEG_INF, bias)


def rpe_bias(rpe_table):
    return pl.pallas_call(
        _rpe_bias_kernel,
        grid=(A_HEADS,),
        in_specs=[pl.BlockSpec(memory_space=pltpu.SMEM)],
        out_specs=pl.BlockSpec((1, A_BLOCK, 2 * A_BLOCK), lambda h: (h, 0, 0)),
        out_shape=jax.ShapeDtypeStruct((A_HEADS, A_BLOCK, 2 * A_BLOCK), F32),
        compiler_params=_cparams("arbitrary"),
        name="rpe_bias",
    )(rpe_table)


def _proj_kernel(x_ref, g_ref, wab_ref, wt_ref, o_ref, xn_ref, *, row_chunk):
    j = pl.program_id(1)

    @pl.when(j == 0)
    def _():
        def body(r, carry):
            rows = pl.ds(pl.multiple_of(r * row_chunk, row_chunk), row_chunk)
            x = x_ref[rows, :]
            ms = jnp.mean(x * x, axis=-1, keepdims=True)
            xn_ref[rows, :] = (x * lax.rsqrt(ms + EPS) * g_ref[...]).astype(BF16)
            return carry
        lax.fori_loop(0, x_ref.shape[0] // row_chunk, body, 0)

    @pl.when(j < N_AB_TILES)
    def _():
        o_ref[...] = jnp.dot(xn_ref[...], wab_ref[...], preferred_element_type=F32).astype(o_ref.dtype)

    @pl.when(j >= N_AB_TILES)
    def _():
        o_ref[...] = jnp.dot(xn_ref[...], wt_ref[...], preferred_element_type=F32).astype(o_ref.dtype)


def _proj_out_tile(j):
    return jnp.where(j < N_AB_TILES, j + N_G_TILES,
                     jnp.where(j == N_AB_TILES, N_G_TILES + N_AB_TILES, j - N_AB_TILES - 1))


def proj(x, g, w_ab, w_tail, layer, *, tm):
    m, k = x.shape
    tn = PROJ_TN
    n_tiles = N_AB_TILES + w_tail.shape[2] // tn
    return pl.pallas_call(
        functools.partial(_proj_kernel, row_chunk=128),
        grid=(m // tm, n_tiles),
        in_specs=[pl.BlockSpec((tm, k), lambda i, j: (i, 0)),
                  pl.BlockSpec((1, k), lambda i, j: (0, 0)),
                  pl.BlockSpec((None, k, tn), lambda i, j: (layer, 0, jnp.minimum(j, N_AB_TILES - 1))),
                  pl.BlockSpec((None, k, tn), lambda i, j: (layer, 0, jnp.maximum(j - N_AB_TILES, 0)))],
        out_specs=pl.BlockSpec((tm, tn), lambda i, j: (i, _proj_out_tile(j))),
        out_shape=jax.ShapeDtypeStruct((m, PROJ_WIDTH), BF16),
        scratch_shapes=[pltpu.VMEM((tm, k), BF16)],
        compiler_params=_cparams("parallel", "arbitrary"),
        name="proj",
    )(x, g.reshape(1, k), w_ab, w_tail)


def _head_rms(x, g):
    ms = jnp.mean(x * x, axis=-1, keepdims=True)
    return x * lax.rsqrt(ms + EPS) * g


def _moba_kernel(q_ref, k_ref, v_ref, qg_ref, kg_ref, bias_ref, o_ref,
                 qn_ref, qa_ref, ka_ref, va_ref, kmean_ref, s_ref, p_ref):
    seq, hd = q_ref.shape
    nb = seq // A_BLOCK
    rc = BF16_ROWS
    qn = _head_rms(q_ref[...].astype(F32), qg_ref[...])
    qn_ref[...] = qn
    qa_ref[:, :hd] = (qn * (hd ** -0.5 * LOG2E)).astype(BF16)
    qa_ref[:, hd:] = jnp.zeros((seq, hd), BF16)
    kn = _head_rms(k_ref[...].astype(F32), kg_ref[...])
    ka_ref[:, :hd] = kn.astype(BF16)
    key_blk = lax.broadcasted_iota(jnp.int32, (seq, hd), 0) // A_BLOCK
    ka_ref[:, hd:] = jnp.where(key_blk == lax.broadcasted_iota(jnp.int32, (seq, hd), 1), 1.0, 0.0).astype(BF16)
    va_ref[:, :hd] = v_ref[...]
    va_ref[:, hd:] = jnp.ones((seq, hd), BF16)
    for j in range(nb):
        kmean_ref[j:j + 1, :] = jnp.mean(kn[j * A_BLOCK:(j + 1) * A_BLOCK], axis=0, keepdims=True)
    kmean = kmean_ref[...]
    blk = lax.broadcasted_iota(jnp.int32, (nb, A_BLOCK), 0)
    eye = jnp.where(lax.broadcasted_iota(jnp.int32, (nb, hd), 0)
                    == lax.broadcasted_iota(jnp.int32, (nb, hd), 1), 1.0, 0.0)

    for i in range(nb):
        rows = slice(i * A_BLOCK, (i + 1) * A_BLOCK)
        nk = (i + 1) * A_BLOCK
        if i > A_TOPK:
            g = lax.dot_general(kmean, qn_ref[rows, :], (((1,), (1,)), ((), ())),
                                precision=lax.Precision.HIGHEST, preferred_element_type=F32)
            rank = jnp.zeros((nb, A_BLOCK), jnp.int32)
            for jp in range(i):
                gj = g[jp:jp + 1, :]
                beats = (gj > g) | ((gj == g) & (jp < blk))
                rank = rank + beats.astype(jnp.int32)
            neg = jnp.where((blk < i) & (rank >= A_TOPK), NEG_INF, 0.0)
            qa_ref[rows, hd:] = lax.dot_general(neg, eye, (((0,), (0,)), ((), ())),
                                                preferred_element_type=F32).astype(BF16)
        s_ref[:, :nk] = lax.dot_general(qa_ref[rows, :], ka_ref[:nk, :], (((1,), (1,)), ((), ())),
                                        preferred_element_type=F32)

        def chunk(c, carry, i=i):
            r = pl.ds(pl.multiple_of(c * rc, rc), rc)
            tiles = []
            for j in range(i + 1):
                sj = s_ref[r, j * A_BLOCK:(j + 1) * A_BLOCK]
                if j == i:
                    sj = sj + bias_ref[0, r, A_BLOCK:]
                elif j == i - 1:
                    sj = sj + bias_ref[0, r, :A_BLOCK]
                tiles.append(sj)
            mt = tiles[0]
            for t in tiles[1:]:
                mt = jnp.maximum(mt, t)
            m = jnp.max(mt, axis=-1, keepdims=True)
            for j, t in enumerate(tiles):
                p_ref[r, j * A_BLOCK:(j + 1) * A_BLOCK] = jnp.exp2(t - m).astype(BF16)
            return carry

        lax.fori_loop(0, A_BLOCK // rc, chunk, 0, unroll=True)
        o = jnp.dot(p_ref[:, :nk], va_ref[:nk, :], preferred_element_type=F32)
        o_ref[rows, :] = (o[:, :hd] / o[:, hd:]).astype(o_ref.dtype)


def moba(proj_out, qg, kg, bias, *, bsz, seq):
    hd = HEAD_DIM
    q0 = AB0 // hd
    return pl.pallas_call(
        _moba_kernel,
        grid=(bsz, A_HEADS),
        in_specs=[pl.BlockSpec((seq, hd), lambda b, h: (b, q0 + h)),
                  pl.BlockSpec((seq, hd), lambda b, h: (b, q0 + A_HEADS + h)),
                  pl.BlockSpec((seq, hd), lambda b, h: (b, q0 + 2 * A_HEADS + h)),
                  pl.BlockSpec((1, hd), lambda b, h: (0, 0)),
                  pl.BlockSpec((1, hd), lambda b, h: (0, 0)),
                  pl.BlockSpec((1, A_BLOCK, 2 * A_BLOCK), lambda b, h: (h, 0, 0))],
        out_specs=pl.BlockSpec((seq, hd), lambda b, h: (b, h)),
        out_shape=jax.ShapeDtypeStruct((bsz * seq, A_WIDTH), BF16),
        scratch_shapes=[pltpu.VMEM((seq, hd), F32),
                        pltpu.VMEM((seq, 2 * hd), BF16),
                        pltpu.VMEM((seq, 2 * hd), BF16),
                        pltpu.VMEM((seq, 2 * hd), BF16),
                        pltpu.VMEM((seq // A_BLOCK, hd), F32),
                        pltpu.VMEM((A_BLOCK, seq), F32),
                        pltpu.VMEM((A_BLOCK, seq), BF16)],
        compiler_params=_cparams("parallel", "parallel"),
        name="moba",
    )(proj_out, proj_out, proj_out, qg.reshape(1, hd), kg.reshape(1, hd), bias)


def _gla_kernel(q_ref, k_ref, v_ref, r_ref, lr_ref, lrw_ref, lrb_ref, og_ref, o_ref,
                qd_ref, ki_ref, ke_ref, dec_ref):
    seq = q_ref.shape[0]
    ck = B_CHUNK
    grp = 4 * ck
    qscale = B_DK ** -0.5

    ti = lax.broadcasted_iota(jnp.int32, (grp, grp), 0)
    si = lax.broadcasted_iota(jnp.int32, (grp, grp), 1)
    tril_grp = jnp.where((si <= ti) & (si // ck == ti // ck), 1.0, 0.0).astype(BF16)
    lrw = lrw_ref[...].astype(BF16)
    for g in range(seq // grp):
        rows = slice(g * grp, (g + 1) * grp)
        z = jnp.dot(lr_ref[rows, :], lrw, preferred_element_type=F32) + lrb_ref[...]
        la = -(jnp.maximum(-z, 0.0) + jnp.log1p(jnp.exp(-jnp.abs(z)))) / B_GATE_NORM
        hi = la.astype(BF16)
        r1 = la - hi.astype(F32)
        mid = r1.astype(BF16)
        lo = (r1 - mid.astype(F32)).astype(BF16)
        parts = jnp.dot(tril_grp, jnp.concatenate([hi, mid, lo], axis=1), preferred_element_type=F32)
        bc = parts[:, :B_DK] + parts[:, B_DK:2 * B_DK] + parts[:, 2 * B_DK:]
        bc3 = bc.reshape(grp // ck, ck, B_DK)
        bl3 = bc3[:, ck - 1:ck, :]
        q = q_ref[rows, :].astype(F32)
        k3 = k_ref[rows, :].astype(F32).reshape(grp // ck, ck, B_DK)
        qd_ref[rows, :] = (q * jnp.exp(bc) * qscale).astype(BF16)
        ki_ref[rows, :] = (k3 * jnp.exp(-bc3)).reshape(grp, B_DK).astype(BF16)
        ke_ref[rows, :] = (k3 * jnp.exp(bl3 - bc3)).reshape(grp, B_DK).astype(BF16)
        dec_ref[g * (grp // ck):(g + 1) * (grp // ck), :] = jnp.exp(bl3).reshape(grp // ck, B_DK)

    tc = lax.broadcasted_iota(jnp.int32, (ck, ck), 0)
    sc = lax.broadcasted_iota(jnp.int32, (ck, ck), 1)
    causal = sc <= tc

    def body(c, st):
        rows = pl.ds(pl.multiple_of(c * ck, ck), ck)
        qd = qd_ref[rows, :]
        v = v_ref[rows, :]
        a = lax.dot_general(qd, ki_ref[rows, :], (((1,), (1,)), ((), ())), preferred_element_type=F32)
        a = jnp.where(causal, a, 0.0).astype(BF16)
        o = (jnp.dot(a, v, preferred_element_type=F32)
             + lax.dot_general(qd, st.astype(BF16), (((1,), (1,)), ((), ())),
                               preferred_element_type=F32))
        st = st * dec_ref[pl.ds(c, 1), :] + lax.dot_general(
            v, ke_ref[rows, :], (((0,), (0,)), ((), ())), preferred_element_type=F32)
        ms = jnp.mean(o * o, axis=-1, keepdims=True)
        y = o * lax.rsqrt(ms + EPS) * og_ref[...]
        r = r_ref[rows, :].astype(F32)
        o_ref[rows, :] = (y * (r * jax.nn.sigmoid(r))).astype(o_ref.dtype)
        return st

    lax.fori_loop(0, seq // ck, body, jnp.zeros((B_DV, B_DK), F32), unroll=4)


def gla(proj_out, lr_w, lr_b, og, layer, *, bsz, seq):
    q0 = (AB0 + 3 * A_WIDTH) // B_DK
    k0 = q0 + B_HEADS
    v0 = (AB0 + 3 * A_WIDTH + 2 * B_KWIDTH) // B_DV
    r0 = v0 + B_HEADS
    lr0 = LR_COL // LANES
    return pl.pallas_call(
        _gla_kernel,
        grid=(bsz, B_HEADS),
        in_specs=[pl.BlockSpec((seq, B_DK), lambda b, h: (b, q0 + h)),
                  pl.BlockSpec((seq, B_DK), lambda b, h: (b, k0 + h)),
                  pl.BlockSpec((seq, B_DV), lambda b, h: (b, v0 + h)),
                  pl.BlockSpec((seq, B_DV), lambda b, h: (b, r0 + h)),
                  pl.BlockSpec((seq, LANES), lambda b, h: (b, lr0)),
                  pl.BlockSpec((None, LANES, B_DK), lambda b, h: (layer, 0, h)),
                  pl.BlockSpec((1, B_DK), lambda b, h: (0, h)),
                  pl.BlockSpec((1, B_DV), lambda b, h: (0, 0))],
        out_specs=pl.BlockSpec((seq, B_DV), lambda b, h: (b, h)),
        out_shape=jax.ShapeDtypeStruct((bsz * seq, B_VWIDTH), BF16),
        scratch_shapes=[pltpu.VMEM((seq, B_DK), BF16),
                        pltpu.VMEM((seq, B_DK), BF16),
                        pltpu.VMEM((seq, B_DK), BF16),
                        pltpu.VMEM((seq // B_CHUNK, B_DK), F32)],
        compiler_params=_cparams("parallel", "parallel"),
        name="gla",
    )(proj_out, proj_out, proj_out, proj_out, proj_out, lr_w, lr_b.reshape(1, B_KWIDTH),
      og.reshape(1, B_DV))


def _gelu(x):
    c = math.sqrt(2.0 / math.pi)
    return 0.5 * x * (1.0 + jnp.tanh(c * (x + 0.044715 * (x * x * x))))


def _sgu_kernel(u_ref, v_ref, lg_ref, lb_ref, w_ref, b_ref, o_ref):
    rows = u_ref.shape[0]
    ti = lax.broadcasted_iota(jnp.int32, (C_CHUNK, C_CHUNK), 0)
    si = lax.broadcasted_iota(jnp.int32, (C_CHUNK, C_CHUNK), 1)
    causal = si <= ti
    ws = [jnp.where(causal, w_ref[g], 0.0).astype(BF16) for g in range(C_GROUPS)]
    for n in range(rows // C_CHUNK):
        r = slice(n * C_CHUNK, (n + 1) * C_CHUNK)
        v = _gelu(v_ref[r, :].astype(F32))
        mu = jnp.mean(v, axis=-1, keepdims=True)
        vc = v - mu
        vn = vc * lax.rsqrt(jnp.mean(vc * vc, axis=-1, keepdims=True) + EPS)
        vn = (vn * lg_ref[...] + lb_ref[...]).astype(BF16)
        u = _gelu(u_ref[r, :].astype(F32))
        for g in range(C_GROUPS):
            cols = slice(g * C_GROUP_DIM, (g + 1) * C_GROUP_DIM)
            mixed = jnp.dot(ws[g], vn[:, cols], preferred_element_type=F32) + b_ref[g]
            o_ref[r, cols] = (u[:, cols] * mixed).astype(o_ref.dtype)


def sgu(proj_out, ln_g, ln_b, w_s, b_s, *, tm):
    m = proj_out.shape[0]
    u0 = U_COL // C_WIDTH
    return pl.pallas_call(
        _sgu_kernel,
        grid=(m // tm,),
        in_specs=[pl.BlockSpec((tm, C_WIDTH), lambda i: (i, u0)),
                  pl.BlockSpec((tm, C_WIDTH), lambda i: (i, u0 + 1)),
                  pl.BlockSpec((1, C_WIDTH), lambda i: (0, 0)),
                  pl.BlockSpec((1, C_WIDTH), lambda i: (0, 0)),
                  pl.BlockSpec((C_GROUPS, C_CHUNK, C_CHUNK), lambda i: (0, 0, 0)),
                  pl.BlockSpec((C_GROUPS, C_CHUNK, 1), lambda i: (0, 0, 0))],
        out_specs=pl.BlockSpec((tm, C_WIDTH), lambda i: (i, 0)),
        out_shape=jax.ShapeDtypeStruct((m, C_WIDTH), BF16),
        compiler_params=_cparams("parallel"),
        name="sgu",
    )(proj_out, proj_out, ln_g.reshape(1, C_WIDTH), ln_b.reshape(1, C_WIDTH), w_s,
      b_s.reshape(C_GROUPS, C_CHUNK, 1))


def _merge_kernel(ya_ref, yb_ref, yc_ref, ga_ref, gb_ref, gc_ref, x_ref,
                  wa_ref, wb_ref, wc_ref, wo_ref, o_ref):
    def gated(g_ref, y_ref, w_ref):
        return (jax.nn.sigmoid(g_ref[...].astype(F32))
                * jnp.dot(y_ref[...], w_ref[...], preferred_element_type=F32))

    merged = gated(ga_ref, ya_ref, wa_ref) + gated(gb_ref, yb_ref, wb_ref) + gated(gc_ref, yc_ref, wc_ref)
    o_ref[...] = x_ref[...] + jnp.dot(merged.astype(BF16), wo_ref[...], preferred_element_type=F32)


def merge(ya, yb, yc, proj_out, x, wa, wb, wc, wo, layer, *, tm):
    m, d = x.shape
    resident = functools.partial(pl.BlockSpec, pipeline_mode=pl.Buffered(1))
    return pl.pallas_call(
        _merge_kernel,
        grid=(m // tm,),
        in_specs=[pl.BlockSpec((tm, A_WIDTH), lambda i: (i, 0)),
                  pl.BlockSpec((tm, B_VWIDTH), lambda i: (i, 0)),
                  pl.BlockSpec((tm, C_WIDTH), lambda i: (i, 0)),
                  pl.BlockSpec((tm, d), lambda i: (i, 0)),
                  pl.BlockSpec((tm, d), lambda i: (i, 1)),
                  pl.BlockSpec((tm, d), lambda i: (i, 2)),
                  pl.BlockSpec((tm, d), lambda i: (i, 0)),
                  resident((None, A_WIDTH, d), lambda i: (layer, 0, 0)),
                  resident((None, B_VWIDTH, d), lambda i: (layer, 0, 0)),
                  resident((None, C_WIDTH, d), lambda i: (layer, 0, 0)),
                  resident((None, d, d), lambda i: (layer, 0, 0))],
        out_specs=pl.BlockSpec((tm, d), lambda i: (i, 0)),
        out_shape=jax.ShapeDtypeStruct((m, d), F32),
        compiler_params=_cparams("parallel"),
        name="merge",
    )(ya, yb, yc, proj_out, proj_out, proj_out, x, wa, wb, wc, wo)


def _mlp_kernel(h_ref, g_ref, w1_ref, w2_ref, o_ref, hn_ref, *, row_chunk):
    @pl.when(pl.program_id(1) == 0)
    def _():
        def body(r, carry):
            rows = pl.ds(pl.multiple_of(r * row_chunk, row_chunk), row_chunk)
            h = h_ref[rows, :]
            ms = jnp.mean(h * h, axis=-1, keepdims=True)
            hn_ref[rows, :] = (h * lax.rsqrt(ms + EPS) * g_ref[...]).astype(BF16)
            o_ref[rows, :] = h
            return carry
        lax.fori_loop(0, h_ref.shape[0] // row_chunk, body, 0)

    a = jnp.dot(hn_ref[...], w1_ref[...].astype(BF16), preferred_element_type=F32)
    a = jnp.square(jnp.maximum(a, 0.0)).astype(BF16)
    o_ref[...] += jnp.dot(a, w2_ref[...].astype(BF16), preferred_element_type=F32)


def mlp(h, g, w1, w2, layer, *, tm, tf):
    m, d = h.shape
    f = w1.shape[2]
    return pl.pallas_call(
        functools.partial(_mlp_kernel, row_chunk=128),
        grid=(m // tm, f // tf),
        in_specs=[pl.BlockSpec((tm, d), lambda i, j: (i, 0), pipeline_mode=pl.Buffered(1)),
                  pl.BlockSpec((1, d), lambda i, j: (0, 0)),
                  pl.BlockSpec((None, d, tf), lambda i, j: (layer, 0, j)),
                  pl.BlockSpec((None, tf, d), lambda i, j: (layer, j, 0))],
        out_specs=pl.BlockSpec((tm, d), lambda i, j: (i, 0)),
        out_shape=jax.ShapeDtypeStruct((m, d), F32),
        scratch_shapes=[pltpu.VMEM((tm, d), BF16)],
        compiler_params=_cparams("parallel", "arbitrary"),
        name="mlp",
    )(h, g.reshape(1, d), w1, w2)


def kernel(x, rpe_table, norm1_g, w_in, q_norm_g, k_norm_g, gla_lr_w, gla_lr_b, gla_out_g,
           sg_ln_g, sg_ln_b, sg_w, sg_b, w_br_a, w_br_b, w_br_c, w_o, norm2_g, w_ff1, w_ff2):
    bsz, seq, d = x.shape
    assert seq % A_BLOCK == 0 and seq % B_CHUNK == 0 and seq % C_CHUNK == 0
    assert w_in.shape[2] == AB_WIDTH + TAIL_WIDTH
    h = x.reshape(bsz * seq, d)
    bias = rpe_bias(rpe_table)
    w_ab = w_in[..., :AB_WIDTH].astype(BF16)
    w_tail = jnp.pad(w_in[..., AB_WIDTH:], ((0, 0), (0, 0), (TAIL_PAD, 0))).astype(BF16)
    wa, wb, wc, wo = (w.astype(BF16) for w in (w_br_a, w_br_b, w_br_c, w_o))
    lr_w = jnp.pad(gla_lr_w, ((0, 0), (LR_LANE, LANES - LR_LANE - B_LOWRANK), (0, 0)))
    for l in range(DEPTH):
        p = proj(h, norm1_g[l], w_ab, w_tail, l, tm=1024)
        y_a = moba(p, q_norm_g[l], k_norm_g[l], bias, bsz=bsz, seq=seq)
        y_b = gla(p, lr_w, gla_lr_b[l], gla_out_g[l], l, bsz=bsz, seq=seq)
        y_c = sgu(p, sg_ln_g[l], sg_ln_b[l], sg_w[l], sg_b[l], tm=1024)
        h = merge(y_a, y_b, y_c, p, h, wa, wb, wc, wo, l, tm=256)
        h = mlp(h, norm2_g[l], w_ff1, w_ff2, l, tm=1024, tf=512)
    return h.reshape(bsz, seq, d)
```

```python
import functools
import math

import jax
import jax.numpy as jnp
from jax import lax
from jax.experimental import pallas as pl
from jax.experimental.pallas import tpu as pltpu

D_MODEL = 2048
DEPTH = 2
HEAD_DIM = 128
A_HEADS = 4
A_WIDTH = A_HEADS * HEAD_DIM
A_BLOCK = 256
A_TOPK = 3
RPE_BUCKETS = 32
RPE_MAX_DIST = 128
B_HEADS = 4
B_DK = 128
B_DV = 256
B_KWIDTH = B_HEADS * B_DK
B_VWIDTH = B_HEADS * B_DV
B_LOWRANK = 16
B_GATE_NORM = 16.0
B_CHUNK = 64
C_GROUPS = 4
C_GROUP_DIM = 128
C_WIDTH = C_GROUPS * C_GROUP_DIM
C_CHUNK = 128
D_FF = 4 * D_MODEL
EPS = 1e-6
NEG_INF = -1e30
LOG2E = 1.4426950408889634

LANES = 128
BF16_ROWS = 16
VMEM_LIMIT = 56 * 1024 * 1024

F32 = jnp.float32
BF16 = jnp.bfloat16

PROJ_TN = 1536
AB_WIDTH = 3 * A_WIDTH + 2 * B_KWIDTH + 2 * B_VWIDTH
G_WIDTH = 3 * D_MODEL
TAIL_WIDTH = B_LOWRANK + 2 * C_WIDTH + G_WIDTH
TAIL_PAD = PROJ_TN - (B_LOWRANK + 2 * C_WIDTH)
AB0 = G_WIDTH
T0 = AB0 + AB_WIDTH
LR_COL = T0 + TAIL_PAD
U_COL = LR_COL + B_LOWRANK
PROJ_WIDTH = T0 + PROJ_TN
N_AB_TILES = AB_WIDTH // PROJ_TN
N_G_TILES = G_WIDTH // PROJ_TN
assert AB_WIDTH % PROJ_TN == 0 and G_WIDTH % PROJ_TN == 0 and U_COL % C_WIDTH == 0
LR_LANE = LR_COL % LANES


def _rpe_thresholds():
    max_exact = RPE_BUCKETS // 2

    def bucket(n):
        if n < max_exact:
            return n
        v = math.log(n / max_exact) / math.log(RPE_MAX_DIST / max_exact) * (RPE_BUCKETS - max_exact)
        return min(max_exact + int(v), RPE_BUCKETS - 1)

    table = [bucket(n) for n in range(4 * RPE_MAX_DIST)]
    return [min(n for n in range(len(table)) if table[n] >= b)
            for b in range(max_exact + 1, RPE_BUCKETS)]


RPE_THRESHOLDS = _rpe_thresholds()
assert RPE_THRESHOLDS[-1] <= A_BLOCK


def _cparams(*sem):
    return pltpu.CompilerParams(dimension_semantics=sem, vmem_limit_bytes=VMEM_LIMIT)


def _rpe_bias_kernel(tbl_ref, bias_ref):
    h = pl.program_id(0)
    t = lax.broadcasted_iota(jnp.int32, (A_BLOCK, 2 * A_BLOCK), 0)
    c = lax.broadcasted_iota(jnp.int32, (A_BLOCK, 2 * A_BLOCK), 1)
    n = jnp.maximum(t - c + A_BLOCK, 0)
    max_exact = RPE_BUCKETS // 2
    large = jnp.full(n.shape, max_exact, jnp.int32)
    for thr in RPE_THRESHOLDS:
        large = large + (n >= thr).astype(jnp.int32)
    bucket = jnp.where(n < max_exact, n, large)
    bias = jnp.zeros(n.shape, F32)
    for b in range(RPE_BUCKETS):
        bias = jnp.where(bucket == b, tbl_ref[b, h], bias)
    bias = (bias - tbl_ref[RPE_BUCKETS - 1, h]) * LOG2E
    bias_ref[0] = jnp.where(c - A_BLOCK > t, NEG_INF, bias)


def rpe_bias(rpe_table):
    return pl.pallas_call(
        _rpe_bias_kernel,
        grid=(A_HEADS,),
        in_specs=[pl.BlockSpec(memory_space=pltpu.SMEM)],
        out_specs=pl.BlockSpec((1, A_BLOCK, 2 * A_BLOCK), lambda h: (h, 0, 0)),
        out_shape=jax.ShapeDtypeStruct((A_HEADS, A_BLOCK, 2 * A_BLOCK), F32),
        compiler_params=_cparams("arbitrary"),
        name="rpe_bias",
    )(rpe_table)


REGROUP_ROWS = 1024
REGROUP_TW = 512
TAIL_SRC0 = AB_WIDTH - TAIL_PAD
TAIL_SHIFT = TAIL_SRC0 % REGROUP_TW
assert TAIL_SHIFT < LANES and (AB_WIDTH + TAIL_WIDTH - TAIL_SRC0) % REGROUP_TW == 0


def _cast_kernel(w_ref, o_ref):
    o_ref[...] = w_ref[...].astype(o_ref.dtype)


def cast_ab(w_in):
    layers, d, _ = w_in.shape
    rows, tn = REGROUP_ROWS, PROJ_TN
    return pl.pallas_call(
        _cast_kernel,
        grid=(layers, d // rows, AB_WIDTH // tn),
        in_specs=[pl.BlockSpec((None, rows, tn), lambda l, r, c: (l, r, c))],
        out_specs=pl.BlockSpec((None, rows, tn), lambda l, r, c: (l, r, c)),
        out_shape=jax.ShapeDtypeStruct((layers, d, AB_WIDTH), BF16),
        compiler_params=_cparams("parallel", "parallel", "parallel"),
        name="cast_ab",
    )(w_in)


def _shift_cast_kernel(a_ref, b_ref, o_ref):
    tw = o_ref.shape[1]
    cat = jnp.concatenate([a_ref[...], b_ref[...]], axis=1)
    y = pltpu.roll(cat, tw + LANES - TAIL_SHIFT, axis=1)[:, :tw]
    lane = lax.broadcasted_iota(jnp.int32, y.shape, 1) + pl.program_id(2) * tw
    o_ref[...] = jnp.where(lane < TAIL_PAD, 0.0, y).astype(o_ref.dtype)


def shift_cast_tail(w_in):
    layers, d, n = w_in.shape
    rows, tw = REGROUP_ROWS, REGROUP_TW
    a0 = TAIL_SRC0 // tw
    out_w = TAIL_PAD + TAIL_WIDTH
    return pl.pallas_call(
        _shift_cast_kernel,
        grid=(layers, d // rows, out_w // tw),
        in_specs=[pl.BlockSpec((None, rows, tw), lambda l, r, c: (l, r, a0 + c)),
                  pl.BlockSpec((None, rows, LANES), lambda l, r, c: (l, r, (a0 + c + 1) * (tw // LANES)))],
        out_specs=pl.BlockSpec((None, rows, tw), lambda l, r, c: (l, r, c)),
        out_shape=jax.ShapeDtypeStruct((layers, d, out_w), BF16),
        compiler_params=_cparams("parallel", "parallel", "parallel"),
        name="shift_cast_tail",
    )(w_in, w_in)


def _proj_kernel(x_ref, g_ref, wab_ref, wt_ref, o_ref, xn_ref, *, row_chunk):
    j = pl.program_id(1)

    @pl.when(j == 0)
    def _():
        def body(r, carry):
            rows = pl.ds(pl.multiple_of(r * row_chunk, row_chunk), row_chunk)
            x = x_ref[rows, :]
            ms = jnp.mean(x * x, axis=-1, keepdims=True)
            xn_ref[rows, :] = (x * lax.rsqrt(ms + EPS) * g_ref[...]).astype(BF16)
            return carry
        lax.fori_loop(0, x_ref.shape[0] // row_chunk, body, 0)

    @pl.when(j < N_AB_TILES)
    def _():
        o_ref[...] = jnp.dot(xn_ref[...], wab_ref[...], preferred_element_type=F32).astype(o_ref.dtype)

    @pl.when(j >= N_AB_TILES)
    def _():
        o_ref[...] = jnp.dot(xn_ref[...], wt_ref[...], preferred_element_type=F32).astype(o_ref.dtype)


def _proj_out_tile(j):
    return jnp.where(j < N_AB_TILES, j + N_G_TILES,
                     jnp.where(j == N_AB_TILES, N_G_TILES + N_AB_TILES, j - N_AB_TILES - 1))


def proj(x, g, w_ab, w_tail, layer, *, tm):
    m, k = x.shape
    tn = PROJ_TN
    n_tiles = N_AB_TILES + w_tail.shape[2] // tn
    return pl.pallas_call(
        functools.partial(_proj_kernel, row_chunk=128),
        grid=(m // tm, n_tiles),
        in_specs=[pl.BlockSpec((tm, k), lambda i, j: (i, 0)),
                  pl.BlockSpec((1, k), lambda i, j: (0, 0)),
                  pl.BlockSpec((None, k, tn), lambda i, j: (layer, 0, jnp.minimum(j, N_AB_TILES - 1))),
                  pl.BlockSpec((None, k, tn), lambda i, j: (layer, 0, jnp.maximum(j - N_AB_TILES, 0)))],
        out_specs=pl.BlockSpec((tm, tn), lambda i, j: (i, _proj_out_tile(j))),
        out_shape=jax.ShapeDtypeStruct((m, PROJ_WIDTH), BF16),
        scratch_shapes=[pltpu.VMEM((tm, k), BF16)],
        compiler_params=_cparams("parallel", "arbitrary"),
        name="proj",
    )(x, g.reshape(1, k), w_ab, w_tail)


def _head_rms(x, g):
    ms = jnp.mean(x * x, axis=-1, keepdims=True)
    return x * lax.rsqrt(ms + EPS) * g


def _moba_kernel(q_ref, k_ref, v_ref, qg_ref, kg_ref, bias_ref, o_ref,
                 qn_ref, qa_ref, ka_ref, va_ref, kmean_ref, s_ref, p_ref):
    seq, hd = q_ref.shape
    nb = seq // A_BLOCK
    rc = BF16_ROWS
    qn = _head_rms(q_ref[...].astype(F32), qg_ref[...])
    qn_ref[...] = qn
    qa_ref[:, :hd] = (qn * (hd ** -0.5 * LOG2E)).astype(BF16)
    qa_ref[:, hd:] = jnp.zeros((seq, hd), BF16)
    kn = _head_rms(k_ref[...].astype(F32), kg_ref[...])
    ka_ref[:, :hd] = kn.astype(BF16)
    key_blk = lax.broadcasted_iota(jnp.int32, (seq, hd), 0) // A_BLOCK
    ka_ref[:, hd:] = jnp.where(key_blk == lax.broadcasted_iota(jnp.int32, (seq, hd), 1), 1.0, 0.0).astype(BF16)
    va_ref[:, :hd] = v_ref[...]
    va_ref[:, hd:] = jnp.ones((seq, hd), BF16)
    for j in range(nb):
        kmean_ref[j:j + 1, :] = jnp.mean(kn[j * A_BLOCK:(j + 1) * A_BLOCK], axis=0, keepdims=True)
    kmean = kmean_ref[...]
    blk = lax.broadcasted_iota(jnp.int32, (nb, A_BLOCK), 0)
    eye = jnp.where(lax.broadcasted_iota(jnp.int32, (nb, hd), 0)
                    == lax.broadcasted_iota(jnp.int32, (nb, hd), 1), 1.0, 0.0)

    for i in range(nb):
        rows = slice(i * A_BLOCK, (i + 1) * A_BLOCK)
        nk = (i + 1) * A_BLOCK
        if i > A_TOPK:
            g = lax.dot_general(kmean, qn_ref[rows, :], (((1,), (1,)), ((), ())),
                                precision=lax.Precision.HIGHEST, preferred_element_type=F32)
            rank = jnp.zeros((nb, A_BLOCK), jnp.int32)
            for jp in range(i):
                gj = g[jp:jp + 1, :]
                beats = (gj > g) | ((gj == g) & (jp < blk))
                rank = rank + beats.astype(jnp.int32)
            neg = jnp.where((blk < i) & (rank >= A_TOPK), NEG_INF, 0.0)
            qa_ref[rows, hd:] = lax.dot_general(neg, eye, (((0,), (0,)), ((), ())),
                                                preferred_element_type=F32).astype(BF16)
        s_ref[:, :nk] = lax.dot_general(qa_ref[rows, :], ka_ref[:nk, :], (((1,), (1,)), ((), ())),
                                        preferred_element_type=F32)

        def chunk(c, carry, i=i):
            r = pl.ds(pl.multiple_of(c * rc, rc), rc)
            tiles = []
            for j in range(i + 1):
                sj = s_ref[r, j * A_BLOCK:(j + 1) * A_BLOCK]
                if j == i:
                    sj = sj + bias_ref[0, r, A_BLOCK:]
                elif j == i - 1:
                    sj = sj + bias_ref[0, r, :A_BLOCK]
                tiles.append(sj)
            mt = tiles[0]
            for t in tiles[1:]:
                mt = jnp.maximum(mt, t)
            m = jnp.max(mt, axis=-1, keepdims=True)
            for j, t in enumerate(tiles):
                p_ref[r, j * A_BLOCK:(j + 1) * A_BLOCK] = jnp.exp2(t - m).astype(BF16)
            return carry

        lax.fori_loop(0, A_BLOCK // rc, chunk, 0, unroll=True)
        o = jnp.dot(p_ref[:, :nk], va_ref[:nk, :], preferred_element_type=F32)
        o_ref[rows, :] = (o[:, :hd] / o[:, hd:]).astype(o_ref.dtype)


def moba(proj_out, qg, kg, bias, *, bsz, seq):
    hd = HEAD_DIM
    q0 = AB0 // hd
    return pl.pallas_call(
        _moba_kernel,
        grid=(bsz, A_HEADS),
        in_specs=[pl.BlockSpec((seq, hd), lambda b, h: (b, q0 + h)),
                  pl.BlockSpec((seq, hd), lambda b, h: (b, q0 + A_HEADS + h)),
                  pl.BlockSpec((seq, hd), lambda b, h: (b, q0 + 2 * A_HEADS + h)),
                  pl.BlockSpec((1, hd), lambda b, h: (0, 0)),
                  pl.BlockSpec((1, hd), lambda b, h: (0, 0)),
                  pl.BlockSpec((1, A_BLOCK, 2 * A_BLOCK), lambda b, h: (h, 0, 0))],
        out_specs=pl.BlockSpec((seq, hd), lambda b, h: (b, h)),
        out_shape=jax.ShapeDtypeStruct((bsz * seq, A_WIDTH), BF16),
        scratch_shapes=[pltpu.VMEM((seq, hd), F32),
                        pltpu.VMEM((seq, 2 * hd), BF16),
                        pltpu.VMEM((seq, 2 * hd), BF16),
                        pltpu.VMEM((seq, 2 * hd), BF16),
                        pltpu.VMEM((seq // A_BLOCK, hd), F32),
                        pltpu.VMEM((A_BLOCK, seq), F32),
                        pltpu.VMEM((A_BLOCK, seq), BF16)],
        compiler_params=_cparams("parallel", "parallel"),
        name="moba",
    )(proj_out, proj_out, proj_out, qg.reshape(1, hd), kg.reshape(1, hd), bias)


def _gla_kernel(q_ref, k_ref, v_ref, r_ref, lr_ref, lrw_ref, lrb_ref, og_ref, o_ref,
                qd_ref, ki_ref, ke_ref, dec_ref):
    seq = q_ref.shape[0]
    ck = B_CHUNK
    grp = 4 * ck
    qscale = B_DK ** -0.5

    ti = lax.broadcasted_iota(jnp.int32, (grp, grp), 0)
    si = lax.broadcasted_iota(jnp.int32, (grp, grp), 1)
    tril_grp = jnp.where((si <= ti) & (si // ck == ti // ck), 1.0, 0.0).astype(BF16)
    lrw = lrw_ref[...].astype(BF16)
    for g in range(seq // grp):
        rows = slice(g * grp, (g + 1) * grp)
        z = jnp.dot(lr_ref[rows, :], lrw, preferred_element_type=F32) + lrb_ref[...]
        la = -(jnp.maximum(-z, 0.0) + jnp.log1p(jnp.exp(-jnp.abs(z)))) / B_GATE_NORM
        hi = la.astype(BF16)
        r1 = la - hi.astype(F32)
        mid = r1.astype(BF16)
        lo = (r1 - mid.astype(F32)).astype(BF16)
        parts = jnp.dot(tril_grp, jnp.concatenate([hi, mid, lo], axis=1), preferred_element_type=F32)
        bc = parts[:, :B_DK] + parts[:, B_DK:2 * B_DK] + parts[:, 2 * B_DK:]
        bc3 = bc.reshape(grp // ck, ck, B_DK)
        bl3 = bc3[:, ck - 1:ck, :]
        q = q_ref[rows, :].astype(F32)
        k3 = k_ref[rows, :].astype(F32).reshape(grp // ck, ck, B_DK)
        qd_ref[rows, :] = (q * jnp.exp(bc) * qscale).astype(BF16)
        ki_ref[rows, :] = (k3 * jnp.exp(-bc3)).reshape(grp, B_DK).astype(BF16)
        ke_ref[rows, :] = (k3 * jnp.exp(bl3 - bc3)).reshape(grp, B_DK).astype(BF16)
        dec_ref[g * (grp // ck):(g + 1) * (grp // ck), :] = jnp.exp(bl3).reshape(grp // ck, B_DK)

    tc = lax.broadcasted_iota(jnp.int32, (ck, ck), 0)
    sc = lax.broadcasted_iota(jnp.int32, (ck, ck), 1)
    causal = sc <= tc

    def body(c, st):
        rows = pl.ds(pl.multiple_of(c * ck, ck), ck)
        qd = qd_ref[rows, :]
        v = v_ref[rows, :]
        a = lax.dot_general(qd, ki_ref[rows, :], (((1,), (1,)), ((), ())), preferred_element_type=F32)
        a = jnp.where(causal, a, 0.0).astype(BF16)
        o = (jnp.dot(a, v, preferred_element_type=F32)
             + lax.dot_general(qd, st.astype(BF16), (((1,), (1,)), ((), ())),
                               preferred_element_type=F32))
        st = st * dec_ref[pl.ds(c, 1), :] + lax.dot_general(
            v, ke_ref[rows, :], (((0,), (0,)), ((), ())), preferred_element_type=F32)
        ms = jnp.mean(o * o, axis=-1, keepdims=True)
        y = o * lax.rsqrt(ms + EPS) * og_ref[...]
        r = r_ref[rows, :].astype(F32)
        o_ref[rows, :] = (y * (r * jax.nn.sigmoid(r))).astype(o_ref.dtype)
        return st

    lax.fori_loop(0, seq // ck, body, jnp.zeros((B_DV, B_DK), F32), unroll=8)


def gla(proj_out, lr_w, lr_b, og, layer, *, bsz, seq):
    q0 = (AB0 + 3 * A_WIDTH) // B_DK
    k0 = q0 + B_HEADS
    v0 = (AB0 + 3 * A_WIDTH + 2 * B_KWIDTH) // B_DV
    r0 = v0 + B_HEADS
    lr0 = LR_COL // LANES
    return pl.pallas_call(
        _gla_kernel,
        grid=(bsz, B_HEADS),
        in_specs=[pl.BlockSpec((seq, B_DK), lambda b, h: (b, q0 + h)),
                  pl.BlockSpec((seq, B_DK), lambda b, h: (b, k0 + h)),
                  pl.BlockSpec((seq, B_DV), lambda b, h: (b, v0 + h)),
                  pl.BlockSpec((seq, B_DV), lambda b, h: (b, r0 + h)),
                  pl.BlockSpec((seq, LANES), lambda b, h: (b, lr0)),
                  pl.BlockSpec((None, LANES, B_DK), lambda b, h: (layer, 0, h)),
                  pl.BlockSpec((1, B_DK), lambda b, h: (0, h)),
                  pl.BlockSpec((1, B_DV), lambda b, h: (0, 0))],
        out_specs=pl.BlockSpec((seq, B_DV), lambda b, h: (b, h)),
        out_shape=jax.ShapeDtypeStruct((bsz * seq, B_VWIDTH), BF16),
        scratch_shapes=[pltpu.VMEM((seq, B_DK), BF16),
                        pltpu.VMEM((seq, B_DK), BF16),
                        pltpu.VMEM((seq, B_DK), BF16),
                        pltpu.VMEM((seq // B_CHUNK, B_DK), F32)],
        compiler_params=_cparams("parallel", "parallel"),
        name="gla",
    )(proj_out, proj_out, proj_out, proj_out, proj_out, lr_w, lr_b.reshape(1, B_KWIDTH),
      og.reshape(1, B_DV))


def _gelu(x):
    c = math.sqrt(2.0 / math.pi)
    return 0.5 * x * (1.0 + jnp.tanh(c * (x + 0.044715 * (x * x * x))))


def _sgu_kernel(u_ref, v_ref, lg_ref, lb_ref, w_ref, b_ref, o_ref):
    rows = u_ref.shape[0]
    ti = lax.broadcasted_iota(jnp.int32, (C_CHUNK, C_CHUNK), 0)
    si = lax.broadcasted_iota(jnp.int32, (C_CHUNK, C_CHUNK), 1)
    causal = si <= ti
    ws = [jnp.where(causal, w_ref[g], 0.0).astype(BF16) for g in range(C_GROUPS)]
    for n in range(rows // C_CHUNK):
        r = slice(n * C_CHUNK, (n + 1) * C_CHUNK)
        v = _gelu(v_ref[r, :].astype(F32))
        mu = jnp.mean(v, axis=-1, keepdims=True)
        vc = v - mu
        vn = vc * lax.rsqrt(jnp.mean(vc * vc, axis=-1, keepdims=True) + EPS)
        vn = (vn * lg_ref[...] + lb_ref[...]).astype(BF16)
        u = _gelu(u_ref[r, :].astype(F32))
        for g in range(C_GROUPS):
            cols = slice(g * C_GROUP_DIM, (g + 1) * C_GROUP_DIM)
            mixed = jnp.dot(ws[g], vn[:, cols], preferred_element_type=F32) + b_ref[g]
            o_ref[r, cols] = (u[:, cols] * mixed).astype(o_ref.dtype)


def sgu(proj_out, ln_g, ln_b, w_s, b_s, *, tm):
    m = proj_out.shape[0]
    u0 = U_COL // C_WIDTH
    return pl.pallas_call(
        _sgu_kernel,
        grid=(m // tm,),
        in_specs=[pl.BlockSpec((tm, C_WIDTH), lambda i: (i, u0)),
                  pl.BlockSpec((tm, C_WIDTH), lambda i: (i, u0 + 1)),
                  pl.BlockSpec((1, C_WIDTH), lambda i: (0, 0)),
                  pl.BlockSpec((1, C_WIDTH), lambda i: (0, 0)),
                  pl.BlockSpec((C_GROUPS, C_CHUNK, C_CHUNK), lambda i: (0, 0, 0)),
                  pl.BlockSpec((C_GROUPS, C_CHUNK, 1), lambda i: (0, 0, 0))],
        out_specs=pl.BlockSpec((tm, C_WIDTH), lambda i: (i, 0)),
        out_shape=jax.ShapeDtypeStruct((m, C_WIDTH), BF16),
        compiler_params=_cparams("parallel"),
        name="sgu",
    )(proj_out, proj_out, ln_g.reshape(1, C_WIDTH), ln_b.reshape(1, C_WIDTH), w_s,
      b_s.reshape(C_GROUPS, C_CHUNK, 1))


def _merge_kernel(ya_ref, yb_ref, yc_ref, ga_ref, gb_ref, gc_ref, x_ref,
                  wa_ref, wb_ref, wc_ref, wo_ref, g2_ref, o_ref, on_ref):
    def gated(g_ref, y_ref, w_ref):
        return (jax.nn.sigmoid(g_ref[...].astype(F32))
                * jnp.dot(y_ref[...], w_ref[...], preferred_element_type=F32))

    merged = gated(ga_ref, ya_ref, wa_ref) + gated(gb_ref, yb_ref, wb_ref) + gated(gc_ref, yc_ref, wc_ref)
    h = x_ref[...] + jnp.dot(merged.astype(BF16), wo_ref[...], preferred_element_type=F32)
    o_ref[...] = h
    ms = jnp.mean(h * h, axis=-1, keepdims=True)
    on_ref[...] = (h * lax.rsqrt(ms + EPS) * g2_ref[...]).astype(on_ref.dtype)


def merge(ya, yb, yc, proj_out, x, wa, wb, wc, wo, g2, layer, *, tm):
    m, d = x.shape
    resident = functools.partial(pl.BlockSpec, pipeline_mode=pl.Buffered(1))
    return pl.pallas_call(
        _merge_kernel,
        grid=(m // tm,),
        in_specs=[pl.BlockSpec((tm, A_WIDTH), lambda i: (i, 0)),
                  pl.BlockSpec((tm, B_VWIDTH), lambda i: (i, 0)),
                  pl.BlockSpec((tm, C_WIDTH), lambda i: (i, 0)),
                  pl.BlockSpec((tm, d), lambda i: (i, 0)),
                  pl.BlockSpec((tm, d), lambda i: (i, 1)),
                  pl.BlockSpec((tm, d), lambda i: (i, 2)),
                  pl.BlockSpec((tm, d), lambda i: (i, 0)),
                  resident((None, A_WIDTH, d), lambda i: (layer, 0, 0)),
                  resident((None, B_VWIDTH, d), lambda i: (layer, 0, 0)),
                  resident((None, C_WIDTH, d), lambda i: (layer, 0, 0)),
                  resident((None, d, d), lambda i: (layer, 0, 0)),
                  pl.BlockSpec((1, d), lambda i: (0, 0))],
        out_specs=[pl.BlockSpec((tm, d), lambda i: (i, 0)),
                   pl.BlockSpec((tm, d), lambda i: (i, 0))],
        out_shape=[jax.ShapeDtypeStruct((m, d), F32),
                   jax.ShapeDtypeStruct((m, d), BF16)],
        compiler_params=_cparams("parallel"),
        name="merge",
    )(ya, yb, yc, proj_out, proj_out, proj_out, x, wa, wb, wc, wo, g2.reshape(1, d))


def _mlp_kernel(h_ref, hn_ref, w1_ref, w2_ref, o_ref):
    def step(first):
        a = jnp.dot(hn_ref[...], w1_ref[...].astype(BF16), preferred_element_type=F32)
        a = jnp.square(jnp.maximum(a, 0.0)).astype(BF16)
        upd = jnp.dot(a, w2_ref[...].astype(BF16), preferred_element_type=F32)
        o_ref[...] = (h_ref[...] if first else o_ref[...]) + upd

    pl.when(pl.program_id(1) == 0)(functools.partial(step, True))
    pl.when(pl.program_id(1) > 0)(functools.partial(step, False))


def mlp(h, hn, w1, w2, layer, *, tm, tf):
    m, d = h.shape
    f = w1.shape[2]
    return pl.pallas_call(
        _mlp_kernel,
        grid=(m // tm, f // tf),
        in_specs=[pl.BlockSpec((tm, d), lambda i, j: (i, 0), pipeline_mode=pl.Buffered(1)),
                  pl.BlockSpec((tm, d), lambda i, j: (i, 0)),
                  pl.BlockSpec((None, d, tf), lambda i, j: (layer, 0, j)),
                  pl.BlockSpec((None, tf, d), lambda i, j: (layer, j, 0))],
        out_specs=pl.BlockSpec((tm, d), lambda i, j: (i, 0)),
        out_shape=jax.ShapeDtypeStruct((m, d), F32),
        compiler_params=_cparams("parallel", "arbitrary"),
        name="mlp",
    )(h, hn, w1, w2)


def kernel(x, rpe_table, norm1_g, w_in, q_norm_g, k_norm_g, gla_lr_w, gla_lr_b, gla_out_g,
           sg_ln_g, sg_ln_b, sg_w, sg_b, w_br_a, w_br_b, w_br_c, w_o, norm2_g, w_ff1, w_ff2):
    bsz, seq, d = x.shape
    assert seq % A_BLOCK == 0 and seq % B_CHUNK == 0 and seq % C_CHUNK == 0
    assert w_in.shape[2] == AB_WIDTH + TAIL_WIDTH
    h = x.reshape(bsz * seq, d)
    bias = rpe_bias(rpe_table)
    w_ab = cast_ab(w_in)
    w_tail = shift_cast_tail(w_in)
    wa, wb, wc, wo = (w.astype(BF16) for w in (w_br_a, w_br_b, w_br_c, w_o))
    lr_w = jnp.pad(gla_lr_w, ((0, 0), (LR_LANE, LANES - LR_LANE - B_LOWRANK), (0, 0)))
    for l in range(DEPTH):
        p = proj(h, norm1_g[l], w_ab, w_tail, l, tm=1024)
        y_a = moba(p, q_norm_g[l], k_norm_g[l], bias, bsz=bsz, seq=seq)
        y_b = gla(p, lr_w, gla_lr_b[l], gla_out_g[l], l, bsz=bsz, seq=seq)
        y_c = sgu(p, sg_ln_g[l], sg_ln_b[l], sg_w[l], sg_b[l], tm=1024)
        h, hn = merge(y_a, y_b, y_c, p, h, wa, wb, wc, wo, norm2_g[l], l, tm=256)
        h = mlp(h, hn, w_ff1, w_ff2, l, tm=1024, tf=512)
    return h.reshape(bsz, seq, d)
```

```python
import functools
import math

import jax
import jax.numpy as jnp
from jax import lax
from jax.experimental import pallas as pl
from jax.experimental.pallas import tpu as pltpu

D_MODEL = 2048
DEPTH = 2
HEAD_DIM = 128
A_HEADS = 4
A_WIDTH = A_HEADS * HEAD_DIM
A_BLOCK = 256
A_TOPK = 3
RPE_BUCKETS = 32
RPE_MAX_DIST = 128
B_HEADS = 4
B_DK = 128
B_DV = 256
B_KWIDTH = B_HEADS * B_DK
B_VWIDTH = B_HEADS * B_DV
B_LOWRANK = 16
B_GATE_NORM = 16.0
B_CHUNK = 64
C_GROUPS = 4
C_GROUP_DIM = 128
C_WIDTH = C_GROUPS * C_GROUP_DIM
C_CHUNK = 128
D_FF = 4 * D_MODEL
EPS = 1e-6
NEG_INF = -1e30
LOG2E = 1.4426950408889634

LANES = 128
BF16_ROWS = 16
VMEM_LIMIT = 56 * 1024 * 1024

F32 = jnp.float32
BF16 = jnp.bfloat16

PROJ_TN = 1536
AB_WIDTH = 3 * A_WIDTH + 2 * B_KWIDTH + 2 * B_VWIDTH
G_WIDTH = 3 * D_MODEL
IN_WIDTH = AB_WIDTH + B_LOWRANK + 2 * C_WIDTH + G_WIDTH
TAIL_PAD = PROJ_TN - (B_LOWRANK + 2 * C_WIDTH)
AB0 = G_WIDTH
T0 = AB0 + AB_WIDTH
LR_COL = T0 + TAIL_PAD
U_COL = LR_COL + B_LOWRANK
PROJ_WIDTH = T0 + PROJ_TN
N_AB_TILES = AB_WIDTH // PROJ_TN
N_G_TILES = G_WIDTH // PROJ_TN
C_SRC = AB_WIDTH - TAIL_PAD
G_SRC = IN_WIDTH - G_WIDTH
assert AB_WIDTH % PROJ_TN == 0 and G_WIDTH % PROJ_TN == 0 and U_COL % C_WIDTH == 0
assert C_SRC % BF16_ROWS == 0 and G_SRC % BF16_ROWS == 0
LR_LANE = LR_COL % LANES


def _rpe_thresholds():
    max_exact = RPE_BUCKETS // 2

    def bucket(n):
        if n < max_exact:
            return n
        v = math.log(n / max_exact) / math.log(RPE_MAX_DIST / max_exact) * (RPE_BUCKETS - max_exact)
        return min(max_exact + int(v), RPE_BUCKETS - 1)

    table = [bucket(n) for n in range(4 * RPE_MAX_DIST)]
    return [min(n for n in range(len(table)) if table[n] >= b)
            for b in range(max_exact + 1, RPE_BUCKETS)]


RPE_THRESHOLDS = _rpe_thresholds()
assert RPE_THRESHOLDS[-1] <= A_BLOCK


def _cparams(*sem):
    return pltpu.CompilerParams(dimension_semantics=sem, vmem_limit_bytes=VMEM_LIMIT)


def _rpe_bias_kernel(tbl_ref, bias_ref):
    h = pl.program_id(0)
    t = lax.broadcasted_iota(jnp.int32, (A_BLOCK, 2 * A_BLOCK), 0)
    c = lax.broadcasted_iota(jnp.int32, (A_BLOCK, 2 * A_BLOCK), 1)
    n = jnp.maximum(t - c + A_BLOCK, 0)
    max_exact = RPE_BUCKETS // 2
    large = jnp.full(n.shape, max_exact, jnp.int32)
    for thr in RPE_THRESHOLDS:
        large = large + (n >= thr).astype(jnp.int32)
    bucket = jnp.where(n < max_exact, n, large)
    bias = jnp.zeros(n.shape, F32)
    for b in range(RPE_BUCKETS):
        bias = jnp.where(bucket == b, tbl_ref[b, h], bias)
    bias = (bias - tbl_ref[RPE_BUCKETS - 1, h]) * LOG2E
    bias_ref[0] = jnp.where(c - A_BLOCK > t, NEG_INF, bias)


def rpe_bias(rpe_table):
    return pl.pallas_call(
        _rpe_bias_kernel,
        grid=(A_HEADS,),
        in_specs=[pl.BlockSpec(memory_space=pltpu.SMEM)],
        out_specs=pl.BlockSpec((1, A_BLOCK, 2 * A_BLOCK), lambda h: (h, 0, 0)),
        out_shape=jax.ShapeDtypeStruct((A_HEADS, A_BLOCK, 2 * A_BLOCK), F32),
        compiler_params=_cparams("arbitrary"),
        name="rpe_bias",
    )(rpe_table)


def _proj_kernel(x_ref, g_ref, w_ref, o_ref, xn_ref, *, row_chunk):
    @pl.when(pl.program_id(1) == 0)
    def _():
        def body(r, carry):
            rows = pl.ds(pl.multiple_of(r * row_chunk, row_chunk), row_chunk)
            x = x_ref[rows, :]
            ms = jnp.mean(x * x, axis=-1, keepdims=True)
            xn_ref[rows, :] = (x * lax.rsqrt(ms + EPS) * g_ref[...]).astype(BF16)
            return carry
        lax.fori_loop(0, x_ref.shape[0] // row_chunk, body, 0)

    o_ref[...] = lax.dot_general(xn_ref[...], w_ref[...].astype(BF16), (((1,), (1,)), ((), ())),
                                 preferred_element_type=F32).astype(o_ref.dtype)


def _proj_src_row(j):
    u = BF16_ROWS
    return u * jnp.where(j < N_AB_TILES, j * (PROJ_TN // u),
                         jnp.where(j == N_AB_TILES, C_SRC // u,
                                   G_SRC // u + (j - N_AB_TILES - 1) * (PROJ_TN // u)))


def _proj_out_tile(j):
    return jnp.where(j < N_AB_TILES, j + N_G_TILES,
                     jnp.where(j == N_AB_TILES, N_G_TILES + N_AB_TILES, j - N_AB_TILES - 1))


def proj(x, g, w_t, layer, *, tm):
    m, k = x.shape
    tn = PROJ_TN
    n_tiles = N_AB_TILES + 1 + N_G_TILES
    return pl.pallas_call(
        functools.partial(_proj_kernel, row_chunk=128),
        grid=(m // tm, n_tiles),
        in_specs=[pl.BlockSpec((tm, k), lambda i, j: (i, 0), pipeline_mode=pl.Buffered(1)),
                  pl.BlockSpec((1, k), lambda i, j: (0, 0)),
                  pl.BlockSpec((None, pl.Element(tn), pl.Element(k)),
                               lambda i, j: (layer, _proj_src_row(j), 0))],
        out_specs=pl.BlockSpec((tm, tn), lambda i, j: (i, _proj_out_tile(j))),
        out_shape=jax.ShapeDtypeStruct((m, PROJ_WIDTH), BF16),
        scratch_shapes=[pltpu.VMEM((tm, k), BF16)],
        compiler_params=_cparams("parallel", "arbitrary"),
        name="proj",
    )(x, g.reshape(1, k), w_t)


def _head_rms(x, g):
    ms = jnp.mean(x * x, axis=-1, keepdims=True)
    return x * lax.rsqrt(ms + EPS) * g


def _moba_kernel(q_ref, k_ref, v_ref, qg_ref, kg_ref, bias_ref, o_ref,
                 qn_ref, qa_ref, ka_ref, va_ref, kmean_ref, s_ref, p_ref):
    seq, hd = q_ref.shape
    nb = seq // A_BLOCK
    rc = BF16_ROWS
    qn = _head_rms(q_ref[...].astype(F32), qg_ref[...])
    qn_ref[...] = qn
    qa_ref[:, :hd] = (qn * (hd ** -0.5 * LOG2E)).astype(BF16)
    qa_ref[:, hd:] = jnp.zeros((seq, hd), BF16)
    kn = _head_rms(k_ref[...].astype(F32), kg_ref[...])
    ka_ref[:, :hd] = kn.astype(BF16)
    key_blk = lax.broadcasted_iota(jnp.int32, (seq, hd), 0) // A_BLOCK
    ka_ref[:, hd:] = jnp.where(key_blk == lax.broadcasted_iota(jnp.int32, (seq, hd), 1), 1.0, 0.0).astype(BF16)
    va_ref[:, :hd] = v_ref[...]
    va_ref[:, hd:] = jnp.ones((seq, hd), BF16)
    for j in range(nb):
        kmean_ref[j:j + 1, :] = jnp.mean(kn[j * A_BLOCK:(j + 1) * A_BLOCK], axis=0, keepdims=True)
    kmean = kmean_ref[...]
    blk = lax.broadcasted_iota(jnp.int32, (nb, A_BLOCK), 0)
    eye = jnp.where(lax.broadcasted_iota(jnp.int32, (nb, hd), 0)
                    == lax.broadcasted_iota(jnp.int32, (nb, hd), 1), 1.0, 0.0)

    for i in range(nb):
        rows = slice(i * A_BLOCK, (i + 1) * A_BLOCK)
        nk = (i + 1) * A_BLOCK
        if i > A_TOPK:
            g = lax.dot_general(kmean, qn_ref[rows, :], (((1,), (1,)), ((), ())),
                                precision=lax.Precision.HIGHEST, preferred_element_type=F32)
            rank = jnp.zeros((nb, A_BLOCK), jnp.int32)
            for jp in range(i):
                gj = g[jp:jp + 1, :]
                beats = (gj > g) | ((gj == g) & (jp < blk))
                rank = rank + beats.astype(jnp.int32)
            neg = jnp.where((blk < i) & (rank >= A_TOPK), NEG_INF, 0.0)
            qa_ref[rows, hd:] = lax.dot_general(neg, eye, (((0,), (0,)), ((), ())),
                                                preferred_element_type=F32).astype(BF16)
        s_ref[:, :nk] = lax.dot_general(qa_ref[rows, :], ka_ref[:nk, :], (((1,), (1,)), ((), ())),
                                        preferred_element_type=F32)

        def chunk(c, carry, i=i):
            r = pl.ds(pl.multiple_of(c * rc, rc), rc)
            tiles = []
            for j in range(i + 1):
                sj = s_ref[r, j * A_BLOCK:(j + 1) * A_BLOCK]
                if j == i:
                    sj = sj + bias_ref[0, r, A_BLOCK:]
                elif j == i - 1:
                    sj = sj + bias_ref[0, r, :A_BLOCK]
                tiles.append(sj)
            mt = tiles[0]
            for t in tiles[1:]:
                mt = jnp.maximum(mt, t)
            m = jnp.max(mt, axis=-1, keepdims=True)
            for j, t in enumerate(tiles):
                p_ref[r, j * A_BLOCK:(j + 1) * A_BLOCK] = jnp.exp2(t - m).astype(BF16)
            return carry

        lax.fori_loop(0, A_BLOCK // rc, chunk, 0, unroll=True)
        o = jnp.dot(p_ref[:, :nk], va_ref[:nk, :], preferred_element_type=F32)
        o_ref[rows, :] = (o[:, :hd] / o[:, hd:]).astype(o_ref.dtype)


def moba(proj_out, qg, kg, bias, *, bsz, seq):
    hd = HEAD_DIM
    q0 = AB0 // hd
    return pl.pallas_call(
        _moba_kernel,
        grid=(bsz, A_HEADS),
        in_specs=[pl.BlockSpec((seq, hd), lambda b, h: (b, q0 + h)),
                  pl.BlockSpec((seq, hd), lambda b, h: (b, q0 + A_HEADS + h)),
                  pl.BlockSpec((seq, hd), lambda b, h: (b, q0 + 2 * A_HEADS + h)),
                  pl.BlockSpec((1, hd), lambda b, h: (0, 0)),
                  pl.BlockSpec((1, hd), lambda b, h: (0, 0)),
                  pl.BlockSpec((1, A_BLOCK, 2 * A_BLOCK), lambda b, h: (h, 0, 0))],
        out_specs=pl.BlockSpec((seq, hd), lambda b, h: (b, h)),
        out_shape=jax.ShapeDtypeStruct((bsz * seq, A_WIDTH), BF16),
        scratch_shapes=[pltpu.VMEM((seq, hd), F32),
                        pltpu.VMEM((seq, 2 * hd), BF16),
                        pltpu.VMEM((seq, 2 * hd), BF16),
                        pltpu.VMEM((seq, 2 * hd), BF16),
                        pltpu.VMEM((seq // A_BLOCK, hd), F32),
                        pltpu.VMEM((A_BLOCK, seq), F32),
                        pltpu.VMEM((A_BLOCK, seq), BF16)],
        compiler_params=_cparams("parallel", "parallel"),
        name="moba",
    )(proj_out, proj_out, proj_out, qg.reshape(1, hd), kg.reshape(1, hd), bias)


def _gla_kernel(q_ref, k_ref, v_ref, r_ref, lr_ref, lrw_ref, lrb_ref, og_ref, o_ref,
                qd_ref, ki_ref, ke_ref, dec_ref):
    seq = q_ref.shape[0]
    ck = B_CHUNK
    grp = 4 * ck
    qscale = B_DK ** -0.5

    ti = lax.broadcasted_iota(jnp.int32, (grp, grp), 0)
    si = lax.broadcasted_iota(jnp.int32, (grp, grp), 1)
    tril_grp = jnp.where((si <= ti) & (si // ck == ti // ck), 1.0, 0.0).astype(BF16)
    lrw = lrw_ref[...].astype(BF16)
    for g in range(seq // grp):
        rows = slice(g * grp, (g + 1) * grp)
        z = jnp.dot(lr_ref[rows, :], lrw, preferred_element_type=F32) + lrb_ref[...]
        la = -(jnp.maximum(-z, 0.0) + jnp.log1p(jnp.exp(-jnp.abs(z)))) / B_GATE_NORM
        hi = la.astype(BF16)
        r1 = la - hi.astype(F32)
        mid = r1.astype(BF16)
        lo = (r1 - mid.astype(F32)).astype(BF16)
        parts = jnp.dot(tril_grp, jnp.concatenate([hi, mid, lo], axis=1), preferred_element_type=F32)
        bc = parts[:, :B_DK] + parts[:, B_DK:2 * B_DK] + parts[:, 2 * B_DK:]
        bc3 = bc.reshape(grp // ck, ck, B_DK)
        bl3 = bc3[:, ck - 1:ck, :]
        q = q_ref[rows, :].astype(F32)
        k3 = k_ref[rows, :].astype(F32).reshape(grp // ck, ck, B_DK)
        qd_ref[rows, :] = (q * jnp.exp(bc) * qscale).astype(BF16)
        ki_ref[rows, :] = (k3 * jnp.exp(-bc3)).reshape(grp, B_DK).astype(BF16)
        ke_ref[rows, :] = (k3 * jnp.exp(bl3 - bc3)).reshape(grp, B_DK).astype(BF16)
        dec_ref[g * (grp // ck):(g + 1) * (grp // ck), :] = jnp.exp(bl3).reshape(grp // ck, B_DK)

    tc = lax.broadcasted_iota(jnp.int32, (ck, ck), 0)
    sc = lax.broadcasted_iota(jnp.int32, (ck, ck), 1)
    causal = sc <= tc

    def body(c, st):
        rows = pl.ds(pl.multiple_of(c * ck, ck), ck)
        qd = qd_ref[rows, :]
        v = v_ref[rows, :]
        a = lax.dot_general(qd, ki_ref[rows, :], (((1,), (1,)), ((), ())), preferred_element_type=F32)
        a = jnp.where(causal, a, 0.0).astype(BF16)
        o = (jnp.dot(a, v, preferred_element_type=F32)
             + lax.dot_general(qd, st.astype(BF16), (((1,), (1,)), ((), ())),
                               preferred_element_type=F32))
        st = st * dec_ref[pl.ds(c, 1), :] + lax.dot_general(
            v, ke_ref[rows, :], (((0,), (0,)), ((), ())), preferred_element_type=F32)
        ms = jnp.mean(o * o, axis=-1, keepdims=True)
        y = o * lax.rsqrt(ms + EPS) * og_ref[...]
        r = r_ref[rows, :].astype(F32)
        o_ref[rows, :] = (y * (r * jax.nn.sigmoid(r))).astype(o_ref.dtype)
        return st

    lax.fori_loop(0, seq // ck, body, jnp.zeros((B_DV, B_DK), F32), unroll=8)


def gla(proj_out, lr_w, lr_b, og, layer, *, bsz, seq):
    q0 = (AB0 + 3 * A_WIDTH) // B_DK
    k0 = q0 + B_HEADS
    v0 = (AB0 + 3 * A_WIDTH + 2 * B_KWIDTH) // B_DV
    r0 = v0 + B_HEADS
    lr0 = LR_COL // LANES
    return pl.pallas_call(
        _gla_kernel,
        grid=(bsz, B_HEADS),
        in_specs=[pl.BlockSpec((seq, B_DK), lambda b, h: (b, q0 + h)),
                  pl.BlockSpec((seq, B_DK), lambda b, h: (b, k0 + h)),
                  pl.BlockSpec((seq, B_DV), lambda b, h: (b, v0 + h)),
                  pl.BlockSpec((seq, B_DV), lambda b, h: (b, r0 + h)),
                  pl.BlockSpec((seq, LANES), lambda b, h: (b, lr0)),
                  pl.BlockSpec((None, LANES, B_DK), lambda b, h: (layer, 0, h)),
                  pl.BlockSpec((1, B_DK), lambda b, h: (0, h)),
                  pl.BlockSpec((1, B_DV), lambda b, h: (0, 0))],
        out_specs=pl.BlockSpec((seq, B_DV), lambda b, h: (b, h)),
        out_shape=jax.ShapeDtypeStruct((bsz * seq, B_VWIDTH), BF16),
        scratch_shapes=[pltpu.VMEM((seq, B_DK), BF16),
                        pltpu.VMEM((seq, B_DK), BF16),
                        pltpu.VMEM((seq, B_DK), BF16),
                        pltpu.VMEM((seq // B_CHUNK, B_DK), F32)],
        compiler_params=_cparams("parallel", "parallel"),
        name="gla",
    )(proj_out, proj_out, proj_out, proj_out, proj_out, lr_w, lr_b.reshape(1, B_KWIDTH),
      og.reshape(1, B_DV))


def _gelu(x):
    c = math.sqrt(2.0 / math.pi)
    return 0.5 * x * (1.0 + jnp.tanh(c * (x + 0.044715 * (x * x * x))))


def _sgu_kernel(u_ref, v_ref, lg_ref, lb_ref, w_ref, b_ref, o_ref):
    rows = u_ref.shape[0]
    ti = lax.broadcasted_iota(jnp.int32, (C_CHUNK, C_CHUNK), 0)
    si = lax.broadcasted_iota(jnp.int32, (C_CHUNK, C_CHUNK), 1)
    causal = si <= ti
    ws = [jnp.where(causal, w_ref[g], 0.0).astype(BF16) for g in range(C_GROUPS)]
    for n in range(rows // C_CHUNK):
        r = slice(n * C_CHUNK, (n + 1) * C_CHUNK)
        v = _gelu(v_ref[r, :].astype(F32))
        mu = jnp.mean(v, axis=-1, keepdims=True)
        vc = v - mu
        vn = vc * lax.rsqrt(jnp.mean(vc * vc, axis=-1, keepdims=True) + EPS)
        vn = (vn * lg_ref[...] + lb_ref[...]).astype(BF16)
        u = _gelu(u_ref[r, :].astype(F32))
        for g in range(C_GROUPS):
            cols = slice(g * C_GROUP_DIM, (g + 1) * C_GROUP_DIM)
            mixed = jnp.dot(ws[g], vn[:, cols], preferred_element_type=F32) + b_ref[g]
            o_ref[r, cols] = (u[:, cols] * mixed).astype(o_ref.dtype)


def sgu(proj_out, ln_g, ln_b, w_s, b_s, *, tm):
    m = proj_out.shape[0]
    u0 = U_COL // C_WIDTH
    return pl.pallas_call(
        _sgu_kernel,
        grid=(m // tm,),
        in_specs=[pl.BlockSpec((tm, C_WIDTH), lambda i: (i, u0)),
                  pl.BlockSpec((tm, C_WIDTH), lambda i: (i, u0 + 1)),
                  pl.BlockSpec((1, C_WIDTH), lambda i: (0, 0)),
                  pl.BlockSpec((1, C_WIDTH), lambda i: (0, 0)),
                  pl.BlockSpec((C_GROUPS, C_CHUNK, C_CHUNK), lambda i: (0, 0, 0)),
                  pl.BlockSpec((C_GROUPS, C_CHUNK, 1), lambda i: (0, 0, 0))],
        out_specs=pl.BlockSpec((tm, C_WIDTH), lambda i: (i, 0)),
        out_shape=jax.ShapeDtypeStruct((m, C_WIDTH), BF16),
        compiler_params=_cparams("parallel"),
        name="sgu",
    )(proj_out, proj_out, ln_g.reshape(1, C_WIDTH), ln_b.reshape(1, C_WIDTH), w_s,
      b_s.reshape(C_GROUPS, C_CHUNK, 1))


def _merge_kernel(ya_ref, yb_ref, yc_ref, ga_ref, gb_ref, gc_ref, x_ref,
                  wa_ref, wb_ref, wc_ref, wo_ref, g2_ref, o_ref, on_ref):
    def gated(g_ref, y_ref, w_ref):
        return (jax.nn.sigmoid(g_ref[...].astype(F32))
                * jnp.dot(y_ref[...], w_ref[...], preferred_element_type=F32))

    merged = gated(ga_ref, ya_ref, wa_ref) + gated(gb_ref, yb_ref, wb_ref) + gated(gc_ref, yc_ref, wc_ref)
    h = x_ref[...] + jnp.dot(merged.astype(BF16), wo_ref[...], preferred_element_type=F32)
    o_ref[...] = h
    ms = jnp.mean(h * h, axis=-1, keepdims=True)
    on_ref[...] = (h * lax.rsqrt(ms + EPS) * g2_ref[...]).astype(on_ref.dtype)


def merge(ya, yb, yc, proj_out, x, wa, wb, wc, wo, g2, layer, *, tm):
    m, d = x.shape
    resident = functools.partial(pl.BlockSpec, pipeline_mode=pl.Buffered(1))
    return pl.pallas_call(
        _merge_kernel,
        grid=(m // tm,),
        in_specs=[pl.BlockSpec((tm, A_WIDTH), lambda i: (i, 0)),
                  pl.BlockSpec((tm, B_VWIDTH), lambda i: (i, 0)),
                  pl.BlockSpec((tm, C_WIDTH), lambda i: (i, 0)),
                  pl.BlockSpec((tm, d), lambda i: (i, 0)),
                  pl.BlockSpec((tm, d), lambda i: (i, 1)),
                  pl.BlockSpec((tm, d), lambda i: (i, 2)),
                  pl.BlockSpec((tm, d), lambda i: (i, 0)),
                  resident((None, A_WIDTH, d), lambda i: (layer, 0, 0)),
                  resident((None, B_VWIDTH, d), lambda i: (layer, 0, 0)),
                  resident((None, C_WIDTH, d), lambda i: (layer, 0, 0)),
                  resident((None, d, d), lambda i: (layer, 0, 0)),
                  pl.BlockSpec((1, d), lambda i: (0, 0))],
        out_specs=[pl.BlockSpec((tm, d), lambda i: (i, 0)),
                   pl.BlockSpec((tm, d), lambda i: (i, 0))],
        out_shape=[jax.ShapeDtypeStruct((m, d), F32),
                   jax.ShapeDtypeStruct((m, d), BF16)],
        compiler_params=_cparams("parallel"),
        name="merge",
    )(ya, yb, yc, proj_out, proj_out, proj_out, x, wa, wb, wc, wo, g2.reshape(1, d))


def _mlp_kernel(h_ref, hn_ref, w1_ref, w2_ref, o_ref):
    def step(first):
        a = jnp.dot(hn_ref[...], w1_ref[...].astype(BF16), preferred_element_type=F32)
        a = jnp.square(jnp.maximum(a, 0.0)).astype(BF16)
        upd = jnp.dot(a, w2_ref[...].astype(BF16), preferred_element_type=F32)
        o_ref[...] = (h_ref[...] if first else o_ref[...]) + upd

    pl.when(pl.program_id(1) == 0)(functools.partial(step, True))
    pl.when(pl.program_id(1) > 0)(functools.partial(step, False))


def mlp(h, hn, w1, w2, layer, *, tm, tf):
    m, d = h.shape
    f = w1.shape[2]
    return pl.pallas_call(
        _mlp_kernel,
        grid=(m // tm, f // tf),
        in_specs=[pl.BlockSpec((tm, d), lambda i, j: (i, 0), pipeline_mode=pl.Buffered(1)),
                  pl.BlockSpec((tm, d), lambda i, j: (i, 0)),
                  pl.BlockSpec((None, d, tf), lambda i, j: (layer, 0, j)),
                  pl.BlockSpec((None, tf, d), lambda i, j: (layer, j, 0))],
        out_specs=pl.BlockSpec((tm, d), lambda i, j: (i, 0)),
        out_shape=jax.ShapeDtypeStruct((m, d), F32),
        compiler_params=_cparams("parallel", "arbitrary"),
        name="mlp",
    )(h, hn, w1, w2)


def kernel(x, rpe_table, norm1_g, w_in, q_norm_g, k_norm_g, gla_lr_w, gla_lr_b, gla_out_g,
           sg_ln_g, sg_ln_b, sg_w, sg_b, w_br_a, w_br_b, w_br_c, w_o, norm2_g, w_ff1, w_ff2):
    bsz, seq, d = x.shape
    assert seq % A_BLOCK == 0 and seq % B_CHUNK == 0 and seq % C_CHUNK == 0
    assert w_in.shape[2] == IN_WIDTH
    h = x.reshape(bsz * seq, d)
    bias = rpe_bias(rpe_table)
    w_t = jnp.swapaxes(w_in, 1, 2)
    wa, wb, wc, wo = (w.astype(BF16) for w in (w_br_a, w_br_b, w_br_c, w_o))
    lr_w = jnp.pad(gla_lr_w, ((0, 0), (LR_LANE, LANES - LR_LANE - B_LOWRANK), (0, 0)))
    for l in range(DEPTH):
        p = proj(h, norm1_g[l], w_t, l, tm=1024)
        y_a = moba(p, q_norm_g[l], k_norm_g[l], bias, bsz=bsz, seq=seq)
        y_b = gla(p, lr_w, gla_lr_b[l], gla_out_g[l], l, bsz=bsz, seq=seq)
        y_c = sgu(p, sg_ln_g[l], sg_ln_b[l], sg_w[l], sg_b[l], tm=1024)
        h, hn = merge(y_a, y_b, y_c, p, h, wa, wb, wc, wo, norm2_g[l], l, tm=256)
        h = mlp(h, hn, w_ff1, w_ff2, l, tm=1024, tf=512)
    return h.reshape(bsz, seq, d)
```

```python
import functools
import math

import jax
import jax.numpy as jnp
from jax import lax
from jax.experimental import pallas as pl
from jax.experimental.pallas import tpu as pltpu

D_MODEL = 2048
DEPTH = 2
HEAD_DIM = 128
A_HEADS = 4
A_WIDTH = A_HEADS * HEAD_DIM
A_BLOCK = 256
A_TOPK = 3
RPE_BUCKETS = 32
RPE_MAX_DIST = 128
B_HEADS = 4
B_DK = 128
B_DV = 256
B_KWIDTH = B_HEADS * B_DK
B_VWIDTH = B_HEADS * B_DV
B_LOWRANK = 16
B_GATE_NORM = 16.0
B_CHUNK = 64
C_GROUPS = 4
C_GROUP_DIM = 128
C_WIDTH = C_GROUPS * C_GROUP_DIM
C_CHUNK = 128
D_FF = 4 * D_MODEL
EPS = 1e-6
NEG_INF = -1e30
LOG2E = 1.4426950408889634

LANES = 128
BF16_ROWS = 16
VMEM_LIMIT = 56 * 1024 * 1024

F32 = jnp.float32
BF16 = jnp.bfloat16

PROJ_TN = 1536
AB_WIDTH = 3 * A_WIDTH + 2 * B_KWIDTH + 2 * B_VWIDTH
G_WIDTH = 3 * D_MODEL
IN_WIDTH = AB_WIDTH + B_LOWRANK + 2 * C_WIDTH + G_WIDTH
TAIL_PAD = PROJ_TN - (B_LOWRANK + 2 * C_WIDTH)
AB0 = G_WIDTH
T0 = AB0 + AB_WIDTH
LR_COL = T0 + TAIL_PAD
U_COL = LR_COL + B_LOWRANK
PROJ_WIDTH = T0 + PROJ_TN
N_AB_TILES = AB_WIDTH // PROJ_TN
N_G_TILES = G_WIDTH // PROJ_TN
C_SRC = AB_WIDTH - TAIL_PAD
G_SRC = IN_WIDTH - G_WIDTH
assert AB_WIDTH % PROJ_TN == 0 and G_WIDTH % PROJ_TN == 0 and U_COL % C_WIDTH == 0
assert C_SRC % BF16_ROWS == 0 and G_SRC % BF16_ROWS == 0
LR_LANE = LR_COL % LANES


def _rpe_thresholds():
    max_exact = RPE_BUCKETS // 2

    def bucket(n):
        if n < max_exact:
            return n
        v = math.log(n / max_exact) / math.log(RPE_MAX_DIST / max_exact) * (RPE_BUCKETS - max_exact)
        return min(max_exact + int(v), RPE_BUCKETS - 1)

    table = [bucket(n) for n in range(4 * RPE_MAX_DIST)]
    return [min(n for n in range(len(table)) if table[n] >= b)
            for b in range(max_exact + 1, RPE_BUCKETS)]


RPE_THRESHOLDS = _rpe_thresholds()
assert RPE_THRESHOLDS[-1] <= A_BLOCK


def _cparams(*sem):
    return pltpu.CompilerParams(dimension_semantics=sem, vmem_limit_bytes=VMEM_LIMIT)


def _rpe_bias_kernel(tbl_ref, bias_ref):
    h = pl.program_id(0)
    t = lax.broadcasted_iota(jnp.int32, (A_BLOCK, 2 * A_BLOCK), 0)
    c = lax.broadcasted_iota(jnp.int32, (A_BLOCK, 2 * A_BLOCK), 1)
    n = jnp.maximum(t - c + A_BLOCK, 0)
    max_exact = RPE_BUCKETS // 2
    large = jnp.full(n.shape, max_exact, jnp.int32)
    for thr in RPE_THRESHOLDS:
        large = large + (n >= thr).astype(jnp.int32)
    bucket = jnp.where(n < max_exact, n, large)
    bias = jnp.zeros(n.shape, F32)
    for b in range(RPE_BUCKETS):
        bias = jnp.where(bucket == b, tbl_ref[b, h], bias)
    bias = (bias - tbl_ref[RPE_BUCKETS - 1, h]) * LOG2E
    bias_ref[0] = jnp.where(c - A_BLOCK > t, NEG_INF, bias)


def rpe_bias(rpe_table):
    return pl.pallas_call(
        _rpe_bias_kernel,
        grid=(A_HEADS,),
        in_specs=[pl.BlockSpec(memory_space=pltpu.SMEM)],
        out_specs=pl.BlockSpec((1, A_BLOCK, 2 * A_BLOCK), lambda h: (h, 0, 0)),
        out_shape=jax.ShapeDtypeStruct((A_HEADS, A_BLOCK, 2 * A_BLOCK), F32),
        compiler_params=_cparams("arbitrary"),
        name="rpe_bias",
    )(rpe_table)


def _proj_kernel(x_ref, g_ref, w_ref, o_ref, xn_ref, *, row_chunk):
    @pl.when(pl.program_id(1) == 0)
    def _():
        def body(r, carry):
            rows = pl.ds(pl.multiple_of(r * row_chunk, row_chunk), row_chunk)
            x = x_ref[rows, :]
            ms = jnp.mean(x * x, axis=-1, keepdims=True)
            xn_ref[rows, :] = (x * lax.rsqrt(ms + EPS) * g_ref[...]).astype(BF16)
            return carry
        lax.fori_loop(0, x_ref.shape[0] // row_chunk, body, 0)

    o_ref[...] = lax.dot_general(xn_ref[...], w_ref[...].astype(BF16), (((1,), (1,)), ((), ())),
                                 preferred_element_type=F32).astype(o_ref.dtype)


def _proj_src_row(j):
    u = BF16_ROWS
    return u * jnp.where(j < N_AB_TILES, j * (PROJ_TN // u),
                         jnp.where(j == N_AB_TILES, C_SRC // u,
                                   G_SRC // u + (j - N_AB_TILES - 1) * (PROJ_TN // u)))


def _proj_out_tile(j):
    return jnp.where(j < N_AB_TILES, j + N_G_TILES,
                     jnp.where(j == N_AB_TILES, N_G_TILES + N_AB_TILES, j - N_AB_TILES - 1))


def proj(x, g, w_t, layer, *, tm):
    m, k = x.shape
    tn = PROJ_TN
    n_tiles = N_AB_TILES + 1 + N_G_TILES
    return pl.pallas_call(
        functools.partial(_proj_kernel, row_chunk=128),
        grid=(m // tm, n_tiles),
        in_specs=[pl.BlockSpec((tm, k), lambda i, j: (i, 0), pipeline_mode=pl.Buffered(1)),
                  pl.BlockSpec((1, k), lambda i, j: (0, 0)),
                  pl.BlockSpec((None, pl.Element(tn), pl.Element(k)),
                               lambda i, j: (layer, _proj_src_row(j), 0))],
        out_specs=pl.BlockSpec((tm, tn), lambda i, j: (i, _proj_out_tile(j))),
        out_shape=jax.ShapeDtypeStruct((m, PROJ_WIDTH), BF16),
        scratch_shapes=[pltpu.VMEM((tm, k), BF16)],
        compiler_params=_cparams("parallel", "arbitrary"),
        name="proj",
    )(x, g.reshape(1, k), w_t)


def _head_rms(x, g):
    ms = jnp.mean(x * x, axis=-1, keepdims=True)
    return x * lax.rsqrt(ms + EPS) * g


def _moba_kernel(q_ref, k_ref, v_ref, qg_ref, kg_ref, bias_ref, o_ref,
                 qn_ref, qa_ref, ka_ref, va_ref, kmean_ref, s_ref, p_ref):
    seq, hd = q_ref.shape
    nb = seq // A_BLOCK
    rc = BF16_ROWS
    qn = _head_rms(q_ref[...].astype(F32), qg_ref[...])
    qn_ref[...] = qn
    qa_ref[:, :hd] = (qn * (hd ** -0.5 * LOG2E)).astype(BF16)
    qa_ref[:, hd:] = jnp.zeros((seq, hd), BF16)
    kn = _head_rms(k_ref[...].astype(F32), kg_ref[...])
    ka_ref[:, :hd] = kn.astype(BF16)
    key_blk = lax.broadcasted_iota(jnp.int32, (seq, hd), 0) // A_BLOCK
    ka_ref[:, hd:] = jnp.where(key_blk == lax.broadcasted_iota(jnp.int32, (seq, hd), 1), 1.0, 0.0).astype(BF16)
    va_ref[:, :hd] = v_ref[...]
    va_ref[:, hd:] = jnp.ones((seq, hd), BF16)
    for j in range(nb):
        kmean_ref[j:j + 1, :] = jnp.mean(kn[j * A_BLOCK:(j + 1) * A_BLOCK], axis=0, keepdims=True)
    kmean = kmean_ref[...]
    blk = lax.broadcasted_iota(jnp.int32, (nb, A_BLOCK), 0)
    eye = jnp.where(lax.broadcasted_iota(jnp.int32, (nb, hd), 0)
                    == lax.broadcasted_iota(jnp.int32, (nb, hd), 1), 1.0, 0.0)

    def scores(i):
        rows = slice(i * A_BLOCK, (i + 1) * A_BLOCK)
        nk = (i + 1) * A_BLOCK
        if i > A_TOPK:
            g = lax.dot_general(kmean, qn_ref[rows, :], (((1,), (1,)), ((), ())),
                                precision=lax.Precision.HIGHEST, preferred_element_type=F32)
            rank = jnp.zeros((nb, A_BLOCK), jnp.int32)
            for jp in range(i):
                gj = g[jp:jp + 1, :]
                beats = (gj > g) | ((gj == g) & (jp < blk))
                rank = rank + beats.astype(jnp.int32)
            neg = jnp.where((blk < i) & (rank >= A_TOPK), NEG_INF, 0.0)
            qa_ref[rows, hd:] = lax.dot_general(neg, eye, (((0,), (0,)), ((), ())),
                                                preferred_element_type=F32).astype(BF16)
        s_ref[i % 2, :, :nk] = lax.dot_general(qa_ref[rows, :], ka_ref[:nk, :], (((1,), (1,)), ((), ())),
                                               preferred_element_type=F32)

    def softmax(i):
        for c in range(A_BLOCK // rc):
            r = slice(c * rc, (c + 1) * rc)
            tiles = []
            for j in range(i + 1):
                sj = s_ref[i % 2, r, j * A_BLOCK:(j + 1) * A_BLOCK]
                if j == i:
                    sj = sj + bias_ref[0, r, A_BLOCK:]
                elif j == i - 1:
                    sj = sj + bias_ref[0, r, :A_BLOCK]
                tiles.append(sj)
            mt = tiles[0]
            for t in tiles[1:]:
                mt = jnp.maximum(mt, t)
            m = jnp.max(mt, axis=-1, keepdims=True)
            for j, t in enumerate(tiles):
                p_ref[i % 2, r, j * A_BLOCK:(j + 1) * A_BLOCK] = jnp.exp2(t - m).astype(BF16)

    def weighted_sum(i):
        rows = slice(i * A_BLOCK, (i + 1) * A_BLOCK)
        nk = (i + 1) * A_BLOCK
        o = jnp.dot(p_ref[i % 2, :, :nk], va_ref[:nk, :], preferred_element_type=F32)
        o_ref[rows, :] = (o[:, :hd] / o[:, hd:]).astype(o_ref.dtype)

    scores(0)
    for i in range(nb):
        if i + 1 < nb:
            scores(i + 1)
        softmax(i)
        weighted_sum(i)


def moba(proj_out, qg, kg, bias, *, bsz, seq):
    hd = HEAD_DIM
    q0 = AB0 // hd
    return pl.pallas_call(
        _moba_kernel,
        grid=(bsz, A_HEADS),
        in_specs=[pl.BlockSpec((seq, hd), lambda b, h: (b, q0 + h)),
                  pl.BlockSpec((seq, hd), lambda b, h: (b, q0 + A_HEADS + h)),
                  pl.BlockSpec((seq, hd), lambda b, h: (b, q0 + 2 * A_HEADS + h)),
                  pl.BlockSpec((1, hd), lambda b, h: (0, 0)),
                  pl.BlockSpec((1, hd), lambda b, h: (0, 0)),
                  pl.BlockSpec((1, A_BLOCK, 2 * A_BLOCK), lambda b, h: (h, 0, 0))],
        out_specs=pl.BlockSpec((seq, hd), lambda b, h: (b, h)),
        out_shape=jax.ShapeDtypeStruct((bsz * seq, A_WIDTH), BF16),
        scratch_shapes=[pltpu.VMEM((seq, hd), F32),
                        pltpu.VMEM((seq, 2 * hd), BF16),
                        pltpu.VMEM((seq, 2 * hd), BF16),
                        pltpu.VMEM((seq, 2 * hd), BF16),
                        pltpu.VMEM((seq // A_BLOCK, hd), F32),
                        pltpu.VMEM((2, A_BLOCK, seq), F32),
                        pltpu.VMEM((2, A_BLOCK, seq), BF16)],
        compiler_params=_cparams("parallel", "parallel"),
        name="moba",
    )(proj_out, proj_out, proj_out, qg.reshape(1, hd), kg.reshape(1, hd), bias)


def _gla_kernel(q_ref, k_ref, v_ref, r_ref, lr_ref, lrw_ref, lrb_ref, og_ref, o_ref,
                hml_ref, bc_ref, qd_ref, ki_ref, ke_ref, dec_ref, oi_ref, ss_ref, sb_ref):
    seq = q_ref.shape[0]
    ck = B_CHUNK
    nc = seq // ck
    grp = 4 * ck
    ng = seq // grp
    cpg = grp // ck
    qscale = B_DK ** -0.5
    nt = (((1,), (1,)), ((), ()))
    tn = (((0,), (0,)), ((), ()))

    lrw = lrw_ref[...].astype(BF16)
    for g in range(ng):
        rows = slice(g * grp, (g + 1) * grp)
        z = jnp.dot(lr_ref[rows, :], lrw, preferred_element_type=F32) + lrb_ref[...]
        la = -(jnp.maximum(-z, 0.0) + jnp.log1p(jnp.exp(-jnp.abs(z)))) / B_GATE_NORM
        hi = la.astype(BF16)
        r1 = la - hi.astype(F32)
        mid = r1.astype(BF16)
        hml_ref[rows, :B_DK] = hi
        hml_ref[rows, B_DK:2 * B_DK] = mid
        hml_ref[rows, 2 * B_DK:] = (r1 - mid.astype(F32)).astype(BF16)

    ti = lax.broadcasted_iota(jnp.int32, (grp, grp), 0)
    si = lax.broadcasted_iota(jnp.int32, (grp, grp), 1)
    same_chunk_causal = (si <= ti) & (si // ck == ti // ck)
    tril_grp = jnp.where(same_chunk_causal, 1.0, 0.0).astype(BF16)
    for g in range(ng):
        rows = slice(g * grp, (g + 1) * grp)
        parts = jnp.dot(tril_grp, hml_ref[rows, :], preferred_element_type=F32)
        bc_ref[rows, :] = parts[:, :B_DK] + parts[:, B_DK:2 * B_DK] + parts[:, 2 * B_DK:]

    for g in range(ng):
        rows = slice(g * grp, (g + 1) * grp)
        bc = bc_ref[rows, :]
        bc3 = bc.reshape(cpg, ck, B_DK)
        bl3 = bc3[:, ck - 1:ck, :]
        q = q_ref[rows, :].astype(F32)
        k3 = k_ref[rows, :].astype(F32).reshape(cpg, ck, B_DK)
        qd_ref[rows, :] = (q * jnp.exp(bc) * qscale).astype(BF16)
        ki_ref[rows, :] = (k3 * jnp.exp(-bc3)).reshape(grp, B_DK).astype(BF16)
        ke_ref[rows, :] = (k3 * jnp.exp(bl3 - bc3)).reshape(grp, B_DK).astype(BF16)
        dec_ref[g * cpg:(g + 1) * cpg, :] = jnp.exp(bl3).reshape(cpg, B_DK)

    for g in range(ng):
        rows = slice(g * grp, (g + 1) * grp)
        a = lax.dot_general(qd_ref[rows, :], ki_ref[rows, :], nt, preferred_element_type=F32)
        a = jnp.where(same_chunk_causal, a, 0.0).astype(BF16)
        oi_ref[rows, :] = jnp.dot(a, v_ref[rows, :], preferred_element_type=F32)

    for c in range(nc):
        rows = slice(c * ck, (c + 1) * ck)
        ss_ref[c] = lax.dot_general(v_ref[rows, :], ke_ref[rows, :], tn, preferred_element_type=F32)

    def scan(c, st):
        sb_ref[c] = st.astype(BF16)
        return st * dec_ref[pl.ds(c, 1), :] + ss_ref[c]

    lax.fori_loop(0, nc, scan, jnp.zeros((B_DV, B_DK), F32), unroll=4)

    for g in range(ng):
        rows = slice(g * grp, (g + 1) * grp)
        inter = [lax.dot_general(qd_ref[c * ck:(c + 1) * ck, :], sb_ref[c], nt, preferred_element_type=F32)
                 for c in range(g * cpg, (g + 1) * cpg)]
        o = oi_ref[rows, :] + jnp.concatenate(inter, axis=0)
        ms = jnp.mean(o * o, axis=-1, keepdims=True)
        y = o * lax.rsqrt(ms + EPS) * og_ref[...]
        r = r_ref[rows, :].astype(F32)
        o_ref[rows, :] = (y * (r * jax.nn.sigmoid(r))).astype(o_ref.dtype)


def gla(proj_out, lr_w, lr_b, og, layer, *, bsz, seq):
    q0 = (AB0 + 3 * A_WIDTH) // B_DK
    k0 = q0 + B_HEADS
    v0 = (AB0 + 3 * A_WIDTH + 2 * B_KWIDTH) // B_DV
    r0 = v0 + B_HEADS
    lr0 = LR_COL // LANES
    return pl.pallas_call(
        _gla_kernel,
        grid=(bsz, B_HEADS),
        in_specs=[pl.BlockSpec((seq, B_DK), lambda b, h: (b, q0 + h)),
                  pl.BlockSpec((seq, B_DK), lambda b, h: (b, k0 + h)),
                  pl.BlockSpec((seq, B_DV), lambda b, h: (b, v0 + h)),
                  pl.BlockSpec((seq, B_DV), lambda b, h: (b, r0 + h)),
                  pl.BlockSpec((seq, LANES), lambda b, h: (b, lr0)),
                  pl.BlockSpec((None, LANES, B_DK), lambda b, h: (layer, 0, h)),
                  pl.BlockSpec((1, B_DK), lambda b, h: (0, h)),
                  pl.BlockSpec((1, B_DV), lambda b, h: (0, 0))],
        out_specs=pl.BlockSpec((seq, B_DV), lambda b, h: (b, h)),
        out_shape=jax.ShapeDtypeStruct((bsz * seq, B_VWIDTH), BF16),
        scratch_shapes=[pltpu.VMEM((seq, 3 * B_DK), BF16),
                        pltpu.VMEM((seq, B_DK), F32),
                        pltpu.VMEM((seq, B_DK), BF16),
                        pltpu.VMEM((seq, B_DK), BF16),
                        pltpu.VMEM((seq, B_DK), BF16),
                        pltpu.VMEM((seq // B_CHUNK, B_DK), F32),
                        pltpu.VMEM((seq, B_DV), F32),
                        pltpu.VMEM((seq // B_CHUNK, B_DV, B_DK), F32),
                        pltpu.VMEM((seq // B_CHUNK, B_DV, B_DK), BF16)],
        compiler_params=_cparams("parallel", "parallel"),
        name="gla",
    )(proj_out, proj_out, proj_out, proj_out, proj_out, lr_w, lr_b.reshape(1, B_KWIDTH),
      og.reshape(1, B_DV))


def _gelu(x):
    c = math.sqrt(2.0 / math.pi)
    return 0.5 * x * (1.0 + jnp.tanh(c * (x + 0.044715 * (x * x * x))))


def _sgu_kernel(u_ref, v_ref, lg_ref, lb_ref, w_ref, b_ref, o_ref):
    rows = u_ref.shape[0]
    ti = lax.broadcasted_iota(jnp.int32, (C_CHUNK, C_CHUNK), 0)
    si = lax.broadcasted_iota(jnp.int32, (C_CHUNK, C_CHUNK), 1)
    causal = si <= ti
    ws = [jnp.where(causal, w_ref[g], 0.0).astype(BF16) for g in range(C_GROUPS)]
    for n in range(rows // C_CHUNK):
        r = slice(n * C_CHUNK, (n + 1) * C_CHUNK)
        v = _gelu(v_ref[r, :].astype(F32))
        mu = jnp.mean(v, axis=-1, keepdims=True)
        vc = v - mu
        vn = vc * lax.rsqrt(jnp.mean(vc * vc, axis=-1, keepdims=True) + EPS)
        vn = (vn * lg_ref[...] + lb_ref[...]).astype(BF16)
        u = _gelu(u_ref[r, :].astype(F32))
        for g in range(C_GROUPS):
            cols = slice(g * C_GROUP_DIM, (g + 1) * C_GROUP_DIM)
            mixed = jnp.dot(ws[g], vn[:, cols], preferred_element_type=F32) + b_ref[g]
            o_ref[r, cols] = (u[:, cols] * mixed).astype(o_ref.dtype)


def sgu(proj_out, ln_g, ln_b, w_s, b_s, *, tm):
    m = proj_out.shape[0]
    u0 = U_COL // C_WIDTH
    return pl.pallas_call(
        _sgu_kernel,
        grid=(m // tm,),
        in_specs=[pl.BlockSpec((tm, C_WIDTH), lambda i: (i, u0)),
                  pl.BlockSpec((tm, C_WIDTH), lambda i: (i, u0 + 1)),
                  pl.BlockSpec((1, C_WIDTH), lambda i: (0, 0)),
                  pl.BlockSpec((1, C_WIDTH), lambda i: (0, 0)),
                  pl.BlockSpec((C_GROUPS, C_CHUNK, C_CHUNK), lambda i: (0, 0, 0)),
                  pl.BlockSpec((C_GROUPS, C_CHUNK, 1), lambda i: (0, 0, 0))],
        out_specs=pl.BlockSpec((tm, C_WIDTH), lambda i: (i, 0)),
        out_shape=jax.ShapeDtypeStruct((m, C_WIDTH), BF16),
        compiler_params=_cparams("parallel"),
        name="sgu",
    )(proj_out, proj_out, ln_g.reshape(1, C_WIDTH), ln_b.reshape(1, C_WIDTH), w_s,
      b_s.reshape(C_GROUPS, C_CHUNK, 1))


def _merge_kernel(ya_ref, yb_ref, yc_ref, ga_ref, gb_ref, gc_ref, x_ref,
                  wa_ref, wb_ref, wc_ref, wo_ref, g2_ref, o_ref, on_ref):
    def gated(g_ref, y_ref, w_ref):
        return (jax.nn.sigmoid(g_ref[...].astype(F32))
                * jnp.dot(y_ref[...], w_ref[...], preferred_element_type=F32))

    merged = gated(ga_ref, ya_ref, wa_ref) + gated(gb_ref, yb_ref, wb_ref) + gated(gc_ref, yc_ref, wc_ref)
    h = x_ref[...] + jnp.dot(merged.astype(BF16), wo_ref[...], preferred_element_type=F32)
    o_ref[...] = h
    ms = jnp.mean(h * h, axis=-1, keepdims=True)
    on_ref[...] = (h * lax.rsqrt(ms + EPS) * g2_ref[...]).astype(on_ref.dtype)


def merge(ya, yb, yc, proj_out, x, wa, wb, wc, wo, g2, layer, *, tm):
    m, d = x.shape
    resident = functools.partial(pl.BlockSpec, pipeline_mode=pl.Buffered(1))
    return pl.pallas_call(
        _merge_kernel,
        grid=(m // tm,),
        in_specs=[pl.BlockSpec((tm, A_WIDTH), lambda i: (i, 0)),
                  pl.BlockSpec((tm, B_VWIDTH), lambda i: (i, 0)),
                  pl.BlockSpec((tm, C_WIDTH), lambda i: (i, 0)),
                  pl.BlockSpec((tm, d), lambda i: (i, 0)),
                  pl.BlockSpec((tm, d), lambda i: (i, 1)),
                  pl.BlockSpec((tm, d), lambda i: (i, 2)),
                  pl.BlockSpec((tm, d), lambda i: (i, 0)),
                  resident((None, A_WIDTH, d), lambda i: (layer, 0, 0)),
                  resident((None, B_VWIDTH, d), lambda i: (layer, 0, 0)),
                  resident((None, C_WIDTH, d), lambda i: (layer, 0, 0)),
                  resident((None, d, d), lambda i: (layer, 0, 0)),
                  pl.BlockSpec((1, d), lambda i: (0, 0))],
        out_specs=[pl.BlockSpec((tm, d), lambda i: (i, 0)),
                   pl.BlockSpec((tm, d), lambda i: (i, 0))],
        out_shape=[jax.ShapeDtypeStruct((m, d), F32),
                   jax.ShapeDtypeStruct((m, d), BF16)],
        compiler_params=_cparams("parallel"),
        name="merge",
    )(ya, yb, yc, proj_out, proj_out, proj_out, x, wa, wb, wc, wo, g2.reshape(1, d))


def _mlp_kernel(h_ref, hn_ref, w1_ref, w2_ref, o_ref):
    def step(first):
        a = jnp.dot(hn_ref[...], w1_ref[...].astype(BF16), preferred_element_type=F32)
        a = jnp.square(jnp.maximum(a, 0.0)).astype(BF16)
        upd = jnp.dot(a, w2_ref[...].astype(BF16), preferred_element_type=F32)
        o_ref[...] = (h_ref[...] if first else o_ref[...]) + upd

    pl.when(pl.program_id(1) == 0)(functools.partial(step, True))
    pl.when(pl.program_id(1) > 0)(functools.partial(step, False))


def mlp(h, hn, w1, w2, layer, *, tm, tf):
    m, d = h.shape
    f = w1.shape[2]
    return pl.pallas_call(
        _mlp_kernel,
        grid=(m // tm, f // tf),
        in_specs=[pl.BlockSpec((tm, d), lambda i, j: (i, 0), pipeline_mode=pl.Buffered(1)),
                  pl.BlockSpec((tm, d), lambda i, j: (i, 0)),
                  pl.BlockSpec((None, d, tf), lambda i, j: (layer, 0, j)),
                  pl.BlockSpec((None, tf, d), lambda i, j: (layer, j, 0))],
        out_specs=pl.BlockSpec((tm, d), lambda i, j: (i, 0)),
        out_shape=jax.ShapeDtypeStruct((m, d), F32),
        compiler_params=_cparams("parallel", "arbitrary"),
        name="mlp",
    )(h, hn, w1, w2)


def kernel(x, rpe_table, norm1_g, w_in, q_norm_g, k_norm_g, gla_lr_w, gla_lr_b, gla_out_g,
           sg_ln_g, sg_ln_b, sg_w, sg_b, w_br_a, w_br_b, w_br_c, w_o, norm2_g, w_ff1, w_ff2):
    bsz, seq, d = x.shape
    assert seq % A_BLOCK == 0 and seq % B_CHUNK == 0 and seq % C_CHUNK == 0
    assert w_in.shape[2] == IN_WIDTH
    h = x.reshape(bsz * seq, d)
    bias = rpe_bias(rpe_table)
    w_t = jnp.swapaxes(w_in, 1, 2)
    wa, wb, wc, wo = (w.astype(BF16) for w in (w_br_a, w_br_b, w_br_c, w_o))
    lr_w = jnp.pad(gla_lr_w, ((0, 0), (LR_LANE, LANES - LR_LANE - B_LOWRANK), (0, 0)))
    for l in range(DEPTH):
        p = proj(h, norm1_g[l], w_t, l, tm=1024)
        y_a = moba(p, q_norm_g[l], k_norm_g[l], bias, bsz=bsz, seq=seq)
        y_b = gla(p, lr_w, gla_lr_b[l], gla_out_g[l], l, bsz=bsz, seq=seq)
        y_c = sgu(p, sg_ln_g[l], sg_ln_b[l], sg_w[l], sg_b[l], tm=1024)
        h, hn = merge(y_a, y_b, y_c, p, h, wa, wb, wc, wo, norm2_g[l], l, tm=256)
        h = mlp(h, hn, w_ff1, w_ff2, l, tm=1024, tf=512)
    return h.reshape(bsz, seq, d)
```

```python
import functools
import math

import jax
import jax.numpy as jnp
from jax import lax
from jax.experimental import pallas as pl
from jax.experimental.pallas import tpu as pltpu

D_MODEL = 2048
DEPTH = 2
HEAD_DIM = 128
A_HEADS = 4
A_WIDTH = A_HEADS * HEAD_DIM
A_BLOCK = 256
A_TOPK = 3
RPE_BUCKETS = 32
RPE_MAX_DIST = 128
B_HEADS = 4
B_DK = 128
B_DV = 256
B_KWIDTH = B_HEADS * B_DK
B_VWIDTH = B_HEADS * B_DV
B_LOWRANK = 16
B_GATE_NORM = 16.0
B_CHUNK = 64
C_GROUPS = 4
C_GROUP_DIM = 128
C_WIDTH = C_GROUPS * C_GROUP_DIM
C_CHUNK = 128
D_FF = 4 * D_MODEL
EPS = 1e-6
NEG_INF = -1e30
LOG2E = 1.4426950408889634

LANES = 128
BF16_ROWS = 16
VMEM_LIMIT = 56 * 1024 * 1024

F32 = jnp.float32
BF16 = jnp.bfloat16

PROJ_TN = 1536
AB_WIDTH = 3 * A_WIDTH + 2 * B_KWIDTH + 2 * B_VWIDTH
G_WIDTH = 3 * D_MODEL
IN_WIDTH = AB_WIDTH + B_LOWRANK + 2 * C_WIDTH + G_WIDTH
TAIL_PAD = PROJ_TN - (B_LOWRANK + 2 * C_WIDTH)
AB0 = G_WIDTH
T0 = AB0 + AB_WIDTH
LR_COL = T0 + TAIL_PAD
U_COL = LR_COL + B_LOWRANK
PROJ_WIDTH = T0 + PROJ_TN
N_AB_TILES = AB_WIDTH // PROJ_TN
N_G_TILES = G_WIDTH // PROJ_TN
C_SRC = AB_WIDTH - TAIL_PAD
G_SRC = IN_WIDTH - G_WIDTH
assert AB_WIDTH % PROJ_TN == 0 and G_WIDTH % PROJ_TN == 0 and U_COL % C_WIDTH == 0
assert C_SRC % BF16_ROWS == 0 and G_SRC % BF16_ROWS == 0
LR_LANE = LR_COL % LANES


def _rpe_thresholds():
    max_exact = RPE_BUCKETS // 2

    def bucket(n):
        if n < max_exact:
            return n
        v = math.log(n / max_exact) / math.log(RPE_MAX_DIST / max_exact) * (RPE_BUCKETS - max_exact)
        return min(max_exact + int(v), RPE_BUCKETS - 1)

    table = [bucket(n) for n in range(4 * RPE_MAX_DIST)]
    return [min(n for n in range(len(table)) if table[n] >= b)
            for b in range(max_exact + 1, RPE_BUCKETS)]


RPE_THRESHOLDS = _rpe_thresholds()
assert RPE_THRESHOLDS[-1] <= A_BLOCK


def _cparams(*sem):
    return pltpu.CompilerParams(dimension_semantics=sem, vmem_limit_bytes=VMEM_LIMIT)


def _rpe_bias_kernel(tbl_ref, bias_ref):
    h = pl.program_id(0)
    t = lax.broadcasted_iota(jnp.int32, (A_BLOCK, 2 * A_BLOCK), 0)
    c = lax.broadcasted_iota(jnp.int32, (A_BLOCK, 2 * A_BLOCK), 1)
    n = jnp.maximum(t - c + A_BLOCK, 0)
    max_exact = RPE_BUCKETS // 2
    large = jnp.full(n.shape, max_exact, jnp.int32)
    for thr in RPE_THRESHOLDS:
        large = large + (n >= thr).astype(jnp.int32)
    bucket = jnp.where(n < max_exact, n, large)
    bias = jnp.zeros(n.shape, F32)
    for b in range(RPE_BUCKETS):
        bias = jnp.where(bucket == b, tbl_ref[b, h], bias)
    bias = (bias - tbl_ref[RPE_BUCKETS - 1, h]) * LOG2E
    bias_ref[0] = jnp.where(c - A_BLOCK > t, NEG_INF, bias)


def rpe_bias(rpe_table):
    return pl.pallas_call(
        _rpe_bias_kernel,
        grid=(A_HEADS,),
        in_specs=[pl.BlockSpec(memory_space=pltpu.SMEM)],
        out_specs=pl.BlockSpec((1, A_BLOCK, 2 * A_BLOCK), lambda h: (h, 0, 0)),
        out_shape=jax.ShapeDtypeStruct((A_HEADS, A_BLOCK, 2 * A_BLOCK), F32),
        compiler_params=_cparams("arbitrary"),
        name="rpe_bias",
    )(rpe_table)


def _proj_kernel(x_hbm, g_ref, w_ref, o_ref, x_ref, xn_ref, x_sem, *, row_chunk):
    i, j = pl.program_id(0), pl.program_id(1)
    tm = x_ref.shape[0]

    def x_copy(tile):
        return pltpu.make_async_copy(x_hbm.at[pl.ds(tile * tm, tm), :], x_ref, x_sem)

    @pl.when(j == 0)
    def _():
        @pl.when(i == 0)
        def _():
            x_copy(0).start()
        x_copy(i).wait()

        def body(r, carry):
            rows = pl.ds(pl.multiple_of(r * row_chunk, row_chunk), row_chunk)
            x = x_ref[rows, :]
            ms = jnp.mean(x * x, axis=-1, keepdims=True)
            xn_ref[rows, :] = (x * lax.rsqrt(ms + EPS) * g_ref[...]).astype(BF16)
            return carry
        lax.fori_loop(0, tm // row_chunk, body, 0)

    @pl.when((j == 1) & (i + 1 < pl.num_programs(0)))
    def _():
        x_copy(i + 1).start()

    o_ref[...] = lax.dot_general(xn_ref[...], w_ref[...].astype(BF16), (((1,), (1,)), ((), ())),
                                 preferred_element_type=F32).astype(o_ref.dtype)


def _proj_src_row(j):
    u = BF16_ROWS
    return u * jnp.where(j < N_AB_TILES, j * (PROJ_TN // u),
                         jnp.where(j == N_AB_TILES, C_SRC // u,
                                   G_SRC // u + (j - N_AB_TILES - 1) * (PROJ_TN // u)))


def _proj_out_tile(j):
    return jnp.where(j < N_AB_TILES, j + N_G_TILES,
                     jnp.where(j == N_AB_TILES, N_G_TILES + N_AB_TILES, j - N_AB_TILES - 1))


def proj(x, g, w_t, layer, *, tm):
    m, k = x.shape
    tn = PROJ_TN
    n_tiles = N_AB_TILES + 1 + N_G_TILES
    return pl.pallas_call(
        functools.partial(_proj_kernel, row_chunk=128),
        grid=(m // tm, n_tiles),
        in_specs=[pl.BlockSpec(memory_space=pl.ANY),
                  pl.BlockSpec((1, k), lambda i, j: (0, 0)),
                  pl.BlockSpec((None, pl.Element(tn), pl.Element(k)),
                               lambda i, j: (layer, _proj_src_row(j), 0))],
        out_specs=pl.BlockSpec((tm, tn), lambda i, j: (i, _proj_out_tile(j))),
        out_shape=jax.ShapeDtypeStruct((m, PROJ_WIDTH), BF16),
        scratch_shapes=[pltpu.VMEM((tm, k), F32),
                        pltpu.VMEM((tm, k), BF16),
                        pltpu.SemaphoreType.DMA(())],
        compiler_params=_cparams("arbitrary", "arbitrary"),
        name="proj",
    )(x, g.reshape(1, k), w_t)


def _head_rms(x, g):
    ms = jnp.mean(x * x, axis=-1, keepdims=True)
    return x * lax.rsqrt(ms + EPS) * g


def _moba_kernel(q_ref, k_ref, v_ref, qg_ref, kg_ref, bias_ref, o_ref,
                 qn_ref, qa_ref, ka_ref, va_ref, kmean_ref, s_ref, p_ref):
    seq, hd = q_ref.shape
    nb = seq // A_BLOCK
    rc = BF16_ROWS
    qn = _head_rms(q_ref[...].astype(F32), qg_ref[...])
    qn_ref[...] = qn
    qa_ref[:, :hd] = (qn * (hd ** -0.5 * LOG2E)).astype(BF16)
    qa_ref[:, hd:] = jnp.zeros((seq, hd), BF16)
    kn = _head_rms(k_ref[...].astype(F32), kg_ref[...])
    ka_ref[:, :hd] = kn.astype(BF16)
    key_blk = lax.broadcasted_iota(jnp.int32, (seq, hd), 0) // A_BLOCK
    ka_ref[:, hd:] = jnp.where(key_blk == lax.broadcasted_iota(jnp.int32, (seq, hd), 1), 1.0, 0.0).astype(BF16)
    va_ref[:, :hd] = v_ref[...]
    va_ref[:, hd:] = jnp.ones((seq, hd), BF16)
    for j in range(nb):
        kmean_ref[j:j + 1, :] = jnp.mean(kn[j * A_BLOCK:(j + 1) * A_BLOCK], axis=0, keepdims=True)
    kmean = kmean_ref[...]
    blk = lax.broadcasted_iota(jnp.int32, (nb, A_BLOCK), 0)
    eye = jnp.where(lax.broadcasted_iota(jnp.int32, (nb, hd), 0)
                    == lax.broadcasted_iota(jnp.int32, (nb, hd), 1), 1.0, 0.0)

    def scores(i):
        rows = slice(i * A_BLOCK, (i + 1) * A_BLOCK)
        nk = (i + 1) * A_BLOCK
        if i > A_TOPK:
            g = lax.dot_general(kmean, qn_ref[rows, :], (((1,), (1,)), ((), ())),
                                precision=lax.Precision.HIGHEST, preferred_element_type=F32)
            rank = jnp.zeros((nb, A_BLOCK), jnp.int32)
            for jp in range(i):
                gj = g[jp:jp + 1, :]
                beats = (gj > g) | ((gj == g) & (jp < blk))
                rank = rank + beats.astype(jnp.int32)
            neg = jnp.where((blk < i) & (rank >= A_TOPK), NEG_INF, 0.0)
            qa_ref[rows, hd:] = lax.dot_general(neg, eye, (((0,), (0,)), ((), ())),
                                                preferred_element_type=F32).astype(BF16)
        s_ref[i % 2, :, :nk] = lax.dot_general(qa_ref[rows, :], ka_ref[:nk, :], (((1,), (1,)), ((), ())),
                                               preferred_element_type=F32)

    def softmax(i):
        for c in range(A_BLOCK // rc):
            r = slice(c * rc, (c + 1) * rc)
            tiles = []
            for j in range(i + 1):
                sj = s_ref[i % 2, r, j * A_BLOCK:(j + 1) * A_BLOCK]
                if j == i:
                    sj = sj + bias_ref[0, r, A_BLOCK:]
                elif j == i - 1:
                    sj = sj + bias_ref[0, r, :A_BLOCK]
                tiles.append(sj)
            mt = tiles[0]
            for t in tiles[1:]:
                mt = jnp.maximum(mt, t)
            m = jnp.max(mt, axis=-1, keepdims=True)
            for j, t in enumerate(tiles):
                p_ref[i % 2, r, j * A_BLOCK:(j + 1) * A_BLOCK] = jnp.exp2(t - m).astype(BF16)

    def weighted_sum(i):
        rows = slice(i * A_BLOCK, (i + 1) * A_BLOCK)
        nk = (i + 1) * A_BLOCK
        o = jnp.dot(p_ref[i % 2, :, :nk], va_ref[:nk, :], preferred_element_type=F32)
        o_ref[rows, :] = (o[:, :hd] / o[:, hd:]).astype(o_ref.dtype)

    scores(0)
    for i in range(nb):
        if i + 1 < nb:
            scores(i + 1)
        softmax(i)
        weighted_sum(i)


def moba(proj_out, qg, kg, bias, *, bsz, seq):
    hd = HEAD_DIM
    q0 = AB0 // hd
    return pl.pallas_call(
        _moba_kernel,
        grid=(bsz, A_HEADS),
        in_specs=[pl.BlockSpec((seq, hd), lambda b, h: (b, q0 + h)),
                  pl.BlockSpec((seq, hd), lambda b, h: (b, q0 + A_HEADS + h)),
                  pl.BlockSpec((seq, hd), lambda b, h: (b, q0 + 2 * A_HEADS + h)),
                  pl.BlockSpec((1, hd), lambda b, h: (0, 0)),
                  pl.BlockSpec((1, hd), lambda b, h: (0, 0)),
                  pl.BlockSpec((1, A_BLOCK, 2 * A_BLOCK), lambda b, h: (h, 0, 0))],
        out_specs=pl.BlockSpec((seq, hd), lambda b, h: (b, h)),
        out_shape=jax.ShapeDtypeStruct((bsz * seq, A_WIDTH), BF16),
        scratch_shapes=[pltpu.VMEM((seq, hd), F32),
                        pltpu.VMEM((seq, 2 * hd), BF16),
                        pltpu.VMEM((seq, 2 * hd), BF16),
                        pltpu.VMEM((seq, 2 * hd), BF16),
                        pltpu.VMEM((seq // A_BLOCK, hd), F32),
                        pltpu.VMEM((2, A_BLOCK, seq), F32),
                        pltpu.VMEM((2, A_BLOCK, seq), BF16)],
        compiler_params=_cparams("parallel", "parallel"),
        name="moba",
    )(proj_out, proj_out, proj_out, qg.reshape(1, hd), kg.reshape(1, hd), bias)


def _gla_kernel(q_ref, k_ref, v_ref, r_ref, lr_ref, lrw_ref, lrb_ref, og_ref, o_ref,
                hml_ref, bc_ref, qd_ref, ki_ref, ke_ref, dec_ref, oi_ref, ss_ref, sb_ref):
    seq = q_ref.shape[0]
    ck = B_CHUNK
    nc = seq // ck
    grp = 4 * ck
    ng = seq // grp
    cpg = grp // ck
    qscale = B_DK ** -0.5
    nt = (((1,), (1,)), ((), ()))
    tn = (((0,), (0,)), ((), ()))

    lrw = lrw_ref[...].astype(BF16)
    for g in range(ng):
        rows = slice(g * grp, (g + 1) * grp)
        z = jnp.dot(lr_ref[rows, :], lrw, preferred_element_type=F32) + lrb_ref[...]
        la = -(jnp.maximum(-z, 0.0) + jnp.log1p(jnp.exp(-jnp.abs(z)))) / B_GATE_NORM
        hi = la.astype(BF16)
        r1 = la - hi.astype(F32)
        mid = r1.astype(BF16)
        hml_ref[rows, :B_DK] = hi
        hml_ref[rows, B_DK:2 * B_DK] = mid
        hml_ref[rows, 2 * B_DK:] = (r1 - mid.astype(F32)).astype(BF16)

    ti = lax.broadcasted_iota(jnp.int32, (grp, grp), 0)
    si = lax.broadcasted_iota(jnp.int32, (grp, grp), 1)
    same_chunk_causal = (si <= ti) & (si // ck == ti // ck)
    tril_grp = jnp.where(same_chunk_causal, 1.0, 0.0).astype(BF16)
    for g in range(ng):
        rows = slice(g * grp, (g + 1) * grp)
        parts = jnp.dot(tril_grp, hml_ref[rows, :], preferred_element_type=F32)
        bc_ref[rows, :] = parts[:, :B_DK] + parts[:, B_DK:2 * B_DK] + parts[:, 2 * B_DK:]

    for g in range(ng):
        rows = slice(g * grp, (g + 1) * grp)
        bc = bc_ref[rows, :]
        bc3 = bc.reshape(cpg, ck, B_DK)
        bl3 = bc3[:, ck - 1:ck, :]
        q = q_ref[rows, :].astype(F32)
        k3 = k_ref[rows, :].astype(F32).reshape(cpg, ck, B_DK)
        qd_ref[rows, :] = (q * jnp.exp(bc) * qscale).astype(BF16)
        ki_ref[rows, :] = (k3 * jnp.exp(-bc3)).reshape(grp, B_DK).astype(BF16)
        ke_ref[rows, :] = (k3 * jnp.exp(bl3 - bc3)).reshape(grp, B_DK).astype(BF16)
        dec_ref[g * cpg:(g + 1) * cpg, :] = jnp.exp(bl3).reshape(cpg, B_DK)

    for g in range(ng):
        rows = slice(g * grp, (g + 1) * grp)
        a = lax.dot_general(qd_ref[rows, :], ki_ref[rows, :], nt, preferred_element_type=F32)
        a = jnp.where(same_chunk_causal, a, 0.0).astype(BF16)
        oi_ref[rows, :] = jnp.dot(a, v_ref[rows, :], preferred_element_type=F32)

    for c in range(nc):
        rows = slice(c * ck, (c + 1) * ck)
        ss_ref[c] = lax.dot_general(v_ref[rows, :], ke_ref[rows, :], tn, preferred_element_type=F32)

    def scan(c, st):
        sb_ref[c] = st.astype(BF16)
        return st * dec_ref[pl.ds(c, 1), :] + ss_ref[c]

    lax.fori_loop(0, nc, scan, jnp.zeros((B_DV, B_DK), F32), unroll=4)

    for g in range(ng):
        rows = slice(g * grp, (g + 1) * grp)
        inter = [lax.dot_general(qd_ref[c * ck:(c + 1) * ck, :], sb_ref[c], nt, preferred_element_type=F32)
                 for c in range(g * cpg, (g + 1) * cpg)]
        o = oi_ref[rows, :] + jnp.concatenate(inter, axis=0)
        ms = jnp.mean(o * o, axis=-1, keepdims=True)
        y = o * lax.rsqrt(ms + EPS) * og_ref[...]
        r = r_ref[rows, :].astype(F32)
        o_ref[rows, :] = (y * (r * jax.nn.sigmoid(r))).astype(o_ref.dtype)


def gla(proj_out, lr_w, lr_b, og, layer, *, bsz, seq):
    q0 = (AB0 + 3 * A_WIDTH) // B_DK
    k0 = q0 + B_HEADS
    v0 = (AB0 + 3 * A_WIDTH + 2 * B_KWIDTH) // B_DV
    r0 = v0 + B_HEADS
    lr0 = LR_COL // LANES
    return pl.pallas_call(
        _gla_kernel,
        grid=(bsz, B_HEADS),
        in_specs=[pl.BlockSpec((seq, B_DK), lambda b, h: (b, q0 + h)),
                  pl.BlockSpec((seq, B_DK), lambda b, h: (b, k0 + h)),
                  pl.BlockSpec((seq, B_DV), lambda b, h: (b, v0 + h)),
                  pl.BlockSpec((seq, B_DV), lambda b, h: (b, r0 + h)),
                  pl.BlockSpec((seq, LANES), lambda b, h: (b, lr0)),
                  pl.BlockSpec((None, LANES, B_DK), lambda b, h: (layer, 0, h)),
                  pl.BlockSpec((1, B_DK), lambda b, h: (0, h)),
                  pl.BlockSpec((1, B_DV), lambda b, h: (0, 0))],
        out_specs=pl.BlockSpec((seq, B_DV), lambda b, h: (b, h)),
        out_shape=jax.ShapeDtypeStruct((bsz * seq, B_VWIDTH), BF16),
        scratch_shapes=[pltpu.VMEM((seq, 3 * B_DK), BF16),
                        pltpu.VMEM((seq, B_DK), F32),
                        pltpu.VMEM((seq, B_DK), BF16),
                        pltpu.VMEM((seq, B_DK), BF16),
                        pltpu.VMEM((seq, B_DK), BF16),
                        pltpu.VMEM((seq // B_CHUNK, B_DK), F32),
                        pltpu.VMEM((seq, B_DV), F32),
                        pltpu.VMEM((seq // B_CHUNK, B_DV, B_DK), F32),
                        pltpu.VMEM((seq // B_CHUNK, B_DV, B_DK), BF16)],
        compiler_params=_cparams("parallel", "parallel"),
        name="gla",
    )(proj_out, proj_out, proj_out, proj_out, proj_out, lr_w, lr_b.reshape(1, B_KWIDTH),
      og.reshape(1, B_DV))


def _gelu(x):
    c = math.sqrt(2.0 / math.pi)
    return 0.5 * x * (1.0 + jnp.tanh(c * (x + 0.044715 * (x * x * x))))


def _sgu_kernel(u_ref, v_ref, lg_ref, lb_ref, w_ref, b_ref, o_ref):
    rows = u_ref.shape[0]
    ti = lax.broadcasted_iota(jnp.int32, (C_CHUNK, C_CHUNK), 0)
    si = lax.broadcasted_iota(jnp.int32, (C_CHUNK, C_CHUNK), 1)
    causal = si <= ti
    ws = [jnp.where(causal, w_ref[g], 0.0).astype(BF16) for g in range(C_GROUPS)]
    for n in range(rows // C_CHUNK):
        r = slice(n * C_CHUNK, (n + 1) * C_CHUNK)
        v = _gelu(v_ref[r, :].astype(F32))
        mu = jnp.mean(v, axis=-1, keepdims=True)
        vc = v - mu
        vn = vc * lax.rsqrt(jnp.mean(vc * vc, axis=-1, keepdims=True) + EPS)
        vn = (vn * lg_ref[...] + lb_ref[...]).astype(BF16)
        u = _gelu(u_ref[r, :].astype(F32))
        for g in range(C_GROUPS):
            cols = slice(g * C_GROUP_DIM, (g + 1) * C_GROUP_DIM)
            mixed = jnp.dot(ws[g], vn[:, cols], preferred_element_type=F32) + b_ref[g]
            o_ref[r, cols] = (u[:, cols] * mixed).astype(o_ref.dtype)


def sgu(proj_out, ln_g, ln_b, w_s, b_s, *, tm):
    m = proj_out.shape[0]
    u0 = U_COL // C_WIDTH
    return pl.pallas_call(
        _sgu_kernel,
        grid=(m // tm,),
        in_specs=[pl.BlockSpec((tm, C_WIDTH), lambda i: (i, u0)),
                  pl.BlockSpec((tm, C_WIDTH), lambda i: (i, u0 + 1)),
                  pl.BlockSpec((1, C_WIDTH), lambda i: (0, 0)),
                  pl.BlockSpec((1, C_WIDTH), lambda i: (0, 0)),
                  pl.BlockSpec((C_GROUPS, C_CHUNK, C_CHUNK), lambda i: (0, 0, 0)),
                  pl.BlockSpec((C_GROUPS, C_CHUNK, 1), lambda i: (0, 0, 0))],
        out_specs=pl.BlockSpec((tm, C_WIDTH), lambda i: (i, 0)),
        out_shape=jax.ShapeDtypeStruct((m, C_WIDTH), BF16),
        compiler_params=_cparams("parallel"),
        name="sgu",
    )(proj_out, proj_out, ln_g.reshape(1, C_WIDTH), ln_b.reshape(1, C_WIDTH), w_s,
      b_s.reshape(C_GROUPS, C_CHUNK, 1))


def _merge_kernel(ya_ref, yb_ref, yc_ref, ga_ref, gb_ref, gc_ref, x_ref,
                  wa_ref, wb_ref, wc_ref, wo_ref, g2_ref, o_ref, on_ref):
    def gated(g_ref, y_ref, w_ref):
        return (jax.nn.sigmoid(g_ref[...].astype(F32))
                * jnp.dot(y_ref[...], w_ref[...], preferred_element_type=F32))

    merged = gated(ga_ref, ya_ref, wa_ref) + gated(gb_ref, yb_ref, wb_ref) + gated(gc_ref, yc_ref, wc_ref)
    h = x_ref[...] + jnp.dot(merged.astype(BF16), wo_ref[...], preferred_element_type=F32)
    o_ref[...] = h
    ms = jnp.mean(h * h, axis=-1, keepdims=True)
    on_ref[...] = (h * lax.rsqrt(ms + EPS) * g2_ref[...]).astype(on_ref.dtype)


def merge(ya, yb, yc, proj_out, x, wa, wb, wc, wo, g2, layer, *, tm):
    m, d = x.shape
    resident = functools.partial(pl.BlockSpec, pipeline_mode=pl.Buffered(1))
    return pl.pallas_call(
        _merge_kernel,
        grid=(m // tm,),
        in_specs=[pl.BlockSpec((tm, A_WIDTH), lambda i: (i, 0)),
                  pl.BlockSpec((tm, B_VWIDTH), lambda i: (i, 0)),
                  pl.BlockSpec((tm, C_WIDTH), lambda i: (i, 0)),
                  pl.BlockSpec((tm, d), lambda i: (i, 0)),
                  pl.BlockSpec((tm, d), lambda i: (i, 1)),
                  pl.BlockSpec((tm, d), lambda i: (i, 2)),
                  pl.BlockSpec((tm, d), lambda i: (i, 0)),
                  resident((None, A_WIDTH, d), lambda i: (layer, 0, 0)),
                  resident((None, B_VWIDTH, d), lambda i: (layer, 0, 0)),
                  resident((None, C_WIDTH, d), lambda i: (layer, 0, 0)),
                  resident((None, d, d), lambda i: (layer, 0, 0)),
                  pl.BlockSpec((1, d), lambda i: (0, 0))],
        out_specs=[pl.BlockSpec((tm, d), lambda i: (i, 0)),
                   pl.BlockSpec((tm, d), lambda i: (i, 0))],
        out_shape=[jax.ShapeDtypeStruct((m, d), F32),
                   jax.ShapeDtypeStruct((m, d), BF16)],
        compiler_params=_cparams("parallel"),
        name="merge",
    )(ya, yb, yc, proj_out, proj_out, proj_out, x, wa, wb, wc, wo, g2.reshape(1, d))


def _mlp_kernel(h_hbm, hn_ref, w1_ref, w2_ref, o_ref, h_ref, h_sem):
    i, j = pl.program_id(0), pl.program_id(1)
    last = pl.num_programs(1) - 1
    tm = h_ref.shape[0]
    h_copy = pltpu.make_async_copy(h_hbm.at[pl.ds(i * tm, tm), :], h_ref, h_sem)

    def update():
        a = jnp.dot(hn_ref[...], w1_ref[...].astype(BF16), preferred_element_type=F32)
        a = jnp.square(jnp.maximum(a, 0.0)).astype(BF16)
        return jnp.dot(a, w2_ref[...].astype(BF16), preferred_element_type=F32)

    @pl.when(j == 0)
    def _():
        h_copy.start()
        o_ref[...] = update()

    @pl.when((j > 0) & (j < last))
    def _():
        o_ref[...] += update()

    @pl.when(j == last)
    def _():
        h_copy.wait()
        o_ref[...] = (o_ref[...] + h_ref[...]) + update()


def mlp(h, hn, w1, w2, layer, *, tm, tf):
    m, d = h.shape
    f = w1.shape[2]
    return pl.pallas_call(
        _mlp_kernel,
        grid=(m // tm, f // tf),
        in_specs=[pl.BlockSpec(memory_space=pl.ANY),
                  pl.BlockSpec((tm, d), lambda i, j: (i, 0)),
                  pl.BlockSpec((None, d, tf), lambda i, j: (layer, 0, j)),
                  pl.BlockSpec((None, tf, d), lambda i, j: (layer, j, 0))],
        out_specs=pl.BlockSpec((tm, d), lambda i, j: (i, 0)),
        out_shape=jax.ShapeDtypeStruct((m, d), F32),
        scratch_shapes=[pltpu.VMEM((tm, d), F32),
                        pltpu.SemaphoreType.DMA(())],
        compiler_params=_cparams("parallel", "arbitrary"),
        name="mlp",
    )(h, hn, w1, w2)


def kernel(x, rpe_table, norm1_g, w_in, q_norm_g, k_norm_g, gla_lr_w, gla_lr_b, gla_out_g,
           sg_ln_g, sg_ln_b, sg_w, sg_b, w_br_a, w_br_b, w_br_c, w_o, norm2_g, w_ff1, w_ff2):
    bsz, seq, d = x.shape
    assert seq % A_BLOCK == 0 and seq % B_CHUNK == 0 and seq % C_CHUNK == 0
    assert w_in.shape[2] == IN_WIDTH
    h = x.reshape(bsz * seq, d)
    bias = rpe_bias(rpe_table)
    w_t = jnp.swapaxes(w_in, 1, 2)
    wa, wb, wc, wo = (w.astype(BF16) for w in (w_br_a, w_br_b, w_br_c, w_o))
    lr_w = jnp.pad(gla_lr_w, ((0, 0), (LR_LANE, LANES - LR_LANE - B_LOWRANK), (0, 0)))
    for l in range(DEPTH):
        p = proj(h, norm1_g[l], w_t, l, tm=1024)
        y_a = moba(p, q_norm_g[l], k_norm_g[l], bias, bsz=bsz, seq=seq)
        y_b = gla(p, lr_w, gla_lr_b[l], gla_out_g[l], l, bsz=bsz, seq=seq)
        y_c = sgu(p, sg_ln_g[l], sg_ln_b[l], sg_w[l], sg_b[l], tm=1024)
        h, hn = merge(y_a, y_b, y_c, p, h, wa, wb, wc, wo, norm2_g[l], l, tm=256)
        h = mlp(h, hn, w_ff1, w_ff2, l, tm=1024, tf=512)
    return h.reshape(bsz, seq, d)
```

```python
import functools
import math

import jax
import jax.numpy as jnp
from jax import lax
from jax.experimental import pallas as pl
from jax.experimental.pallas import tpu as pltpu

D_MODEL = 2048
DEPTH = 2
HEAD_DIM = 128
A_HEADS = 4
A_WIDTH = A_HEADS * HEAD_DIM
A_BLOCK = 256
A_TOPK = 3
RPE_BUCKETS = 32
RPE_MAX_DIST = 128
B_HEADS = 4
B_DK = 128
B_DV = 256
B_KWIDTH = B_HEADS * B_DK
B_VWIDTH = B_HEADS * B_DV
B_LOWRANK = 16
B_GATE_NORM = 16.0
B_CHUNK = 64
C_GROUPS = 4
C_GROUP_DIM = 128
C_WIDTH = C_GROUPS * C_GROUP_DIM
C_CHUNK = 128
D_FF = 4 * D_MODEL
EPS = 1e-6
NEG_INF = -1e30
LOG2E = 1.4426950408889634

LANES = 128
BF16_ROWS = 16
VMEM_LIMIT = 56 * 1024 * 1024

F32 = jnp.float32
BF16 = jnp.bfloat16

PROJ_TM = 1024
MLP_TM = 1024
MLP_TF = 512
MERGE_TM = 256
SGU_TM = 1024

PROJ_TN = 1536
AB_WIDTH = 3 * A_WIDTH + 2 * B_KWIDTH + 2 * B_VWIDTH
G_WIDTH = 3 * D_MODEL
IN_WIDTH = AB_WIDTH + B_LOWRANK + 2 * C_WIDTH + G_WIDTH
TAIL_PAD = PROJ_TN - (B_LOWRANK + 2 * C_WIDTH)
AB0 = G_WIDTH
T0 = AB0 + AB_WIDTH
LR_COL = T0 + TAIL_PAD
U_COL = LR_COL + B_LOWRANK
PROJ_WIDTH = T0 + PROJ_TN
N_AB_TILES = AB_WIDTH // PROJ_TN
N_G_TILES = G_WIDTH // PROJ_TN
C_SRC = AB_WIDTH - TAIL_PAD
G_SRC = IN_WIDTH - G_WIDTH
assert AB_WIDTH % PROJ_TN == 0 and G_WIDTH % PROJ_TN == 0 and U_COL % C_WIDTH == 0
assert C_SRC % BF16_ROWS == 0 and G_SRC % BF16_ROWS == 0
LR_LANE = LR_COL % LANES


def _rpe_thresholds():
    max_exact = RPE_BUCKETS // 2

    def bucket(n):
        if n < max_exact:
            return n
        v = math.log(n / max_exact) / math.log(RPE_MAX_DIST / max_exact) * (RPE_BUCKETS - max_exact)
        return min(max_exact + int(v), RPE_BUCKETS - 1)

    table = [bucket(n) for n in range(4 * RPE_MAX_DIST)]
    return [min(n for n in range(len(table)) if table[n] >= b)
            for b in range(max_exact + 1, RPE_BUCKETS)]


RPE_THRESHOLDS = _rpe_thresholds()
assert RPE_THRESHOLDS[-1] <= A_BLOCK


def _cparams(*sem):
    return pltpu.CompilerParams(dimension_semantics=sem, vmem_limit_bytes=VMEM_LIMIT)


def _rpe_bias_kernel(tbl_ref, bias_ref):
    h = pl.program_id(0)
    t = lax.broadcasted_iota(jnp.int32, (A_BLOCK, 2 * A_BLOCK), 0)
    c = lax.broadcasted_iota(jnp.int32, (A_BLOCK, 2 * A_BLOCK), 1)
    n = jnp.maximum(t - c + A_BLOCK, 0)
    max_exact = RPE_BUCKETS // 2
    large = jnp.full(n.shape, max_exact, jnp.int32)
    for thr in RPE_THRESHOLDS:
        large = large + (n >= thr).astype(jnp.int32)
    bucket = jnp.where(n < max_exact, n, large)
    bias = jnp.zeros(n.shape, F32)
    for b in range(RPE_BUCKETS):
        bias = jnp.where(bucket == b, tbl_ref[b, h], bias)
    bias = (bias - tbl_ref[RPE_BUCKETS - 1, h]) * LOG2E
    bias_ref[0] = jnp.where(c - A_BLOCK > t, NEG_INF, bias)


def rpe_bias(rpe_table):
    return pl.pallas_call(
        _rpe_bias_kernel,
        grid=(A_HEADS,),
        in_specs=[pl.BlockSpec(memory_space=pltpu.SMEM)],
        out_specs=pl.BlockSpec((1, A_BLOCK, 2 * A_BLOCK), lambda h: (h, 0, 0)),
        out_shape=jax.ShapeDtypeStruct((A_HEADS, A_BLOCK, 2 * A_BLOCK), F32),
        compiler_params=_cparams("arbitrary"),
        name="rpe_bias",
    )(rpe_table)


def _proj_kernel(x_hbm, g_ref, w_ref, o_ref, x_ref, xn_ref, x_sem, *, row_chunk):
    i, j = pl.program_id(0), pl.program_id(1)
    tm = x_ref.shape[0]

    def x_copy(tile):
        return pltpu.make_async_copy(x_hbm.at[pl.ds(tile * tm, tm), :], x_ref, x_sem)

    @pl.when(j == 0)
    def _():
        @pl.when(i == 0)
        def _():
            x_copy(0).start()
        x_copy(i).wait()

        def body(r, carry):
            rows = pl.ds(pl.multiple_of(r * row_chunk, row_chunk), row_chunk)
            x = x_ref[rows, :]
            ms = jnp.mean(x * x, axis=-1, keepdims=True)
            xn_ref[rows, :] = (x * lax.rsqrt(ms + EPS) * g_ref[...]).astype(BF16)
            return carry
        lax.fori_loop(0, tm // row_chunk, body, 0)

    @pl.when((j == 1) & (i + 1 < pl.num_programs(0)))
    def _():
        x_copy(i + 1).start()

    o_ref[...] = lax.dot_general(xn_ref[...], w_ref[...].astype(BF16), (((1,), (1,)), ((), ())),
                                 preferred_element_type=F32).astype(o_ref.dtype)


def _proj_src_row(j):
    u = BF16_ROWS
    return u * jnp.where(j < N_AB_TILES, j * (PROJ_TN // u),
                         jnp.where(j == N_AB_TILES, C_SRC // u,
                                   G_SRC // u + (j - N_AB_TILES - 1) * (PROJ_TN // u)))


def _proj_out_tile(j):
    return jnp.where(j < N_AB_TILES, j + N_G_TILES,
                     jnp.where(j == N_AB_TILES, N_G_TILES + N_AB_TILES, j - N_AB_TILES - 1))


def proj(x, g, w_t, layer, *, tm):
    m, k = x.shape
    tn = PROJ_TN
    n_tiles = N_AB_TILES + 1 + N_G_TILES
    return pl.pallas_call(
        functools.partial(_proj_kernel, row_chunk=128),
        grid=(m // tm, n_tiles),
        in_specs=[pl.BlockSpec(memory_space=pl.ANY),
                  pl.BlockSpec((1, k), lambda i, j: (0, 0)),
                  pl.BlockSpec((None, pl.Element(tn), pl.Element(k)),
                               lambda i, j: (layer, _proj_src_row(j), 0))],
        out_specs=pl.BlockSpec((tm, tn), lambda i, j: (i, _proj_out_tile(j))),
        out_shape=jax.ShapeDtypeStruct((m, PROJ_WIDTH), BF16),
        scratch_shapes=[pltpu.VMEM((tm, k), F32),
                        pltpu.VMEM((tm, k), BF16),
                        pltpu.SemaphoreType.DMA(())],
        compiler_params=_cparams("arbitrary", "arbitrary"),
        name="proj",
    )(x, g.reshape(1, k), w_t)


def _head_rms(x, g):
    ms = jnp.mean(x * x, axis=-1, keepdims=True)
    return x * lax.rsqrt(ms + EPS) * g


def _moba_stages(q_ref, k_ref, v_ref, qg_ref, kg_ref, bias_ref, o_ref,
                 qn_ref, qa_ref, ka_ref, va_ref, kmean_ref, s_ref, p_ref):
    seq, hd = q_ref.shape
    nb = seq // A_BLOCK
    rc = BF16_ROWS

    def prologue():
        qn = _head_rms(q_ref[...].astype(F32), qg_ref[...])
        qn_ref[...] = qn
        qa_ref[:, :hd] = (qn * (hd ** -0.5 * LOG2E)).astype(BF16)
        qa_ref[:, hd:] = jnp.zeros((seq, hd), BF16)
        kn = _head_rms(k_ref[...].astype(F32), kg_ref[...])
        ka_ref[:, :hd] = kn.astype(BF16)
        key_blk = lax.broadcasted_iota(jnp.int32, (seq, hd), 0) // A_BLOCK
        ka_ref[:, hd:] = jnp.where(key_blk == lax.broadcasted_iota(jnp.int32, (seq, hd), 1),
                                   1.0, 0.0).astype(BF16)
        va_ref[:, :hd] = v_ref[...]
        va_ref[:, hd:] = jnp.ones((seq, hd), BF16)
        for j in range(nb):
            kmean_ref[j:j + 1, :] = jnp.mean(kn[j * A_BLOCK:(j + 1) * A_BLOCK], axis=0, keepdims=True)

    def scores(i):
        rows = slice(i * A_BLOCK, (i + 1) * A_BLOCK)
        nk = (i + 1) * A_BLOCK
        if i > A_TOPK:
            blk = lax.broadcasted_iota(jnp.int32, (nb, A_BLOCK), 0)
            eye = jnp.where(lax.broadcasted_iota(jnp.int32, (nb, hd), 0)
                            == lax.broadcasted_iota(jnp.int32, (nb, hd), 1), 1.0, 0.0)
            g = lax.dot_general(kmean_ref[...], qn_ref[rows, :], (((1,), (1,)), ((), ())),
                                precision=lax.Precision.HIGHEST, preferred_element_type=F32)
            rank = jnp.zeros((nb, A_BLOCK), jnp.int32)
            for jp in range(i):
                gj = g[jp:jp + 1, :]
                beats = (gj > g) | ((gj == g) & (jp < blk))
                rank = rank + beats.astype(jnp.int32)
            neg = jnp.where((blk < i) & (rank >= A_TOPK), NEG_INF, 0.0)
            qa_ref[rows, hd:] = lax.dot_general(neg, eye, (((0,), (0,)), ((), ())),
                                                preferred_element_type=F32).astype(BF16)
        s_ref[i % 2, :, :nk] = lax.dot_general(qa_ref[rows, :], ka_ref[:nk, :], (((1,), (1,)), ((), ())),
                                               preferred_element_type=F32)

    def softmax(i):
        for c in range(A_BLOCK // rc):
            r = slice(c * rc, (c + 1) * rc)
            tiles = []
            for j in range(i + 1):
                sj = s_ref[i % 2, r, j * A_BLOCK:(j + 1) * A_BLOCK]
                if j == i:
                    sj = sj + bias_ref[0, r, A_BLOCK:]
                elif j == i - 1:
                    sj = sj + bias_ref[0, r, :A_BLOCK]
                tiles.append(sj)
            mt = tiles[0]
            for t in tiles[1:]:
                mt = jnp.maximum(mt, t)
            m = jnp.max(mt, axis=-1, keepdims=True)
            for j, t in enumerate(tiles):
                p_ref[i % 2, r, j * A_BLOCK:(j + 1) * A_BLOCK] = jnp.exp2(t - m).astype(BF16)

    def weighted_sum(i):
        rows = slice(i * A_BLOCK, (i + 1) * A_BLOCK)
        nk = (i + 1) * A_BLOCK
        o = jnp.dot(p_ref[i % 2, :, :nk], va_ref[:nk, :], preferred_element_type=F32)
        o_ref[rows, :] = (o[:, :hd] / o[:, hd:]).astype(o_ref.dtype)

    stages = [prologue, functools.partial(scores, 0)]
    for i in range(nb):
        if i + 1 < nb:
            stages.append(functools.partial(scores, i + 1))
        stages += [functools.partial(softmax, i), functools.partial(weighted_sum, i)]
    return stages


def _gla_stages(q_ref, k_ref, v_ref, r_ref, lr_ref, lrw_ref, lrb_ref, og_ref, o_ref,
                hml_ref, bc_ref, qd_ref, ki_ref, ke_ref, dec_ref, oi_ref, ss_ref, sb_ref):
    seq = q_ref.shape[0]
    ck = B_CHUNK
    nc = seq // ck
    grp = 4 * ck
    ng = seq // grp
    cpg = grp // ck
    qscale = B_DK ** -0.5
    nt = (((1,), (1,)), ((), ()))
    tn = (((0,), (0,)), ((), ()))

    def same_chunk_causal():
        ti = lax.broadcasted_iota(jnp.int32, (grp, grp), 0)
        si = lax.broadcasted_iota(jnp.int32, (grp, grp), 1)
        return (si <= ti) & (si // ck == ti // ck)

    def log_decay(g):
        rows = slice(g * grp, (g + 1) * grp)
        z = jnp.dot(lr_ref[rows, :], lrw_ref[...].astype(BF16), preferred_element_type=F32) + lrb_ref[...]
        la = (jnp.minimum(z, 0.0) - jnp.log(1.0 + jnp.exp(-jnp.abs(z)))) * (1.0 / B_GATE_NORM)
        hi = la.astype(BF16)
        r1 = la - hi.astype(F32)
        mid = r1.astype(BF16)
        hml_ref[rows, :B_DK] = hi
        hml_ref[rows, B_DK:2 * B_DK] = mid
        hml_ref[rows, 2 * B_DK:] = (r1 - mid.astype(F32)).astype(BF16)

    def cumsum(g):
        rows = slice(g * grp, (g + 1) * grp)
        tril_grp = jnp.where(same_chunk_causal(), 1.0, 0.0).astype(BF16)
        parts = jnp.dot(tril_grp, hml_ref[rows, :], preferred_element_type=F32)
        bc_ref[rows, :] = parts[:, :B_DK] + parts[:, B_DK:2 * B_DK] + parts[:, 2 * B_DK:]

    def decayed_qk(g):
        rows = slice(g * grp, (g + 1) * grp)
        bc = bc_ref[rows, :]
        bc3 = bc.reshape(cpg, ck, B_DK)
        bl3 = bc3[:, ck - 1:ck, :]
        q = q_ref[rows, :].astype(F32)
        k3 = k_ref[rows, :].astype(F32).reshape(cpg, ck, B_DK)
        qd_ref[rows, :] = (q * jnp.exp(bc) * qscale).astype(BF16)
        ki_ref[rows, :] = (k3 * jnp.exp(-bc3)).reshape(grp, B_DK).astype(BF16)
        ke_ref[rows, :] = (k3 * jnp.exp(bl3 - bc3)).reshape(grp, B_DK).astype(BF16)
        dec_ref[g * cpg:(g + 1) * cpg, :] = jnp.exp(bl3).reshape(cpg, B_DK)

    def intra(g):
        rows = slice(g * grp, (g + 1) * grp)
        a = lax.dot_general(qd_ref[rows, :], ki_ref[rows, :], nt, preferred_element_type=F32)
        a = jnp.where(same_chunk_causal(), a, 0.0).astype(BF16)
        oi_ref[rows, :] = jnp.dot(a, v_ref[rows, :], preferred_element_type=F32)

    def state_terms(g):
        for c in range(g * cpg, (g + 1) * cpg):
            rows = slice(c * ck, (c + 1) * ck)
            ss_ref[c] = lax.dot_general(v_ref[rows, :], ke_ref[rows, :], tn, preferred_element_type=F32)

    def scan():
        slab = B_DV // 4
        st = [jnp.zeros((slab, B_DK), F32) for _ in range(4)]
        for c in range(nc):
            dec = dec_ref[c:c + 1, :]
            for s in range(4):
                r = slice(s * slab, (s + 1) * slab)
                sb_ref[c, r, :] = st[s].astype(BF16)
                st[s] = st[s] * dec + ss_ref[c, r, :]

    def outputs(g):
        rows = slice(g * grp, (g + 1) * grp)
        inter = [lax.dot_general(qd_ref[c * ck:(c + 1) * ck, :], sb_ref[c], nt, preferred_element_type=F32)
                 for c in range(g * cpg, (g + 1) * cpg)]
        o = oi_ref[rows, :] + jnp.concatenate(inter, axis=0)
        ms = jnp.mean(o * o, axis=-1, keepdims=True)
        y = o * lax.rsqrt(ms + EPS) * og_ref[...]
        r = r_ref[rows, :].astype(F32)
        o_ref[rows, :] = (y * (r * jax.nn.sigmoid(r))).astype(o_ref.dtype)

    groups = [[functools.partial(fn, g) for g in range(ng)]
              for fn in (log_decay, cumsum, decayed_qk, intra, state_terms)]
    return groups + [[scan], [functools.partial(outputs, g) for g in range(ng)]]


N_MOBA_IN, N_MOBA_SCRATCH = 6, 7
N_GLA_IN = 8


def _mixer_ab_kernel(*refs):
    ins, (oa_ref, ob_ref), scratch = (refs[:N_MOBA_IN + N_GLA_IN],
                                      refs[N_MOBA_IN + N_GLA_IN:N_MOBA_IN + N_GLA_IN + 2],
                                      refs[N_MOBA_IN + N_GLA_IN + 2:])
    moba_stages = _moba_stages(*ins[:N_MOBA_IN], oa_ref, *scratch[:N_MOBA_SCRATCH])
    gla_steps = [fn for group in _gla_stages(*ins[N_MOBA_IN:], ob_ref, *scratch[N_MOBA_SCRATCH:])
                 for fn in group]
    per_slot = -(-len(gla_steps) // len(moba_stages))
    for n, stage in enumerate(moba_stages):
        stage()
        for fn in gla_steps[n * per_slot:(n + 1) * per_slot]:
            fn()


def mixer_ab(proj_out, qg, kg, bias, lr_w, lr_b, og, layer, *, bsz, seq):
    hd = HEAD_DIM
    assert A_HEADS == B_HEADS
    qa0 = AB0 // hd
    qb0 = (AB0 + 3 * A_WIDTH) // B_DK
    kb0 = qb0 + B_HEADS
    vb0 = (AB0 + 3 * A_WIDTH + 2 * B_KWIDTH) // B_DV
    rb0 = vb0 + B_HEADS
    lr0 = LR_COL // LANES
    nc = seq // B_CHUNK
    return pl.pallas_call(
        _mixer_ab_kernel,
        grid=(bsz, A_HEADS),
        in_specs=[pl.BlockSpec((seq, hd), lambda b, h: (b, qa0 + h)),
                  pl.BlockSpec((seq, hd), lambda b, h: (b, qa0 + A_HEADS + h)),
                  pl.BlockSpec((seq, hd), lambda b, h: (b, qa0 + 2 * A_HEADS + h)),
                  pl.BlockSpec((1, hd), lambda b, h: (0, 0)),
                  pl.BlockSpec((1, hd), lambda b, h: (0, 0)),
                  pl.BlockSpec((1, A_BLOCK, 2 * A_BLOCK), lambda b, h: (h, 0, 0)),
                  pl.BlockSpec((seq, B_DK), lambda b, h: (b, qb0 + h)),
                  pl.BlockSpec((seq, B_DK), lambda b, h: (b, kb0 + h)),
                  pl.BlockSpec((seq, B_DV), lambda b, h: (b, vb0 + h)),
                  pl.BlockSpec((seq, B_DV), lambda b, h: (b, rb0 + h)),
                  pl.BlockSpec((seq, LANES), lambda b, h: (b, lr0)),
                  pl.BlockSpec((None, LANES, B_DK), lambda b, h: (layer, 0, h)),
                  pl.BlockSpec((1, B_DK), lambda b, h: (0, h)),
                  pl.BlockSpec((1, B_DV), lambda b, h: (0, 0))],
        out_specs=[pl.BlockSpec((seq, hd), lambda b, h: (b, h)),
                   pl.BlockSpec((seq, B_DV), lambda b, h: (b, h))],
        out_shape=[jax.ShapeDtypeStruct((bsz * seq, A_WIDTH), BF16),
                   jax.ShapeDtypeStruct((bsz * seq, B_VWIDTH), BF16)],
        scratch_shapes=[pltpu.VMEM((seq, hd), F32),
                        pltpu.VMEM((seq, 2 * hd), BF16),
                        pltpu.VMEM((seq, 2 * hd), BF16),
                        pltpu.VMEM((seq, 2 * hd), BF16),
                        pltpu.VMEM((seq // A_BLOCK, hd), F32),
                        pltpu.VMEM((2, A_BLOCK, seq), F32),
                        pltpu.VMEM((2, A_BLOCK, seq), BF16),
                        pltpu.VMEM((seq, 3 * B_DK), BF16),
                        pltpu.VMEM((seq, B_DK), F32),
                        pltpu.VMEM((seq, B_DK), BF16),
                        pltpu.VMEM((seq, B_DK), BF16),
                        pltpu.VMEM((seq, B_DK), BF16),
                        pltpu.VMEM((nc, B_DK), F32),
                        pltpu.VMEM((seq, B_DV), F32),
                        pltpu.VMEM((nc, B_DV, B_DK), F32),
                        pltpu.VMEM((nc, B_DV, B_DK), BF16)],
        compiler_params=_cparams("parallel", "parallel"),
        name="mixer_ab",
    )(proj_out, proj_out, proj_out, qg.reshape(1, hd), kg.reshape(1, hd), bias,
      proj_out, proj_out, proj_out, proj_out, proj_out, lr_w, lr_b.reshape(1, B_KWIDTH),
      og.reshape(1, B_DV))


def _gelu(x):
    c = math.sqrt(2.0 / math.pi)
    return 0.5 * x * (1.0 + jnp.tanh(c * (x + 0.044715 * (x * x * x))))


def _sgu_kernel(u_ref, v_ref, lg_ref, lb_ref, w_ref, b_ref, o_ref):
    rows = u_ref.shape[0]
    ti = lax.broadcasted_iota(jnp.int32, (C_CHUNK, C_CHUNK), 0)
    si = lax.broadcasted_iota(jnp.int32, (C_CHUNK, C_CHUNK), 1)
    causal = si <= ti
    ws = [jnp.where(causal, w_ref[g], 0.0).astype(BF16) for g in range(C_GROUPS)]
    for n in range(rows // C_CHUNK):
        r = slice(n * C_CHUNK, (n + 1) * C_CHUNK)
        v = _gelu(v_ref[r, :].astype(F32))
        mu = jnp.mean(v, axis=-1, keepdims=True)
        vc = v - mu
        vn = vc * lax.rsqrt(jnp.mean(vc * vc, axis=-1, keepdims=True) + EPS)
        vn = (vn * lg_ref[...] + lb_ref[...]).astype(BF16)
        u = _gelu(u_ref[r, :].astype(F32))
        for g in range(C_GROUPS):
            cols = slice(g * C_GROUP_DIM, (g + 1) * C_GROUP_DIM)
            mixed = jnp.dot(ws[g], vn[:, cols], preferred_element_type=F32) + b_ref[g]
            o_ref[r, cols] = (u[:, cols] * mixed).astype(o_ref.dtype)


def sgu(proj_out, ln_g, ln_b, w_s, b_s, *, tm):
    m = proj_out.shape[0]
    u0 = U_COL // C_WIDTH
    return pl.pallas_call(
        _sgu_kernel,
        grid=(m // tm,),
        in_specs=[pl.BlockSpec((tm, C_WIDTH), lambda i: (i, u0)),
                  pl.BlockSpec((tm, C_WIDTH), lambda i: (i, u0 + 1)),
                  pl.BlockSpec((1, C_WIDTH), lambda i: (0, 0)),
                  pl.BlockSpec((1, C_WIDTH), lambda i: (0, 0)),
                  pl.BlockSpec((C_GROUPS, C_CHUNK, C_CHUNK), lambda i: (0, 0, 0)),
                  pl.BlockSpec((C_GROUPS, C_CHUNK, 1), lambda i: (0, 0, 0))],
        out_specs=pl.BlockSpec((tm, C_WIDTH), lambda i: (i, 0)),
        out_shape=jax.ShapeDtypeStruct((m, C_WIDTH), BF16),
        compiler_params=_cparams("parallel"),
        name="sgu",
    )(proj_out, proj_out, ln_g.reshape(1, C_WIDTH), ln_b.reshape(1, C_WIDTH), w_s,
      b_s.reshape(C_GROUPS, C_CHUNK, 1))


def _merge_kernel(ya_ref, yb_ref, yc_ref, ga_ref, gb_ref, gc_ref, x_ref,
                  wa_ref, wb_ref, wc_ref, wo_ref, g2_ref, o_ref, on_ref):
    def gated(g_ref, y_ref, w_ref):
        return (jax.nn.sigmoid(g_ref[...].astype(F32))
                * jnp.dot(y_ref[...], w_ref[...], preferred_element_type=F32))

    merged = gated(ga_ref, ya_ref, wa_ref) + gated(gb_ref, yb_ref, wb_ref) + gated(gc_ref, yc_ref, wc_ref)
    h = x_ref[...] + jnp.dot(merged.astype(BF16), wo_ref[...], preferred_element_type=F32)
    o_ref[...] = h
    ms = jnp.mean(h * h, axis=-1, keepdims=True)
    on_ref[...] = (h * lax.rsqrt(ms + EPS) * g2_ref[...]).astype(on_ref.dtype)


def merge(ya, yb, yc, proj_out, x, wa, wb, wc, wo, g2, layer, *, tm):
    m, d = x.shape
    resident = functools.partial(pl.BlockSpec, pipeline_mode=pl.Buffered(1))
    return pl.pallas_call(
        _merge_kernel,
        grid=(m // tm,),
        in_specs=[pl.BlockSpec((tm, A_WIDTH), lambda i: (i, 0)),
                  pl.BlockSpec((tm, B_VWIDTH), lambda i: (i, 0)),
                  pl.BlockSpec((tm, C_WIDTH), lambda i: (i, 0)),
                  pl.BlockSpec((tm, d), lambda i: (i, 0)),
                  pl.BlockSpec((tm, d), lambda i: (i, 1)),
                  pl.BlockSpec((tm, d), lambda i: (i, 2)),
                  pl.BlockSpec((tm, d), lambda i: (i, 0)),
                  resident((None, A_WIDTH, d), lambda i: (layer, 0, 0)),
                  resident((None, B_VWIDTH, d), lambda i: (layer, 0, 0)),
                  resident((None, C_WIDTH, d), lambda i: (layer, 0, 0)),
                  resident((None, d, d), lambda i: (layer, 0, 0)),
                  pl.BlockSpec((1, d), lambda i: (0, 0))],
        out_specs=[pl.BlockSpec((tm, d), lambda i: (i, 0)),
                   pl.BlockSpec((tm, d), lambda i: (i, 0))],
        out_shape=[jax.ShapeDtypeStruct((m, d), F32),
                   jax.ShapeDtypeStruct((m, d), BF16)],
        compiler_params=_cparams("parallel"),
        name="merge",
    )(ya, yb, yc, proj_out, proj_out, proj_out, x, wa, wb, wc, wo, g2.reshape(1, d))


def _mlp_kernel(h_hbm, hn_ref, w1_ref, w2_ref, o_ref, h_ref, h_sem):
    i, j = pl.program_id(0), pl.program_id(1)
    last = pl.num_programs(1) - 1
    tm = h_ref.shape[0]
    h_copy = pltpu.make_async_copy(h_hbm.at[pl.ds(i * tm, tm), :], h_ref, h_sem)

    def update():
        a = jnp.dot(hn_ref[...], w1_ref[...].astype(BF16), preferred_element_type=F32)
        a = jnp.square(jnp.maximum(a, 0.0)).astype(BF16)
        return jnp.dot(a, w2_ref[...].astype(BF16), preferred_element_type=F32)

    @pl.when(j == 0)
    def _():
        h_copy.start()
        o_ref[...] = update()

    @pl.when((j > 0) & (j < last))
    def _():
        o_ref[...] += update()

    @pl.when(j == last)
    def _():
        h_copy.wait()
        o_ref[...] = (o_ref[...] + h_ref[...]) + update()


def mlp(h, hn, w1, w2, layer, *, tm, tf):
    m, d = h.shape
    f = w1.shape[2]
    return pl.pallas_call(
        _mlp_kernel,
        grid=(m // tm, f // tf),
        in_specs=[pl.BlockSpec(memory_space=pl.ANY),
                  pl.BlockSpec((tm, d), lambda i, j: (i, 0)),
                  pl.BlockSpec((None, d, tf), lambda i, j: (layer, 0, j)),
                  pl.BlockSpec((None, tf, d), lambda i, j: (layer, j, 0))],
        out_specs=pl.BlockSpec((tm, d), lambda i, j: (i, 0)),
        out_shape=jax.ShapeDtypeStruct((m, d), F32),
        scratch_shapes=[pltpu.VMEM((tm, d), F32),
                        pltpu.SemaphoreType.DMA(())],
        compiler_params=_cparams("parallel", "arbitrary"),
        name="mlp",
    )(h, hn, w1, w2)


def kernel(x, rpe_table, norm1_g, w_in, q_norm_g, k_norm_g, gla_lr_w, gla_lr_b, gla_out_g,
           sg_ln_g, sg_ln_b, sg_w, sg_b, w_br_a, w_br_b, w_br_c, w_o, norm2_g, w_ff1, w_ff2):
    bsz, seq, d = x.shape
    assert seq % A_BLOCK == 0 and seq % B_CHUNK == 0 and seq % C_CHUNK == 0
    assert w_in.shape[2] == IN_WIDTH
    h = x.reshape(bsz * seq, d)
    bias = rpe_bias(rpe_table)
    w_t = jnp.swapaxes(w_in, 1, 2)
    wa, wb, wc, wo = (w.astype(BF16) for w in (w_br_a, w_br_b, w_br_c, w_o))
    lr_w = jnp.pad(gla_lr_w, ((0, 0), (LR_LANE, LANES - LR_LANE - B_LOWRANK), (0, 0)))
    for l in range(DEPTH):
        p = proj(h, norm1_g[l], w_t, l, tm=PROJ_TM)
        y_a, y_b = mixer_ab(p, q_norm_g[l], k_norm_g[l], bias, lr_w, gla_lr_b[l], gla_out_g[l], l,
                            bsz=bsz, seq=seq)
        y_c = sgu(p, sg_ln_g[l], sg_ln_b[l], sg_w[l], sg_b[l], tm=SGU_TM)
        h, hn = merge(y_a, y_b, y_c, p, h, wa, wb, wc, wo, norm2_g[l], l, tm=MERGE_TM)
        h = mlp(h, hn, w_ff1, w_ff2, l, tm=MLP_TM, tf=MLP_TF)
    return h.reshape(bsz, seq, d)
```

```python
import functools
import math

import jax
import jax.numpy as jnp
from jax import lax
from jax.experimental import pallas as pl
from jax.experimental.pallas import tpu as pltpu

D_MODEL = 2048
DEPTH = 2
HEAD_DIM = 128
A_HEADS = 4
A_WIDTH = A_HEADS * HEAD_DIM
A_BLOCK = 256
A_TOPK = 3
RPE_BUCKETS = 32
RPE_MAX_DIST = 128
B_HEADS = 4
B_DK = 128
B_DV = 256
B_KWIDTH = B_HEADS * B_DK
B_VWIDTH = B_HEADS * B_DV
B_LOWRANK = 16
B_GATE_NORM = 16.0
B_CHUNK = 64
C_GROUPS = 4
C_GROUP_DIM = 128
C_WIDTH = C_GROUPS * C_GROUP_DIM
C_CHUNK = 128
D_FF = 4 * D_MODEL
EPS = 1e-6
NEG_INF = -1e30
LOG2E = 1.4426950408889634

LANES = 128
BF16_ROWS = 16
VMEM_LIMIT = 56 * 1024 * 1024

F32 = jnp.float32
BF16 = jnp.bfloat16

PROJ_TM = 1024
MLP_TM = 1024
MLP_TF = 1024
MERGE_TM = 256
SGU_TM = 1024

PROJ_TN = 1536
AB_WIDTH = 3 * A_WIDTH + 2 * B_KWIDTH + 2 * B_VWIDTH
G_WIDTH = 3 * D_MODEL
IN_WIDTH = AB_WIDTH + B_LOWRANK + 2 * C_WIDTH + G_WIDTH
TAIL_PAD = PROJ_TN - (B_LOWRANK + 2 * C_WIDTH)
AB0 = G_WIDTH
T0 = AB0 + AB_WIDTH
LR_COL = T0 + TAIL_PAD
U_COL = LR_COL + B_LOWRANK
PROJ_WIDTH = T0 + PROJ_TN
N_AB_TILES = AB_WIDTH // PROJ_TN
N_G_TILES = G_WIDTH // PROJ_TN
C_SRC = AB_WIDTH - TAIL_PAD
G_SRC = IN_WIDTH - G_WIDTH
assert AB_WIDTH % PROJ_TN == 0 and G_WIDTH % PROJ_TN == 0 and U_COL % C_WIDTH == 0
assert C_SRC % BF16_ROWS == 0 and G_SRC % BF16_ROWS == 0
LR_LANE = LR_COL % LANES


def _rpe_thresholds():
    max_exact = RPE_BUCKETS // 2

    def bucket(n):
        if n < max_exact:
            return n
        v = math.log(n / max_exact) / math.log(RPE_MAX_DIST / max_exact) * (RPE_BUCKETS - max_exact)
        return min(max_exact + int(v), RPE_BUCKETS - 1)

    table = [bucket(n) for n in range(4 * RPE_MAX_DIST)]
    return [min(n for n in range(len(table)) if table[n] >= b)
            for b in range(max_exact + 1, RPE_BUCKETS)]


RPE_THRESHOLDS = _rpe_thresholds()
assert RPE_THRESHOLDS[-1] <= A_BLOCK


def _cparams(*sem):
    return pltpu.CompilerParams(dimension_semantics=sem, vmem_limit_bytes=VMEM_LIMIT)


def _rpe_bias_kernel(tbl_ref, bias_ref):
    h = pl.program_id(0)
    t = lax.broadcasted_iota(jnp.int32, (A_BLOCK, 2 * A_BLOCK), 0)
    c = lax.broadcasted_iota(jnp.int32, (A_BLOCK, 2 * A_BLOCK), 1)
    n = jnp.maximum(t - c + A_BLOCK, 0)
    max_exact = RPE_BUCKETS // 2
    large = jnp.full(n.shape, max_exact, jnp.int32)
    for thr in RPE_THRESHOLDS:
        large = large + (n >= thr).astype(jnp.int32)
    bucket = jnp.where(n < max_exact, n, large)
    bias = jnp.zeros(n.shape, F32)
    for b in range(RPE_BUCKETS):
        bias = jnp.where(bucket == b, tbl_ref[b, h], bias)
    bias = (bias - tbl_ref[RPE_BUCKETS - 1, h]) * LOG2E
    bias_ref[0] = jnp.where(c - A_BLOCK > t, NEG_INF, bias)


def rpe_bias(rpe_table):
    return pl.pallas_call(
        _rpe_bias_kernel,
        grid=(A_HEADS,),
        in_specs=[pl.BlockSpec(memory_space=pltpu.SMEM)],
        out_specs=pl.BlockSpec((1, A_BLOCK, 2 * A_BLOCK), lambda h: (h, 0, 0)),
        out_shape=jax.ShapeDtypeStruct((A_HEADS, A_BLOCK, 2 * A_BLOCK), F32),
        compiler_params=_cparams("arbitrary"),
        name="rpe_bias",
    )(rpe_table)


def _cast_specs(to_cast, layer, n_steps, step_of):
    in_specs, out_specs, shapes = [], [], []
    for w in to_cast:
        _, rows, cols = w.shape
        slice_rows = max(rows // n_steps, BF16_ROWS)
        n_slices = rows // slice_rows
        assert rows % slice_rows == 0 and slice_rows % BF16_ROWS == 0 and n_slices <= n_steps

        def slice_of(*idx, n_slices=n_slices):
            return jnp.minimum(step_of(*idx), n_slices - 1)

        in_specs.append(pl.BlockSpec((None, slice_rows, cols), lambda *idx, f=slice_of: (layer, f(*idx), 0)))
        out_specs.append(pl.BlockSpec((slice_rows, cols), lambda *idx, f=slice_of: (f(*idx), 0)))
        shapes.append(jax.ShapeDtypeStruct((rows, cols), BF16))
    return in_specs, out_specs, shapes


def _cast_all(srcs, dsts):
    for src, dst in zip(srcs, dsts):
        dst[...] = src[...].astype(dst.dtype)


def _proj_kernel(*refs, n_cast, row_chunk):
    x_hbm, g_ref, w_ref = refs[:3]
    cast_in = refs[3:3 + n_cast]
    o_ref = refs[3 + n_cast]
    cast_out = refs[4 + n_cast:4 + 2 * n_cast]
    x_ref, xn_ref, x_sem = refs[4 + 2 * n_cast:]
    i, j = pl.program_id(0), pl.program_id(1)
    tm = x_ref.shape[0]


    def x_copy(tile):
        return pltpu.make_async_copy(x_hbm.at[pl.ds(tile * tm, tm), :], x_ref, x_sem)

    @pl.when(j == 0)
    def _():
        @pl.when(i == 0)
        def _():
            x_copy(0).start()
        x_copy(i).wait()

        def body(r, carry):
            rows = pl.ds(pl.multiple_of(r * row_chunk, row_chunk), row_chunk)
            x = x_ref[rows, :]
            ms = jnp.mean(x * x, axis=-1, keepdims=True)
            xn_ref[rows, :] = (x * lax.rsqrt(ms + EPS) * g_ref[...]).astype(BF16)
            return carry
        lax.fori_loop(0, tm // row_chunk, body, 0)

    @pl.when((j == 1) & (i + 1 < pl.num_programs(0)))
    def _():
        x_copy(i + 1).start()

    o_ref[...] = lax.dot_general(xn_ref[...], w_ref[...].astype(BF16), (((1,), (1,)), ((), ())),
                                 preferred_element_type=F32).astype(o_ref.dtype)
    _cast_all(cast_in, cast_out)


def _proj_src_row(j):
    u = BF16_ROWS
    return u * jnp.where(j < N_AB_TILES, j * (PROJ_TN // u),
                         jnp.where(j == N_AB_TILES, C_SRC // u,
                                   G_SRC // u + (j - N_AB_TILES - 1) * (PROJ_TN // u)))


def _proj_out_tile(j):
    return jnp.where(j < N_AB_TILES, j + N_G_TILES,
                     jnp.where(j == N_AB_TILES, N_G_TILES + N_AB_TILES, j - N_AB_TILES - 1))


def proj(x, g, w_t, layer, to_cast, *, tm):
    m, k = x.shape
    tn = PROJ_TN
    n_tiles = N_AB_TILES + 1 + N_G_TILES
    cast_in_specs, cast_out_specs, cast_shapes = _cast_specs(
        to_cast, layer, (m // tm) * n_tiles, lambda i, j: i * n_tiles + j)
    return pl.pallas_call(
        functools.partial(_proj_kernel, n_cast=len(to_cast), row_chunk=128),
        grid=(m // tm, n_tiles),
        in_specs=[pl.BlockSpec(memory_space=pl.ANY),
                  pl.BlockSpec((1, k), lambda i, j: (0, 0)),
                  pl.BlockSpec((None, pl.Element(tn), pl.Element(k)),
                               lambda i, j: (layer, _proj_src_row(j), 0))] + cast_in_specs,
        out_specs=[pl.BlockSpec((tm, tn), lambda i, j: (i, _proj_out_tile(j)))] + cast_out_specs,
        out_shape=[jax.ShapeDtypeStruct((m, PROJ_WIDTH), BF16)] + cast_shapes,
        scratch_shapes=[pltpu.VMEM((tm, k), F32),
                        pltpu.VMEM((tm, k), BF16),
                        pltpu.SemaphoreType.DMA(())],
        compiler_params=_cparams("arbitrary", "arbitrary"),
        name="proj",
    )(x, g.reshape(1, k), w_t, *to_cast)


def _head_rms(x, g):
    ms = jnp.mean(x * x, axis=-1, keepdims=True)
    return x * lax.rsqrt(ms + EPS) * g


def _moba_stages(q_ref, k_ref, v_ref, qg_ref, kg_ref, bias_ref, o_ref,
                 qn_ref, qa_ref, ka_ref, va_ref, kmean_ref, s_ref, p_ref):
    seq, hd = q_ref.shape
    nb = seq // A_BLOCK
    rc = BF16_ROWS

    def prologue():
        qn = _head_rms(q_ref[...].astype(F32), qg_ref[...])
        qn_ref[...] = qn
        qa_ref[:, :hd] = (qn * (hd ** -0.5 * LOG2E)).astype(BF16)
        qa_ref[:, hd:] = jnp.zeros((seq, hd), BF16)
        kn = _head_rms(k_ref[...].astype(F32), kg_ref[...])
        ka_ref[:, :hd] = kn.astype(BF16)
        key_blk = lax.broadcasted_iota(jnp.int32, (seq, hd), 0) // A_BLOCK
        ka_ref[:, hd:] = jnp.where(key_blk == lax.broadcasted_iota(jnp.int32, (seq, hd), 1),
                                   1.0, 0.0).astype(BF16)
        va_ref[:, :hd] = v_ref[...]
        va_ref[:, hd:] = jnp.ones((seq, hd), BF16)
        for j in range(nb):
            kmean_ref[j:j + 1, :] = jnp.mean(kn[j * A_BLOCK:(j + 1) * A_BLOCK], axis=0, keepdims=True)

    def scores(i):
        rows = slice(i * A_BLOCK, (i + 1) * A_BLOCK)
        nk = (i + 1) * A_BLOCK
        if i > A_TOPK:
            blk = lax.broadcasted_iota(jnp.int32, (nb, A_BLOCK), 0)
            eye = jnp.where(lax.broadcasted_iota(jnp.int32, (nb, hd), 0)
                            == lax.broadcasted_iota(jnp.int32, (nb, hd), 1), 1.0, 0.0)
            g = lax.dot_general(kmean_ref[...], qn_ref[rows, :], (((1,), (1,)), ((), ())),
                                precision=lax.Precision.HIGHEST, preferred_element_type=F32)
            rank = jnp.zeros((nb, A_BLOCK), jnp.int32)
            for jp in range(i):
                gj = g[jp:jp + 1, :]
                beats = (gj > g) | ((gj == g) & (jp < blk))
                rank = rank + beats.astype(jnp.int32)
            neg = jnp.where((blk < i) & (rank >= A_TOPK), NEG_INF, 0.0)
            qa_ref[rows, hd:] = lax.dot_general(neg, eye, (((0,), (0,)), ((), ())),
                                                preferred_element_type=F32).astype(BF16)
        s_ref[i % 2, :, :nk] = lax.dot_general(qa_ref[rows, :], ka_ref[:nk, :], (((1,), (1,)), ((), ())),
                                               preferred_element_type=F32)

    def softmax(i):
        for c in range(A_BLOCK // rc):
            r = slice(c * rc, (c + 1) * rc)
            tiles = []
            for j in range(i + 1):
                sj = s_ref[i % 2, r, j * A_BLOCK:(j + 1) * A_BLOCK]
                if j == i:
                    sj = sj + bias_ref[0, r, A_BLOCK:]
                elif j == i - 1:
                    sj = sj + bias_ref[0, r, :A_BLOCK]
                tiles.append(sj)
            mt = tiles[0]
            for t in tiles[1:]:
                mt = jnp.maximum(mt, t)
            m = jnp.max(mt, axis=-1, keepdims=True)
            for j, t in enumerate(tiles):
                p_ref[i % 2, r, j * A_BLOCK:(j + 1) * A_BLOCK] = jnp.exp2(t - m).astype(BF16)

    def weighted_sum(i):
        rows = slice(i * A_BLOCK, (i + 1) * A_BLOCK)
        nk = (i + 1) * A_BLOCK
        o = jnp.dot(p_ref[i % 2, :, :nk], va_ref[:nk, :], preferred_element_type=F32)
        o_ref[rows, :] = (o[:, :hd] / o[:, hd:]).astype(o_ref.dtype)

    stages = [prologue, functools.partial(scores, 0)]
    for i in range(nb):
        if i + 1 < nb:
            stages.append(functools.partial(scores, i + 1))
        stages += [functools.partial(softmax, i), functools.partial(weighted_sum, i)]
    return stages


def _gla_stages(q_ref, k_ref, v_ref, r_ref, lr_ref, lrw_ref, lrb_ref, og_ref, o_ref,
                hml_ref, bc_ref, qd_ref, ki_ref, ke_ref, dec_ref, oi_ref, ss_ref, sb_ref):
    seq = q_ref.shape[0]
    ck = B_CHUNK
    nc = seq // ck
    grp = 4 * ck
    ng = seq // grp
    cpg = grp // ck
    qscale = B_DK ** -0.5
    nt = (((1,), (1,)), ((), ()))
    tn = (((0,), (0,)), ((), ()))

    def same_chunk_causal():
        ti = lax.broadcasted_iota(jnp.int32, (grp, grp), 0)
        si = lax.broadcasted_iota(jnp.int32, (grp, grp), 1)
        return (si <= ti) & (si // ck == ti // ck)

    def log_decay(g):
        rows = slice(g * grp, (g + 1) * grp)
        z = jnp.dot(lr_ref[rows, :], lrw_ref[...].astype(BF16), preferred_element_type=F32) + lrb_ref[...]
        la = (jnp.minimum(z, 0.0) - jnp.log(1.0 + jnp.exp(-jnp.abs(z)))) * (1.0 / B_GATE_NORM)
        hi = la.astype(BF16)
        r1 = la - hi.astype(F32)
        mid = r1.astype(BF16)
        hml_ref[rows, :B_DK] = hi
        hml_ref[rows, B_DK:2 * B_DK] = mid
        hml_ref[rows, 2 * B_DK:] = (r1 - mid.astype(F32)).astype(BF16)

    def cumsum(g):
        rows = slice(g * grp, (g + 1) * grp)
        tril_grp = jnp.where(same_chunk_causal(), 1.0, 0.0).astype(BF16)
        parts = jnp.dot(tril_grp, hml_ref[rows, :], preferred_element_type=F32)
        bc_ref[rows, :] = parts[:, :B_DK] + parts[:, B_DK:2 * B_DK] + parts[:, 2 * B_DK:]

    def decayed_qk(g):
        rows = slice(g * grp, (g + 1) * grp)
        bc = bc_ref[rows, :]
        bc3 = bc.reshape(cpg, ck, B_DK)
        bl3 = bc3[:, ck - 1:ck, :]
        q = q_ref[rows, :].astype(F32)
        k3 = k_ref[rows, :].astype(F32).reshape(cpg, ck, B_DK)
        qd_ref[rows, :] = (q * jnp.exp(bc) * qscale).astype(BF16)
        ki_ref[rows, :] = (k3 * jnp.exp(-bc3)).reshape(grp, B_DK).astype(BF16)
        ke_ref[rows, :] = (k3 * jnp.exp(bl3 - bc3)).reshape(grp, B_DK).astype(BF16)
        dec_ref[g * cpg:(g + 1) * cpg, :] = jnp.exp(bl3).reshape(cpg, B_DK)

    def intra(g):
        rows = slice(g * grp, (g + 1) * grp)
        a = lax.dot_general(qd_ref[rows, :], ki_ref[rows, :], nt, preferred_element_type=F32)
        a = jnp.where(same_chunk_causal(), a, 0.0).astype(BF16)
        oi_ref[rows, :] = jnp.dot(a, v_ref[rows, :], preferred_element_type=F32)

    def state_terms(g):
        for c in range(g * cpg, (g + 1) * cpg):
            rows = slice(c * ck, (c + 1) * ck)
            ss_ref[c] = lax.dot_general(v_ref[rows, :], ke_ref[rows, :], tn, preferred_element_type=F32)

    def scan():
        slab = B_DV // 4
        st = [jnp.zeros((slab, B_DK), F32) for _ in range(4)]
        for c in range(nc):
            dec = dec_ref[c:c + 1, :]
            for s in range(4):
                r = slice(s * slab, (s + 1) * slab)
                sb_ref[c, r, :] = st[s].astype(BF16)
                st[s] = st[s] * dec + ss_ref[c, r, :]

    def outputs(g):
        rows = slice(g * grp, (g + 1) * grp)
        inter = [lax.dot_general(qd_ref[c * ck:(c + 1) * ck, :], sb_ref[c], nt, preferred_element_type=F32)
                 for c in range(g * cpg, (g + 1) * cpg)]
        o = oi_ref[rows, :] + jnp.concatenate(inter, axis=0)
        ms = jnp.mean(o * o, axis=-1, keepdims=True)
        y = o * lax.rsqrt(ms + EPS) * og_ref[...]
        r = r_ref[rows, :].astype(F32)
        o_ref[rows, :] = (y * (r * jax.nn.sigmoid(r))).astype(o_ref.dtype)

    groups = [[functools.partial(fn, g) for g in range(ng)]
              for fn in (log_decay, cumsum, decayed_qk, intra, state_terms)]
    return groups + [[scan], [functools.partial(outputs, g) for g in range(ng)]]


N_MOBA_IN, N_MOBA_SCRATCH = 6, 7
N_GLA_IN = 8


def _mixer_ab_kernel(*refs):
    ins, (oa_ref, ob_ref), scratch = (refs[:N_MOBA_IN + N_GLA_IN],
                                      refs[N_MOBA_IN + N_GLA_IN:N_MOBA_IN + N_GLA_IN + 2],
                                      refs[N_MOBA_IN + N_GLA_IN + 2:])
    moba_stages = _moba_stages(*ins[:N_MOBA_IN], oa_ref, *scratch[:N_MOBA_SCRATCH])
    gla_steps = [fn for group in _gla_stages(*ins[N_MOBA_IN:], ob_ref, *scratch[N_MOBA_SCRATCH:])
                 for fn in group]
    per_slot = -(-len(gla_steps) // len(moba_stages))
    for n, stage in enumerate(moba_stages):
        stage()
        for fn in gla_steps[n * per_slot:(n + 1) * per_slot]:
            fn()


def mixer_ab(proj_out, qg, kg, bias, lr_w, lr_b, og, layer, *, bsz, seq):
    hd = HEAD_DIM
    assert A_HEADS == B_HEADS
    qa0 = AB0 // hd
    qb0 = (AB0 + 3 * A_WIDTH) // B_DK
    kb0 = qb0 + B_HEADS
    vb0 = (AB0 + 3 * A_WIDTH + 2 * B_KWIDTH) // B_DV
    rb0 = vb0 + B_HEADS
    lr0 = LR_COL // LANES
    nc = seq // B_CHUNK
    return pl.pallas_call(
        _mixer_ab_kernel,
        grid=(bsz, A_HEADS),
        in_specs=[pl.BlockSpec((seq, hd), lambda b, h: (b, qa0 + h)),
                  pl.BlockSpec((seq, hd), lambda b, h: (b, qa0 + A_HEADS + h)),
                  pl.BlockSpec((seq, hd), lambda b, h: (b, qa0 + 2 * A_HEADS + h)),
                  pl.BlockSpec((1, hd), lambda b, h: (0, 0)),
                  pl.BlockSpec((1, hd), lambda b, h: (0, 0)),
                  pl.BlockSpec((1, A_BLOCK, 2 * A_BLOCK), lambda b, h: (h, 0, 0)),
                  pl.BlockSpec((seq, B_DK), lambda b, h: (b, qb0 + h)),
                  pl.BlockSpec((seq, B_DK), lambda b, h: (b, kb0 + h)),
                  pl.BlockSpec((seq, B_DV), lambda b, h: (b, vb0 + h)),
                  pl.BlockSpec((seq, B_DV), lambda b, h: (b, rb0 + h)),
                  pl.BlockSpec((seq, LANES), lambda b, h: (b, lr0)),
                  pl.BlockSpec((None, LANES, B_DK), lambda b, h: (layer, 0, h)),
                  pl.BlockSpec((1, B_DK), lambda b, h: (0, h)),
                  pl.BlockSpec((1, B_DV), lambda b, h: (0, 0))],
        out_specs=[pl.BlockSpec((seq, hd), lambda b, h: (b, h)),
                   pl.BlockSpec((seq, B_DV), lambda b, h: (b, h))],
        out_shape=[jax.ShapeDtypeStruct((bsz * seq, A_WIDTH), BF16),
                   jax.ShapeDtypeStruct((bsz * seq, B_VWIDTH), BF16)],
        scratch_shapes=[pltpu.VMEM((seq, hd), F32),
                        pltpu.VMEM((seq, 2 * hd), BF16),
                        pltpu.VMEM((seq, 2 * hd), BF16),
                        pltpu.VMEM((seq, 2 * hd), BF16),
                        pltpu.VMEM((seq // A_BLOCK, hd), F32),
                        pltpu.VMEM((2, A_BLOCK, seq), F32),
                        pltpu.VMEM((2, A_BLOCK, seq), BF16),
                        pltpu.VMEM((seq, 3 * B_DK), BF16),
                        pltpu.VMEM((seq, B_DK), F32),
                        pltpu.VMEM((seq, B_DK), BF16),
                        pltpu.VMEM((seq, B_DK), BF16),
                        pltpu.VMEM((seq, B_DK), BF16),
                        pltpu.VMEM((nc, B_DK), F32),
                        pltpu.VMEM((seq, B_DV), F32),
                        pltpu.VMEM((nc, B_DV, B_DK), F32),
                        pltpu.VMEM((nc, B_DV, B_DK), BF16)],
        compiler_params=_cparams("parallel", "parallel"),
        name="mixer_ab",
    )(proj_out, proj_out, proj_out, qg.reshape(1, hd), kg.reshape(1, hd), bias,
      proj_out, proj_out, proj_out, proj_out, proj_out, lr_w, lr_b.reshape(1, B_KWIDTH),
      og.reshape(1, B_DV))


def _gelu(x):
    c = math.sqrt(2.0 / math.pi)
    return 0.5 * x * (1.0 + jnp.tanh(c * (x + 0.044715 * (x * x * x))))


def _sgu_kernel(*refs, n_cast):
    u_ref, v_ref, lg_ref, lb_ref, w_ref, b_ref = refs[:6]
    o_ref = refs[6 + n_cast]
    _cast_all(refs[6:6 + n_cast], refs[7 + n_cast:])
    rows = u_ref.shape[0]
    ti = lax.broadcasted_iota(jnp.int32, (C_CHUNK, C_CHUNK), 0)
    si = lax.broadcasted_iota(jnp.int32, (C_CHUNK, C_CHUNK), 1)
    causal = si <= ti
    ws = [jnp.where(causal, w_ref[g], 0.0).astype(BF16) for g in range(C_GROUPS)]
    for n in range(rows // C_CHUNK):
        r = slice(n * C_CHUNK, (n + 1) * C_CHUNK)
        v = _gelu(v_ref[r, :].astype(F32))
        mu = jnp.mean(v, axis=-1, keepdims=True)
        vc = v - mu
        vn = vc * lax.rsqrt(jnp.mean(vc * vc, axis=-1, keepdims=True) + EPS)
        vn = (vn * lg_ref[...] + lb_ref[...]).astype(BF16)
        u = _gelu(u_ref[r, :].astype(F32))
        for g in range(C_GROUPS):
            cols = slice(g * C_GROUP_DIM, (g + 1) * C_GROUP_DIM)
            mixed = jnp.dot(ws[g], vn[:, cols], preferred_element_type=F32) + b_ref[g]
            o_ref[r, cols] = (u[:, cols] * mixed).astype(o_ref.dtype)


def sgu(proj_out, ln_g, ln_b, w_s, b_s, layer, to_cast, *, tm):
    m = proj_out.shape[0]
    u0 = U_COL // C_WIDTH
    cast_in_specs, cast_out_specs, cast_shapes = _cast_specs(to_cast, layer, m // tm, lambda i: i)
    return pl.pallas_call(
        functools.partial(_sgu_kernel, n_cast=len(to_cast)),
        grid=(m // tm,),
        in_specs=[pl.BlockSpec((tm, C_WIDTH), lambda i: (i, u0)),
                  pl.BlockSpec((tm, C_WIDTH), lambda i: (i, u0 + 1)),
                  pl.BlockSpec((1, C_WIDTH), lambda i: (0, 0)),
                  pl.BlockSpec((1, C_WIDTH), lambda i: (0, 0)),
                  pl.BlockSpec((C_GROUPS, C_CHUNK, C_CHUNK), lambda i: (0, 0, 0)),
                  pl.BlockSpec((C_GROUPS, C_CHUNK, 1), lambda i: (0, 0, 0))] + cast_in_specs,
        out_specs=[pl.BlockSpec((tm, C_WIDTH), lambda i: (i, 0))] + cast_out_specs,
        out_shape=[jax.ShapeDtypeStruct((m, C_WIDTH), BF16)] + cast_shapes,
        compiler_params=_cparams("parallel"),
        name="sgu",
    )(proj_out, proj_out, ln_g.reshape(1, C_WIDTH), ln_b.reshape(1, C_WIDTH), w_s,
      b_s.reshape(C_GROUPS, C_CHUNK, 1), *to_cast)


def _merge_kernel(ya_ref, yb_ref, yc_ref, ga_ref, gb_ref, gc_ref, x_ref,
                  wa_ref, wb_ref, wc_ref, wo_ref, g2_ref, o_ref, on_ref):
    def gated(g_ref, y_ref, w_ref):
        return (jax.nn.sigmoid(g_ref[...].astype(F32))
                * jnp.dot(y_ref[...], w_ref[...], preferred_element_type=F32))

    merged = gated(ga_ref, ya_ref, wa_ref) + gated(gb_ref, yb_ref, wb_ref) + gated(gc_ref, yc_ref, wc_ref)
    h = x_ref[...] + jnp.dot(merged.astype(BF16), wo_ref[...], preferred_element_type=F32)
    o_ref[...] = h
    ms = jnp.mean(h * h, axis=-1, keepdims=True)
    on_ref[...] = (h * lax.rsqrt(ms + EPS) * g2_ref[...]).astype(on_ref.dtype)


def merge(ya, yb, yc, proj_out, x, wa, wb, wc, wo, g2, *, tm):
    m, d = x.shape
    resident = functools.partial(pl.BlockSpec, pipeline_mode=pl.Buffered(1))
    return pl.pallas_call(
        _merge_kernel,
        grid=(m // tm,),
        in_specs=[pl.BlockSpec((tm, A_WIDTH), lambda i: (i, 0)),
                  pl.BlockSpec((tm, B_VWIDTH), lambda i: (i, 0)),
                  pl.BlockSpec((tm, C_WIDTH), lambda i: (i, 0)),
                  pl.BlockSpec((tm, d), lambda i: (i, 0)),
                  pl.BlockSpec((tm, d), lambda i: (i, 1)),
                  pl.BlockSpec((tm, d), lambda i: (i, 2)),
                  pl.BlockSpec((tm, d), lambda i: (i, 0)),
                  resident((A_WIDTH, d), lambda i: (0, 0)),
                  resident((B_VWIDTH, d), lambda i: (0, 0)),
                  resident((C_WIDTH, d), lambda i: (0, 0)),
                  resident((d, d), lambda i: (0, 0)),
                  pl.BlockSpec((1, d), lambda i: (0, 0))],
        out_specs=[pl.BlockSpec((tm, d), lambda i: (i, 0)),
                   pl.BlockSpec((tm, d), lambda i: (i, 0))],
        out_shape=[jax.ShapeDtypeStruct((m, d), F32),
                   jax.ShapeDtypeStruct((m, d), BF16)],
        compiler_params=_cparams("parallel"),
        name="merge",
    )(ya, yb, yc, proj_out, proj_out, proj_out, x, wa, wb, wc, wo, g2.reshape(1, d))


def _mlp_kernel(h_hbm, hn_ref, w1_ref, w2_ref, o_ref, h_ref, h_sem):
    i, j = pl.program_id(0), pl.program_id(1)
    last = pl.num_programs(1) - 1
    tm = h_ref.shape[0]
    h_copy = pltpu.make_async_copy(h_hbm.at[pl.ds(i * tm, tm), :], h_ref, h_sem)

    def update():
        a = jnp.dot(hn_ref[...], w1_ref[...], preferred_element_type=F32)
        a = jnp.square(jnp.maximum(a, 0.0)).astype(BF16)
        return jnp.dot(a, w2_ref[...], preferred_element_type=F32)

    @pl.when(j == 0)
    def _():
        h_copy.start()
        o_ref[...] = update()

    @pl.when((j > 0) & (j < last))
    def _():
        o_ref[...] += update()

    @pl.when(j == last)
    def _():
        h_copy.wait()
        o_ref[...] = (o_ref[...] + h_ref[...]) + update()


def mlp(h, hn, w1, w2, *, tm, tf):
    m, d = h.shape
    f = w1.shape[1]
    return pl.pallas_call(
        _mlp_kernel,
        grid=(m // tm, f // tf),
        in_specs=[pl.BlockSpec(memory_space=pl.ANY),
                  pl.BlockSpec((tm, d), lambda i, j: (i, 0)),
                  pl.BlockSpec((d, tf), lambda i, j: (0, j)),
                  pl.BlockSpec((tf, d), lambda i, j: (j, 0))],
        out_specs=pl.BlockSpec((tm, d), lambda i, j: (i, 0)),
        out_shape=jax.ShapeDtypeStruct((m, d), F32),
        scratch_shapes=[pltpu.VMEM((tm, d), F32),
                        pltpu.SemaphoreType.DMA(())],
        compiler_params=_cparams("parallel", "arbitrary"),
        name="mlp",
    )(h, hn, w1, w2)


def kernel(x, rpe_table, norm1_g, w_in, q_norm_g, k_norm_g, gla_lr_w, gla_lr_b, gla_out_g,
           sg_ln_g, sg_ln_b, sg_w, sg_b, w_br_a, w_br_b, w_br_c, w_o, norm2_g, w_ff1, w_ff2):
    bsz, seq, d = x.shape
    assert seq % A_BLOCK == 0 and seq % B_CHUNK == 0 and seq % C_CHUNK == 0
    assert w_in.shape[2] == IN_WIDTH
    h = x.reshape(bsz * seq, d)
    bias = rpe_bias(rpe_table)
    w_t = jnp.swapaxes(w_in, 1, 2)
    lr_w = jnp.pad(gla_lr_w, ((0, 0), (LR_LANE, LANES - LR_LANE - B_LOWRANK), (0, 0)))
    for l in range(DEPTH):
        p, w1, w2 = proj(h, norm1_g[l], w_t, l, (w_ff1, w_ff2), tm=PROJ_TM)
        y_a, y_b = mixer_ab(p, q_norm_g[l], k_norm_g[l], bias, lr_w, gla_lr_b[l], gla_out_g[l], l,
                            bsz=bsz, seq=seq)
        y_c, wa, wb, wc, wo = sgu(p, sg_ln_g[l], sg_ln_b[l], sg_w[l], sg_b[l], l,
                                  (w_br_a, w_br_b, w_br_c, w_o), tm=SGU_TM)
        h, hn = merge(y_a, y_b, y_c, p, h, wa, wb, wc, wo, norm2_g[l], tm=MERGE_TM)
        h = mlp(h, hn, w1, w2, tm=MLP_TM, tf=MLP_TF)
    return h.reshape(bsz, seq, d)
```

```python
import functools
import math

import jax
import jax.numpy as jnp
from jax import lax
from jax.experimental import pallas as pl
from jax.experimental.pallas import tpu as pltpu

D_MODEL = 2048
DEPTH = 2
HEAD_DIM = 128
A_HEADS = 4
A_WIDTH = A_HEADS * HEAD_DIM
A_BLOCK = 256
A_TOPK = 3
RPE_BUCKETS = 32
RPE_MAX_DIST = 128
B_HEADS = 4
B_DK = 128
B_DV = 256
B_KWIDTH = B_HEADS * B_DK
B_VWIDTH = B_HEADS * B_DV
B_LOWRANK = 16
B_GATE_NORM = 16.0
B_CHUNK = 64
C_GROUPS = 4
C_GROUP_DIM = 128
C_WIDTH = C_GROUPS * C_GROUP_DIM
C_CHUNK = 128
D_FF = 4 * D_MODEL
EPS = 1e-6
NEG_INF = -1e30
LOG2E = 1.4426950408889634

LANES = 128
BF16_ROWS = 16
VMEM_LIMIT = 56 * 1024 * 1024

F32 = jnp.float32
BF16 = jnp.bfloat16

PROJ_TM = 1024
MLP_TM = 1024
MLP_TF = 1024
MERGE_TM = 256
SGU_TM = 1024

PROJ_TN = 1536
AB_WIDTH = 3 * A_WIDTH + 2 * B_KWIDTH + 2 * B_VWIDTH
G_WIDTH = 3 * D_MODEL
IN_WIDTH = AB_WIDTH + B_LOWRANK + 2 * C_WIDTH + G_WIDTH
AB0 = G_WIDTH
PROJ_WIDTH = G_WIDTH + AB_WIDTH
N_AB_TILES = AB_WIDTH // PROJ_TN
N_G_TILES = G_WIDTH // PROJ_TN
G_SRC = IN_WIDTH - G_WIDTH
C_PAD = LANES - B_LOWRANK
C_SRC = AB_WIDTH - C_PAD
C_WIN = LANES + 2 * C_WIDTH
LR_LANE = C_PAD
assert AB_WIDTH % PROJ_TN == 0 and G_WIDTH % PROJ_TN == 0
assert C_SRC % BF16_ROWS == 0 and G_SRC % BF16_ROWS == 0


def _rpe_thresholds():
    max_exact = RPE_BUCKETS // 2

    def bucket(n):
        if n < max_exact:
            return n
        v = math.log(n / max_exact) / math.log(RPE_MAX_DIST / max_exact) * (RPE_BUCKETS - max_exact)
        return min(max_exact + int(v), RPE_BUCKETS - 1)

    table = [bucket(n) for n in range(4 * RPE_MAX_DIST)]
    return [min(n for n in range(len(table)) if table[n] >= b)
            for b in range(max_exact + 1, RPE_BUCKETS)]


RPE_THRESHOLDS = _rpe_thresholds()
assert RPE_THRESHOLDS[-1] <= A_BLOCK


def _cparams(*sem):
    return pltpu.CompilerParams(dimension_semantics=sem, vmem_limit_bytes=VMEM_LIMIT)


def _rpe_bias_kernel(tbl_ref, bias_ref):
    h = pl.program_id(0)
    t = lax.broadcasted_iota(jnp.int32, (A_BLOCK, 2 * A_BLOCK), 0)
    c = lax.broadcasted_iota(jnp.int32, (A_BLOCK, 2 * A_BLOCK), 1)
    n = jnp.maximum(t - c + A_BLOCK, 0)
    max_exact = RPE_BUCKETS // 2
    large = jnp.full(n.shape, max_exact, jnp.int32)
    for thr in RPE_THRESHOLDS:
        large = large + (n >= thr).astype(jnp.int32)
    bucket = jnp.where(n < max_exact, n, large)
    bias = jnp.zeros(n.shape, F32)
    for b in range(RPE_BUCKETS):
        bias = jnp.where(bucket == b, tbl_ref[b, h], bias)
    bias = (bias - tbl_ref[RPE_BUCKETS - 1, h]) * LOG2E
    bias_ref[0] = jnp.where(c - A_BLOCK > t, NEG_INF, bias)


def rpe_bias(rpe_table):
    return pl.pallas_call(
        _rpe_bias_kernel,
        grid=(A_HEADS,),
        in_specs=[pl.BlockSpec(memory_space=pltpu.SMEM)],
        out_specs=pl.BlockSpec((1, A_BLOCK, 2 * A_BLOCK), lambda h: (h, 0, 0)),
        out_shape=jax.ShapeDtypeStruct((A_HEADS, A_BLOCK, 2 * A_BLOCK), F32),
        compiler_params=_cparams("arbitrary"),
        name="rpe_bias",
    )(rpe_table)


def _cast_specs(to_cast, layer, n_steps, step_of):
    in_specs, out_specs, shapes = [], [], []
    for w in to_cast:
        _, rows, cols = w.shape
        slice_rows = next(r for r in range(BF16_ROWS, rows + 1, BF16_ROWS)
                          if rows % r == 0 and rows // r <= n_steps)
        n_slices = rows // slice_rows

        def slice_of(*idx, n_slices=n_slices):
            return jnp.minimum(step_of(*idx), n_slices - 1)

        in_specs.append(pl.BlockSpec((None, slice_rows, cols), lambda *idx, f=slice_of: (layer, f(*idx), 0)))
        out_specs.append(pl.BlockSpec((slice_rows, cols), lambda *idx, f=slice_of: (f(*idx), 0)))
        shapes.append(jax.ShapeDtypeStruct((rows, cols), BF16))
    return in_specs, out_specs, shapes


def _cast_all(srcs, dsts):
    for src, dst in zip(srcs, dsts):
        dst[...] = src[...].astype(dst.dtype)


def _proj_kernel(*refs, n_cast, row_chunk):
    x_hbm, g_ref, w_ref = refs[:3]
    cast_in = refs[3:3 + n_cast]
    o_ref = refs[3 + n_cast]
    cast_out = refs[4 + n_cast:4 + 2 * n_cast]
    x_ref, xn_ref, x_sem = refs[4 + 2 * n_cast:]
    i, j = pl.program_id(0), pl.program_id(1)
    tm = x_ref.shape[0]


    def x_copy(tile):
        return pltpu.make_async_copy(x_hbm.at[pl.ds(tile * tm, tm), :], x_ref, x_sem)

    @pl.when(j == 0)
    def _():
        @pl.when(i == 0)
        def _():
            x_copy(0).start()
        x_copy(i).wait()

        def body(r, carry):
            rows = pl.ds(pl.multiple_of(r * row_chunk, row_chunk), row_chunk)
            x = x_ref[rows, :]
            ms = jnp.mean(x * x, axis=-1, keepdims=True)
            xn_ref[rows, :] = (x * lax.rsqrt(ms + EPS) * g_ref[...]).astype(BF16)
            return carry
        lax.fori_loop(0, tm // row_chunk, body, 0)

    @pl.when((j == 1) & (i + 1 < pl.num_programs(0)))
    def _():
        x_copy(i + 1).start()

    o_ref[...] = lax.dot_general(xn_ref[...], w_ref[...].astype(BF16), (((1,), (1,)), ((), ())),
                                 preferred_element_type=F32).astype(o_ref.dtype)
    _cast_all(cast_in, cast_out)


def _proj_src_row(j):
    u = BF16_ROWS
    return u * jnp.where(j < N_AB_TILES, j * (PROJ_TN // u),
                         G_SRC // u + (j - N_AB_TILES) * (PROJ_TN // u))


def _proj_out_tile(j):
    return jnp.where(j < N_AB_TILES, j + N_G_TILES, j - N_AB_TILES)


def proj(x, g, w_t, layer, to_cast, *, tm):
    m, k = x.shape
    tn = PROJ_TN
    n_tiles = N_AB_TILES + N_G_TILES
    cast_in_specs, cast_out_specs, cast_shapes = _cast_specs(
        to_cast, layer, (m // tm) * n_tiles, lambda i, j: i * n_tiles + j)
    return pl.pallas_call(
        functools.partial(_proj_kernel, n_cast=len(to_cast), row_chunk=128),
        grid=(m // tm, n_tiles),
        in_specs=[pl.BlockSpec(memory_space=pl.ANY),
                  pl.BlockSpec((1, k), lambda i, j: (0, 0)),
                  pl.BlockSpec((None, pl.Element(tn), pl.Element(k)),
                               lambda i, j: (layer, _proj_src_row(j), 0))] + cast_in_specs,
        out_specs=[pl.BlockSpec((tm, tn), lambda i, j: (i, _proj_out_tile(j)))] + cast_out_specs,
        out_shape=[jax.ShapeDtypeStruct((m, PROJ_WIDTH), BF16)] + cast_shapes,
        scratch_shapes=[pltpu.VMEM((tm, k), F32),
                        pltpu.VMEM((tm, k), BF16),
                        pltpu.SemaphoreType.DMA(())],
        compiler_params=_cparams("arbitrary", "arbitrary"),
        name="proj",
    )(x, g.reshape(1, k), w_t, *to_cast)


def _head_rms(x, g):
    ms = jnp.mean(x * x, axis=-1, keepdims=True)
    return x * lax.rsqrt(ms + EPS) * g


def _moba_stages(q_ref, k_ref, v_ref, qg_ref, kg_ref, bias_ref, o_ref,
                 qn_ref, qa_ref, ka_ref, va_ref, kmean_ref, s_ref, p_ref):
    seq, hd = q_ref.shape
    nb = seq // A_BLOCK
    rc = BF16_ROWS

    def prologue():
        qn = _head_rms(q_ref[...].astype(F32), qg_ref[...])
        qn_ref[...] = qn
        qa_ref[:, :hd] = (qn * (hd ** -0.5 * LOG2E)).astype(BF16)
        qa_ref[:, hd:] = jnp.zeros((seq, hd), BF16)
        kn = _head_rms(k_ref[...].astype(F32), kg_ref[...])
        ka_ref[:, :hd] = kn.astype(BF16)
        key_blk = lax.broadcasted_iota(jnp.int32, (seq, hd), 0) // A_BLOCK
        ka_ref[:, hd:] = jnp.where(key_blk == lax.broadcasted_iota(jnp.int32, (seq, hd), 1),
                                   1.0, 0.0).astype(BF16)
        va_ref[:, :hd] = v_ref[...]
        va_ref[:, hd:] = jnp.ones((seq, hd), BF16)
        for j in range(nb):
            kmean_ref[j:j + 1, :] = jnp.mean(kn[j * A_BLOCK:(j + 1) * A_BLOCK], axis=0, keepdims=True)

    def scores(i):
        rows = slice(i * A_BLOCK, (i + 1) * A_BLOCK)
        nk = (i + 1) * A_BLOCK
        if i > A_TOPK:
            blk = lax.broadcasted_iota(jnp.int32, (nb, A_BLOCK), 0)
            eye = jnp.where(lax.broadcasted_iota(jnp.int32, (nb, hd), 0)
                            == lax.broadcasted_iota(jnp.int32, (nb, hd), 1), 1.0, 0.0)
            g = lax.dot_general(kmean_ref[...], qn_ref[rows, :], (((1,), (1,)), ((), ())),
                                precision=lax.Precision.HIGHEST, preferred_element_type=F32)
            rank = jnp.zeros((nb, A_BLOCK), jnp.int32)
            for jp in range(i):
                gj = g[jp:jp + 1, :]
                beats = (gj > g) | ((gj == g) & (jp < blk))
                rank = rank + beats.astype(jnp.int32)
            neg = jnp.where((blk < i) & (rank >= A_TOPK), NEG_INF, 0.0)
            qa_ref[rows, hd:] = lax.dot_general(neg, eye, (((0,), (0,)), ((), ())),
                                                preferred_element_type=F32).astype(BF16)
        s_ref[i % 2, :, :nk] = lax.dot_general(qa_ref[rows, :], ka_ref[:nk, :], (((1,), (1,)), ((), ())),
                                               preferred_element_type=F32)

    def softmax(i):
        for c in range(A_BLOCK // rc):
            r = slice(c * rc, (c + 1) * rc)
            tiles = []
            for j in range(i + 1):
                sj = s_ref[i % 2, r, j * A_BLOCK:(j + 1) * A_BLOCK]
                if j == i:
                    sj = sj + bias_ref[0, r, A_BLOCK:]
                elif j == i - 1:
                    sj = sj + bias_ref[0, r, :A_BLOCK]
                tiles.append(sj)
            mt = tiles[0]
            for t in tiles[1:]:
                mt = jnp.maximum(mt, t)
            m = jnp.max(mt, axis=-1, keepdims=True)
            for j, t in enumerate(tiles):
                p_ref[i % 2, r, j * A_BLOCK:(j + 1) * A_BLOCK] = jnp.exp2(t - m).astype(BF16)

    def weighted_sum(i):
        rows = slice(i * A_BLOCK, (i + 1) * A_BLOCK)
        nk = (i + 1) * A_BLOCK
        o = jnp.dot(p_ref[i % 2, :, :nk], va_ref[:nk, :], preferred_element_type=F32)
        o_ref[rows, :] = (o[:, :hd] / o[:, hd:]).astype(o_ref.dtype)

    stages = [prologue, functools.partial(scores, 0)]
    for i in range(nb):
        if i + 1 < nb:
            stages.append(functools.partial(scores, i + 1))
        stages += [functools.partial(softmax, i), functools.partial(weighted_sum, i)]
    return stages


def _gla_stages(q_ref, k_ref, v_ref, r_ref, lr_ref, lrw_ref, lrb_ref, og_ref, o_ref,
                hml_ref, bc_ref, qd_ref, ki_ref, ke_ref, dec_ref, oi_ref, ss_ref, sb_ref):
    seq = q_ref.shape[0]
    ck = B_CHUNK
    nc = seq // ck
    grp = 4 * ck
    ng = seq // grp
    cpg = grp // ck
    qscale = B_DK ** -0.5
    nt = (((1,), (1,)), ((), ()))
    tn = (((0,), (0,)), ((), ()))

    def same_chunk_causal():
        ti = lax.broadcasted_iota(jnp.int32, (grp, grp), 0)
        si = lax.broadcasted_iota(jnp.int32, (grp, grp), 1)
        return (si <= ti) & (si // ck == ti // ck)

    def log_decay(g):
        rows = slice(g * grp, (g + 1) * grp)
        z = jnp.dot(lr_ref[rows, :], lrw_ref[...].astype(BF16), preferred_element_type=F32) + lrb_ref[...]
        la = (jnp.minimum(z, 0.0) - jnp.log(1.0 + jnp.exp(-jnp.abs(z)))) * (1.0 / B_GATE_NORM)
        hi = la.astype(BF16)
        r1 = la - hi.astype(F32)
        mid = r1.astype(BF16)
        hml_ref[rows, :B_DK] = hi
        hml_ref[rows, B_DK:2 * B_DK] = mid
        hml_ref[rows, 2 * B_DK:] = (r1 - mid.astype(F32)).astype(BF16)

    def cumsum(g):
        rows = slice(g * grp, (g + 1) * grp)
        tril_grp = jnp.where(same_chunk_causal(), 1.0, 0.0).astype(BF16)
        parts = jnp.dot(tril_grp, hml_ref[rows, :], preferred_element_type=F32)
        bc_ref[rows, :] = parts[:, :B_DK] + parts[:, B_DK:2 * B_DK] + parts[:, 2 * B_DK:]

    def decayed_qk(g):
        rows = slice(g * grp, (g + 1) * grp)
        bc = bc_ref[rows, :]
        bc3 = bc.reshape(cpg, ck, B_DK)
        bl3 = bc3[:, ck - 1:ck, :]
        q = q_ref[rows, :].astype(F32)
        k3 = k_ref[rows, :].astype(F32).reshape(cpg, ck, B_DK)
        qd_ref[rows, :] = (q * jnp.exp(bc) * qscale).astype(BF16)
        ki_ref[rows, :] = (k3 * jnp.exp(-bc3)).reshape(grp, B_DK).astype(BF16)
        ke_ref[rows, :] = (k3 * jnp.exp(bl3 - bc3)).reshape(grp, B_DK).astype(BF16)
        dec_ref[g * cpg:(g + 1) * cpg, :] = jnp.exp(bl3).reshape(cpg, B_DK)

    def intra(g):
        rows = slice(g * grp, (g + 1) * grp)
        a = lax.dot_general(qd_ref[rows, :], ki_ref[rows, :], nt, preferred_element_type=F32)
        a = jnp.where(same_chunk_causal(), a, 0.0).astype(BF16)
        oi_ref[rows, :] = jnp.dot(a, v_ref[rows, :], preferred_element_type=F32)

    def state_terms(g):
        for c in range(g * cpg, (g + 1) * cpg):
            rows = slice(c * ck, (c + 1) * ck)
            ss_ref[c] = lax.dot_general(v_ref[rows, :], ke_ref[rows, :], tn, preferred_element_type=F32)

    def scan():
        slab = B_DV // 4
        st = [jnp.zeros((slab, B_DK), F32) for _ in range(4)]
        for c in range(nc):
            dec = dec_ref[c:c + 1, :]
            for s in range(4):
                r = slice(s * slab, (s + 1) * slab)
                sb_ref[c, r, :] = st[s].astype(BF16)
                st[s] = st[s] * dec + ss_ref[c, r, :]

    def outputs(g):
        rows = slice(g * grp, (g + 1) * grp)
        inter = [lax.dot_general(qd_ref[c * ck:(c + 1) * ck, :], sb_ref[c], nt, preferred_element_type=F32)
                 for c in range(g * cpg, (g + 1) * cpg)]
        o = oi_ref[rows, :] + jnp.concatenate(inter, axis=0)
        ms = jnp.mean(o * o, axis=-1, keepdims=True)
        y = o * lax.rsqrt(ms + EPS) * og_ref[...]
        r = r_ref[rows, :].astype(F32)
        o_ref[rows, :] = (y * (r * jax.nn.sigmoid(r))).astype(o_ref.dtype)

    groups = [[functools.partial(fn, g) for g in range(ng)]
              for fn in (log_decay, cumsum, decayed_qk, intra, state_terms)]
    return groups + [[scan], [functools.partial(outputs, g) for g in range(ng)]]


N_MOBA_IN, N_MOBA_SCRATCH = 6, 7
N_GLA_IN = 8


def _mixer_ab_kernel(*refs, n_cast):
    n_in = N_MOBA_IN + N_GLA_IN
    ins, cast_in = refs[:n_in], refs[n_in:n_in + n_cast]
    oa_ref, ob_ref = refs[n_in + n_cast:n_in + n_cast + 2]
    cast_out = refs[n_in + n_cast + 2:n_in + 2 * n_cast + 2]
    scratch = refs[n_in + 2 * n_cast + 2:]
    _cast_all(cast_in, cast_out)
    moba_stages = _moba_stages(*ins[:N_MOBA_IN], oa_ref, *scratch[:N_MOBA_SCRATCH])
    gla_steps = [fn for group in _gla_stages(*ins[N_MOBA_IN:], ob_ref, *scratch[N_MOBA_SCRATCH:])
                 for fn in group]
    per_slot = -(-len(gla_steps) // len(moba_stages))
    for n, stage in enumerate(moba_stages):
        stage()
        for fn in gla_steps[n * per_slot:(n + 1) * per_slot]:
            fn()


def mixer_ab(proj_out, lr_blk, qg, kg, bias, lr_w, lr_b, og, layer, to_cast, *, bsz, seq):
    hd = HEAD_DIM
    assert A_HEADS == B_HEADS
    qa0 = AB0 // hd
    qb0 = (AB0 + 3 * A_WIDTH) // B_DK
    kb0 = qb0 + B_HEADS
    vb0 = (AB0 + 3 * A_WIDTH + 2 * B_KWIDTH) // B_DV
    rb0 = vb0 + B_HEADS
    nc = seq // B_CHUNK
    cast_in_specs, cast_out_specs, cast_shapes = _cast_specs(
        to_cast, layer, bsz * A_HEADS, lambda b, h: b * A_HEADS + h)
    return pl.pallas_call(
        functools.partial(_mixer_ab_kernel, n_cast=len(to_cast)),
        grid=(bsz, A_HEADS),
        in_specs=[pl.BlockSpec((seq, hd), lambda b, h: (b, qa0 + h)),
                  pl.BlockSpec((seq, hd), lambda b, h: (b, qa0 + A_HEADS + h)),
                  pl.BlockSpec((seq, hd), lambda b, h: (b, qa0 + 2 * A_HEADS + h)),
                  pl.BlockSpec((1, hd), lambda b, h: (0, 0)),
                  pl.BlockSpec((1, hd), lambda b, h: (0, 0)),
                  pl.BlockSpec((1, A_BLOCK, 2 * A_BLOCK), lambda b, h: (h, 0, 0)),
                  pl.BlockSpec((seq, B_DK), lambda b, h: (b, qb0 + h)),
                  pl.BlockSpec((seq, B_DK), lambda b, h: (b, kb0 + h)),
                  pl.BlockSpec((seq, B_DV), lambda b, h: (b, vb0 + h)),
                  pl.BlockSpec((seq, B_DV), lambda b, h: (b, rb0 + h)),
                  pl.BlockSpec((seq, LANES), lambda b, h: (b, 0)),
                  pl.BlockSpec((None, LANES, B_DK), lambda b, h: (layer, 0, h)),
                  pl.BlockSpec((1, B_DK), lambda b, h: (0, h)),
                  pl.BlockSpec((1, B_DV), lambda b, h: (0, 0))] + cast_in_specs,
        out_specs=[pl.BlockSpec((seq, hd), lambda b, h: (b, h)),
                   pl.BlockSpec((seq, B_DV), lambda b, h: (b, h))] + cast_out_specs,
        out_shape=[jax.ShapeDtypeStruct((bsz * seq, A_WIDTH), BF16),
                   jax.ShapeDtypeStruct((bsz * seq, B_VWIDTH), BF16)] + cast_shapes,
        scratch_shapes=[pltpu.VMEM((seq, hd), F32),
                        pltpu.VMEM((seq, 2 * hd), BF16),
                        pltpu.VMEM((seq, 2 * hd), BF16),
                        pltpu.VMEM((seq, 2 * hd), BF16),
                        pltpu.VMEM((seq // A_BLOCK, hd), F32),
                        pltpu.VMEM((2, A_BLOCK, seq), F32),
                        pltpu.VMEM((2, A_BLOCK, seq), BF16),
                        pltpu.VMEM((seq, 3 * B_DK), BF16),
                        pltpu.VMEM((seq, B_DK), F32),
                        pltpu.VMEM((seq, B_DK), BF16),
                        pltpu.VMEM((seq, B_DK), BF16),
                        pltpu.VMEM((seq, B_DK), BF16),
                        pltpu.VMEM((nc, B_DK), F32),
                        pltpu.VMEM((seq, B_DV), F32),
                        pltpu.VMEM((nc, B_DV, B_DK), F32),
                        pltpu.VMEM((nc, B_DV, B_DK), BF16)],
        compiler_params=_cparams("parallel", "parallel"),
        name="mixer_ab",
    )(proj_out, proj_out, proj_out, qg.reshape(1, hd), kg.reshape(1, hd), bias,
      proj_out, proj_out, proj_out, proj_out, lr_blk, lr_w, lr_b.reshape(1, B_KWIDTH),
      og.reshape(1, B_DV), *to_cast)


def _gelu(x):
    c = math.sqrt(2.0 / math.pi)
    return 0.5 * x * (1.0 + jnp.tanh(c * (x + 0.044715 * (x * x * x))))


SGU_SUB = 2 * C_CHUNK


def _sgu_kernel(*refs, n_cast):
    x_ref, g1_ref, wsrc_ref, lg_ref, lb_ref, w_ref, b_ref = refs[:7]
    o_ref, lr_ref = refs[7 + n_cast:9 + n_cast]
    wc_ref, luv_ref = refs[-2:]

    @pl.when(pl.program_id(0) == 0)
    def _():
        wc_ref[...] = wsrc_ref[...].astype(BF16)

    _cast_all(refs[7:7 + n_cast], refs[9 + n_cast:9 + 2 * n_cast])
    ti = lax.broadcasted_iota(jnp.int32, (C_CHUNK, C_CHUNK), 0)
    si = lax.broadcasted_iota(jnp.int32, (C_CHUNK, C_CHUNK), 1)
    causal = si <= ti
    ws = [jnp.where(causal, w_ref[g], 0.0).astype(BF16) for g in range(C_GROUPS)]
    n_sub = x_ref.shape[0] // SGU_SUB

    def project(s):
        rs = slice(s * SGU_SUB, (s + 1) * SGU_SUB)
        x = x_ref[rs, :]
        ms = jnp.mean(x * x, axis=-1, keepdims=True)
        xn = (x * lax.rsqrt(ms + EPS) * g1_ref[...]).astype(BF16)
        luv = lax.dot_general(xn, wc_ref[...], (((1,), (1,)), ((), ())), preferred_element_type=F32)
        luv_ref[s % 2] = luv
        lr_ref[rs, :] = luv[:, :LANES].astype(lr_ref.dtype)

    def gate(s):
        for n in range(SGU_SUB // C_CHUNK):
            rl = slice(n * C_CHUNK, (n + 1) * C_CHUNK)
            ro = slice(s * SGU_SUB + n * C_CHUNK, s * SGU_SUB + (n + 1) * C_CHUNK)
            v = _gelu(luv_ref[s % 2, rl, LANES + C_WIDTH:])
            mu = jnp.mean(v, axis=-1, keepdims=True)
            vc = v - mu
            vn = vc * lax.rsqrt(jnp.mean(vc * vc, axis=-1, keepdims=True) + EPS)
            vn = (vn * lg_ref[...] + lb_ref[...]).astype(BF16)
            u = _gelu(luv_ref[s % 2, rl, LANES:LANES + C_WIDTH])
            for g in range(C_GROUPS):
                cols = slice(g * C_GROUP_DIM, (g + 1) * C_GROUP_DIM)
                mixed = jnp.dot(ws[g], vn[:, cols], preferred_element_type=F32) + b_ref[g]
                o_ref[ro, cols] = (u[:, cols] * mixed).astype(o_ref.dtype)

    project(0)
    for s in range(n_sub):
        if s + 1 < n_sub:
            project(s + 1)
        gate(s)


def sgu(x, g1, w_t, ln_g, ln_b, w_s, b_s, layer, to_cast, *, tm):
    m, d = x.shape
    cast_in_specs, cast_out_specs, cast_shapes = _cast_specs(to_cast, layer, m // tm, lambda i: i)
    return pl.pallas_call(
        functools.partial(_sgu_kernel, n_cast=len(to_cast)),
        grid=(m // tm,),
        in_specs=[pl.BlockSpec((tm, d), lambda i: (i, 0)),
                  pl.BlockSpec((1, d), lambda i: (0, 0)),
                  pl.BlockSpec((None, pl.Element(C_WIN), pl.Element(d)), lambda i: (layer, C_SRC, 0),
                               pipeline_mode=pl.Buffered(1)),
                  pl.BlockSpec((1, C_WIDTH), lambda i: (0, 0)),
                  pl.BlockSpec((1, C_WIDTH), lambda i: (0, 0)),
                  pl.BlockSpec((C_GROUPS, C_CHUNK, C_CHUNK), lambda i: (0, 0, 0)),
                  pl.BlockSpec((C_GROUPS, C_CHUNK, 1), lambda i: (0, 0, 0))] + cast_in_specs,
        out_specs=[pl.BlockSpec((tm, C_WIDTH), lambda i: (i, 0)),
                   pl.BlockSpec((tm, LANES), lambda i: (i, 0))] + cast_out_specs,
        out_shape=[jax.ShapeDtypeStruct((m, C_WIDTH), BF16),
                   jax.ShapeDtypeStruct((m, LANES), BF16)] + cast_shapes,
        scratch_shapes=[pltpu.VMEM((C_WIN, d), BF16),
                        pltpu.VMEM((2, SGU_SUB, C_WIN), F32)],
        compiler_params=_cparams("arbitrary"),
        name="sgu",
    )(x, g1.reshape(1, d), w_t, ln_g.reshape(1, C_WIDTH), ln_b.reshape(1, C_WIDTH), w_s,
      b_s.reshape(C_GROUPS, C_CHUNK, 1), *to_cast)


def _merge_kernel(ya_ref, yb_ref, yc_ref, ga_ref, gb_ref, gc_ref, x_ref,
                  wa_ref, wb_ref, wc_ref, wo_ref, g2_ref, o_ref, on_ref):
    def gated(g_ref, y_ref, w_ref):
        return (jax.nn.sigmoid(g_ref[...].astype(F32))
                * jnp.dot(y_ref[...], w_ref[...], preferred_element_type=F32))

    merged = gated(ga_ref, ya_ref, wa_ref) + gated(gb_ref, yb_ref, wb_ref) + gated(gc_ref, yc_ref, wc_ref)
    h = x_ref[...] + jnp.dot(merged.astype(BF16), wo_ref[...], preferred_element_type=F32)
    o_ref[...] = h
    ms = jnp.mean(h * h, axis=-1, keepdims=True)
    on_ref[...] = (h * lax.rsqrt(ms + EPS) * g2_ref[...]).astype(on_ref.dtype)


def merge(ya, yb, yc, proj_out, x, wa, wb, wc, wo, g2, *, tm):
    m, d = x.shape
    resident = functools.partial(pl.BlockSpec, pipeline_mode=pl.Buffered(1))
    return pl.pallas_call(
        _merge_kernel,
        grid=(m // tm,),
        in_specs=[pl.BlockSpec((tm, A_WIDTH), lambda i: (i, 0)),
                  pl.BlockSpec((tm, B_VWIDTH), lambda i: (i, 0)),
                  pl.BlockSpec((tm, C_WIDTH), lambda i: (i, 0)),
                  pl.BlockSpec((tm, d), lambda i: (i, 0)),
                  pl.BlockSpec((tm, d), lambda i: (i, 1)),
                  pl.BlockSpec((tm, d), lambda i: (i, 2)),
                  pl.BlockSpec((tm, d), lambda i: (i, 0)),
                  resident((A_WIDTH, d), lambda i: (0, 0)),
                  resident((B_VWIDTH, d), lambda i: (0, 0)),
                  resident((C_WIDTH, d), lambda i: (0, 0)),
                  resident((d, d), lambda i: (0, 0)),
                  pl.BlockSpec((1, d), lambda i: (0, 0))],
        out_specs=[pl.BlockSpec((tm, d), lambda i: (i, 0)),
                   pl.BlockSpec((tm, d), lambda i: (i, 0))],
        out_shape=[jax.ShapeDtypeStruct((m, d), F32),
                   jax.ShapeDtypeStruct((m, d), BF16)],
        compiler_params=_cparams("parallel"),
        name="merge",
    )(ya, yb, yc, proj_out, proj_out, proj_out, x, wa, wb, wc, wo, g2.reshape(1, d))


def _mlp_kernel(h_hbm, hn_ref, w1_ref, w2_ref, o_ref, h_ref, h_sem):
    i, j = pl.program_id(0), pl.program_id(1)
    last = pl.num_programs(1) - 1
    tm = h_ref.shape[0]
    h_copy = pltpu.make_async_copy(h_hbm.at[pl.ds(i * tm, tm), :], h_ref, h_sem)

    def update():
        a = jnp.dot(hn_ref[...], w1_ref[...], preferred_element_type=F32)
        a = jnp.square(jnp.maximum(a, 0.0)).astype(BF16)
        return jnp.dot(a, w2_ref[...], preferred_element_type=F32)

    @pl.when(j == 0)
    def _():
        h_copy.start()
        o_ref[...] = update()

    @pl.when((j > 0) & (j < last))
    def _():
        o_ref[...] += update()

    @pl.when(j == last)
    def _():
        h_copy.wait()
        o_ref[...] = (o_ref[...] + h_ref[...]) + update()


def mlp(h, hn, w1, w2, *, tm, tf):
    m, d = h.shape
    f = w1.shape[1]
    return pl.pallas_call(
        _mlp_kernel,
        grid=(m // tm, f // tf),
        in_specs=[pl.BlockSpec(memory_space=pl.ANY),
                  pl.BlockSpec((tm, d), lambda i, j: (i, 0)),
                  pl.BlockSpec((d, tf), lambda i, j: (0, j)),
                  pl.BlockSpec((tf, d), lambda i, j: (j, 0))],
        out_specs=pl.BlockSpec((tm, d), lambda i, j: (i, 0)),
        out_shape=jax.ShapeDtypeStruct((m, d), F32),
        scratch_shapes=[pltpu.VMEM((tm, d), F32),
                        pltpu.SemaphoreType.DMA(())],
        compiler_params=_cparams("parallel", "arbitrary"),
        name="mlp",
    )(h, hn, w1, w2)


def kernel(x, rpe_table, norm1_g, w_in, q_norm_g, k_norm_g, gla_lr_w, gla_lr_b, gla_out_g,
           sg_ln_g, sg_ln_b, sg_w, sg_b, w_br_a, w_br_b, w_br_c, w_o, norm2_g, w_ff1, w_ff2):
    bsz, seq, d = x.shape
    assert seq % A_BLOCK == 0 and seq % B_CHUNK == 0 and seq % C_CHUNK == 0
    assert w_in.shape[2] == IN_WIDTH
    h = x.reshape(bsz * seq, d)
    bias = rpe_bias(rpe_table)
    w_t = jnp.swapaxes(w_in, 1, 2)
    lr_w = jnp.pad(gla_lr_w, ((0, 0), (LR_LANE, LANES - LR_LANE - B_LOWRANK), (0, 0)))
    for l in range(DEPTH):
        p, w1 = proj(h, norm1_g[l], w_t, l, (w_ff1,), tm=PROJ_TM)
        y_c, lr_blk, wa, wb, wc, wo = sgu(h, norm1_g[l], w_t, sg_ln_g[l], sg_ln_b[l], sg_w[l], sg_b[l], l,
                                          (w_br_a, w_br_b, w_br_c, w_o), tm=SGU_TM)
        y_a, y_b, w2 = mixer_ab(p, lr_blk, q_norm_g[l], k_norm_g[l], bias, lr_w, gla_lr_b[l],
                                gla_out_g[l], l, (w_ff2,), bsz=bsz, seq=seq)
        h, hn = merge(y_a, y_b, y_c, p, h, wa, wb, wc, wo, norm2_g[l], tm=MERGE_TM)
        h = mlp(h, hn, w1, w2, tm=MLP_TM, tf=MLP_TF)
    return h.reshape(bsz, seq, d)
```

```python
import functools
import math

import jax
import jax.numpy as jnp
from jax import lax
from jax.experimental import pallas as pl
from jax.experimental.pallas import tpu as pltpu

D_MODEL = 2048
DEPTH = 2
HEAD_DIM = 128
A_HEADS = 4
A_WIDTH = A_HEADS * HEAD_DIM
A_BLOCK = 256
A_TOPK = 3
RPE_BUCKETS = 32
RPE_MAX_DIST = 128
B_HEADS = 4
B_DK = 128
B_DV = 256
B_KWIDTH = B_HEADS * B_DK
B_VWIDTH = B_HEADS * B_DV
B_LOWRANK = 16
B_GATE_NORM = 16.0
B_CHUNK = 64
C_GROUPS = 4
C_GROUP_DIM = 128
C_WIDTH = C_GROUPS * C_GROUP_DIM
C_CHUNK = 128
D_FF = 4 * D_MODEL
EPS = 1e-6
NEG_INF = -1e30
LOG2E = 1.4426950408889634

LANES = 128
BF16_ROWS = 16
VMEM_LIMIT = 56 * 1024 * 1024

F32 = jnp.float32
BF16 = jnp.bfloat16

PROJ_TM = 1024
MLP_TM = 1024
MLP_TF = 1024
MERGE_TM = 256
SGU_TM = 1024

PROJ_TN = 1536
AB_WIDTH = 3 * A_WIDTH + 2 * B_KWIDTH + 2 * B_VWIDTH
G_WIDTH = 3 * D_MODEL
IN_WIDTH = AB_WIDTH + B_LOWRANK + 2 * C_WIDTH + G_WIDTH
AB0 = G_WIDTH
PROJ_WIDTH = G_WIDTH + AB_WIDTH
N_AB_TILES = AB_WIDTH // PROJ_TN
N_G_TILES = G_WIDTH // PROJ_TN
G_SRC = IN_WIDTH - G_WIDTH
C_PAD = LANES - B_LOWRANK
C_SRC = AB_WIDTH - C_PAD
C_WIN = LANES + 2 * C_WIDTH
LR_LANE = C_PAD
assert AB_WIDTH % PROJ_TN == 0 and G_WIDTH % PROJ_TN == 0
assert C_SRC % BF16_ROWS == 0 and G_SRC % BF16_ROWS == 0


def _rpe_thresholds():
    max_exact = RPE_BUCKETS // 2

    def bucket(n):
        if n < max_exact:
            return n
        v = math.log(n / max_exact) / math.log(RPE_MAX_DIST / max_exact) * (RPE_BUCKETS - max_exact)
        return min(max_exact + int(v), RPE_BUCKETS - 1)

    table = [bucket(n) for n in range(4 * RPE_MAX_DIST)]
    return [min(n for n in range(len(table)) if table[n] >= b)
            for b in range(max_exact + 1, RPE_BUCKETS)]


RPE_THRESHOLDS = _rpe_thresholds()
assert RPE_THRESHOLDS[-1] <= A_BLOCK


def _cparams(*sem):
    return pltpu.CompilerParams(dimension_semantics=sem, vmem_limit_bytes=VMEM_LIMIT)


def _rpe_bias_kernel(tbl_ref, bias_ref):
    h = pl.program_id(0)
    t = lax.broadcasted_iota(jnp.int32, (A_BLOCK, 2 * A_BLOCK), 0)
    c = lax.broadcasted_iota(jnp.int32, (A_BLOCK, 2 * A_BLOCK), 1)
    n = jnp.maximum(t - c + A_BLOCK, 0)
    max_exact = RPE_BUCKETS // 2
    large = jnp.full(n.shape, max_exact, jnp.int32)
    for thr in RPE_THRESHOLDS:
        large = large + (n >= thr).astype(jnp.int32)
    bucket = jnp.where(n < max_exact, n, large)
    bias = jnp.zeros(n.shape, F32)
    for b in range(RPE_BUCKETS):
        bias = jnp.where(bucket == b, tbl_ref[b, h], bias)
    bias = (bias - tbl_ref[RPE_BUCKETS - 1, h]) * LOG2E
    bias_ref[0] = jnp.where(c - A_BLOCK > t, NEG_INF, bias)


def rpe_bias(rpe_table):
    return pl.pallas_call(
        _rpe_bias_kernel,
        grid=(A_HEADS,),
        in_specs=[pl.BlockSpec(memory_space=pltpu.SMEM)],
        out_specs=pl.BlockSpec((1, A_BLOCK, 2 * A_BLOCK), lambda h: (h, 0, 0)),
        out_shape=jax.ShapeDtypeStruct((A_HEADS, A_BLOCK, 2 * A_BLOCK), F32),
        compiler_params=_cparams("arbitrary"),
        name="rpe_bias",
    )(rpe_table)


def _cast_specs(to_cast, layer, n_steps, step_of):
    in_specs, out_specs, shapes = [], [], []
    for w in to_cast:
        _, rows, cols = w.shape
        slice_rows = next(r for r in range(BF16_ROWS, rows + 1, BF16_ROWS)
                          if rows % r == 0 and rows // r <= n_steps)
        n_slices = rows // slice_rows

        def slice_of(*idx, n_slices=n_slices):
            return jnp.minimum(step_of(*idx), n_slices - 1)

        in_specs.append(pl.BlockSpec((None, slice_rows, cols), lambda *idx, f=slice_of: (layer, f(*idx), 0)))
        out_specs.append(pl.BlockSpec((slice_rows, cols), lambda *idx, f=slice_of: (f(*idx), 0)))
        shapes.append(jax.ShapeDtypeStruct((rows, cols), BF16))
    return in_specs, out_specs, shapes


def _cast_all(srcs, dsts):
    for src, dst in zip(srcs, dsts):
        dst[...] = src[...].astype(dst.dtype)


def _proj_kernel(*refs, n_cast):
    xn_ref, w_ref = refs[:2]
    o_ref = refs[2 + n_cast]
    o_ref[...] = lax.dot_general(xn_ref[...], w_ref[...].astype(BF16), (((1,), (1,)), ((), ())),
                                 preferred_element_type=F32).astype(o_ref.dtype)
    _cast_all(refs[2:2 + n_cast], refs[3 + n_cast:])


def _proj_src_row(j):
    u = BF16_ROWS
    return u * jnp.where(j < N_AB_TILES, j * (PROJ_TN // u),
                         G_SRC // u + (j - N_AB_TILES) * (PROJ_TN // u))


def _proj_out_tile(j):
    return jnp.where(j < N_AB_TILES, j + N_G_TILES, j - N_AB_TILES)


def proj(xn, w_t, layer, to_cast, *, tm):
    m, k = xn.shape
    tn = PROJ_TN
    n_tiles = N_AB_TILES + N_G_TILES
    cast_in_specs, cast_out_specs, cast_shapes = _cast_specs(
        to_cast, layer, (m // tm) * n_tiles, lambda i, j: i * n_tiles + j)
    return pl.pallas_call(
        functools.partial(_proj_kernel, n_cast=len(to_cast)),
        grid=(m // tm, n_tiles),
        in_specs=[pl.BlockSpec((tm, k), lambda i, j: (i, 0)),
                  pl.BlockSpec((None, pl.Element(tn), pl.Element(k)),
                               lambda i, j: (layer, _proj_src_row(j), 0))] + cast_in_specs,
        out_specs=[pl.BlockSpec((tm, tn), lambda i, j: (i, _proj_out_tile(j)))] + cast_out_specs,
        out_shape=[jax.ShapeDtypeStruct((m, PROJ_WIDTH), BF16)] + cast_shapes,
        compiler_params=_cparams("parallel", "arbitrary"),
        name="proj",
    )(xn, w_t, *to_cast)


def _head_rms(x, g):
    ms = jnp.mean(x * x, axis=-1, keepdims=True)
    return x * lax.rsqrt(ms + EPS) * g


def _moba_stages(q_ref, k_ref, v_ref, qg_ref, kg_ref, bias_ref, o_ref,
                 qn_ref, qa_ref, ka_ref, va_ref, kmean_ref, s_ref, p_ref):
    seq, hd = q_ref.shape
    nb = seq // A_BLOCK
    rc = BF16_ROWS

    def prologue():
        qn = _head_rms(q_ref[...].astype(F32), qg_ref[...])
        qn_ref[...] = qn
        qa_ref[:, :hd] = (qn * (hd ** -0.5 * LOG2E)).astype(BF16)
        qa_ref[:, hd:] = jnp.zeros((seq, hd), BF16)
        kn = _head_rms(k_ref[...].astype(F32), kg_ref[...])
        ka_ref[:, :hd] = kn.astype(BF16)
        key_blk = lax.broadcasted_iota(jnp.int32, (seq, hd), 0) // A_BLOCK
        ka_ref[:, hd:] = jnp.where(key_blk == lax.broadcasted_iota(jnp.int32, (seq, hd), 1),
                                   1.0, 0.0).astype(BF16)
        va_ref[:, :hd] = v_ref[...]
        va_ref[:, hd:] = jnp.ones((seq, hd), BF16)
        for j in range(nb):
            kmean_ref[j:j + 1, :] = jnp.mean(kn[j * A_BLOCK:(j + 1) * A_BLOCK], axis=0, keepdims=True)

    def scores(i):
        rows = slice(i * A_BLOCK, (i + 1) * A_BLOCK)
        nk = (i + 1) * A_BLOCK
        if i > A_TOPK:
            blk = lax.broadcasted_iota(jnp.int32, (nb, A_BLOCK), 0)
            eye = jnp.where(lax.broadcasted_iota(jnp.int32, (nb, hd), 0)
                            == lax.broadcasted_iota(jnp.int32, (nb, hd), 1), 1.0, 0.0)
            g = lax.dot_general(kmean_ref[...], qn_ref[rows, :], (((1,), (1,)), ((), ())),
                                precision=lax.Precision.HIGHEST, preferred_element_type=F32)
            rank = jnp.zeros((nb, A_BLOCK), jnp.int32)
            for jp in range(i):
                gj = g[jp:jp + 1, :]
                beats = (gj > g) | ((gj == g) & (jp < blk))
                rank = rank + beats.astype(jnp.int32)
            neg = jnp.where((blk < i) & (rank >= A_TOPK), NEG_INF, 0.0)
            qa_ref[rows, hd:] = lax.dot_general(neg, eye, (((0,), (0,)), ((), ())),
                                                preferred_element_type=F32).astype(BF16)
        s_ref[i % 2, :, :nk] = lax.dot_general(qa_ref[rows, :], ka_ref[:nk, :], (((1,), (1,)), ((), ())),
                                               preferred_element_type=F32)

    def softmax(i):
        for c in range(A_BLOCK // rc):
            r = slice(c * rc, (c + 1) * rc)
            tiles = []
            for j in range(i + 1):
                sj = s_ref[i % 2, r, j * A_BLOCK:(j + 1) * A_BLOCK]
                if j == i:
                    sj = sj + bias_ref[0, r, A_BLOCK:]
                elif j == i - 1:
                    sj = sj + bias_ref[0, r, :A_BLOCK]
                tiles.append(sj)
            mt = tiles[0]
            for t in tiles[1:]:
                mt = jnp.maximum(mt, t)
            m = jnp.max(mt, axis=-1, keepdims=True)
            for j, t in enumerate(tiles):
                p_ref[i % 2, r, j * A_BLOCK:(j + 1) * A_BLOCK] = jnp.exp2(t - m).astype(BF16)

    def weighted_sum(i):
        rows = slice(i * A_BLOCK, (i + 1) * A_BLOCK)
        nk = (i + 1) * A_BLOCK
        o = jnp.dot(p_ref[i % 2, :, :nk], va_ref[:nk, :], preferred_element_type=F32)
        o_ref[rows, :] = (o[:, :hd] / o[:, hd:]).astype(o_ref.dtype)

    stages = [prologue, functools.partial(scores, 0)]
    for i in range(nb):
        if i + 1 < nb:
            stages.append(functools.partial(scores, i + 1))
        stages += [functools.partial(softmax, i), functools.partial(weighted_sum, i)]
    return stages


def _gla_stages(q_ref, k_ref, v_ref, r_ref, lr_ref, lrw_ref, lrb_ref, og_ref, o_ref,
                hml_ref, bc_ref, qd_ref, ki_ref, ke_ref, dec_ref, oi_ref, ss_ref, sb_ref):
    seq = q_ref.shape[0]
    ck = B_CHUNK
    nc = seq // ck
    grp = 4 * ck
    ng = seq // grp
    cpg = grp // ck
    qscale = B_DK ** -0.5
    nt = (((1,), (1,)), ((), ()))
    tn = (((0,), (0,)), ((), ()))

    def same_chunk_causal():
        ti = lax.broadcasted_iota(jnp.int32, (grp, grp), 0)
        si = lax.broadcasted_iota(jnp.int32, (grp, grp), 1)
        return (si <= ti) & (si // ck == ti // ck)

    def log_decay(g):
        rows = slice(g * grp, (g + 1) * grp)
        z = jnp.dot(lr_ref[rows, :], lrw_ref[...].astype(BF16), preferred_element_type=F32) + lrb_ref[...]
        la = (jnp.minimum(z, 0.0) - jnp.log(1.0 + jnp.exp(-jnp.abs(z)))) * (1.0 / B_GATE_NORM)
        hi = la.astype(BF16)
        r1 = la - hi.astype(F32)
        mid = r1.astype(BF16)
        hml_ref[rows, :B_DK] = hi
        hml_ref[rows, B_DK:2 * B_DK] = mid
        hml_ref[rows, 2 * B_DK:] = (r1 - mid.astype(F32)).astype(BF16)

    def cumsum(g):
        rows = slice(g * grp, (g + 1) * grp)
        tril_grp = jnp.where(same_chunk_causal(), 1.0, 0.0).astype(BF16)
        parts = jnp.dot(tril_grp, hml_ref[rows, :], preferred_element_type=F32)
        bc_ref[rows, :] = parts[:, :B_DK] + parts[:, B_DK:2 * B_DK] + parts[:, 2 * B_DK:]

    def decayed_qk(g):
        rows = slice(g * grp, (g + 1) * grp)
        bc = bc_ref[rows, :]
        bc3 = bc.reshape(cpg, ck, B_DK)
        bl3 = bc3[:, ck - 1:ck, :]
        q = q_ref[rows, :].astype(F32)
        k3 = k_ref[rows, :].astype(F32).reshape(cpg, ck, B_DK)
        qd_ref[rows, :] = (q * jnp.exp(bc) * qscale).astype(BF16)
        ki_ref[rows, :] = (k3 * jnp.exp(-bc3)).reshape(grp, B_DK).astype(BF16)
        ke_ref[rows, :] = (k3 * jnp.exp(bl3 - bc3)).reshape(grp, B_DK).astype(BF16)
        dec_ref[g * cpg:(g + 1) * cpg, :] = jnp.exp(bl3).reshape(cpg, B_DK)

    def intra(g):
        rows = slice(g * grp, (g + 1) * grp)
        a = lax.dot_general(qd_ref[rows, :], ki_ref[rows, :], nt, preferred_element_type=F32)
        a = jnp.where(same_chunk_causal(), a, 0.0).astype(BF16)
        oi_ref[rows, :] = jnp.dot(a, v_ref[rows, :], preferred_element_type=F32)

    def state_terms(g):
        for c in range(g * cpg, (g + 1) * cpg):
            rows = slice(c * ck, (c + 1) * ck)
            ss_ref[c] = lax.dot_general(v_ref[rows, :], ke_ref[rows, :], tn, preferred_element_type=F32)

    def scan():
        slab = B_DV // 4
        st = [jnp.zeros((slab, B_DK), F32) for _ in range(4)]
        for c in range(nc):
            dec = dec_ref[c:c + 1, :]
            for s in range(4):
                r = slice(s * slab, (s + 1) * slab)
                sb_ref[c, r, :] = st[s].astype(BF16)
                st[s] = st[s] * dec + ss_ref[c, r, :]

    def outputs(g):
        rows = slice(g * grp, (g + 1) * grp)
        inter = [lax.dot_general(qd_ref[c * ck:(c + 1) * ck, :], sb_ref[c], nt, preferred_element_type=F32)
                 for c in range(g * cpg, (g + 1) * cpg)]
        o = oi_ref[rows, :] + jnp.concatenate(inter, axis=0)
        ms = jnp.mean(o * o, axis=-1, keepdims=True)
        y = o * lax.rsqrt(ms + EPS) * og_ref[...]
        r = r_ref[rows, :].astype(F32)
        o_ref[rows, :] = (y * (r * jax.nn.sigmoid(r))).astype(o_ref.dtype)

    groups = [[functools.partial(fn, g) for g in range(ng)]
              for fn in (log_decay, cumsum, decayed_qk, intra, state_terms)]
    return groups + [[scan], [functools.partial(outputs, g) for g in range(ng)]]


N_MOBA_IN, N_MOBA_SCRATCH = 6, 7
N_GLA_IN = 8


def _mixer_ab_kernel(*refs, n_cast):
    n_in = N_MOBA_IN + N_GLA_IN
    ins, cast_in = refs[:n_in], refs[n_in:n_in + n_cast]
    oa_ref, ob_ref = refs[n_in + n_cast:n_in + n_cast + 2]
    cast_out = refs[n_in + n_cast + 2:n_in + 2 * n_cast + 2]
    scratch = refs[n_in + 2 * n_cast + 2:]
    _cast_all(cast_in, cast_out)
    moba_stages = _moba_stages(*ins[:N_MOBA_IN], oa_ref, *scratch[:N_MOBA_SCRATCH])
    gla_steps = [fn for group in _gla_stages(*ins[N_MOBA_IN:], ob_ref, *scratch[N_MOBA_SCRATCH:])
                 for fn in group]
    per_slot = -(-len(gla_steps) // len(moba_stages))
    for n, stage in enumerate(moba_stages):
        stage()
        for fn in gla_steps[n * per_slot:(n + 1) * per_slot]:
            fn()


def mixer_ab(proj_out, lr_blk, qg, kg, bias, lr_w, lr_b, og, layer, to_cast, *, bsz, seq):
    hd = HEAD_DIM
    assert A_HEADS == B_HEADS
    qa0 = AB0 // hd
    qb0 = (AB0 + 3 * A_WIDTH) // B_DK
    kb0 = qb0 + B_HEADS
    vb0 = (AB0 + 3 * A_WIDTH + 2 * B_KWIDTH) // B_DV
    rb0 = vb0 + B_HEADS
    nc = seq // B_CHUNK
    cast_in_specs, cast_out_specs, cast_shapes = _cast_specs(
        to_cast, layer, bsz * A_HEADS, lambda b, h: b * A_HEADS + h)
    return pl.pallas_call(
        functools.partial(_mixer_ab_kernel, n_cast=len(to_cast)),
        grid=(bsz, A_HEADS),
        in_specs=[pl.BlockSpec((seq, hd), lambda b, h: (b, qa0 + h)),
                  pl.BlockSpec((seq, hd), lambda b, h: (b, qa0 + A_HEADS + h)),
                  pl.BlockSpec((seq, hd), lambda b, h: (b, qa0 + 2 * A_HEADS + h)),
                  pl.BlockSpec((1, hd), lambda b, h: (0, 0)),
                  pl.BlockSpec((1, hd), lambda b, h: (0, 0)),
                  pl.BlockSpec((1, A_BLOCK, 2 * A_BLOCK), lambda b, h: (h, 0, 0)),
                  pl.BlockSpec((seq, B_DK), lambda b, h: (b, qb0 + h)),
                  pl.BlockSpec((seq, B_DK), lambda b, h: (b, kb0 + h)),
                  pl.BlockSpec((seq, B_DV), lambda b, h: (b, vb0 + h)),
                  pl.BlockSpec((seq, B_DV), lambda b, h: (b, rb0 + h)),
                  pl.BlockSpec((seq, LANES), lambda b, h: (b, 0)),
                  pl.BlockSpec((None, LANES, B_DK), lambda b, h: (layer, 0, h)),
                  pl.BlockSpec((1, B_DK), lambda b, h: (0, h)),
                  pl.BlockSpec((1, B_DV), lambda b, h: (0, 0))] + cast_in_specs,
        out_specs=[pl.BlockSpec((seq, hd), lambda b, h: (b, h)),
                   pl.BlockSpec((seq, B_DV), lambda b, h: (b, h))] + cast_out_specs,
        out_shape=[jax.ShapeDtypeStruct((bsz * seq, A_WIDTH), BF16),
                   jax.ShapeDtypeStruct((bsz * seq, B_VWIDTH), BF16)] + cast_shapes,
        scratch_shapes=[pltpu.VMEM((seq, hd), F32),
                        pltpu.VMEM((seq, 2 * hd), BF16),
                        pltpu.VMEM((seq, 2 * hd), BF16),
                        pltpu.VMEM((seq, 2 * hd), BF16),
                        pltpu.VMEM((seq // A_BLOCK, hd), F32),
                        pltpu.VMEM((2, A_BLOCK, seq), F32),
                        pltpu.VMEM((2, A_BLOCK, seq), BF16),
                        pltpu.VMEM((seq, 3 * B_DK), BF16),
                        pltpu.VMEM((seq, B_DK), F32),
                        pltpu.VMEM((seq, B_DK), BF16),
                        pltpu.VMEM((seq, B_DK), BF16),
                        pltpu.VMEM((seq, B_DK), BF16),
                        pltpu.VMEM((nc, B_DK), F32),
                        pltpu.VMEM((seq, B_DV), F32),
                        pltpu.VMEM((nc, B_DV, B_DK), F32),
                        pltpu.VMEM((nc, B_DV, B_DK), BF16)],
        compiler_params=_cparams("parallel", "parallel"),
        name="mixer_ab",
    )(proj_out, proj_out, proj_out, qg.reshape(1, hd), kg.reshape(1, hd), bias,
      proj_out, proj_out, proj_out, proj_out, lr_blk, lr_w, lr_b.reshape(1, B_KWIDTH),
      og.reshape(1, B_DV), *to_cast)


def _gelu(x):
    c = math.sqrt(2.0 / math.pi)
    return 0.5 * x * (1.0 + jnp.tanh(c * (x + 0.044715 * (x * x * x))))


SGU_SUB = 2 * C_CHUNK


def _sgu_kernel(*refs, n_cast):
    x_ref, g1_ref, wsrc_ref, lg_ref, lb_ref, w_ref, b_ref = refs[:7]
    o_ref, lr_ref, xn_ref = refs[7 + n_cast:10 + n_cast]
    wc_ref, luv_ref = refs[-2:]

    @pl.when(pl.program_id(0) == 0)
    def _():
        wc_ref[...] = wsrc_ref[...].astype(BF16)

    _cast_all(refs[7:7 + n_cast], refs[10 + n_cast:10 + 2 * n_cast])
    ti = lax.broadcasted_iota(jnp.int32, (C_CHUNK, C_CHUNK), 0)
    si = lax.broadcasted_iota(jnp.int32, (C_CHUNK, C_CHUNK), 1)
    causal = si <= ti
    ws = [jnp.where(causal, w_ref[g], 0.0).astype(BF16) for g in range(C_GROUPS)]
    n_sub = x_ref.shape[0] // SGU_SUB

    def project(s):
        rs = slice(s * SGU_SUB, (s + 1) * SGU_SUB)
        x = x_ref[rs, :]
        ms = jnp.mean(x * x, axis=-1, keepdims=True)
        xn = (x * lax.rsqrt(ms + EPS) * g1_ref[...]).astype(BF16)
        xn_ref[rs, :] = xn
        luv = lax.dot_general(xn, wc_ref[...], (((1,), (1,)), ((), ())), preferred_element_type=F32)
        luv_ref[s % 2] = luv
        lr_ref[rs, :] = luv[:, :LANES].astype(lr_ref.dtype)

    def gate(s):
        for n in range(SGU_SUB // C_CHUNK):
            rl = slice(n * C_CHUNK, (n + 1) * C_CHUNK)
            ro = slice(s * SGU_SUB + n * C_CHUNK, s * SGU_SUB + (n + 1) * C_CHUNK)
            v = _gelu(luv_ref[s % 2, rl, LANES + C_WIDTH:])
            mu = jnp.mean(v, axis=-1, keepdims=True)
            vc = v - mu
            vn = vc * lax.rsqrt(jnp.mean(vc * vc, axis=-1, keepdims=True) + EPS)
            vn = (vn * lg_ref[...] + lb_ref[...]).astype(BF16)
            u = _gelu(luv_ref[s % 2, rl, LANES:LANES + C_WIDTH])
            for g in range(C_GROUPS):
                cols = slice(g * C_GROUP_DIM, (g + 1) * C_GROUP_DIM)
                mixed = jnp.dot(ws[g], vn[:, cols], preferred_element_type=F32) + b_ref[g]
                o_ref[ro, cols] = (u[:, cols] * mixed).astype(o_ref.dtype)

    project(0)
    for s in range(n_sub):
        if s + 1 < n_sub:
            project(s + 1)
        gate(s)


def sgu(x, g1, w_t, ln_g, ln_b, w_s, b_s, layer, to_cast, *, tm):
    m, d = x.shape
    cast_in_specs, cast_out_specs, cast_shapes = _cast_specs(to_cast, layer, m // tm, lambda i: i)
    return pl.pallas_call(
        functools.partial(_sgu_kernel, n_cast=len(to_cast)),
        grid=(m // tm,),
        in_specs=[pl.BlockSpec((tm, d), lambda i: (i, 0)),
                  pl.BlockSpec((1, d), lambda i: (0, 0)),
                  pl.BlockSpec((None, pl.Element(C_WIN), pl.Element(d)), lambda i: (layer, C_SRC, 0),
                               pipeline_mode=pl.Buffered(1)),
                  pl.BlockSpec((1, C_WIDTH), lambda i: (0, 0)),
                  pl.BlockSpec((1, C_WIDTH), lambda i: (0, 0)),
                  pl.BlockSpec((C_GROUPS, C_CHUNK, C_CHUNK), lambda i: (0, 0, 0)),
                  pl.BlockSpec((C_GROUPS, C_CHUNK, 1), lambda i: (0, 0, 0))] + cast_in_specs,
        out_specs=[pl.BlockSpec((tm, C_WIDTH), lambda i: (i, 0)),
                   pl.BlockSpec((tm, LANES), lambda i: (i, 0)),
                   pl.BlockSpec((tm, d), lambda i: (i, 0))] + cast_out_specs,
        out_shape=[jax.ShapeDtypeStruct((m, C_WIDTH), BF16),
                   jax.ShapeDtypeStruct((m, LANES), BF16),
                   jax.ShapeDtypeStruct((m, d), BF16)] + cast_shapes,
        scratch_shapes=[pltpu.VMEM((C_WIN, d), BF16),
                        pltpu.VMEM((2, SGU_SUB, C_WIN), F32)],
        compiler_params=_cparams("arbitrary"),
        name="sgu",
    )(x, g1.reshape(1, d), w_t, ln_g.reshape(1, C_WIDTH), ln_b.reshape(1, C_WIDTH), w_s,
      b_s.reshape(C_GROUPS, C_CHUNK, 1), *to_cast)


def _merge_kernel(ya_ref, yb_ref, yc_ref, ga_ref, gb_ref, gc_ref, x_ref,
                  wa_ref, wb_ref, wc_ref, wo_ref, g2_ref, o_ref, on_ref):
    def gated(g_ref, y_ref, w_ref):
        return (jax.nn.sigmoid(g_ref[...].astype(F32))
                * jnp.dot(y_ref[...], w_ref[...], preferred_element_type=F32))

    merged = gated(ga_ref, ya_ref, wa_ref) + gated(gb_ref, yb_ref, wb_ref) + gated(gc_ref, yc_ref, wc_ref)
    h = x_ref[...] + jnp.dot(merged.astype(BF16), wo_ref[...], preferred_element_type=F32)
    o_ref[...] = h
    ms = jnp.mean(h * h, axis=-1, keepdims=True)
    on_ref[...] = (h * lax.rsqrt(ms + EPS) * g2_ref[...]).astype(on_ref.dtype)


def merge(ya, yb, yc, proj_out, x, wa, wb, wc, wo, g2, *, tm):
    m, d = x.shape
    resident = functools.partial(pl.BlockSpec, pipeline_mode=pl.Buffered(1))
    return pl.pallas_call(
        _merge_kernel,
        grid=(m // tm,),
        in_specs=[pl.BlockSpec((tm, A_WIDTH), lambda i: (i, 0)),
                  pl.BlockSpec((tm, B_VWIDTH), lambda i: (i, 0)),
                  pl.BlockSpec((tm, C_WIDTH), lambda i: (i, 0)),
                  pl.BlockSpec((tm, d), lambda i: (i, 0)),
                  pl.BlockSpec((tm, d), lambda i: (i, 1)),
                  pl.BlockSpec((tm, d), lambda i: (i, 2)),
                  pl.BlockSpec((tm, d), lambda i: (i, 0)),
                  resident((A_WIDTH, d), lambda i: (0, 0)),
                  resident((B_VWIDTH, d), lambda i: (0, 0)),
                  resident((C_WIDTH, d), lambda i: (0, 0)),
                  resident((d, d), lambda i: (0, 0)),
                  pl.BlockSpec((1, d), lambda i: (0, 0))],
        out_specs=[pl.BlockSpec((tm, d), lambda i: (i, 0)),
                   pl.BlockSpec((tm, d), lambda i: (i, 0))],
        out_shape=[jax.ShapeDtypeStruct((m, d), F32),
                   jax.ShapeDtypeStruct((m, d), BF16)],
        compiler_params=_cparams("parallel"),
        name="merge",
    )(ya, yb, yc, proj_out, proj_out, proj_out, x, wa, wb, wc, wo, g2.reshape(1, d))


def _mlp_kernel(h_hbm, hn_ref, w1_ref, w2_ref, o_ref, h_ref, h_sem):
    i, j = pl.program_id(0), pl.program_id(1)
    last = pl.num_programs(1) - 1
    tm = h_ref.shape[0]
    h_copy = pltpu.make_async_copy(h_hbm.at[pl.ds(i * tm, tm), :], h_ref, h_sem)

    def update():
        a = jnp.dot(hn_ref[...], w1_ref[...], preferred_element_type=F32)
        a = jnp.square(jnp.maximum(a, 0.0)).astype(BF16)
        return jnp.dot(a, w2_ref[...], preferred_element_type=F32)

    @pl.when(j == 0)
    def _():
        h_copy.start()
        o_ref[...] = update()

    @pl.when((j > 0) & (j < last))
    def _():
        o_ref[...] += update()

    @pl.when(j == last)
    def _():
        h_copy.wait()
        o_ref[...] = (o_ref[...] + h_ref[...]) + update()


def mlp(h, hn, w1, w2, *, tm, tf):
    m, d = h.shape
    f = w1.shape[1]
    return pl.pallas_call(
        _mlp_kernel,
        grid=(m // tm, f // tf),
        in_specs=[pl.BlockSpec(memory_space=pl.ANY),
                  pl.BlockSpec((tm, d), lambda i, j: (i, 0)),
                  pl.BlockSpec((d, tf), lambda i, j: (0, j)),
                  pl.BlockSpec((tf, d), lambda i, j: (j, 0))],
        out_specs=pl.BlockSpec((tm, d), lambda i, j: (i, 0)),
        out_shape=jax.ShapeDtypeStruct((m, d), F32),
        scratch_shapes=[pltpu.VMEM((tm, d), F32),
                        pltpu.SemaphoreType.DMA(())],
        compiler_params=_cparams("parallel", "arbitrary"),
        name="mlp",
    )(h, hn, w1, w2)


def kernel(x, rpe_table, norm1_g, w_in, q_norm_g, k_norm_g, gla_lr_w, gla_lr_b, gla_out_g,
           sg_ln_g, sg_ln_b, sg_w, sg_b, w_br_a, w_br_b, w_br_c, w_o, norm2_g, w_ff1, w_ff2):
    bsz, seq, d = x.shape
    assert seq % A_BLOCK == 0 and seq % B_CHUNK == 0 and seq % C_CHUNK == 0
    assert w_in.shape[2] == IN_WIDTH
    h = x.reshape(bsz * seq, d)
    bias = rpe_bias(rpe_table)
    w_t = jnp.swapaxes(w_in, 1, 2)
    lr_w = jnp.pad(gla_lr_w, ((0, 0), (LR_LANE, LANES - LR_LANE - B_LOWRANK), (0, 0)))
    for l in range(DEPTH):
        y_c, lr_blk, xn, wa, wb, wc, wo = sgu(h, norm1_g[l], w_t, sg_ln_g[l], sg_ln_b[l], sg_w[l], sg_b[l],
                                              l, (w_br_a, w_br_b, w_br_c, w_o), tm=SGU_TM)
        p, w1 = proj(xn, w_t, l, (w_ff1,), tm=PROJ_TM)
        y_a, y_b, w2 = mixer_ab(p, lr_blk, q_norm_g[l], k_norm_g[l], bias, lr_w, gla_lr_b[l],
                                gla_out_g[l], l, (w_ff2,), bsz=bsz, seq=seq)
        h, hn = merge(y_a, y_b, y_c, p, h, wa, wb, wc, wo, norm2_g[l], tm=MERGE_TM)
        h = mlp(h, hn, w1, w2, tm=MLP_TM, tf=MLP_TF)
    return h.reshape(bsz, seq, d)
```

```python
import functools
import math

import jax
import jax.numpy as jnp
from jax import lax
from jax.experimental import pallas as pl
from jax.experimental.pallas import tpu as pltpu

D_MODEL = 2048
DEPTH = 2
HEAD_DIM = 128
A_HEADS = 4
A_WIDTH = A_HEADS * HEAD_DIM
A_BLOCK = 256
A_TOPK = 3
RPE_BUCKETS = 32
RPE_MAX_DIST = 128
B_HEADS = 4
B_DK = 128
B_DV = 256
B_KWIDTH = B_HEADS * B_DK
B_VWIDTH = B_HEADS * B_DV
B_LOWRANK = 16
B_GATE_NORM = 16.0
B_CHUNK = 64
C_GROUPS = 4
C_GROUP_DIM = 128
C_WIDTH = C_GROUPS * C_GROUP_DIM
C_CHUNK = 128
D_FF = 4 * D_MODEL
EPS = 1e-6
NEG_INF = -1e30
LOG2E = 1.4426950408889634

LANES = 128
BF16_ROWS = 16
VMEM_LIMIT = 56 * 1024 * 1024

F32 = jnp.float32
BF16 = jnp.bfloat16

PROJ_TM = 1024
MLP_TM = 1024
MLP_TF = 1024
MERGE_TM = 256
SGU_TM = 1024

PROJ_TN = 1536
AB_WIDTH = 3 * A_WIDTH + 2 * B_KWIDTH + 2 * B_VWIDTH
G_WIDTH = 3 * D_MODEL
IN_WIDTH = AB_WIDTH + B_LOWRANK + 2 * C_WIDTH + G_WIDTH
AB0 = G_WIDTH
PROJ_WIDTH = G_WIDTH + AB_WIDTH
N_AB_TILES = AB_WIDTH // PROJ_TN
N_G_TILES = G_WIDTH // PROJ_TN
G_SRC = IN_WIDTH - G_WIDTH
C_PAD = LANES - B_LOWRANK
C_SRC = AB_WIDTH - C_PAD
C_WIN = LANES + 2 * C_WIDTH
LR_LANE = C_PAD
assert AB_WIDTH % PROJ_TN == 0 and G_WIDTH % PROJ_TN == 0
assert C_SRC % BF16_ROWS == 0 and G_SRC % BF16_ROWS == 0


def _rpe_thresholds():
    max_exact = RPE_BUCKETS // 2

    def bucket(n):
        if n < max_exact:
            return n
        v = math.log(n / max_exact) / math.log(RPE_MAX_DIST / max_exact) * (RPE_BUCKETS - max_exact)
        return min(max_exact + int(v), RPE_BUCKETS - 1)

    table = [bucket(n) for n in range(4 * RPE_MAX_DIST)]
    return [min(n for n in range(len(table)) if table[n] >= b)
            for b in range(max_exact + 1, RPE_BUCKETS)]


RPE_THRESHOLDS = _rpe_thresholds()
assert RPE_THRESHOLDS[-1] <= A_BLOCK


def _cparams(*sem):
    return pltpu.CompilerParams(dimension_semantics=sem, vmem_limit_bytes=VMEM_LIMIT)


def _rpe_bias_kernel(tbl_ref, bias_ref):
    h = pl.program_id(0)
    t = lax.broadcasted_iota(jnp.int32, (A_BLOCK, 2 * A_BLOCK), 0)
    c = lax.broadcasted_iota(jnp.int32, (A_BLOCK, 2 * A_BLOCK), 1)
    n = jnp.maximum(t - c + A_BLOCK, 0)
    max_exact = RPE_BUCKETS // 2
    large = jnp.full(n.shape, max_exact, jnp.int32)
    for thr in RPE_THRESHOLDS:
        large = large + (n >= thr).astype(jnp.int32)
    bucket = jnp.where(n < max_exact, n, large)
    bias = jnp.zeros(n.shape, F32)
    for b in range(RPE_BUCKETS):
        bias = jnp.where(bucket == b, tbl_ref[b, h], bias)
    bias = (bias - tbl_ref[RPE_BUCKETS - 1, h]) * LOG2E
    bias_ref[0] = jnp.where(c - A_BLOCK > t, NEG_INF, bias)


def rpe_bias(rpe_table):
    return pl.pallas_call(
        _rpe_bias_kernel,
        grid=(A_HEADS,),
        in_specs=[pl.BlockSpec(memory_space=pltpu.SMEM)],
        out_specs=pl.BlockSpec((1, A_BLOCK, 2 * A_BLOCK), lambda h: (h, 0, 0)),
        out_shape=jax.ShapeDtypeStruct((A_HEADS, A_BLOCK, 2 * A_BLOCK), F32),
        compiler_params=_cparams("arbitrary"),
        name="rpe_bias",
    )(rpe_table)


def _cast_specs(to_cast, layer, n_steps, step_of):
    in_specs, out_specs, shapes = [], [], []
    for w in to_cast:
        _, rows, cols = w.shape
        slice_rows = next(r for r in range(BF16_ROWS, rows + 1, BF16_ROWS)
                          if rows % r == 0 and rows // r <= n_steps)
        n_slices = rows // slice_rows

        def slice_of(*idx, n_slices=n_slices):
            return jnp.minimum(step_of(*idx), n_slices - 1)

        in_specs.append(pl.BlockSpec((None, slice_rows, cols), lambda *idx, f=slice_of: (layer, f(*idx), 0)))
        out_specs.append(pl.BlockSpec((slice_rows, cols), lambda *idx, f=slice_of: (f(*idx), 0)))
        shapes.append(jax.ShapeDtypeStruct((rows, cols), BF16))
    return in_specs, out_specs, shapes


def _cast_all(srcs, dsts):
    for src, dst in zip(srcs, dsts):
        dst[...] = src[...].astype(dst.dtype)


def _proj_kernel(*refs, n_cast):
    xn_ref, w_ref = refs[:2]
    o_ref = refs[2 + n_cast]
    o_ref[...] = lax.dot_general(xn_ref[...], w_ref[...].astype(BF16), (((1,), (1,)), ((), ())),
                                 preferred_element_type=F32).astype(o_ref.dtype)
    _cast_all(refs[2:2 + n_cast], refs[3 + n_cast:])


def _proj_src_row(j):
    u = BF16_ROWS
    return u * jnp.where(j < N_AB_TILES, j * (PROJ_TN // u),
                         G_SRC // u + (j - N_AB_TILES) * (PROJ_TN // u))


def _proj_out_tile(j):
    return jnp.where(j < N_AB_TILES, j + N_G_TILES, j - N_AB_TILES)


def proj(xn, w_t, layer, to_cast, *, tm):
    m, k = xn.shape
    tn = PROJ_TN
    n_tiles = N_AB_TILES + N_G_TILES
    cast_in_specs, cast_out_specs, cast_shapes = _cast_specs(
        to_cast, layer, (m // tm) * n_tiles, lambda i, j: i * n_tiles + j)
    return pl.pallas_call(
        functools.partial(_proj_kernel, n_cast=len(to_cast)),
        grid=(m // tm, n_tiles),
        in_specs=[pl.BlockSpec((tm, k), lambda i, j: (i, 0)),
                  pl.BlockSpec((None, pl.Element(tn), pl.Element(k)),
                               lambda i, j: (layer, _proj_src_row(j), 0))] + cast_in_specs,
        out_specs=[pl.BlockSpec((tm, tn), lambda i, j: (i, _proj_out_tile(j)))] + cast_out_specs,
        out_shape=[jax.ShapeDtypeStruct((m, PROJ_WIDTH), BF16)] + cast_shapes,
        compiler_params=_cparams("parallel", "arbitrary"),
        name="proj",
    )(xn, w_t, *to_cast)


def _head_rms(x, g):
    ms = jnp.mean(x * x, axis=-1, keepdims=True)
    return x * lax.rsqrt(ms + EPS) * g


def _moba_stages(q_ref, k_ref, v_ref, qg_ref, kg_ref, bias_ref, o_ref,
                 qn_ref, qa_ref, ka_ref, va_ref, kmean_ref, s_ref, p_ref):
    seq, hd = q_ref.shape
    nb = seq // A_BLOCK
    rc = BF16_ROWS

    def prologue():
        lane = lax.broadcasted_iota(jnp.int32, (A_BLOCK, hd), 1)
        for j in range(nb):
            rows = slice(j * A_BLOCK, (j + 1) * A_BLOCK)
            qn = _head_rms(q_ref[rows, :].astype(F32), qg_ref[...])
            qn_ref[rows, :] = qn
            qa_ref[rows, :hd] = (qn * (hd ** -0.5 * LOG2E)).astype(BF16)
            qa_ref[rows, hd:] = jnp.zeros((A_BLOCK, hd), BF16)
            kn = _head_rms(k_ref[rows, :].astype(F32), kg_ref[...])
            ka_ref[rows, :hd] = kn.astype(BF16)
            ka_ref[rows, hd:] = jnp.where(lane == j, 1.0, 0.0).astype(BF16)
            va_ref[rows, :hd] = v_ref[rows, :]
            va_ref[rows, hd:] = jnp.ones((A_BLOCK, hd), BF16)
            kmean_ref[j:j + 1, :] = jnp.mean(kn, axis=0, keepdims=True)

    def scores(i):
        rows = slice(i * A_BLOCK, (i + 1) * A_BLOCK)
        nk = (i + 1) * A_BLOCK
        if i > A_TOPK:
            blk = lax.broadcasted_iota(jnp.int32, (nb, A_BLOCK), 0)
            eye = jnp.where(lax.broadcasted_iota(jnp.int32, (nb, hd), 0)
                            == lax.broadcasted_iota(jnp.int32, (nb, hd), 1), 1.0, 0.0)
            g = lax.dot_general(kmean_ref[...], qn_ref[rows, :], (((1,), (1,)), ((), ())),
                                precision=lax.Precision.HIGHEST, preferred_element_type=F32)
            rank = jnp.zeros((nb, A_BLOCK), jnp.int32)
            for jp in range(i):
                gj = g[jp:jp + 1, :]
                beats = (gj > g) | ((gj == g) & (jp < blk))
                rank = rank + beats.astype(jnp.int32)
            neg = jnp.where((blk < i) & (rank >= A_TOPK), NEG_INF, 0.0)
            qa_ref[rows, hd:] = lax.dot_general(neg, eye, (((0,), (0,)), ((), ())),
                                                preferred_element_type=F32).astype(BF16)
        s_ref[i % 2, :, :nk] = lax.dot_general(qa_ref[rows, :], ka_ref[:nk, :], (((1,), (1,)), ((), ())),
                                               preferred_element_type=F32)

    def softmax(i):
        for c in range(A_BLOCK // rc):
            r = slice(c * rc, (c + 1) * rc)
            tiles = []
            for j in range(i + 1):
                sj = s_ref[i % 2, r, j * A_BLOCK:(j + 1) * A_BLOCK]
                if j == i:
                    sj = sj + bias_ref[0, r, A_BLOCK:]
                elif j == i - 1:
                    sj = sj + bias_ref[0, r, :A_BLOCK]
                tiles.append(sj)
            mt = tiles[0]
            for t in tiles[1:]:
                mt = jnp.maximum(mt, t)
            m = jnp.max(mt, axis=-1, keepdims=True)
            for j, t in enumerate(tiles):
                p_ref[i % 2, r, j * A_BLOCK:(j + 1) * A_BLOCK] = jnp.exp2(t - m).astype(BF16)

    def weighted_sum(i):
        rows = slice(i * A_BLOCK, (i + 1) * A_BLOCK)
        nk = (i + 1) * A_BLOCK
        o = jnp.dot(p_ref[i % 2, :, :nk], va_ref[:nk, :], preferred_element_type=F32)
        o_ref[rows, :] = (o[:, :hd] / o[:, hd:]).astype(o_ref.dtype)

    stages = [prologue, functools.partial(scores, 0)]
    for i in range(nb):
        if i + 1 < nb:
            stages.append(functools.partial(scores, i + 1))
        stages += [functools.partial(softmax, i), functools.partial(weighted_sum, i)]
    return stages


def _gla_stages(q_ref, k_ref, v_ref, r_ref, lr_ref, lrw_ref, lrb_ref, og_ref, o_ref,
                hml_ref, bc_ref, qd_ref, ki_ref, ke_ref, dec_ref, oi_ref, ss_ref, sb_ref):
    seq = q_ref.shape[0]
    ck = B_CHUNK
    nc = seq // ck
    grp = 4 * ck
    ng = seq // grp
    cpg = grp // ck
    qscale = B_DK ** -0.5
    nt = (((1,), (1,)), ((), ()))
    tn = (((0,), (0,)), ((), ()))

    def same_chunk_causal():
        ti = lax.broadcasted_iota(jnp.int32, (grp, grp), 0)
        si = lax.broadcasted_iota(jnp.int32, (grp, grp), 1)
        return (si <= ti) & (si // ck == ti // ck)

    def log_decay(g):
        rows = slice(g * grp, (g + 1) * grp)
        z = jnp.dot(lr_ref[rows, :], lrw_ref[...].astype(BF16), preferred_element_type=F32) + lrb_ref[...]
        la = (jnp.minimum(z, 0.0) - jnp.log(1.0 + jnp.exp(-jnp.abs(z)))) * (1.0 / B_GATE_NORM)
        hi = la.astype(BF16)
        r1 = la - hi.astype(F32)
        mid = r1.astype(BF16)
        hml_ref[rows, :B_DK] = hi
        hml_ref[rows, B_DK:2 * B_DK] = mid
        hml_ref[rows, 2 * B_DK:] = (r1 - mid.astype(F32)).astype(BF16)

    def cumsum(g):
        rows = slice(g * grp, (g + 1) * grp)
        tril_grp = jnp.where(same_chunk_causal(), 1.0, 0.0).astype(BF16)
        parts = jnp.dot(tril_grp, hml_ref[rows, :], preferred_element_type=F32)
        bc_ref[rows, :] = parts[:, :B_DK] + parts[:, B_DK:2 * B_DK] + parts[:, 2 * B_DK:]

    def decayed_qk(g):
        for c in range(g * cpg, (g + 1) * cpg):
            rows = slice(c * ck, (c + 1) * ck)
            bc = bc_ref[rows, :]
            bl = bc_ref[(c + 1) * ck - 1:(c + 1) * ck, :]
            q = q_ref[rows, :].astype(F32)
            k = k_ref[rows, :].astype(F32)
            qd_ref[rows, :] = (q * jnp.exp(bc) * qscale).astype(BF16)
            ki_ref[rows, :] = (k * jnp.exp(-bc)).astype(BF16)
            ke_ref[rows, :] = (k * jnp.exp(bl - bc)).astype(BF16)
            dec_ref[c:c + 1, :] = jnp.exp(bl)

    def intra(g):
        rows = slice(g * grp, (g + 1) * grp)
        a = lax.dot_general(qd_ref[rows, :], ki_ref[rows, :], nt, preferred_element_type=F32)
        a = jnp.where(same_chunk_causal(), a, 0.0).astype(BF16)
        oi_ref[rows, :] = jnp.dot(a, v_ref[rows, :], preferred_element_type=F32)

    def state_terms(g):
        for c in range(g * cpg, (g + 1) * cpg):
            rows = slice(c * ck, (c + 1) * ck)
            ss_ref[c] = lax.dot_general(v_ref[rows, :], ke_ref[rows, :], tn, preferred_element_type=F32)

    def scan():
        slab = B_DV // 4
        st = [jnp.zeros((slab, B_DK), F32) for _ in range(4)]
        for c in range(nc):
            dec = dec_ref[c:c + 1, :]
            for s in range(4):
                r = slice(s * slab, (s + 1) * slab)
                sb_ref[c, r, :] = st[s].astype(BF16)
                st[s] = st[s] * dec + ss_ref[c, r, :]

    def outputs(g):
        for c in range(g * cpg, (g + 1) * cpg):
            rows = slice(c * ck, (c + 1) * ck)
            o = oi_ref[rows, :] + lax.dot_general(qd_ref[rows, :], sb_ref[c], nt,
                                                  preferred_element_type=F32)
            ms = jnp.mean(o * o, axis=-1, keepdims=True)
            y = o * lax.rsqrt(ms + EPS) * og_ref[...]
            r = r_ref[rows, :].astype(F32)
            o_ref[rows, :] = (y * (r * jax.nn.sigmoid(r))).astype(o_ref.dtype)

    groups = [[functools.partial(fn, g) for g in range(ng)]
              for fn in (log_decay, cumsum, decayed_qk, intra, state_terms)]
    return groups + [[scan], [functools.partial(outputs, g) for g in range(ng)]]


N_MOBA_IN, N_MOBA_SCRATCH = 6, 7
N_GLA_IN = 8


def _mixer_ab_kernel(*refs, n_cast):
    n_in = N_MOBA_IN + N_GLA_IN
    ins, cast_in = refs[:n_in], refs[n_in:n_in + n_cast]
    oa_ref, ob_ref = refs[n_in + n_cast:n_in + n_cast + 2]
    cast_out = refs[n_in + n_cast + 2:n_in + 2 * n_cast + 2]
    scratch = refs[n_in + 2 * n_cast + 2:]
    moba_stages = _moba_stages(*ins[:N_MOBA_IN], oa_ref, *scratch[:N_MOBA_SCRATCH])
    gla_steps = [fn for group in _gla_stages(*ins[N_MOBA_IN:], ob_ref, *scratch[N_MOBA_SCRATCH:])
                 for fn in group]
    per_slot = -(-len(gla_steps) // len(moba_stages))
    for n, stage in enumerate(moba_stages):
        stage()
        for fn in gla_steps[n * per_slot:(n + 1) * per_slot]:
            fn()
    _cast_all(cast_in, cast_out)


def mixer_ab(proj_out, lr_blk, qg, kg, bias, lr_w, lr_b, og, layer, to_cast, *, bsz, seq):
    hd = HEAD_DIM
    assert A_HEADS == B_HEADS
    qa0 = AB0 // hd
    qb0 = (AB0 + 3 * A_WIDTH) // B_DK
    kb0 = qb0 + B_HEADS
    vb0 = (AB0 + 3 * A_WIDTH + 2 * B_KWIDTH) // B_DV
    rb0 = vb0 + B_HEADS
    nc = seq // B_CHUNK
    cast_in_specs, cast_out_specs, cast_shapes = _cast_specs(
        to_cast, layer, bsz * A_HEADS, lambda b, h: b * A_HEADS + h)
    return pl.pallas_call(
        functools.partial(_mixer_ab_kernel, n_cast=len(to_cast)),
        grid=(bsz, A_HEADS),
        in_specs=[pl.BlockSpec((seq, hd), lambda b, h: (b, qa0 + h)),
                  pl.BlockSpec((seq, hd), lambda b, h: (b, qa0 + A_HEADS + h)),
                  pl.BlockSpec((seq, hd), lambda b, h: (b, qa0 + 2 * A_HEADS + h)),
                  pl.BlockSpec((1, hd), lambda b, h: (0, 0)),
                  pl.BlockSpec((1, hd), lambda b, h: (0, 0)),
                  pl.BlockSpec((1, A_BLOCK, 2 * A_BLOCK), lambda b, h: (h, 0, 0)),
                  pl.BlockSpec((seq, B_DK), lambda b, h: (b, qb0 + h)),
                  pl.BlockSpec((seq, B_DK), lambda b, h: (b, kb0 + h)),
                  pl.BlockSpec((seq, B_DV), lambda b, h: (b, vb0 + h)),
                  pl.BlockSpec((seq, B_DV), lambda b, h: (b, rb0 + h)),
                  pl.BlockSpec((seq, LANES), lambda b, h: (b, 0)),
                  pl.BlockSpec((None, LANES, B_DK), lambda b, h: (layer, 0, h)),
                  pl.BlockSpec((1, B_DK), lambda b, h: (0, h)),
                  pl.BlockSpec((1, B_DV), lambda b, h: (0, 0))] + cast_in_specs,
        out_specs=[pl.BlockSpec((seq, hd), lambda b, h: (b, h)),
                   pl.BlockSpec((seq, B_DV), lambda b, h: (b, h))] + cast_out_specs,
        out_shape=[jax.ShapeDtypeStruct((bsz * seq, A_WIDTH), BF16),
                   jax.ShapeDtypeStruct((bsz * seq, B_VWIDTH), BF16)] + cast_shapes,
        scratch_shapes=[pltpu.VMEM((seq, hd), F32),
                        pltpu.VMEM((seq, 2 * hd), BF16),
                        pltpu.VMEM((seq, 2 * hd), BF16),
                        pltpu.VMEM((seq, 2 * hd), BF16),
                        pltpu.VMEM((seq // A_BLOCK, hd), F32),
                        pltpu.VMEM((2, A_BLOCK, seq), F32),
                        pltpu.VMEM((2, A_BLOCK, seq), BF16),
                        pltpu.VMEM((seq, 3 * B_DK), BF16),
                        pltpu.VMEM((seq, B_DK), F32),
                        pltpu.VMEM((seq, B_DK), BF16),
                        pltpu.VMEM((seq, B_DK), BF16),
                        pltpu.VMEM((seq, B_DK), BF16),
                        pltpu.VMEM((nc, B_DK), F32),
                        pltpu.VMEM((seq, B_DV), F32),
                        pltpu.VMEM((nc, B_DV, B_DK), F32),
                        pltpu.VMEM((nc, B_DV, B_DK), BF16)],
        compiler_params=_cparams("parallel", "parallel"),
        name="mixer_ab",
    )(proj_out, proj_out, proj_out, qg.reshape(1, hd), kg.reshape(1, hd), bias,
      proj_out, proj_out, proj_out, proj_out, lr_blk, lr_w, lr_b.reshape(1, B_KWIDTH),
      og.reshape(1, B_DV), *to_cast)


def _gelu(x):
    c = math.sqrt(2.0 / math.pi)
    return 0.5 * x * (1.0 + jnp.tanh(c * (x + 0.044715 * (x * x * x))))


SGU_SUB = 2 * C_CHUNK


def _sgu_kernel(*refs, n_cast):
    x_ref, g1_ref, wsrc_ref, lg_ref, lb_ref, w_ref, b_ref = refs[:7]
    o_ref, lr_ref, xn_ref = refs[7 + n_cast:10 + n_cast]
    wc_ref, luv_ref = refs[-2:]

    @pl.when(pl.program_id(0) == 0)
    def _():
        wc_ref[...] = wsrc_ref[...].astype(BF16)

    _cast_all(refs[7:7 + n_cast], refs[10 + n_cast:10 + 2 * n_cast])
    ti = lax.broadcasted_iota(jnp.int32, (C_CHUNK, C_CHUNK), 0)
    si = lax.broadcasted_iota(jnp.int32, (C_CHUNK, C_CHUNK), 1)
    causal = si <= ti
    ws = [jnp.where(causal, w_ref[g], 0.0).astype(BF16) for g in range(C_GROUPS)]
    n_sub = x_ref.shape[0] // SGU_SUB

    def project(s):
        rs = slice(s * SGU_SUB, (s + 1) * SGU_SUB)
        x = x_ref[rs, :]
        ms = jnp.mean(x * x, axis=-1, keepdims=True)
        xn = (x * lax.rsqrt(ms + EPS) * g1_ref[...]).astype(BF16)
        xn_ref[rs, :] = xn
        luv = lax.dot_general(xn, wc_ref[...], (((1,), (1,)), ((), ())), preferred_element_type=F32)
        luv_ref[s % 2] = luv
        lr_ref[rs, :] = luv[:, :LANES].astype(lr_ref.dtype)

    def gate(s):
        for n in range(SGU_SUB // C_CHUNK):
            rl = slice(n * C_CHUNK, (n + 1) * C_CHUNK)
            ro = slice(s * SGU_SUB + n * C_CHUNK, s * SGU_SUB + (n + 1) * C_CHUNK)
            v = _gelu(luv_ref[s % 2, rl, LANES + C_WIDTH:])
            mu = jnp.mean(v, axis=-1, keepdims=True)
            vc = v - mu
            vn = vc * lax.rsqrt(jnp.mean(vc * vc, axis=-1, keepdims=True) + EPS)
            vn = (vn * lg_ref[...] + lb_ref[...]).astype(BF16)
            u = _gelu(luv_ref[s % 2, rl, LANES:LANES + C_WIDTH])
            for g in range(C_GROUPS):
                cols = slice(g * C_GROUP_DIM, (g + 1) * C_GROUP_DIM)
                mixed = jnp.dot(ws[g], vn[:, cols], preferred_element_type=F32) + b_ref[g]
                o_ref[ro, cols] = (u[:, cols] * mixed).astype(o_ref.dtype)

    project(0)
    for s in range(n_sub):
        if s + 1 < n_sub:
            project(s + 1)
        gate(s)


def sgu(x, g1, w_t, ln_g, ln_b, w_s, b_s, layer, to_cast, *, tm):
    m, d = x.shape
    cast_in_specs, cast_out_specs, cast_shapes = _cast_specs(to_cast, layer, m // tm, lambda i: i)
    return pl.pallas_call(
        functools.partial(_sgu_kernel, n_cast=len(to_cast)),
        grid=(m // tm,),
        in_specs=[pl.BlockSpec((tm, d), lambda i: (i, 0)),
                  pl.BlockSpec((1, d), lambda i: (0, 0)),
                  pl.BlockSpec((None, pl.Element(C_WIN), pl.Element(d)), lambda i: (layer, C_SRC, 0),
                               pipeline_mode=pl.Buffered(1)),
                  pl.BlockSpec((1, C_WIDTH), lambda i: (0, 0)),
                  pl.BlockSpec((1, C_WIDTH), lambda i: (0, 0)),
                  pl.BlockSpec((C_GROUPS, C_CHUNK, C_CHUNK), lambda i: (0, 0, 0)),
                  pl.BlockSpec((C_GROUPS, C_CHUNK, 1), lambda i: (0, 0, 0))] + cast_in_specs,
        out_specs=[pl.BlockSpec((tm, C_WIDTH), lambda i: (i, 0)),
                   pl.BlockSpec((tm, LANES), lambda i: (i, 0)),
                   pl.BlockSpec((tm, d), lambda i: (i, 0))] + cast_out_specs,
        out_shape=[jax.ShapeDtypeStruct((m, C_WIDTH), BF16),
                   jax.ShapeDtypeStruct((m, LANES), BF16),
                   jax.ShapeDtypeStruct((m, d), BF16)] + cast_shapes,
        scratch_shapes=[pltpu.VMEM((C_WIN, d), BF16),
                        pltpu.VMEM((2, SGU_SUB, C_WIN), F32)],
        compiler_params=_cparams("arbitrary"),
        name="sgu",
    )(x, g1.reshape(1, d), w_t, ln_g.reshape(1, C_WIDTH), ln_b.reshape(1, C_WIDTH), w_s,
      b_s.reshape(C_GROUPS, C_CHUNK, 1), *to_cast)


def _merge_kernel(ya_ref, yb_ref, yc_ref, ga_ref, gb_ref, gc_ref, x_ref,
                  wa_ref, wb_ref, wc_ref, wo_ref, g2_ref, o_ref, on_ref):
    def gated(g_ref, y_ref, w_ref):
        return (jax.nn.sigmoid(g_ref[...].astype(F32))
                * jnp.dot(y_ref[...], w_ref[...], preferred_element_type=F32))

    merged = gated(ga_ref, ya_ref, wa_ref) + gated(gb_ref, yb_ref, wb_ref) + gated(gc_ref, yc_ref, wc_ref)
    h = x_ref[...] + jnp.dot(merged.astype(BF16), wo_ref[...], preferred_element_type=F32)
    o_ref[...] = h
    ms = jnp.mean(h * h, axis=-1, keepdims=True)
    on_ref[...] = (h * lax.rsqrt(ms + EPS) * g2_ref[...]).astype(on_ref.dtype)


def merge(ya, yb, yc, proj_out, x, wa, wb, wc, wo, g2, *, tm):
    m, d = x.shape
    resident = functools.partial(pl.BlockSpec, pipeline_mode=pl.Buffered(1))
    return pl.pallas_call(
        _merge_kernel,
        grid=(m // tm,),
        in_specs=[pl.BlockSpec((tm, A_WIDTH), lambda i: (i, 0)),
                  pl.BlockSpec((tm, B_VWIDTH), lambda i: (i, 0)),
                  pl.BlockSpec((tm, C_WIDTH), lambda i: (i, 0)),
                  pl.BlockSpec((tm, d), lambda i: (i, 0)),
                  pl.BlockSpec((tm, d), lambda i: (i, 1)),
                  pl.BlockSpec((tm, d), lambda i: (i, 2)),
                  pl.BlockSpec((tm, d), lambda i: (i, 0)),
                  resident((A_WIDTH, d), lambda i: (0, 0)),
                  resident((B_VWIDTH, d), lambda i: (0, 0)),
                  resident((C_WIDTH, d), lambda i: (0, 0)),
                  resident((d, d), lambda i: (0, 0)),
                  pl.BlockSpec((1, d), lambda i: (0, 0))],
        out_specs=[pl.BlockSpec((tm, d), lambda i: (i, 0)),
                   pl.BlockSpec((tm, d), lambda i: (i, 0))],
        out_shape=[jax.ShapeDtypeStruct((m, d), F32),
                   jax.ShapeDtypeStruct((m, d), BF16)],
        compiler_params=_cparams("parallel"),
        name="merge",
    )(ya, yb, yc, proj_out, proj_out, proj_out, x, wa, wb, wc, wo, g2.reshape(1, d))


def _mlp_kernel(h_hbm, hn_ref, w1_ref, w2_ref, o_ref, h_ref, h_sem):
    i, j = pl.program_id(0), pl.program_id(1)
    last = pl.num_programs(1) - 1
    tm = h_ref.shape[0]
    h_copy = pltpu.make_async_copy(h_hbm.at[pl.ds(i * tm, tm), :], h_ref, h_sem)

    def update():
        a = jnp.dot(hn_ref[...], w1_ref[...], preferred_element_type=F32)
        a = jnp.square(jnp.maximum(a, 0.0)).astype(BF16)
        return jnp.dot(a, w2_ref[...], preferred_element_type=F32)

    @pl.when(j == 0)
    def _():
        h_copy.start()
        o_ref[...] = update()

    @pl.when((j > 0) & (j < last))
    def _():
        o_ref[...] += update()

    @pl.when(j == last)
    def _():
        h_copy.wait()
        o_ref[...] = (o_ref[...] + h_ref[...]) + update()


def mlp(h, hn, w1, w2, *, tm, tf):
    m, d = h.shape
    f = w1.shape[1]
    return pl.pallas_call(
        _mlp_kernel,
        grid=(m // tm, f // tf),
        in_specs=[pl.BlockSpec(memory_space=pl.ANY),
                  pl.BlockSpec((tm, d), lambda i, j: (i, 0)),
                  pl.BlockSpec((d, tf), lambda i, j: (0, j)),
                  pl.BlockSpec((tf, d), lambda i, j: (j, 0))],
        out_specs=pl.BlockSpec((tm, d), lambda i, j: (i, 0)),
        out_shape=jax.ShapeDtypeStruct((m, d), F32),
        scratch_shapes=[pltpu.VMEM((tm, d), F32),
                        pltpu.SemaphoreType.DMA(())],
        compiler_params=_cparams("parallel", "arbitrary"),
        name="mlp",
    )(h, hn, w1, w2)


def kernel(x, rpe_table, norm1_g, w_in, q_norm_g, k_norm_g, gla_lr_w, gla_lr_b, gla_out_g,
           sg_ln_g, sg_ln_b, sg_w, sg_b, w_br_a, w_br_b, w_br_c, w_o, norm2_g, w_ff1, w_ff2):
    bsz, seq, d = x.shape
    assert seq % A_BLOCK == 0 and seq % B_CHUNK == 0 and seq % C_CHUNK == 0
    assert w_in.shape[2] == IN_WIDTH
    h = x.reshape(bsz * seq, d)
    bias = rpe_bias(rpe_table)
    w_t = jnp.swapaxes(w_in, 1, 2)
    lr_w = jnp.pad(gla_lr_w, ((0, 0), (LR_LANE, LANES - LR_LANE - B_LOWRANK), (0, 0)))
    for l in range(DEPTH):
        y_c, lr_blk, xn, wa, wb, wc, wo = sgu(h, norm1_g[l], w_t, sg_ln_g[l], sg_ln_b[l], sg_w[l], sg_b[l],
                                              l, (w_br_a, w_br_b, w_br_c, w_o), tm=SGU_TM)
        p, w1 = proj(xn, w_t, l, (w_ff1,), tm=PROJ_TM)
        y_a, y_b, w2 = mixer_ab(p, lr_blk, q_norm_g[l], k_norm_g[l], bias, lr_w, gla_lr_b[l],
                                gla_out_g[l], l, (w_ff2,), bsz=bsz, seq=seq)
        h, hn = merge(y_a, y_b, y_c, p, h, wa, wb, wc, wo, norm2_g[l], tm=MERGE_TM)
        h = mlp(h, hn, w1, w2, tm=MLP_TM, tf=MLP_TF)
    return h.reshape(bsz, seq, d)
```

```python
import functools
import math

import jax
import jax.numpy as jnp
from jax import lax
from jax.experimental import pallas as pl
from jax.experimental.pallas import tpu as pltpu

D_MODEL = 2048
DEPTH = 2
HEAD_DIM = 128
A_HEADS = 4
A_WIDTH = A_HEADS * HEAD_DIM
A_BLOCK = 256
A_TOPK = 3
RPE_BUCKETS = 32
RPE_MAX_DIST = 128
B_HEADS = 4
B_DK = 128
B_DV = 256
B_KWIDTH = B_HEADS * B_DK
B_VWIDTH = B_HEADS * B_DV
B_LOWRANK = 16
B_GATE_NORM = 16.0
B_CHUNK = 64
C_GROUPS = 4
C_GROUP_DIM = 128
C_WIDTH = C_GROUPS * C_GROUP_DIM
C_CHUNK = 128
D_FF = 4 * D_MODEL
EPS = 1e-6
NEG_INF = -1e30
LOG2E = 1.4426950408889634

LANES = 128
BF16_ROWS = 16
VMEM_LIMIT = 56 * 1024 * 1024

F32 = jnp.float32
BF16 = jnp.bfloat16

PROJ_TM = 1024
MLP_TM = 1024
MLP_TF = 1024
MERGE_TM = 256
SGU_TM = 1024

PROJ_TN = 1536
AB_WIDTH = 3 * A_WIDTH + 2 * B_KWIDTH + 2 * B_VWIDTH
G_WIDTH = 3 * D_MODEL
IN_WIDTH = AB_WIDTH + B_LOWRANK + 2 * C_WIDTH + G_WIDTH
AB0 = G_WIDTH
PROJ_WIDTH = G_WIDTH + AB_WIDTH
N_AB_TILES = AB_WIDTH // PROJ_TN
N_G_TILES = G_WIDTH // PROJ_TN
G_SRC = IN_WIDTH - G_WIDTH
C_PAD = LANES - B_LOWRANK
C_SRC = AB_WIDTH - C_PAD
C_WIN = LANES + 2 * C_WIDTH
LR_LANE = C_PAD
assert AB_WIDTH % PROJ_TN == 0 and G_WIDTH % PROJ_TN == 0
assert C_SRC % BF16_ROWS == 0 and G_SRC % BF16_ROWS == 0


def _rpe_thresholds():
    max_exact = RPE_BUCKETS // 2

    def bucket(n):
        if n < max_exact:
            return n
        v = math.log(n / max_exact) / math.log(RPE_MAX_DIST / max_exact) * (RPE_BUCKETS - max_exact)
        return min(max_exact + int(v), RPE_BUCKETS - 1)

    table = [bucket(n) for n in range(4 * RPE_MAX_DIST)]
    return [min(n for n in range(len(table)) if table[n] >= b)
            for b in range(max_exact + 1, RPE_BUCKETS)]


RPE_THRESHOLDS = _rpe_thresholds()
assert RPE_THRESHOLDS[-1] <= A_BLOCK


VEC_FIELDS = (("norm1_g", D_MODEL), ("norm2_g", D_MODEL), ("sg_ln_g", C_WIDTH), ("sg_ln_b", C_WIDTH),
              ("gla_lr_b", B_KWIDTH), ("gla_out_g", B_DV), ("q_norm_g", HEAD_DIM), ("k_norm_g", HEAD_DIM))
VEC_OFFSET = {}
VEC_WIDTH = 0
for _name, _width in VEC_FIELDS:
    assert VEC_WIDTH % _width == 0
    VEC_OFFSET[_name] = VEC_WIDTH
    VEC_WIDTH += _width


def _vec_spec(name, layer, width=None, block_of=None):
    width = width or dict(VEC_FIELDS)[name]
    first = VEC_OFFSET[name] // width
    if block_of is None:
        return pl.BlockSpec((None, 1, width), lambda *idx: (layer, 0, first))
    return pl.BlockSpec((None, 1, width), lambda *idx: (layer, 0, first + block_of(*idx)))


def _cparams(*sem):
    return pltpu.CompilerParams(dimension_semantics=sem, vmem_limit_bytes=VMEM_LIMIT)


def _rpe_bias_kernel(*refs):
    tbl_ref, lrw_ref = refs[:2]
    fields = refs[2:2 + len(VEC_FIELDS)]
    bias_ref, vec_ref, lrw_pad_ref = refs[2 + len(VEC_FIELDS):]
    h = pl.program_id(0)

    @pl.when(h == 0)
    def _():
        for (name, width), src in zip(VEC_FIELDS, fields):
            for l in range(src.shape[0]):
                vec_ref[l, :, VEC_OFFSET[name]:VEC_OFFSET[name] + width] = src[l:l + 1, :]
        lrw_pad_ref[...] = jnp.zeros(lrw_pad_ref.shape, lrw_pad_ref.dtype)
        lrw_pad_ref[:, LR_LANE:LR_LANE + B_LOWRANK, :] = lrw_ref[...]

    t = lax.broadcasted_iota(jnp.int32, (A_BLOCK, 2 * A_BLOCK), 0)
    c = lax.broadcasted_iota(jnp.int32, (A_BLOCK, 2 * A_BLOCK), 1)
    n = jnp.maximum(t - c + A_BLOCK, 0)
    max_exact = RPE_BUCKETS // 2
    large = jnp.full(n.shape, max_exact, jnp.int32)
    for thr in RPE_THRESHOLDS:
        large = large + (n >= thr).astype(jnp.int32)
    bucket = jnp.where(n < max_exact, n, large)
    bias = jnp.zeros(n.shape, F32)
    for b in range(RPE_BUCKETS):
        bias = jnp.where(bucket == b, tbl_ref[b, h], bias)
    bias = (bias - tbl_ref[RPE_BUCKETS - 1, h]) * LOG2E
    bias_ref[0] = jnp.where(c - A_BLOCK > t, NEG_INF, bias)


def rpe_bias(rpe_table, lr_w, fields):
    layers = lr_w.shape[0]
    whole = lambda a: pl.BlockSpec(a.shape, lambda h: (0,) * a.ndim)
    return pl.pallas_call(
        _rpe_bias_kernel,
        grid=(A_HEADS,),
        in_specs=[pl.BlockSpec(memory_space=pltpu.SMEM), whole(lr_w)] + [whole(f) for f in fields],
        out_specs=[pl.BlockSpec((1, A_BLOCK, 2 * A_BLOCK), lambda h: (h, 0, 0)),
                   pl.BlockSpec((layers, 1, VEC_WIDTH), lambda h: (0, 0, 0)),
                   pl.BlockSpec((layers, LANES, B_KWIDTH), lambda h: (0, 0, 0))],
        out_shape=[jax.ShapeDtypeStruct((A_HEADS, A_BLOCK, 2 * A_BLOCK), F32),
                   jax.ShapeDtypeStruct((layers, 1, VEC_WIDTH), F32),
                   jax.ShapeDtypeStruct((layers, LANES, B_KWIDTH), F32)],
        compiler_params=_cparams("arbitrary"),
        name="rpe_bias",
    )(rpe_table, lr_w, *fields)


def _cast_specs(to_cast, layer, n_steps, step_of):
    in_specs, out_specs, shapes = [], [], []
    for w in to_cast:
        _, rows, cols = w.shape
        slice_rows = next(r for r in range(BF16_ROWS, rows + 1, BF16_ROWS)
                          if rows % r == 0 and rows // r <= n_steps)
        n_slices = rows // slice_rows

        def slice_of(*idx, n_slices=n_slices):
            return jnp.minimum(step_of(*idx), n_slices - 1)

        in_specs.append(pl.BlockSpec((None, slice_rows, cols), lambda *idx, f=slice_of: (layer, f(*idx), 0)))
        out_specs.append(pl.BlockSpec((slice_rows, cols), lambda *idx, f=slice_of: (f(*idx), 0)))
        shapes.append(jax.ShapeDtypeStruct((rows, cols), BF16))
    return in_specs, out_specs, shapes


def _cast_all(srcs, dsts):
    for src, dst in zip(srcs, dsts):
        dst[...] = src[...].astype(dst.dtype)


def _proj_kernel(*refs, n_cast):
    xn_ref, w_ref = refs[:2]
    o_ref = refs[2 + n_cast]
    o_ref[...] = lax.dot_general(xn_ref[...], w_ref[...].astype(BF16), (((1,), (1,)), ((), ())),
                                 preferred_element_type=F32).astype(o_ref.dtype)
    _cast_all(refs[2:2 + n_cast], refs[3 + n_cast:])


def _proj_src_row(j):
    u = BF16_ROWS
    return u * jnp.where(j < N_AB_TILES, j * (PROJ_TN // u),
                         G_SRC // u + (j - N_AB_TILES) * (PROJ_TN // u))


def _proj_out_tile(j):
    return jnp.where(j < N_AB_TILES, j + N_G_TILES, j - N_AB_TILES)


def proj(xn, w_t, layer, to_cast, *, tm):
    m, k = xn.shape
    tn = PROJ_TN
    n_tiles = N_AB_TILES + N_G_TILES
    cast_in_specs, cast_out_specs, cast_shapes = _cast_specs(
        to_cast, layer, (m // tm) * n_tiles, lambda i, j: i * n_tiles + j)
    return pl.pallas_call(
        functools.partial(_proj_kernel, n_cast=len(to_cast)),
        grid=(m // tm, n_tiles),
        in_specs=[pl.BlockSpec((tm, k), lambda i, j: (i, 0)),
                  pl.BlockSpec((None, pl.Element(tn), pl.Element(k)),
                               lambda i, j: (layer, _proj_src_row(j), 0))] + cast_in_specs,
        out_specs=[pl.BlockSpec((tm, tn), lambda i, j: (i, _proj_out_tile(j)))] + cast_out_specs,
        out_shape=[jax.ShapeDtypeStruct((m, PROJ_WIDTH), BF16)] + cast_shapes,
        compiler_params=_cparams("parallel", "arbitrary"),
        name="proj",
    )(xn, w_t, *to_cast)


def _head_rms(x, g):
    ms = jnp.mean(x * x, axis=-1, keepdims=True)
    return x * lax.rsqrt(ms + EPS) * g


def _moba_stages(q_ref, k_ref, v_ref, qg_ref, kg_ref, bias_ref, o_ref,
                 qn_ref, qa_ref, ka_ref, va_ref, kmean_ref, s_ref, p_ref):
    seq, hd = q_ref.shape
    nb = seq // A_BLOCK
    rc = BF16_ROWS

    def prologue():
        lane = lax.broadcasted_iota(jnp.int32, (A_BLOCK, hd), 1)
        for j in range(nb):
            rows = slice(j * A_BLOCK, (j + 1) * A_BLOCK)
            qn = _head_rms(q_ref[rows, :].astype(F32), qg_ref[...])
            qn_ref[rows, :] = qn
            qa_ref[rows, :hd] = (qn * (hd ** -0.5 * LOG2E)).astype(BF16)
            qa_ref[rows, hd:] = jnp.zeros((A_BLOCK, hd), BF16)
            kn = _head_rms(k_ref[rows, :].astype(F32), kg_ref[...])
            ka_ref[rows, :hd] = kn.astype(BF16)
            ka_ref[rows, hd:] = jnp.where(lane == j, 1.0, 0.0).astype(BF16)
            va_ref[rows, :hd] = v_ref[rows, :]
            va_ref[rows, hd:] = jnp.ones((A_BLOCK, hd), BF16)
            kmean_ref[j:j + 1, :] = jnp.mean(kn, axis=0, keepdims=True)

    def scores(i):
        rows = slice(i * A_BLOCK, (i + 1) * A_BLOCK)
        nk = (i + 1) * A_BLOCK
        if i > A_TOPK:
            blk = lax.broadcasted_iota(jnp.int32, (nb, A_BLOCK), 0)
            eye = jnp.where(lax.broadcasted_iota(jnp.int32, (nb, hd), 0)
                            == lax.broadcasted_iota(jnp.int32, (nb, hd), 1), 1.0, 0.0)
            g = lax.dot_general(kmean_ref[...], qn_ref[rows, :], (((1,), (1,)), ((), ())),
                                precision=lax.Precision.HIGHEST, preferred_element_type=F32)
            rank = jnp.zeros((nb, A_BLOCK), jnp.int32)
            for jp in range(i):
                gj = g[jp:jp + 1, :]
                beats = (gj > g) | ((gj == g) & (jp < blk))
                rank = rank + beats.astype(jnp.int32)
            neg = jnp.where((blk < i) & (rank >= A_TOPK), NEG_INF, 0.0)
            qa_ref[rows, hd:] = lax.dot_general(neg, eye, (((0,), (0,)), ((), ())),
                                                preferred_element_type=F32).astype(BF16)
        s_ref[i % 2, :, :nk] = lax.dot_general(qa_ref[rows, :], ka_ref[:nk, :], (((1,), (1,)), ((), ())),
                                               preferred_element_type=F32)

    def softmax(i):
        for c in range(A_BLOCK // rc):
            r = slice(c * rc, (c + 1) * rc)
            tiles = []
            for j in range(i + 1):
                sj = s_ref[i % 2, r, j * A_BLOCK:(j + 1) * A_BLOCK]
                if j == i:
                    sj = sj + bias_ref[0, r, A_BLOCK:]
                elif j == i - 1:
                    sj = sj + bias_ref[0, r, :A_BLOCK]
                tiles.append(sj)
            mt = tiles[0]
            for t in tiles[1:]:
                mt = jnp.maximum(mt, t)
            m = jnp.max(mt, axis=-1, keepdims=True)
            for j, t in enumerate(tiles):
                p_ref[i % 2, r, j * A_BLOCK:(j + 1) * A_BLOCK] = jnp.exp2(t - m).astype(BF16)

    def weighted_sum(i):
        rows = slice(i * A_BLOCK, (i + 1) * A_BLOCK)
        nk = (i + 1) * A_BLOCK
        o = jnp.dot(p_ref[i % 2, :, :nk], va_ref[:nk, :], preferred_element_type=F32)
        o_ref[rows, :] = (o[:, :hd] / o[:, hd:]).astype(o_ref.dtype)

    stages = [prologue, functools.partial(scores, 0)]
    for i in range(nb):
        if i + 1 < nb:
            stages.append(functools.partial(scores, i + 1))
        stages += [functools.partial(softmax, i), functools.partial(weighted_sum, i)]
    return stages


def _gla_stages(q_ref, k_ref, v_ref, r_ref, lr_ref, lrw_ref, lrb_ref, og_ref, o_ref,
                hml_ref, bc_ref, qd_ref, ki_ref, ke_ref, dec_ref, oi_ref, ss_ref, sb_ref):
    seq = q_ref.shape[0]
    ck = B_CHUNK
    nc = seq // ck
    grp = 4 * ck
    ng = seq // grp
    cpg = grp // ck
    qscale = B_DK ** -0.5
    nt = (((1,), (1,)), ((), ()))
    tn = (((0,), (0,)), ((), ()))

    def same_chunk_causal():
        ti = lax.broadcasted_iota(jnp.int32, (grp, grp), 0)
        si = lax.broadcasted_iota(jnp.int32, (grp, grp), 1)
        return (si <= ti) & (si // ck == ti // ck)

    def log_decay(g):
        rows = slice(g * grp, (g + 1) * grp)
        z = jnp.dot(lr_ref[rows, :], lrw_ref[...].astype(BF16), preferred_element_type=F32) + lrb_ref[...]
        la = (jnp.minimum(z, 0.0) - jnp.log(1.0 + jnp.exp(-jnp.abs(z)))) * (1.0 / B_GATE_NORM)
        hi = la.astype(BF16)
        r1 = la - hi.astype(F32)
        mid = r1.astype(BF16)
        hml_ref[rows, :B_DK] = hi
        hml_ref[rows, B_DK:2 * B_DK] = mid
        hml_ref[rows, 2 * B_DK:] = (r1 - mid.astype(F32)).astype(BF16)

    def cumsum(g):
        rows = slice(g * grp, (g + 1) * grp)
        tril_grp = jnp.where(same_chunk_causal(), 1.0, 0.0).astype(BF16)
        parts = jnp.dot(tril_grp, hml_ref[rows, :], preferred_element_type=F32)
        bc_ref[rows, :] = parts[:, :B_DK] + parts[:, B_DK:2 * B_DK] + parts[:, 2 * B_DK:]

    def decayed_qk(g):
        for c in range(g * cpg, (g + 1) * cpg):
            rows = slice(c * ck, (c + 1) * ck)
            bc = bc_ref[rows, :]
            bl = bc_ref[(c + 1) * ck - 1:(c + 1) * ck, :]
            q = q_ref[rows, :].astype(F32)
            k = k_ref[rows, :].astype(F32)
            qd_ref[rows, :] = (q * jnp.exp(bc) * qscale).astype(BF16)
            ki_ref[rows, :] = (k * jnp.exp(-bc)).astype(BF16)
            ke_ref[rows, :] = (k * jnp.exp(bl - bc)).astype(BF16)
            dec_ref[c:c + 1, :] = jnp.exp(bl)

    def intra(g):
        rows = slice(g * grp, (g + 1) * grp)
        a = lax.dot_general(qd_ref[rows, :], ki_ref[rows, :], nt, preferred_element_type=F32)
        a = jnp.where(same_chunk_causal(), a, 0.0).astype(BF16)
        oi_ref[rows, :] = jnp.dot(a, v_ref[rows, :], preferred_element_type=F32)

    def state_terms(g):
        for c in range(g * cpg, (g + 1) * cpg):
            rows = slice(c * ck, (c + 1) * ck)
            ss_ref[c] = lax.dot_general(v_ref[rows, :], ke_ref[rows, :], tn, preferred_element_type=F32)

    def scan():
        slab = B_DV // 4
        st = [jnp.zeros((slab, B_DK), F32) for _ in range(4)]
        for c in range(nc):
            dec = dec_ref[c:c + 1, :]
            for s in range(4):
                r = slice(s * slab, (s + 1) * slab)
                sb_ref[c, r, :] = st[s].astype(BF16)
                st[s] = st[s] * dec + ss_ref[c, r, :]

    def outputs(g):
        for c in range(g * cpg, (g + 1) * cpg):
            rows = slice(c * ck, (c + 1) * ck)
            o = oi_ref[rows, :] + lax.dot_general(qd_ref[rows, :], sb_ref[c], nt,
                                                  preferred_element_type=F32)
            ms = jnp.mean(o * o, axis=-1, keepdims=True)
            y = o * lax.rsqrt(ms + EPS) * og_ref[...]
            r = r_ref[rows, :]
            o_ref[rows, :] = y.astype(o_ref.dtype) * (r * jax.nn.sigmoid(r))

    groups = [[functools.partial(fn, g) for g in range(ng)]
              for fn in (log_decay, cumsum, decayed_qk, intra, state_terms)]
    return groups + [[scan], [functools.partial(outputs, g) for g in range(ng)]]


N_MOBA_IN, N_MOBA_SCRATCH = 6, 7
N_GLA_IN = 8


def _mixer_ab_kernel(*refs, n_cast):
    n_in = N_MOBA_IN + N_GLA_IN
    ins, cast_in = refs[:n_in], refs[n_in:n_in + n_cast]
    oa_ref, ob_ref = refs[n_in + n_cast:n_in + n_cast + 2]
    cast_out = refs[n_in + n_cast + 2:n_in + 2 * n_cast + 2]
    scratch = refs[n_in + 2 * n_cast + 2:]
    moba_stages = _moba_stages(*ins[:N_MOBA_IN], oa_ref, *scratch[:N_MOBA_SCRATCH])
    gla_steps = [fn for group in _gla_stages(*ins[N_MOBA_IN:], ob_ref, *scratch[N_MOBA_SCRATCH:])
                 for fn in group]
    per_slot = -(-len(gla_steps) // len(moba_stages))
    for n, stage in enumerate(moba_stages):
        stage()
        for fn in gla_steps[n * per_slot:(n + 1) * per_slot]:
            fn()
    _cast_all(cast_in, cast_out)


def mixer_ab(proj_out, lr_blk, vec, bias, lr_w, layer, to_cast, *, bsz, seq):
    hd = HEAD_DIM
    assert A_HEADS == B_HEADS
    qa0 = AB0 // hd
    qb0 = (AB0 + 3 * A_WIDTH) // B_DK
    kb0 = qb0 + B_HEADS
    vb0 = (AB0 + 3 * A_WIDTH + 2 * B_KWIDTH) // B_DV
    rb0 = vb0 + B_HEADS
    nc = seq // B_CHUNK
    cast_in_specs, cast_out_specs, cast_shapes = _cast_specs(
        to_cast, layer, bsz * A_HEADS, lambda b, h: b * A_HEADS + h)
    return pl.pallas_call(
        functools.partial(_mixer_ab_kernel, n_cast=len(to_cast)),
        grid=(bsz, A_HEADS),
        in_specs=[pl.BlockSpec((seq, hd), lambda b, h: (b, qa0 + h)),
                  pl.BlockSpec((seq, hd), lambda b, h: (b, qa0 + A_HEADS + h)),
                  pl.BlockSpec((seq, hd), lambda b, h: (b, qa0 + 2 * A_HEADS + h)),
                  _vec_spec("q_norm_g", layer),
                  _vec_spec("k_norm_g", layer),
                  pl.BlockSpec((1, A_BLOCK, 2 * A_BLOCK), lambda b, h: (h, 0, 0)),
                  pl.BlockSpec((seq, B_DK), lambda b, h: (b, qb0 + h)),
                  pl.BlockSpec((seq, B_DK), lambda b, h: (b, kb0 + h)),
                  pl.BlockSpec((seq, B_DV), lambda b, h: (b, vb0 + h)),
                  pl.BlockSpec((seq, B_DV), lambda b, h: (b, rb0 + h)),
                  pl.BlockSpec((seq, LANES), lambda b, h: (b, 0)),
                  pl.BlockSpec((None, LANES, B_DK), lambda b, h: (layer, 0, h)),
                  _vec_spec("gla_lr_b", layer, B_DK, lambda b, h: h),
                  _vec_spec("gla_out_g", layer)] + cast_in_specs,
        out_specs=[pl.BlockSpec((seq, hd), lambda b, h: (b, h)),
                   pl.BlockSpec((seq, B_DV), lambda b, h: (b, h))] + cast_out_specs,
        out_shape=[jax.ShapeDtypeStruct((bsz * seq, A_WIDTH), BF16),
                   jax.ShapeDtypeStruct((bsz * seq, B_VWIDTH), BF16)] + cast_shapes,
        scratch_shapes=[pltpu.VMEM((seq, hd), F32),
                        pltpu.VMEM((seq, 2 * hd), BF16),
                        pltpu.VMEM((seq, 2 * hd), BF16),
                        pltpu.VMEM((seq, 2 * hd), BF16),
                        pltpu.VMEM((seq // A_BLOCK, hd), F32),
                        pltpu.VMEM((2, A_BLOCK, seq), F32),
                        pltpu.VMEM((2, A_BLOCK, seq), BF16),
                        pltpu.VMEM((seq, 3 * B_DK), BF16),
                        pltpu.VMEM((seq, B_DK), F32),
                        pltpu.VMEM((seq, B_DK), BF16),
                        pltpu.VMEM((seq, B_DK), BF16),
                        pltpu.VMEM((seq, B_DK), BF16),
                        pltpu.VMEM((nc, B_DK), F32),
                        pltpu.VMEM((seq, B_DV), F32),
                        pltpu.VMEM((nc, B_DV, B_DK), F32),
                        pltpu.VMEM((nc, B_DV, B_DK), BF16)],
        compiler_params=_cparams("parallel", "parallel"),
        name="mixer_ab",
    )(proj_out, proj_out, proj_out, vec, vec, bias,
      proj_out, proj_out, proj_out, proj_out, lr_blk, lr_w, vec, vec, *to_cast)


def _gelu(x):
    c = math.sqrt(2.0 / math.pi)
    return 0.5 * x * (1.0 + jnp.tanh(c * (x + 0.044715 * (x * x * x))))


SGU_SUB = 2 * C_CHUNK


def _sgu_kernel(*refs, n_cast):
    x_ref, g1_ref, wsrc_ref, lg_ref, lb_ref, w_ref, b_ref = refs[:7]
    o_ref, lr_ref, xn_ref = refs[7 + n_cast:10 + n_cast]
    wc_ref, luv_ref = refs[-2:]

    @pl.when(pl.program_id(0) == 0)
    def _():
        wc_ref[...] = wsrc_ref[...].astype(BF16)

    _cast_all(refs[7:7 + n_cast], refs[10 + n_cast:10 + 2 * n_cast])
    ti = lax.broadcasted_iota(jnp.int32, (C_CHUNK, C_CHUNK), 0)
    si = lax.broadcasted_iota(jnp.int32, (C_CHUNK, C_CHUNK), 1)
    causal = si <= ti
    ws = [jnp.where(causal, w_ref[g], 0.0).astype(BF16) for g in range(C_GROUPS)]
    n_sub = x_ref.shape[0] // SGU_SUB

    def project(s):
        rs = slice(s * SGU_SUB, (s + 1) * SGU_SUB)
        x = x_ref[rs, :]
        ms = jnp.mean(x * x, axis=-1, keepdims=True)
        xn = (x * lax.rsqrt(ms + EPS) * g1_ref[...]).astype(BF16)
        xn_ref[rs, :] = xn
        luv = lax.dot_general(xn, wc_ref[...], (((1,), (1,)), ((), ())), preferred_element_type=F32)
        luv_ref[s % 2] = luv
        lr_ref[rs, :] = luv[:, :LANES].astype(lr_ref.dtype)

    def gate(s):
        for n in range(SGU_SUB // C_CHUNK):
            rl = slice(n * C_CHUNK, (n + 1) * C_CHUNK)
            ro = slice(s * SGU_SUB + n * C_CHUNK, s * SGU_SUB + (n + 1) * C_CHUNK)
            v = _gelu(luv_ref[s % 2, rl, LANES + C_WIDTH:])
            mu = jnp.mean(v, axis=-1, keepdims=True)
            vc = v - mu
            vn = vc * lax.rsqrt(jnp.mean(vc * vc, axis=-1, keepdims=True) + EPS)
            vn = (vn * lg_ref[...] + lb_ref[...]).astype(BF16)
            u = _gelu(luv_ref[s % 2, rl, LANES:LANES + C_WIDTH])
            for g in range(C_GROUPS):
                cols = slice(g * C_GROUP_DIM, (g + 1) * C_GROUP_DIM)
                mixed = jnp.dot(ws[g], vn[:, cols], preferred_element_type=F32) + b_ref[g]
                o_ref[ro, cols] = (u[:, cols] * mixed).astype(o_ref.dtype)

    project(0)
    for s in range(n_sub):
        if s + 1 < n_sub:
            project(s + 1)
        gate(s)


def sgu(x, vec, w_t, w_s, b_s, layer, to_cast, *, tm):
    m, d = x.shape
    cast_in_specs, cast_out_specs, cast_shapes = _cast_specs(to_cast, layer, m // tm, lambda i: i)
    return pl.pallas_call(
        functools.partial(_sgu_kernel, n_cast=len(to_cast)),
        grid=(m // tm,),
        in_specs=[pl.BlockSpec((tm, d), lambda i: (i, 0)),
                  _vec_spec("norm1_g", layer),
                  pl.BlockSpec((None, pl.Element(C_WIN), pl.Element(d)), lambda i: (layer, C_SRC, 0),
                               pipeline_mode=pl.Buffered(1)),
                  _vec_spec("sg_ln_g", layer),
                  _vec_spec("sg_ln_b", layer),
                  pl.BlockSpec((None, C_GROUPS, C_CHUNK, C_CHUNK), lambda i: (layer, 0, 0, 0)),
                  pl.BlockSpec((None, C_GROUPS, C_CHUNK, 1), lambda i: (layer, 0, 0, 0))] + cast_in_specs,
        out_specs=[pl.BlockSpec((tm, C_WIDTH), lambda i: (i, 0)),
                   pl.BlockSpec((tm, LANES), lambda i: (i, 0)),
                   pl.BlockSpec((tm, d), lambda i: (i, 0))] + cast_out_specs,
        out_shape=[jax.ShapeDtypeStruct((m, C_WIDTH), BF16),
                   jax.ShapeDtypeStruct((m, LANES), BF16),
                   jax.ShapeDtypeStruct((m, d), BF16)] + cast_shapes,
        scratch_shapes=[pltpu.VMEM((C_WIN, d), BF16),
                        pltpu.VMEM((2, SGU_SUB, C_WIN), F32)],
        compiler_params=_cparams("arbitrary"),
        name="sgu",
    )(x, vec, w_t, vec, vec, w_s, b_s, *to_cast)


def _merge_kernel(ya_ref, yb_ref, yc_ref, ga_ref, gb_ref, gc_ref, x_ref,
                  wa_ref, wb_ref, wc_ref, wo_ref, g2_ref, o_ref, on_ref):
    def gated(g_ref, y_ref, w_ref):
        return (jax.nn.sigmoid(g_ref[...].astype(F32))
                * jnp.dot(y_ref[...], w_ref[...], preferred_element_type=F32))

    merged = gated(ga_ref, ya_ref, wa_ref) + gated(gb_ref, yb_ref, wb_ref) + gated(gc_ref, yc_ref, wc_ref)
    h = x_ref[...] + jnp.dot(merged.astype(BF16), wo_ref[...], preferred_element_type=F32)
    o_ref[...] = h
    ms = jnp.mean(h * h, axis=-1, keepdims=True)
    on_ref[...] = (h * lax.rsqrt(ms + EPS) * g2_ref[...]).astype(on_ref.dtype)


def merge(ya, yb, yc, proj_out, x, wa, wb, wc, wo, vec, layer, *, tm):
    m, d = x.shape
    resident = functools.partial(pl.BlockSpec, pipeline_mode=pl.Buffered(1))
    return pl.pallas_call(
        _merge_kernel,
        grid=(m // tm,),
        in_specs=[pl.BlockSpec((tm, A_WIDTH), lambda i: (i, 0)),
                  pl.BlockSpec((tm, B_VWIDTH), lambda i: (i, 0)),
                  pl.BlockSpec((tm, C_WIDTH), lambda i: (i, 0)),
                  pl.BlockSpec((tm, d), lambda i: (i, 0)),
                  pl.BlockSpec((tm, d), lambda i: (i, 1)),
                  pl.BlockSpec((tm, d), lambda i: (i, 2)),
                  pl.BlockSpec((tm, d), lambda i: (i, 0)),
                  resident((A_WIDTH, d), lambda i: (0, 0)),
                  resident((B_VWIDTH, d), lambda i: (0, 0)),
                  resident((C_WIDTH, d), lambda i: (0, 0)),
                  resident((d, d), lambda i: (0, 0)),
                  _vec_spec("norm2_g", layer)],
        out_specs=[pl.BlockSpec((tm, d), lambda i: (i, 0)),
                   pl.BlockSpec((tm, d), lambda i: (i, 0))],
        out_shape=[jax.ShapeDtypeStruct((m, d), F32),
                   jax.ShapeDtypeStruct((m, d), BF16)],
        compiler_params=_cparams("parallel"),
        name="merge",
    )(ya, yb, yc, proj_out, proj_out, proj_out, x, wa, wb, wc, wo, vec)


def _mlp_kernel(h_hbm, hn_ref, w1_ref, w2_ref, o_ref, h_ref, h_sem):
    i, j = pl.program_id(0), pl.program_id(1)
    last = pl.num_programs(1) - 1
    tm = h_ref.shape[0]
    h_copy = pltpu.make_async_copy(h_hbm.at[pl.ds(i * tm, tm), :], h_ref, h_sem)

    def update():
        a = jnp.dot(hn_ref[...], w1_ref[...], preferred_element_type=F32)
        a = jnp.square(jnp.maximum(a, 0.0)).astype(BF16)
        return jnp.dot(a, w2_ref[...], preferred_element_type=F32)

    @pl.when(j == 0)
    def _():
        h_copy.start()
        o_ref[...] = update()

    @pl.when((j > 0) & (j < last))
    def _():
        o_ref[...] += update()

    @pl.when(j == last)
    def _():
        h_copy.wait()
        o_ref[...] = (o_ref[...] + h_ref[...]) + update()


def mlp(h, hn, w1, w2, *, tm, tf):
    m, d = h.shape
    f = w1.shape[1]
    return pl.pallas_call(
        _mlp_kernel,
        grid=(m // tm, f // tf),
        in_specs=[pl.BlockSpec(memory_space=pl.ANY),
                  pl.BlockSpec((tm, d), lambda i, j: (i, 0)),
                  pl.BlockSpec((d, tf), lambda i, j: (0, j)),
                  pl.BlockSpec((tf, d), lambda i, j: (j, 0))],
        out_specs=pl.BlockSpec((tm, d), lambda i, j: (i, 0)),
        out_shape=jax.ShapeDtypeStruct((m, d), F32),
        scratch_shapes=[pltpu.VMEM((tm, d), F32),
                        pltpu.SemaphoreType.DMA(())],
        compiler_params=_cparams("parallel", "arbitrary"),
        name="mlp",
    )(h, hn, w1, w2)


def kernel(x, rpe_table, norm1_g, w_in, q_norm_g, k_norm_g, gla_lr_w, gla_lr_b, gla_out_g,
           sg_ln_g, sg_ln_b, sg_w, sg_b, w_br_a, w_br_b, w_br_c, w_o, norm2_g, w_ff1, w_ff2):
    bsz, seq, d = x.shape
    assert seq % A_BLOCK == 0 and seq % B_CHUNK == 0 and seq % C_CHUNK == 0
    assert w_in.shape[2] == IN_WIDTH
    h = x.reshape(bsz * seq, d)
    fields = dict(norm1_g=norm1_g, norm2_g=norm2_g, sg_ln_g=sg_ln_g, sg_ln_b=sg_ln_b, gla_lr_b=gla_lr_b,
                  gla_out_g=gla_out_g, q_norm_g=q_norm_g, k_norm_g=k_norm_g)
    bias, vec, lr_w = rpe_bias(rpe_table, gla_lr_w, [fields[name] for name, _ in VEC_FIELDS])
    w_t = jnp.swapaxes(w_in, 1, 2)
    sg_b_col = sg_b[..., None]
    for l in range(DEPTH):
        y_c, lr_blk, xn, wa, wb, wc, wo = sgu(h, vec, w_t, sg_w, sg_b_col, l,
                                              (w_br_a, w_br_b, w_br_c, w_o), tm=SGU_TM)
        p, w1 = proj(xn, w_t, l, (w_ff1,), tm=PROJ_TM)
        y_a, y_b, w2 = mixer_ab(p, lr_blk, vec, bias, lr_w, l, (w_ff2,), bsz=bsz, seq=seq)
        h, hn = merge(y_a, y_b, y_c, p, h, wa, wb, wc, wo, vec, l, tm=MERGE_TM)
        h = mlp(h, hn, w1, w2, tm=MLP_TM, tf=MLP_TF)
    return h.reshape(bsz, seq, d)
```

```python
import functools
import math

import jax
import jax.numpy as jnp
from jax import lax
from jax.experimental import pallas as pl
from jax.experimental.pallas import tpu as pltpu

D_MODEL = 2048
DEPTH = 2
HEAD_DIM = 128
A_HEADS = 4
A_WIDTH = A_HEADS * HEAD_DIM
A_BLOCK = 256
A_TOPK = 3
RPE_BUCKETS = 32
RPE_MAX_DIST = 128
B_HEADS = 4
B_DK = 128
B_DV = 256
B_KWIDTH = B_HEADS * B_DK
B_VWIDTH = B_HEADS * B_DV
B_LOWRANK = 16
B_GATE_NORM = 16.0
B_CHUNK = 64
C_GROUPS = 4
C_GROUP_DIM = 128
C_WIDTH = C_GROUPS * C_GROUP_DIM
C_CHUNK = 128
D_FF = 4 * D_MODEL
EPS = 1e-6
NEG_INF = -1e30
LOG2E = 1.4426950408889634

LANES = 128
BF16_ROWS = 16
VMEM_LIMIT = 56 * 1024 * 1024

F32 = jnp.float32
BF16 = jnp.bfloat16

PROJ_TM = 1024
MLP_TM = 1024
MLP_TF = 1024
MERGE_TM = 512
SGU_TM = 1024

PROJ_TN = 1536
AB_WIDTH = 3 * A_WIDTH + 2 * B_KWIDTH + 2 * B_VWIDTH
G_WIDTH = 3 * D_MODEL
IN_WIDTH = AB_WIDTH + B_LOWRANK + 2 * C_WIDTH + G_WIDTH
AB0 = G_WIDTH
PROJ_WIDTH = G_WIDTH + AB_WIDTH
N_AB_TILES = AB_WIDTH // PROJ_TN
N_G_TILES = G_WIDTH // PROJ_TN
G_SRC = IN_WIDTH - G_WIDTH
C_PAD = LANES - B_LOWRANK
C_SRC = AB_WIDTH - C_PAD
C_WIN = LANES + 2 * C_WIDTH
LR_LANE = C_PAD
assert AB_WIDTH % PROJ_TN == 0 and G_WIDTH % PROJ_TN == 0
assert C_SRC % BF16_ROWS == 0 and G_SRC % BF16_ROWS == 0


def _rpe_thresholds():
    max_exact = RPE_BUCKETS // 2

    def bucket(n):
        if n < max_exact:
            return n
        v = math.log(n / max_exact) / math.log(RPE_MAX_DIST / max_exact) * (RPE_BUCKETS - max_exact)
        return min(max_exact + int(v), RPE_BUCKETS - 1)

    table = [bucket(n) for n in range(4 * RPE_MAX_DIST)]
    return [min(n for n in range(len(table)) if table[n] >= b)
            for b in range(max_exact + 1, RPE_BUCKETS)]


RPE_THRESHOLDS = _rpe_thresholds()
assert RPE_THRESHOLDS[-1] <= A_BLOCK


VEC_FIELDS = (("norm1_g", D_MODEL), ("norm2_g", D_MODEL), ("sg_ln_g", C_WIDTH), ("sg_ln_b", C_WIDTH),
              ("gla_lr_b", B_KWIDTH), ("gla_out_g", B_DV), ("q_norm_g", HEAD_DIM), ("k_norm_g", HEAD_DIM))
VEC_OFFSET = {}
VEC_WIDTH = 0
for _name, _width in VEC_FIELDS:
    assert VEC_WIDTH % _width == 0
    VEC_OFFSET[_name] = VEC_WIDTH
    VEC_WIDTH += _width


def _vec_spec(name, layer, width=None, block_of=None):
    width = width or dict(VEC_FIELDS)[name]
    first = VEC_OFFSET[name] // width
    if block_of is None:
        return pl.BlockSpec((None, 1, width), lambda *idx: (layer, 0, first))
    return pl.BlockSpec((None, 1, width), lambda *idx: (layer, 0, first + block_of(*idx)))


def _cparams(*sem):
    return pltpu.CompilerParams(dimension_semantics=sem, vmem_limit_bytes=VMEM_LIMIT)


def _rpe_bias_kernel(*refs):
    tbl_ref, lrw_ref = refs[:2]
    fields = refs[2:2 + len(VEC_FIELDS)]
    bias_ref, vec_ref, lrw_pad_ref = refs[2 + len(VEC_FIELDS):]
    h = pl.program_id(0)

    @pl.when(h == 0)
    def _():
        for (name, width), src in zip(VEC_FIELDS, fields):
            for l in range(src.shape[0]):
                vec_ref[l, :, VEC_OFFSET[name]:VEC_OFFSET[name] + width] = src[l:l + 1, :]
        lrw_pad_ref[...] = jnp.zeros(lrw_pad_ref.shape, lrw_pad_ref.dtype)
        lrw_pad_ref[:, LR_LANE:LR_LANE + B_LOWRANK, :] = lrw_ref[...]

    t = lax.broadcasted_iota(jnp.int32, (A_BLOCK, 2 * A_BLOCK), 0)
    c = lax.broadcasted_iota(jnp.int32, (A_BLOCK, 2 * A_BLOCK), 1)
    n = jnp.maximum(t - c + A_BLOCK, 0)
    max_exact = RPE_BUCKETS // 2
    large = jnp.full(n.shape, max_exact, jnp.int32)
    for thr in RPE_THRESHOLDS:
        large = large + (n >= thr).astype(jnp.int32)
    bucket = jnp.where(n < max_exact, n, large)
    bias = jnp.zeros(n.shape, F32)
    for b in range(RPE_BUCKETS):
        bias = jnp.where(bucket == b, tbl_ref[b, h], bias)
    bias = (bias - tbl_ref[RPE_BUCKETS - 1, h]) * LOG2E
    bias_ref[0] = jnp.where(c - A_BLOCK > t, NEG_INF, bias)


def rpe_bias(rpe_table, lr_w, fields):
    layers = lr_w.shape[0]
    whole = lambda a: pl.BlockSpec(a.shape, lambda h: (0,) * a.ndim)
    return pl.pallas_call(
        _rpe_bias_kernel,
        grid=(A_HEADS,),
        in_specs=[pl.BlockSpec(memory_space=pltpu.SMEM), whole(lr_w)] + [whole(f) for f in fields],
        out_specs=[pl.BlockSpec((1, A_BLOCK, 2 * A_BLOCK), lambda h: (h, 0, 0)),
                   pl.BlockSpec((layers, 1, VEC_WIDTH), lambda h: (0, 0, 0)),
                   pl.BlockSpec((layers, LANES, B_KWIDTH), lambda h: (0, 0, 0))],
        out_shape=[jax.ShapeDtypeStruct((A_HEADS, A_BLOCK, 2 * A_BLOCK), F32),
                   jax.ShapeDtypeStruct((layers, 1, VEC_WIDTH), F32),
                   jax.ShapeDtypeStruct((layers, LANES, B_KWIDTH), F32)],
        compiler_params=_cparams("arbitrary"),
        name="rpe_bias",
    )(rpe_table, lr_w, *fields)


def _cast_specs(to_cast, layer, n_steps, step_of):
    in_specs, out_specs, shapes = [], [], []
    for w in to_cast:
        _, rows, cols = w.shape
        slice_rows = next(r for r in range(BF16_ROWS, rows + 1, BF16_ROWS)
                          if rows % r == 0 and rows // r <= n_steps)
        n_slices = rows // slice_rows

        def slice_of(*idx, n_slices=n_slices):
            return jnp.minimum(step_of(*idx), n_slices - 1)

        in_specs.append(pl.BlockSpec((None, slice_rows, cols), lambda *idx, f=slice_of: (layer, f(*idx), 0)))
        out_specs.append(pl.BlockSpec((slice_rows, cols), lambda *idx, f=slice_of: (f(*idx), 0)))
        shapes.append(jax.ShapeDtypeStruct((rows, cols), BF16))
    return in_specs, out_specs, shapes


def _cast_all(srcs, dsts):
    for src, dst in zip(srcs, dsts):
        dst[...] = src[...].astype(dst.dtype)


def _proj_kernel(*refs, n_cast):
    xn_ref, w_ref = refs[:2]
    o_ref = refs[2 + n_cast]
    o_ref[...] = lax.dot_general(xn_ref[...], w_ref[...].astype(BF16), (((1,), (1,)), ((), ())),
                                 preferred_element_type=F32).astype(o_ref.dtype)
    _cast_all(refs[2:2 + n_cast], refs[3 + n_cast:])


def _proj_src_row(j):
    u = BF16_ROWS
    return u * jnp.where(j < N_AB_TILES, j * (PROJ_TN // u),
                         G_SRC // u + (j - N_AB_TILES) * (PROJ_TN // u))


def _proj_out_tile(j):
    return jnp.where(j < N_AB_TILES, j + N_G_TILES, j - N_AB_TILES)


def proj(xn, w_t, layer, to_cast, *, tm):
    m, k = xn.shape
    tn = PROJ_TN
    n_tiles = N_AB_TILES + N_G_TILES
    cast_in_specs, cast_out_specs, cast_shapes = _cast_specs(
        to_cast, layer, (m // tm) * n_tiles, lambda i, j: i * n_tiles + j)
    return pl.pallas_call(
        functools.partial(_proj_kernel, n_cast=len(to_cast)),
        grid=(m // tm, n_tiles),
        in_specs=[pl.BlockSpec((tm, k), lambda i, j: (i, 0)),
                  pl.BlockSpec((None, pl.Element(tn), pl.Element(k)),
                               lambda i, j: (layer, _proj_src_row(j), 0))] + cast_in_specs,
        out_specs=[pl.BlockSpec((tm, tn), lambda i, j: (i, _proj_out_tile(j)))] + cast_out_specs,
        out_shape=[jax.ShapeDtypeStruct((m, PROJ_WIDTH), BF16)] + cast_shapes,
        compiler_params=_cparams("parallel", "arbitrary"),
        name="proj",
    )(xn, w_t, *to_cast)


def _head_rms(x, g):
    ms = jnp.mean(x * x, axis=-1, keepdims=True)
    return x * lax.rsqrt(ms + EPS) * g


def _moba_stages(q_ref, k_ref, v_ref, qg_ref, kg_ref, bias_ref, o_ref,
                 qn_ref, qa_ref, ka_ref, va_ref, kmean_ref, s_ref, p_ref):
    seq, hd = q_ref.shape
    nb = seq // A_BLOCK
    rc = BF16_ROWS

    def prologue():
        lane = lax.broadcasted_iota(jnp.int32, (A_BLOCK, hd), 1)
        for j in range(nb):
            rows = slice(j * A_BLOCK, (j + 1) * A_BLOCK)
            qn = _head_rms(q_ref[rows, :].astype(F32), qg_ref[...])
            qn_ref[rows, :] = qn
            qa_ref[rows, :hd] = (qn * (hd ** -0.5 * LOG2E)).astype(BF16)
            qa_ref[rows, hd:] = jnp.zeros((A_BLOCK, hd), BF16)
            kn = _head_rms(k_ref[rows, :].astype(F32), kg_ref[...])
            ka_ref[rows, :hd] = kn.astype(BF16)
            ka_ref[rows, hd:] = jnp.where(lane == j, 1.0, 0.0).astype(BF16)
            va_ref[rows, :hd] = v_ref[rows, :]
            va_ref[rows, hd:] = jnp.ones((A_BLOCK, hd), BF16)
            kmean_ref[j:j + 1, :] = jnp.mean(kn, axis=0, keepdims=True)

    def scores(i):
        rows = slice(i * A_BLOCK, (i + 1) * A_BLOCK)
        nk = (i + 1) * A_BLOCK
        if i > A_TOPK:
            blk = lax.broadcasted_iota(jnp.int32, (nb, A_BLOCK), 0)
            eye = jnp.where(lax.broadcasted_iota(jnp.int32, (nb, hd), 0)
                            == lax.broadcasted_iota(jnp.int32, (nb, hd), 1), 1.0, 0.0)
            g = lax.dot_general(kmean_ref[...], qn_ref[rows, :], (((1,), (1,)), ((), ())),
                                precision=lax.Precision.HIGHEST, preferred_element_type=F32)
            rank = jnp.zeros((nb, A_BLOCK), jnp.int32)
            for jp in range(i):
                gj = g[jp:jp + 1, :]
                beats = (gj > g) | ((gj == g) & (jp < blk))
                rank = rank + beats.astype(jnp.int32)
            neg = jnp.where((blk < i) & (rank >= A_TOPK), NEG_INF, 0.0)
            qa_ref[rows, hd:] = lax.dot_general(neg, eye, (((0,), (0,)), ((), ())),
                                                preferred_element_type=F32).astype(BF16)
        s_ref[i % 2, :, :nk] = lax.dot_general(qa_ref[rows, :], ka_ref[:nk, :], (((1,), (1,)), ((), ())),
                                               preferred_element_type=F32)

    def softmax(i):
        for c in range(A_BLOCK // rc):
            r = slice(c * rc, (c + 1) * rc)
            tiles = []
            for j in range(i + 1):
                sj = s_ref[i % 2, r, j * A_BLOCK:(j + 1) * A_BLOCK]
                if j == i:
                    sj = sj + bias_ref[0, r, A_BLOCK:]
                elif j == i - 1:
                    sj = sj + bias_ref[0, r, :A_BLOCK]
                tiles.append(sj)
            mt = tiles[0]
            for t in tiles[1:]:
                mt = jnp.maximum(mt, t)
            m = jnp.max(mt, axis=-1, keepdims=True)
            for j, t in enumerate(tiles):
                p_ref[i % 2, r, j * A_BLOCK:(j + 1) * A_BLOCK] = jnp.exp2(t - m).astype(BF16)

    def weighted_sum(i):
        rows = slice(i * A_BLOCK, (i + 1) * A_BLOCK)
        nk = (i + 1) * A_BLOCK
        o = jnp.dot(p_ref[i % 2, :, :nk], va_ref[:nk, :], preferred_element_type=F32)
        o_ref[rows, :] = (o[:, :hd] / o[:, hd:]).astype(o_ref.dtype)

    stages = [prologue, functools.partial(scores, 0)]
    for i in range(nb):
        if i + 1 < nb:
            stages.append(functools.partial(scores, i + 1))
        stages += [functools.partial(softmax, i), functools.partial(weighted_sum, i)]
    return stages


def _gla_stages(q_ref, k_ref, v_ref, r_ref, lr_ref, lrw_ref, lrb_ref, og_ref, o_ref,
                hml_ref, bc_ref, qd_ref, ki_ref, ke_ref, dec_ref, oi_ref, ss_ref, sb_ref):
    seq = q_ref.shape[0]
    ck = B_CHUNK
    nc = seq // ck
    grp = 4 * ck
    ng = seq // grp
    cpg = grp // ck
    qscale = B_DK ** -0.5
    nt = (((1,), (1,)), ((), ()))
    tn = (((0,), (0,)), ((), ()))

    def same_chunk_causal():
        ti = lax.broadcasted_iota(jnp.int32, (grp, grp), 0)
        si = lax.broadcasted_iota(jnp.int32, (grp, grp), 1)
        return (si <= ti) & (si // ck == ti // ck)

    def log_decay(g):
        rows = slice(g * grp, (g + 1) * grp)
        z = jnp.dot(lr_ref[rows, :], lrw_ref[...].astype(BF16), preferred_element_type=F32) + lrb_ref[...]
        la = (jnp.minimum(z, 0.0) - jnp.log(1.0 + jnp.exp(-jnp.abs(z)))) * (1.0 / B_GATE_NORM)
        hi = la.astype(BF16)
        r1 = la - hi.astype(F32)
        mid = r1.astype(BF16)
        hml_ref[rows, :B_DK] = hi
        hml_ref[rows, B_DK:2 * B_DK] = mid
        hml_ref[rows, 2 * B_DK:] = (r1 - mid.astype(F32)).astype(BF16)

    def cumsum(g):
        rows = slice(g * grp, (g + 1) * grp)
        tril_grp = jnp.where(same_chunk_causal(), 1.0, 0.0).astype(BF16)
        parts = jnp.dot(tril_grp, hml_ref[rows, :], preferred_element_type=F32)
        bc_ref[rows, :] = parts[:, :B_DK] + parts[:, B_DK:2 * B_DK] + parts[:, 2 * B_DK:]

    def decayed_qk(g):
        for c in range(g * cpg, (g + 1) * cpg):
            rows = slice(c * ck, (c + 1) * ck)
            bc = bc_ref[rows, :]
            bl = bc_ref[(c + 1) * ck - 1:(c + 1) * ck, :]
            q = q_ref[rows, :].astype(F32)
            k = k_ref[rows, :].astype(F32)
            qd_ref[rows, :] = (q * jnp.exp(bc) * qscale).astype(BF16)
            ki_ref[rows, :] = (k * jnp.exp(-bc)).astype(BF16)
            ke_ref[rows, :] = (k * jnp.exp(bl - bc)).astype(BF16)
            dec_ref[c:c + 1, :] = jnp.exp(bl)

    def intra(g):
        rows = slice(g * grp, (g + 1) * grp)
        a = lax.dot_general(qd_ref[rows, :], ki_ref[rows, :], nt, preferred_element_type=F32)
        a = jnp.where(same_chunk_causal(), a, 0.0).astype(BF16)
        oi_ref[rows, :] = jnp.dot(a, v_ref[rows, :], preferred_element_type=F32)

    def state_terms(g):
        for c in range(g * cpg, (g + 1) * cpg):
            rows = slice(c * ck, (c + 1) * ck)
            ss_ref[c] = lax.dot_general(v_ref[rows, :], ke_ref[rows, :], tn, preferred_element_type=F32)

    def scan():
        slab = B_DV // 4
        st = [jnp.zeros((slab, B_DK), F32) for _ in range(4)]
        for c in range(nc):
            dec = dec_ref[c:c + 1, :]
            for s in range(4):
                r = slice(s * slab, (s + 1) * slab)
                sb_ref[c, r, :] = st[s].astype(BF16)
                st[s] = st[s] * dec + ss_ref[c, r, :]

    def outputs(g):
        for c in range(g * cpg, (g + 1) * cpg):
            rows = slice(c * ck, (c + 1) * ck)
            o = oi_ref[rows, :] + lax.dot_general(qd_ref[rows, :], sb_ref[c], nt,
                                                  preferred_element_type=F32)
            ms = jnp.mean(o * o, axis=-1, keepdims=True)
            y = o * lax.rsqrt(ms + EPS) * og_ref[...]
            r = r_ref[rows, :]
            o_ref[rows, :] = y.astype(o_ref.dtype) * (r * jax.nn.sigmoid(r))

    groups = [[functools.partial(fn, g) for g in range(ng)]
              for fn in (log_decay, cumsum, decayed_qk, intra, state_terms)]
    return groups + [[scan], [functools.partial(outputs, g) for g in range(ng)]]


N_MOBA_IN, N_MOBA_SCRATCH = 6, 7
N_GLA_IN = 8


def _mixer_ab_kernel(*refs, n_cast):
    n_in = N_MOBA_IN + N_GLA_IN
    ins, cast_in = refs[:n_in], refs[n_in:n_in + n_cast]
    oa_ref, ob_ref = refs[n_in + n_cast:n_in + n_cast + 2]
    cast_out = refs[n_in + n_cast + 2:n_in + 2 * n_cast + 2]
    scratch = refs[n_in + 2 * n_cast + 2:]
    moba_stages = _moba_stages(*ins[:N_MOBA_IN], oa_ref, *scratch[:N_MOBA_SCRATCH])
    gla_steps = [fn for group in _gla_stages(*ins[N_MOBA_IN:], ob_ref, *scratch[N_MOBA_SCRATCH:])
                 for fn in group]
    per_slot = -(-len(gla_steps) // len(moba_stages))
    for n, stage in enumerate(moba_stages):
        stage()
        for fn in gla_steps[n * per_slot:(n + 1) * per_slot]:
            fn()
    _cast_all(cast_in, cast_out)


def mixer_ab(proj_out, lr_blk, vec, bias, lr_w, layer, to_cast, *, bsz, seq):
    hd = HEAD_DIM
    assert A_HEADS == B_HEADS
    qa0 = AB0 // hd
    qb0 = (AB0 + 3 * A_WIDTH) // B_DK
    kb0 = qb0 + B_HEADS
    vb0 = (AB0 + 3 * A_WIDTH + 2 * B_KWIDTH) // B_DV
    rb0 = vb0 + B_HEADS
    nc = seq // B_CHUNK
    cast_in_specs, cast_out_specs, cast_shapes = _cast_specs(
        to_cast, layer, bsz * A_HEADS, lambda b, h: b * A_HEADS + h)
    return pl.pallas_call(
        functools.partial(_mixer_ab_kernel, n_cast=len(to_cast)),
        grid=(bsz, A_HEADS),
        in_specs=[pl.BlockSpec((seq, hd), lambda b, h: (b, qa0 + h)),
                  pl.BlockSpec((seq, hd), lambda b, h: (b, qa0 + A_HEADS + h)),
                  pl.BlockSpec((seq, hd), lambda b, h: (b, qa0 + 2 * A_HEADS + h)),
                  _vec_spec("q_norm_g", layer),
                  _vec_spec("k_norm_g", layer),
                  pl.BlockSpec((1, A_BLOCK, 2 * A_BLOCK), lambda b, h: (h, 0, 0)),
                  pl.BlockSpec((seq, B_DK), lambda b, h: (b, qb0 + h)),
                  pl.BlockSpec((seq, B_DK), lambda b, h: (b, kb0 + h)),
                  pl.BlockSpec((seq, B_DV), lambda b, h: (b, vb0 + h)),
                  pl.BlockSpec((seq, B_DV), lambda b, h: (b, rb0 + h)),
                  pl.BlockSpec((seq, LANES), lambda b, h: (b, 0)),
                  pl.BlockSpec((None, LANES, B_DK), lambda b, h: (layer, 0, h)),
                  _vec_spec("gla_lr_b", layer, B_DK, lambda b, h: h),
                  _vec_spec("gla_out_g", layer)] + cast_in_specs,
        out_specs=[pl.BlockSpec((seq, hd), lambda b, h: (b, h)),
                   pl.BlockSpec((seq, B_DV), lambda b, h: (b, h))] + cast_out_specs,
        out_shape=[jax.ShapeDtypeStruct((bsz * seq, A_WIDTH), BF16),
                   jax.ShapeDtypeStruct((bsz * seq, B_VWIDTH), BF16)] + cast_shapes,
        scratch_shapes=[pltpu.VMEM((seq, hd), F32),
                        pltpu.VMEM((seq, 2 * hd), BF16),
                        pltpu.VMEM((seq, 2 * hd), BF16),
                        pltpu.VMEM((seq, 2 * hd), BF16),
                        pltpu.VMEM((seq // A_BLOCK, hd), F32),
                        pltpu.VMEM((2, A_BLOCK, seq), F32),
                        pltpu.VMEM((2, A_BLOCK, seq), BF16),
                        pltpu.VMEM((seq, 3 * B_DK), BF16),
                        pltpu.VMEM((seq, B_DK), F32),
                        pltpu.VMEM((seq, B_DK), BF16),
                        pltpu.VMEM((seq, B_DK), BF16),
                        pltpu.VMEM((seq, B_DK), BF16),
                        pltpu.VMEM((nc, B_DK), F32),
                        pltpu.VMEM((seq, B_DV), F32),
                        pltpu.VMEM((nc, B_DV, B_DK), F32),
                        pltpu.VMEM((nc, B_DV, B_DK), BF16)],
        compiler_params=_cparams("parallel", "parallel"),
        name="mixer_ab",
    )(proj_out, proj_out, proj_out, vec, vec, bias,
      proj_out, proj_out, proj_out, proj_out, lr_blk, lr_w, vec, vec, *to_cast)


def _gelu(x):
    c = math.sqrt(2.0 / math.pi)
    return 0.5 * x * (1.0 + jnp.tanh(c * (x + 0.044715 * (x * x * x))))


SGU_SUB = 2 * C_CHUNK


def _sgu_kernel(*refs, n_cast):
    x_ref, g1_ref, wsrc_ref, lg_ref, lb_ref, w_ref, b_ref = refs[:7]
    o_ref, lr_ref, xn_ref = refs[7 + n_cast:10 + n_cast]
    wc_ref, luv_ref = refs[-2:]

    @pl.when(pl.program_id(0) == 0)
    def _():
        wc_ref[...] = wsrc_ref[...].astype(BF16)

    _cast_all(refs[7:7 + n_cast], refs[10 + n_cast:10 + 2 * n_cast])
    ti = lax.broadcasted_iota(jnp.int32, (C_CHUNK, C_CHUNK), 0)
    si = lax.broadcasted_iota(jnp.int32, (C_CHUNK, C_CHUNK), 1)
    causal = si <= ti
    ws = [jnp.where(causal, w_ref[g], 0.0).astype(BF16) for g in range(C_GROUPS)]
    n_sub = x_ref.shape[0] // SGU_SUB

    def project(s):
        rs = slice(s * SGU_SUB, (s + 1) * SGU_SUB)
        x = x_ref[rs, :]
        ms = jnp.mean(x * x, axis=-1, keepdims=True)
        xn = (x * lax.rsqrt(ms + EPS) * g1_ref[...]).astype(BF16)
        xn_ref[rs, :] = xn
        luv = lax.dot_general(xn, wc_ref[...], (((1,), (1,)), ((), ())), preferred_element_type=F32)
        luv_ref[s % 2] = luv
        lr_ref[rs, :] = luv[:, :LANES].astype(lr_ref.dtype)

    def gate(s):
        for n in range(SGU_SUB // C_CHUNK):
            rl = slice(n * C_CHUNK, (n + 1) * C_CHUNK)
            ro = slice(s * SGU_SUB + n * C_CHUNK, s * SGU_SUB + (n + 1) * C_CHUNK)
            v = _gelu(luv_ref[s % 2, rl, LANES + C_WIDTH:])
            mu = jnp.mean(v, axis=-1, keepdims=True)
            vc = v - mu
            vn = vc * lax.rsqrt(jnp.mean(vc * vc, axis=-1, keepdims=True) + EPS)
            vn = (vn * lg_ref[...] + lb_ref[...]).astype(BF16)
            u = _gelu(luv_ref[s % 2, rl, LANES:LANES + C_WIDTH])
            for g in range(C_GROUPS):
                cols = slice(g * C_GROUP_DIM, (g + 1) * C_GROUP_DIM)
                mixed = jnp.dot(ws[g], vn[:, cols], preferred_element_type=F32) + b_ref[g]
                o_ref[ro, cols] = (u[:, cols] * mixed).astype(o_ref.dtype)

    project(0)
    for s in range(n_sub):
        if s + 1 < n_sub:
            project(s + 1)
        gate(s)


def sgu(x, vec, w_t, w_s, b_s, layer, to_cast, *, tm):
    m, d = x.shape
    cast_in_specs, cast_out_specs, cast_shapes = _cast_specs(to_cast, layer, m // tm, lambda i: i)
    return pl.pallas_call(
        functools.partial(_sgu_kernel, n_cast=len(to_cast)),
        grid=(m // tm,),
        in_specs=[pl.BlockSpec((tm, d), lambda i: (i, 0)),
                  _vec_spec("norm1_g", layer),
                  pl.BlockSpec((None, pl.Element(C_WIN), pl.Element(d)), lambda i: (layer, C_SRC, 0),
                               pipeline_mode=pl.Buffered(1)),
                  _vec_spec("sg_ln_g", layer),
                  _vec_spec("sg_ln_b", layer),
                  pl.BlockSpec((None, C_GROUPS, C_CHUNK, C_CHUNK), lambda i: (layer, 0, 0, 0)),
                  pl.BlockSpec((None, C_GROUPS, C_CHUNK, 1), lambda i: (layer, 0, 0, 0))] + cast_in_specs,
        out_specs=[pl.BlockSpec((tm, C_WIDTH), lambda i: (i, 0)),
                   pl.BlockSpec((tm, LANES), lambda i: (i, 0)),
                   pl.BlockSpec((tm, d), lambda i: (i, 0))] + cast_out_specs,
        out_shape=[jax.ShapeDtypeStruct((m, C_WIDTH), BF16),
                   jax.ShapeDtypeStruct((m, LANES), BF16),
                   jax.ShapeDtypeStruct((m, d), BF16)] + cast_shapes,
        scratch_shapes=[pltpu.VMEM((C_WIN, d), BF16),
                        pltpu.VMEM((2, SGU_SUB, C_WIN), F32)],
        compiler_params=_cparams("arbitrary"),
        name="sgu",
    )(x, vec, w_t, vec, vec, w_s, b_s, *to_cast)


def _merge_kernel(ya_ref, yb_ref, yc_ref, ga_ref, gb_ref, gc_ref, x_ref,
                  wa_ref, wb_ref, wc_ref, wo_ref, g2_ref, o_ref, on_ref):
    sub = 256
    for s in range(x_ref.shape[0] // sub):
        r = slice(s * sub, (s + 1) * sub)

        def gated(g_ref, y_ref, w_ref):
            return (jax.nn.sigmoid(g_ref[r, :].astype(F32))
                    * jnp.dot(y_ref[r, :], w_ref[...], preferred_element_type=F32))

        merged = gated(ga_ref, ya_ref, wa_ref) + gated(gb_ref, yb_ref, wb_ref) + gated(gc_ref, yc_ref, wc_ref)
        h = x_ref[r, :] + jnp.dot(merged.astype(BF16), wo_ref[...], preferred_element_type=F32)
        o_ref[r, :] = h
        ms = jnp.mean(h * h, axis=-1, keepdims=True)
        on_ref[r, :] = (h * lax.rsqrt(ms + EPS) * g2_ref[...]).astype(on_ref.dtype)


def merge(ya, yb, yc, proj_out, x, wa, wb, wc, wo, vec, layer, *, tm):
    m, d = x.shape
    resident = functools.partial(pl.BlockSpec, pipeline_mode=pl.Buffered(1))
    return pl.pallas_call(
        _merge_kernel,
        grid=(m // tm,),
        in_specs=[pl.BlockSpec((tm, A_WIDTH), lambda i: (i, 0)),
                  pl.BlockSpec((tm, B_VWIDTH), lambda i: (i, 0)),
                  pl.BlockSpec((tm, C_WIDTH), lambda i: (i, 0)),
                  pl.BlockSpec((tm, d), lambda i: (i, 0)),
                  pl.BlockSpec((tm, d), lambda i: (i, 1)),
                  pl.BlockSpec((tm, d), lambda i: (i, 2)),
                  pl.BlockSpec((tm, d), lambda i: (i, 0)),
                  resident((A_WIDTH, d), lambda i: (0, 0)),
                  resident((B_VWIDTH, d), lambda i: (0, 0)),
                  resident((C_WIDTH, d), lambda i: (0, 0)),
                  resident((d, d), lambda i: (0, 0)),
                  _vec_spec("norm2_g", layer)],
        out_specs=[pl.BlockSpec((tm, d), lambda i: (i, 0)),
                   pl.BlockSpec((tm, d), lambda i: (i, 0))],
        out_shape=[jax.ShapeDtypeStruct((m, d), F32),
                   jax.ShapeDtypeStruct((m, d), BF16)],
        compiler_params=_cparams("parallel"),
        name="merge",
    )(ya, yb, yc, proj_out, proj_out, proj_out, x, wa, wb, wc, wo, vec)


def _mlp_kernel(h_hbm, hn_ref, w1_ref, w2_ref, o_ref, h_ref, h_sem):
    i, j = pl.program_id(0), pl.program_id(1)
    last = pl.num_programs(1) - 1
    tm = h_ref.shape[0]
    h_copy = pltpu.make_async_copy(h_hbm.at[pl.ds(i * tm, tm), :], h_ref, h_sem)

    def update():
        a = jnp.dot(hn_ref[...], w1_ref[...], preferred_element_type=F32)
        a = jnp.square(jnp.maximum(a, 0.0)).astype(BF16)
        return jnp.dot(a, w2_ref[...], preferred_element_type=F32)

    @pl.when(j == 0)
    def _():
        h_copy.start()
        o_ref[...] = update()

    @pl.when((j > 0) & (j < last))
    def _():
        o_ref[...] += update()

    @pl.when(j == last)
    def _():
        h_copy.wait()
        o_ref[...] = (o_ref[...] + h_ref[...]) + update()


def mlp(h, hn, w1, w2, *, tm, tf):
    m, d = h.shape
    f = w1.shape[1]
    return pl.pallas_call(
        _mlp_kernel,
        grid=(m // tm, f // tf),
        in_specs=[pl.BlockSpec(memory_space=pl.ANY),
                  pl.BlockSpec((tm, d), lambda i, j: (i, 0)),
                  pl.BlockSpec((d, tf), lambda i, j: (0, j)),
                  pl.BlockSpec((tf, d), lambda i, j: (j, 0))],
        out_specs=pl.BlockSpec((tm, d), lambda i, j: (i, 0)),
        out_shape=jax.ShapeDtypeStruct((m, d), F32),
        scratch_shapes=[pltpu.VMEM((tm, d), F32),
                        pltpu.SemaphoreType.DMA(())],
        compiler_params=_cparams("parallel", "arbitrary"),
        name="mlp",
    )(h, hn, w1, w2)


def kernel(x, rpe_table, norm1_g, w_in, q_norm_g, k_norm_g, gla_lr_w, gla_lr_b, gla_out_g,
           sg_ln_g, sg_ln_b, sg_w, sg_b, w_br_a, w_br_b, w_br_c, w_o, norm2_g, w_ff1, w_ff2):
    bsz, seq, d = x.shape
    assert seq % A_BLOCK == 0 and seq % B_CHUNK == 0 and seq % C_CHUNK == 0
    assert w_in.shape[2] == IN_WIDTH
    h = x.reshape(bsz * seq, d)
    fields = dict(norm1_g=norm1_g, norm2_g=norm2_g, sg_ln_g=sg_ln_g, sg_ln_b=sg_ln_b, gla_lr_b=gla_lr_b,
                  gla_out_g=gla_out_g, q_norm_g=q_norm_g, k_norm_g=k_norm_g)
    bias, vec, lr_w = rpe_bias(rpe_table, gla_lr_w, [fields[name] for name, _ in VEC_FIELDS])
    w_t = jnp.swapaxes(w_in, 1, 2)
    sg_b_col = sg_b[..., None]
    for l in range(DEPTH):
        y_c, lr_blk, xn, wa, wb, wc, wo = sgu(h, vec, w_t, sg_w, sg_b_col, l,
                                              (w_br_a, w_br_b, w_br_c, w_o), tm=SGU_TM)
        p, w1 = proj(xn, w_t, l, (w_ff1,), tm=PROJ_TM)
        y_a, y_b, w2 = mixer_ab(p, lr_blk, vec, bias, lr_w, l, (w_ff2,), bsz=bsz, seq=seq)
        h, hn = merge(y_a, y_b, y_c, p, h, wa, wb, wc, wo, vec, l, tm=MERGE_TM)
        h = mlp(h, hn, w1, w2, tm=MLP_TM, tf=MLP_TF)
    return h.reshape(bsz, seq, d)
```

```python
import functools
import math

import jax
import jax.numpy as jnp
from jax import lax
from jax.experimental import pallas as pl
from jax.experimental.pallas import tpu as pltpu

D_MODEL = 2048
DEPTH = 2
HEAD_DIM = 128
A_HEADS = 4
A_WIDTH = A_HEADS * HEAD_DIM
A_BLOCK = 256
A_TOPK = 3
RPE_BUCKETS = 32
RPE_MAX_DIST = 128
B_HEADS = 4
B_DK = 128
B_DV = 256
B_KWIDTH = B_HEADS * B_DK
B_VWIDTH = B_HEADS * B_DV
B_LOWRANK = 16
B_GATE_NORM = 16.0
B_CHUNK = 64
C_GROUPS = 4
C_GROUP_DIM = 128
C_WIDTH = C_GROUPS * C_GROUP_DIM
C_CHUNK = 128
D_FF = 4 * D_MODEL
EPS = 1e-6
NEG_INF = -1e30
LOG2E = 1.4426950408889634

LANES = 128
BF16_ROWS = 16
VMEM_LIMIT = 56 * 1024 * 1024

F32 = jnp.float32
BF16 = jnp.bfloat16

PROJ_TM = 1024
MLP_TM = 1024
MLP_TF = 1024
MERGE_TM = 512
SGU_TM = 1024

PROJ_TN = 1536
AB_WIDTH = 3 * A_WIDTH + 2 * B_KWIDTH + 2 * B_VWIDTH
G_WIDTH = 3 * D_MODEL
IN_WIDTH = AB_WIDTH + B_LOWRANK + 2 * C_WIDTH + G_WIDTH
AB0 = G_WIDTH
PROJ_WIDTH = G_WIDTH + AB_WIDTH
N_AB_TILES = AB_WIDTH // PROJ_TN
N_G_TILES = G_WIDTH // PROJ_TN
G_SRC = IN_WIDTH - G_WIDTH
C_PAD = LANES - B_LOWRANK
C_SRC = AB_WIDTH - C_PAD
C_WIN = LANES + 2 * C_WIDTH
LR_LANE = C_PAD
assert AB_WIDTH % PROJ_TN == 0 and G_WIDTH % PROJ_TN == 0
assert C_SRC % BF16_ROWS == 0 and G_SRC % BF16_ROWS == 0


def _rpe_thresholds():
    max_exact = RPE_BUCKETS // 2

    def bucket(n):
        if n < max_exact:
            return n
        v = math.log(n / max_exact) / math.log(RPE_MAX_DIST / max_exact) * (RPE_BUCKETS - max_exact)
        return min(max_exact + int(v), RPE_BUCKETS - 1)

    table = [bucket(n) for n in range(4 * RPE_MAX_DIST)]
    return [min(n for n in range(len(table)) if table[n] >= b)
            for b in range(max_exact + 1, RPE_BUCKETS)]


RPE_THRESHOLDS = _rpe_thresholds()
assert RPE_THRESHOLDS[-1] <= A_BLOCK


VEC_FIELDS = (("norm1_g", D_MODEL), ("norm2_g", D_MODEL), ("sg_ln_g", C_WIDTH), ("sg_ln_b", C_WIDTH),
              ("gla_lr_b", B_KWIDTH), ("gla_out_g", B_DV), ("q_norm_g", HEAD_DIM), ("k_norm_g", HEAD_DIM))
VEC_OFFSET = {}
VEC_WIDTH = 0
for _name, _width in VEC_FIELDS:
    assert VEC_WIDTH % _width == 0
    VEC_OFFSET[_name] = VEC_WIDTH
    VEC_WIDTH += _width


def _vec_spec(name, layer, width=None, block_of=None):
    width = width or dict(VEC_FIELDS)[name]
    first = VEC_OFFSET[name] // width
    if block_of is None:
        return pl.BlockSpec((None, 1, width), lambda *idx: (layer, 0, first))
    return pl.BlockSpec((None, 1, width), lambda *idx: (layer, 0, first + block_of(*idx)))


def _cparams(*sem):
    return pltpu.CompilerParams(dimension_semantics=sem, vmem_limit_bytes=VMEM_LIMIT)


def _rpe_bias_kernel(*refs):
    tbl_ref, lrw_ref = refs[:2]
    fields = refs[2:2 + len(VEC_FIELDS)]
    bias_ref, vec_ref, lrw_pad_ref = refs[2 + len(VEC_FIELDS):]
    h = pl.program_id(0)

    @pl.when(h == 0)
    def _():
        for (name, width), src in zip(VEC_FIELDS, fields):
            for l in range(src.shape[0]):
                vec_ref[l, :, VEC_OFFSET[name]:VEC_OFFSET[name] + width] = src[l:l + 1, :]
        lrw_pad_ref[...] = jnp.zeros(lrw_pad_ref.shape, lrw_pad_ref.dtype)
        lrw_pad_ref[:, LR_LANE:LR_LANE + B_LOWRANK, :] = lrw_ref[...]

    t = lax.broadcasted_iota(jnp.int32, (A_BLOCK, 2 * A_BLOCK), 0)
    c = lax.broadcasted_iota(jnp.int32, (A_BLOCK, 2 * A_BLOCK), 1)
    n = jnp.maximum(t - c + A_BLOCK, 0)
    max_exact = RPE_BUCKETS // 2
    large = jnp.full(n.shape, max_exact, jnp.int32)
    for thr in RPE_THRESHOLDS:
        large = large + (n >= thr).astype(jnp.int32)
    bucket = jnp.where(n < max_exact, n, large)
    bias = jnp.zeros(n.shape, F32)
    for b in range(RPE_BUCKETS):
        bias = jnp.where(bucket == b, tbl_ref[b, h], bias)
    bias = (bias - tbl_ref[RPE_BUCKETS - 1, h]) * LOG2E
    bias_ref[0] = jnp.where(c - A_BLOCK > t, NEG_INF, bias)


def rpe_bias(rpe_table, lr_w, fields):
    layers = lr_w.shape[0]
    whole = lambda a: pl.BlockSpec(a.shape, lambda h: (0,) * a.ndim)
    return pl.pallas_call(
        _rpe_bias_kernel,
        grid=(A_HEADS,),
        in_specs=[pl.BlockSpec(memory_space=pltpu.SMEM), whole(lr_w)] + [whole(f) for f in fields],
        out_specs=[pl.BlockSpec((1, A_BLOCK, 2 * A_BLOCK), lambda h: (h, 0, 0)),
                   pl.BlockSpec((layers, 1, VEC_WIDTH), lambda h: (0, 0, 0)),
                   pl.BlockSpec((layers, LANES, B_KWIDTH), lambda h: (0, 0, 0))],
        out_shape=[jax.ShapeDtypeStruct((A_HEADS, A_BLOCK, 2 * A_BLOCK), F32),
                   jax.ShapeDtypeStruct((layers, 1, VEC_WIDTH), F32),
                   jax.ShapeDtypeStruct((layers, LANES, B_KWIDTH), F32)],
        compiler_params=_cparams("arbitrary"),
        name="rpe_bias",
    )(rpe_table, lr_w, *fields)


def _cast_specs(to_cast, layer, n_steps, step_of):
    in_specs, out_specs, shapes = [], [], []
    for w in to_cast:
        _, rows, cols = w.shape
        slice_rows = next(r for r in range(BF16_ROWS, rows + 1, BF16_ROWS)
                          if rows % r == 0 and rows // r <= n_steps)
        n_slices = rows // slice_rows

        def slice_of(*idx, n_slices=n_slices):
            return jnp.minimum(step_of(*idx), n_slices - 1)

        in_specs.append(pl.BlockSpec((None, slice_rows, cols), lambda *idx, f=slice_of: (layer, f(*idx), 0)))
        out_specs.append(pl.BlockSpec((slice_rows, cols), lambda *idx, f=slice_of: (f(*idx), 0)))
        shapes.append(jax.ShapeDtypeStruct((rows, cols), BF16))
    return in_specs, out_specs, shapes


def _cast_all(srcs, dsts):
    for src, dst in zip(srcs, dsts):
        dst[...] = src[...].astype(dst.dtype)


def _proj_kernel(*refs, n_cast):
    xn_ref, w_ref = refs[:2]
    o_ref = refs[2 + n_cast]
    o_ref[...] = lax.dot_general(xn_ref[...], w_ref[...].astype(BF16), (((1,), (1,)), ((), ())),
                                 preferred_element_type=F32).astype(o_ref.dtype)
    _cast_all(refs[2:2 + n_cast], refs[3 + n_cast:])


def _proj_src_row(j):
    u = BF16_ROWS
    return u * jnp.where(j < N_AB_TILES, j * (PROJ_TN // u),
                         G_SRC // u + (j - N_AB_TILES) * (PROJ_TN // u))


def _proj_out_tile(j):
    return jnp.where(j < N_AB_TILES, j + N_G_TILES, j - N_AB_TILES)


def proj(xn, w_t, layer, to_cast, *, tm):
    m, k = xn.shape
    tn = PROJ_TN
    n_tiles = N_AB_TILES + N_G_TILES
    cast_in_specs, cast_out_specs, cast_shapes = _cast_specs(
        to_cast, layer, (m // tm) * n_tiles, lambda i, j: i * n_tiles + j)
    return pl.pallas_call(
        functools.partial(_proj_kernel, n_cast=len(to_cast)),
        grid=(m // tm, n_tiles),
        in_specs=[pl.BlockSpec((tm, k), lambda i, j: (i, 0)),
                  pl.BlockSpec((None, pl.Element(tn), pl.Element(k)),
                               lambda i, j: (layer, _proj_src_row(j), 0))] + cast_in_specs,
        out_specs=[pl.BlockSpec((tm, tn), lambda i, j: (i, _proj_out_tile(j)))] + cast_out_specs,
        out_shape=[jax.ShapeDtypeStruct((m, PROJ_WIDTH), BF16)] + cast_shapes,
        compiler_params=_cparams("parallel", "arbitrary"),
        name="proj",
    )(xn, w_t, *to_cast)


def _head_rms(x, g):
    ms = jnp.mean(x * x, axis=-1, keepdims=True)
    return x * lax.rsqrt(ms + EPS) * g


def _moba_stages(q_ref, k_ref, v_ref, qg_ref, kg_ref, bias_ref, o_ref,
                 qn_ref, qa_ref, ka_ref, va_ref, kmean_ref, s_ref, p_ref):
    seq, hd = q_ref.shape
    nb = seq // A_BLOCK
    rc = BF16_ROWS

    def prologue():
        lane = lax.broadcasted_iota(jnp.int32, (A_BLOCK, hd), 1)
        for j in range(nb):
            rows = slice(j * A_BLOCK, (j + 1) * A_BLOCK)
            qn = _head_rms(q_ref[rows, :].astype(F32), qg_ref[...])
            qn_ref[rows, :] = qn
            qa_ref[rows, :hd] = (qn * (hd ** -0.5 * LOG2E)).astype(BF16)
            qa_ref[rows, hd:] = jnp.zeros((A_BLOCK, hd), BF16)
            kn = _head_rms(k_ref[rows, :].astype(F32), kg_ref[...])
            ka_ref[rows, :hd] = kn.astype(BF16)
            ka_ref[rows, hd:] = jnp.where(lane == j, 1.0, 0.0).astype(BF16)
            va_ref[rows, :hd] = v_ref[rows, :]
            va_ref[rows, hd:] = jnp.ones((A_BLOCK, hd), BF16)
            kmean_ref[j:j + 1, :] = jnp.mean(kn, axis=0, keepdims=True)

    def scores(i):
        rows = slice(i * A_BLOCK, (i + 1) * A_BLOCK)
        nk = (i + 1) * A_BLOCK
        if i > A_TOPK:
            blk = lax.broadcasted_iota(jnp.int32, (nb, A_BLOCK), 0)
            eye = jnp.where(lax.broadcasted_iota(jnp.int32, (nb, hd), 0)
                            == lax.broadcasted_iota(jnp.int32, (nb, hd), 1), 1.0, 0.0)
            g = lax.dot_general(kmean_ref[...], qn_ref[rows, :], (((1,), (1,)), ((), ())),
                                precision=lax.Precision.HIGHEST, preferred_element_type=F32)
            rank = jnp.zeros((nb, A_BLOCK), jnp.int32)
            for jp in range(i):
                gj = g[jp:jp + 1, :]
                beats = (gj > g) | ((gj == g) & (jp < blk))
                rank = rank + beats.astype(jnp.int32)
            neg = jnp.where((blk < i) & (rank >= A_TOPK), NEG_INF, 0.0)
            qa_ref[rows, hd:] = lax.dot_general(neg, eye, (((0,), (0,)), ((), ())),
                                                preferred_element_type=F32).astype(BF16)
        s_ref[i % 2, :, :nk] = lax.dot_general(qa_ref[rows, :], ka_ref[:nk, :], (((1,), (1,)), ((), ())),
                                               preferred_element_type=F32)

    def softmax(i):
        for c in range(A_BLOCK // rc):
            r = slice(c * rc, (c + 1) * rc)
            tiles = []
            for j in range(i + 1):
                sj = s_ref[i % 2, r, j * A_BLOCK:(j + 1) * A_BLOCK]
                if j == i:
                    sj = sj + bias_ref[0, r, A_BLOCK:]
                elif j == i - 1:
                    sj = sj + bias_ref[0, r, :A_BLOCK]
                tiles.append(sj)
            mt = tiles[0]
            for t in tiles[1:]:
                mt = jnp.maximum(mt, t)
            m = jnp.max(mt, axis=-1, keepdims=True)
            for j, t in enumerate(tiles):
                p_ref[i % 2, r, j * A_BLOCK:(j + 1) * A_BLOCK] = jnp.exp2(t - m).astype(BF16)

    def weighted_sum(i):
        rows = slice(i * A_BLOCK, (i + 1) * A_BLOCK)
        nk = (i + 1) * A_BLOCK
        o = jnp.dot(p_ref[i % 2, :, :nk], va_ref[:nk, :], preferred_element_type=F32)
        o_ref[rows, :] = (o[:, :hd] / o[:, hd:]).astype(o_ref.dtype)

    stages = [prologue, functools.partial(scores, 0)]
    for i in range(nb):
        if i + 1 < nb:
            stages.append(functools.partial(scores, i + 1))
        stages += [functools.partial(softmax, i), functools.partial(weighted_sum, i)]
    return stages


def _gla_stages(q_ref, k_ref, v_ref, r_ref, lr_ref, lrw_ref, lrb_ref, og_ref, o_ref,
                hml_ref, bc_ref, qd_ref, ki_ref, ke_ref, dec_ref, oi_ref, ss_ref, sb_ref):
    seq = q_ref.shape[0]
    ck = B_CHUNK
    nc = seq // ck
    grp = 4 * ck
    ng = seq // grp
    cpg = grp // ck
    qscale = B_DK ** -0.5
    nt = (((1,), (1,)), ((), ()))
    tn = (((0,), (0,)), ((), ()))

    def same_chunk_causal():
        ti = lax.broadcasted_iota(jnp.int32, (grp, grp), 0)
        si = lax.broadcasted_iota(jnp.int32, (grp, grp), 1)
        return (si <= ti) & (si // ck == ti // ck)

    def log_decay(g):
        rows = slice(g * grp, (g + 1) * grp)
        z = jnp.dot(lr_ref[rows, :], lrw_ref[...].astype(BF16), preferred_element_type=F32) + lrb_ref[...]
        la = (jnp.minimum(z, 0.0) - jnp.log(1.0 + jnp.exp(-jnp.abs(z)))) * (1.0 / B_GATE_NORM)
        hi = la.astype(BF16)
        r1 = la - hi.astype(F32)
        mid = r1.astype(BF16)
        hml_ref[rows, :B_DK] = hi
        hml_ref[rows, B_DK:2 * B_DK] = mid
        hml_ref[rows, 2 * B_DK:] = (r1 - mid.astype(F32)).astype(BF16)

    def cumsum(g):
        rows = slice(g * grp, (g + 1) * grp)
        tril_grp = jnp.where(same_chunk_causal(), 1.0, 0.0).astype(BF16)
        parts = jnp.dot(tril_grp, hml_ref[rows, :], preferred_element_type=F32)
        bc_ref[rows, :] = parts[:, :B_DK] + parts[:, B_DK:2 * B_DK] + parts[:, 2 * B_DK:]

    def decayed_qk(g):
        for c in range(g * cpg, (g + 1) * cpg):
            rows = slice(c * ck, (c + 1) * ck)
            bc = bc_ref[rows, :]
            bl = bc_ref[(c + 1) * ck - 1:(c + 1) * ck, :]
            q = q_ref[rows, :].astype(F32)
            k = k_ref[rows, :].astype(F32)
            qd_ref[rows, :] = (q * jnp.exp(bc) * qscale).astype(BF16)
            ki_ref[rows, :] = (k * jnp.exp(-bc)).astype(BF16)
            ke_ref[rows, :] = (k * jnp.exp(bl - bc)).astype(BF16)
            dec_ref[c:c + 1, :] = jnp.exp(bl)

    def intra(g):
        rows = slice(g * grp, (g + 1) * grp)
        a = lax.dot_general(qd_ref[rows, :], ki_ref[rows, :], nt, preferred_element_type=F32)
        a = jnp.where(same_chunk_causal(), a, 0.0).astype(BF16)
        oi_ref[rows, :] = jnp.dot(a, v_ref[rows, :], preferred_element_type=F32)

    def state_terms(g):
        for c in range(g * cpg, (g + 1) * cpg):
            rows = slice(c * ck, (c + 1) * ck)
            ss_ref[c] = lax.dot_general(v_ref[rows, :], ke_ref[rows, :], tn, preferred_element_type=F32)

    def scan():
        slab = B_DV // 4
        st = [jnp.zeros((slab, B_DK), F32) for _ in range(4)]
        for c in range(nc):
            dec = dec_ref[c:c + 1, :]
            for s in range(4):
                r = slice(s * slab, (s + 1) * slab)
                sb_ref[c, r, :] = st[s].astype(BF16)
                st[s] = st[s] * dec + ss_ref[c, r, :]

    def outputs(g):
        for c in range(g * cpg, (g + 1) * cpg):
            rows = slice(c * ck, (c + 1) * ck)
            o = oi_ref[rows, :] + lax.dot_general(qd_ref[rows, :], sb_ref[c], nt,
                                                  preferred_element_type=F32)
            ms = jnp.mean(o * o, axis=-1, keepdims=True)
            y = o * lax.rsqrt(ms + EPS) * og_ref[...]
            r = r_ref[rows, :]
            half_r = 0.5 * r
            o_ref[rows, :] = y.astype(o_ref.dtype) * (half_r + half_r * jnp.tanh(half_r))

    groups = [[functools.partial(fn, g) for g in range(ng)]
              for fn in (log_decay, cumsum, decayed_qk, intra, state_terms)]
    return groups + [[scan], [functools.partial(outputs, g) for g in range(ng)]]


N_MOBA_IN, N_MOBA_SCRATCH = 6, 7
N_GLA_IN = 8


def _mixer_ab_kernel(*refs, n_cast):
    n_in = N_MOBA_IN + N_GLA_IN
    ins, cast_in = refs[:n_in], refs[n_in:n_in + n_cast]
    oa_ref, ob_ref = refs[n_in + n_cast:n_in + n_cast + 2]
    cast_out = refs[n_in + n_cast + 2:n_in + 2 * n_cast + 2]
    scratch = refs[n_in + 2 * n_cast + 2:]
    moba_stages = _moba_stages(*ins[:N_MOBA_IN], oa_ref, *scratch[:N_MOBA_SCRATCH])
    gla_steps = [fn for group in _gla_stages(*ins[N_MOBA_IN:], ob_ref, *scratch[N_MOBA_SCRATCH:])
                 for fn in group]
    per_slot = -(-len(gla_steps) // len(moba_stages))
    for n, stage in enumerate(moba_stages):
        stage()
        for fn in gla_steps[n * per_slot:(n + 1) * per_slot]:
            fn()
    _cast_all(cast_in, cast_out)


def mixer_ab(proj_out, lr_blk, vec, bias, lr_w, layer, to_cast, *, bsz, seq):
    hd = HEAD_DIM
    assert A_HEADS == B_HEADS
    qa0 = AB0 // hd
    qb0 = (AB0 + 3 * A_WIDTH) // B_DK
    kb0 = qb0 + B_HEADS
    vb0 = (AB0 + 3 * A_WIDTH + 2 * B_KWIDTH) // B_DV
    rb0 = vb0 + B_HEADS
    nc = seq // B_CHUNK
    cast_in_specs, cast_out_specs, cast_shapes = _cast_specs(
        to_cast, layer, bsz * A_HEADS, lambda b, h: b * A_HEADS + h)
    return pl.pallas_call(
        functools.partial(_mixer_ab_kernel, n_cast=len(to_cast)),
        grid=(bsz, A_HEADS),
        in_specs=[pl.BlockSpec((seq, hd), lambda b, h: (b, qa0 + h)),
                  pl.BlockSpec((seq, hd), lambda b, h: (b, qa0 + A_HEADS + h)),
                  pl.BlockSpec((seq, hd), lambda b, h: (b, qa0 + 2 * A_HEADS + h)),
                  _vec_spec("q_norm_g", layer),
                  _vec_spec("k_norm_g", layer),
                  pl.BlockSpec((1, A_BLOCK, 2 * A_BLOCK), lambda b, h: (h, 0, 0)),
                  pl.BlockSpec((seq, B_DK), lambda b, h: (b, qb0 + h)),
                  pl.BlockSpec((seq, B_DK), lambda b, h: (b, kb0 + h)),
                  pl.BlockSpec((seq, B_DV), lambda b, h: (b, vb0 + h)),
                  pl.BlockSpec((seq, B_DV), lambda b, h: (b, rb0 + h)),
                  pl.BlockSpec((seq, LANES), lambda b, h: (b, 0)),
                  pl.BlockSpec((None, LANES, B_DK), lambda b, h: (layer, 0, h)),
                  _vec_spec("gla_lr_b", layer, B_DK, lambda b, h: h),
                  _vec_spec("gla_out_g", layer)] + cast_in_specs,
        out_specs=[pl.BlockSpec((seq, hd), lambda b, h: (b, h)),
                   pl.BlockSpec((seq, B_DV), lambda b, h: (b, h))] + cast_out_specs,
        out_shape=[jax.ShapeDtypeStruct((bsz * seq, A_WIDTH), BF16),
                   jax.ShapeDtypeStruct((bsz * seq, B_VWIDTH), BF16)] + cast_shapes,
        scratch_shapes=[pltpu.VMEM((seq, hd), F32),
                        pltpu.VMEM((seq, 2 * hd), BF16),
                        pltpu.VMEM((seq, 2 * hd), BF16),
                        pltpu.VMEM((seq, 2 * hd), BF16),
                        pltpu.VMEM((seq // A_BLOCK, hd), F32),
                        pltpu.VMEM((2, A_BLOCK, seq), F32),
                        pltpu.VMEM((2, A_BLOCK, seq), BF16),
                        pltpu.VMEM((seq, 3 * B_DK), BF16),
                        pltpu.VMEM((seq, B_DK), F32),
                        pltpu.VMEM((seq, B_DK), BF16),
                        pltpu.VMEM((seq, B_DK), BF16),
                        pltpu.VMEM((seq, B_DK), BF16),
                        pltpu.VMEM((nc, B_DK), F32),
                        pltpu.VMEM((seq, B_DV), F32),
                        pltpu.VMEM((nc, B_DV, B_DK), F32),
                        pltpu.VMEM((nc, B_DV, B_DK), BF16)],
        compiler_params=_cparams("parallel", "parallel"),
        name="mixer_ab",
    )(proj_out, proj_out, proj_out, vec, vec, bias,
      proj_out, proj_out, proj_out, proj_out, lr_blk, lr_w, vec, vec, *to_cast)


def _gelu(x):
    c = math.sqrt(2.0 / math.pi)
    half_x = 0.5 * x
    return half_x + half_x * jnp.tanh(x * (c + (c * 0.044715) * (x * x)))


SGU_SUB = 2 * C_CHUNK


def _sgu_kernel(*refs, n_cast):
    x_ref, g1_ref, wsrc_ref, lg_ref, lb_ref, w_ref, b_ref = refs[:7]
    o_ref, lr_ref, xn_ref = refs[7 + n_cast:10 + n_cast]
    wc_ref, luv_ref = refs[-2:]

    @pl.when(pl.program_id(0) == 0)
    def _():
        wc_ref[...] = wsrc_ref[...].astype(BF16)

    _cast_all(refs[7:7 + n_cast], refs[10 + n_cast:10 + 2 * n_cast])
    ti = lax.broadcasted_iota(jnp.int32, (C_CHUNK, C_CHUNK), 0)
    si = lax.broadcasted_iota(jnp.int32, (C_CHUNK, C_CHUNK), 1)
    causal = si <= ti
    ws = [jnp.where(causal, w_ref[g], 0.0).astype(BF16) for g in range(C_GROUPS)]
    n_sub = x_ref.shape[0] // SGU_SUB

    def project(s):
        rs = slice(s * SGU_SUB, (s + 1) * SGU_SUB)
        x = x_ref[rs, :]
        ms = jnp.mean(x * x, axis=-1, keepdims=True)
        xn = (x * lax.rsqrt(ms + EPS) * g1_ref[...]).astype(BF16)
        xn_ref[rs, :] = xn
        luv = lax.dot_general(xn, wc_ref[...], (((1,), (1,)), ((), ())), preferred_element_type=F32)
        luv_ref[s % 2] = luv
        lr_ref[rs, :] = luv[:, :LANES].astype(lr_ref.dtype)

    def gate(s):
        for n in range(SGU_SUB // C_CHUNK):
            rl = slice(n * C_CHUNK, (n + 1) * C_CHUNK)
            ro = slice(s * SGU_SUB + n * C_CHUNK, s * SGU_SUB + (n + 1) * C_CHUNK)
            v = _gelu(luv_ref[s % 2, rl, LANES + C_WIDTH:])
            mu = jnp.mean(v, axis=-1, keepdims=True)
            vc = v - mu
            vn = vc * lax.rsqrt(jnp.mean(vc * vc, axis=-1, keepdims=True) + EPS)
            vn = (vn * lg_ref[...] + lb_ref[...]).astype(BF16)
            u = _gelu(luv_ref[s % 2, rl, LANES:LANES + C_WIDTH])
            for g in range(C_GROUPS):
                cols = slice(g * C_GROUP_DIM, (g + 1) * C_GROUP_DIM)
                mixed = jnp.dot(ws[g], vn[:, cols], preferred_element_type=F32) + b_ref[g]
                o_ref[ro, cols] = (u[:, cols] * mixed).astype(o_ref.dtype)

    project(0)
    for s in range(n_sub):
        if s + 1 < n_sub:
            project(s + 1)
        gate(s)


def sgu(x, vec, w_t, w_s, b_s, layer, to_cast, *, tm):
    m, d = x.shape
    cast_in_specs, cast_out_specs, cast_shapes = _cast_specs(to_cast, layer, m // tm, lambda i: i)
    return pl.pallas_call(
        functools.partial(_sgu_kernel, n_cast=len(to_cast)),
        grid=(m // tm,),
        in_specs=[pl.BlockSpec((tm, d), lambda i: (i, 0)),
                  _vec_spec("norm1_g", layer),
                  pl.BlockSpec((None, pl.Element(C_WIN), pl.Element(d)), lambda i: (layer, C_SRC, 0),
                               pipeline_mode=pl.Buffered(1)),
                  _vec_spec("sg_ln_g", layer),
                  _vec_spec("sg_ln_b", layer),
                  pl.BlockSpec((None, C_GROUPS, C_CHUNK, C_CHUNK), lambda i: (layer, 0, 0, 0)),
                  pl.BlockSpec((None, C_GROUPS, C_CHUNK, 1), lambda i: (layer, 0, 0, 0))] + cast_in_specs,
        out_specs=[pl.BlockSpec((tm, C_WIDTH), lambda i: (i, 0)),
                   pl.BlockSpec((tm, LANES), lambda i: (i, 0)),
                   pl.BlockSpec((tm, d), lambda i: (i, 0))] + cast_out_specs,
        out_shape=[jax.ShapeDtypeStruct((m, C_WIDTH), BF16),
                   jax.ShapeDtypeStruct((m, LANES), BF16),
                   jax.ShapeDtypeStruct((m, d), BF16)] + cast_shapes,
        scratch_shapes=[pltpu.VMEM((C_WIN, d), BF16),
                        pltpu.VMEM((2, SGU_SUB, C_WIN), F32)],
        compiler_params=_cparams("arbitrary"),
        name="sgu",
    )(x, vec, w_t, vec, vec, w_s, b_s, *to_cast)


def _merge_kernel(ya_ref, yb_ref, yc_ref, ga_ref, gb_ref, gc_ref, x_ref,
                  wa_ref, wb_ref, wc_ref, wo_ref, g2_ref, o_ref, on_ref):
    sub = 256
    for s in range(x_ref.shape[0] // sub):
        r = slice(s * sub, (s + 1) * sub)

        def gated(g_ref, y_ref, w_ref):
            gate = 0.5 * jnp.tanh(0.5 * g_ref[r, :].astype(F32)) + 0.5
            return gate * jnp.dot(y_ref[r, :], w_ref[...], preferred_element_type=F32)

        merged = gated(ga_ref, ya_ref, wa_ref) + gated(gb_ref, yb_ref, wb_ref) + gated(gc_ref, yc_ref, wc_ref)
        h = x_ref[r, :] + jnp.dot(merged.astype(BF16), wo_ref[...], preferred_element_type=F32)
        o_ref[r, :] = h
        ms = jnp.mean(h * h, axis=-1, keepdims=True)
        on_ref[r, :] = (h * lax.rsqrt(ms + EPS) * g2_ref[...]).astype(on_ref.dtype)


def merge(ya, yb, yc, proj_out, x, wa, wb, wc, wo, vec, layer, *, tm):
    m, d = x.shape
    resident = functools.partial(pl.BlockSpec, pipeline_mode=pl.Buffered(1))
    return pl.pallas_call(
        _merge_kernel,
        grid=(m // tm,),
        in_specs=[pl.BlockSpec((tm, A_WIDTH), lambda i: (i, 0)),
                  pl.BlockSpec((tm, B_VWIDTH), lambda i: (i, 0)),
                  pl.BlockSpec((tm, C_WIDTH), lambda i: (i, 0)),
                  pl.BlockSpec((tm, d), lambda i: (i, 0)),
                  pl.BlockSpec((tm, d), lambda i: (i, 1)),
                  pl.BlockSpec((tm, d), lambda i: (i, 2)),
                  pl.BlockSpec((tm, d), lambda i: (i, 0)),
                  resident((A_WIDTH, d), lambda i: (0, 0)),
                  resident((B_VWIDTH, d), lambda i: (0, 0)),
                  resident((C_WIDTH, d), lambda i: (0, 0)),
                  resident((d, d), lambda i: (0, 0)),
                  _vec_spec("norm2_g", layer)],
        out_specs=[pl.BlockSpec((tm, d), lambda i: (i, 0)),
                   pl.BlockSpec((tm, d), lambda i: (i, 0))],
        out_shape=[jax.ShapeDtypeStruct((m, d), F32),
                   jax.ShapeDtypeStruct((m, d), BF16)],
        compiler_params=_cparams("parallel"),
        name="merge",
    )(ya, yb, yc, proj_out, proj_out, proj_out, x, wa, wb, wc, wo, vec)


def _mlp_kernel(h_hbm, hn_ref, w1_ref, w2_ref, o_ref, h_ref, h_sem):
    i, j = pl.program_id(0), pl.program_id(1)
    last = pl.num_programs(1) - 1
    tm = h_ref.shape[0]
    h_copy = pltpu.make_async_copy(h_hbm.at[pl.ds(i * tm, tm), :], h_ref, h_sem)

    def update():
        a = jnp.dot(hn_ref[...], w1_ref[...], preferred_element_type=F32)
        a = jnp.square(jnp.maximum(a, 0.0)).astype(BF16)
        return jnp.dot(a, w2_ref[...], preferred_element_type=F32)

    @pl.when(j == 0)
    def _():
        h_copy.start()
        o_ref[...] = update()

    @pl.when((j > 0) & (j < last))
    def _():
        o_ref[...] += update()

    @pl.when(j == last)
    def _():
        h_copy.wait()
        o_ref[...] = (o_ref[...] + h_ref[...]) + update()


def mlp(h, hn, w1, w2, *, tm, tf):
    m, d = h.shape
    f = w1.shape[1]
    return pl.pallas_call(
        _mlp_kernel,
        grid=(m // tm, f // tf),
        in_specs=[pl.BlockSpec(memory_space=pl.ANY),
                  pl.BlockSpec((tm, d), lambda i, j: (i, 0)),
                  pl.BlockSpec((d, tf), lambda i, j: (0, j)),
                  pl.BlockSpec((tf, d), lambda i, j: (j, 0))],
        out_specs=pl.BlockSpec((tm, d), lambda i, j: (i, 0)),
        out_shape=jax.ShapeDtypeStruct((m, d), F32),
        scratch_shapes=[pltpu.VMEM((tm, d), F32),
                        pltpu.SemaphoreType.DMA(())],
        compiler_params=_cparams("parallel", "arbitrary"),
        name="mlp",
    )(h, hn, w1, w2)


def kernel(x, rpe_table, norm1_g, w_in, q_norm_g, k_norm_g, gla_lr_w, gla_lr_b, gla_out_g,
           sg_ln_g, sg_ln_b, sg_w, sg_b, w_br_a, w_br_b, w_br_c, w_o, norm2_g, w_ff1, w_ff2):
    bsz, seq, d = x.shape
    assert seq % A_BLOCK == 0 and seq % B_CHUNK == 0 and seq % C_CHUNK == 0
    assert w_in.shape[2] == IN_WIDTH
    h = x.reshape(bsz * seq, d)
    fields = dict(norm1_g=norm1_g, norm2_g=norm2_g, sg_ln_g=sg_ln_g, sg_ln_b=sg_ln_b, gla_lr_b=gla_lr_b,
                  gla_out_g=gla_out_g, q_norm_g=q_norm_g, k_norm_g=k_norm_g)
    bias, vec, lr_w = rpe_bias(rpe_table, gla_lr_w, [fields[name] for name, _ in VEC_FIELDS])
    w_t = jnp.swapaxes(w_in, 1, 2)
    sg_b_col = sg_b[..., None]
    for l in range(DEPTH):
        y_c, lr_blk, xn, wa, wb, wc, wo = sgu(h, vec, w_t, sg_w, sg_b_col, l,
                                              (w_br_a, w_br_b, w_br_c, w_o), tm=SGU_TM)
        p, w1 = proj(xn, w_t, l, (w_ff1,), tm=PROJ_TM)
        y_a, y_b, w2 = mixer_ab(p, lr_blk, vec, bias, lr_w, l, (w_ff2,), bsz=bsz, seq=seq)
        h, hn = merge(y_a, y_b, y_c, p, h, wa, wb, wc, wo, vec, l, tm=MERGE_TM)
        h = mlp(h, hn, w1, w2, tm=MLP_TM, tf=MLP_TF)
    return h.reshape(bsz, seq, d)
```

```python
import functools
import math

import jax
import jax.numpy as jnp
from jax import lax
from jax.experimental import pallas as pl
from jax.experimental.pallas import tpu as pltpu

D_MODEL = 2048
DEPTH = 2
HEAD_DIM = 128
A_HEADS = 4
A_WIDTH = A_HEADS * HEAD_DIM
A_BLOCK = 256
A_TOPK = 3
RPE_BUCKETS = 32
RPE_MAX_DIST = 128
B_HEADS = 4
B_DK = 128
B_DV = 256
B_KWIDTH = B_HEADS * B_DK
B_VWIDTH = B_HEADS * B_DV
B_LOWRANK = 16
B_GATE_NORM = 16.0
B_CHUNK = 64
C_GROUPS = 4
C_GROUP_DIM = 128
C_WIDTH = C_GROUPS * C_GROUP_DIM
C_CHUNK = 128
D_FF = 4 * D_MODEL
EPS = 1e-6
NEG_INF = -1e30
LOG2E = 1.4426950408889634

LANES = 128
BF16_ROWS = 16
VMEM_LIMIT = 56 * 1024 * 1024

F32 = jnp.float32
BF16 = jnp.bfloat16

PROJ_TM = 1024
MLP_TM = 1024
MLP_TF = 1024
MERGE_TM = 512
SGU_TM = 1024

PROJ_TN = 1536
AB_WIDTH = 3 * A_WIDTH + 2 * B_KWIDTH + 2 * B_VWIDTH
G_WIDTH = 3 * D_MODEL
IN_WIDTH = AB_WIDTH + B_LOWRANK + 2 * C_WIDTH + G_WIDTH
AB0 = G_WIDTH
PROJ_WIDTH = G_WIDTH + AB_WIDTH
N_AB_TILES = AB_WIDTH // PROJ_TN
N_G_TILES = G_WIDTH // PROJ_TN
G_SRC = IN_WIDTH - G_WIDTH
C_PAD = LANES - B_LOWRANK
C_SRC = AB_WIDTH - C_PAD
C_WIN = LANES + 2 * C_WIDTH
LR_LANE = C_PAD
assert AB_WIDTH % PROJ_TN == 0 and G_WIDTH % PROJ_TN == 0
assert C_SRC % BF16_ROWS == 0 and G_SRC % BF16_ROWS == 0


def _rpe_thresholds():
    max_exact = RPE_BUCKETS // 2

    def bucket(n):
        if n < max_exact:
            return n
        v = math.log(n / max_exact) / math.log(RPE_MAX_DIST / max_exact) * (RPE_BUCKETS - max_exact)
        return min(max_exact + int(v), RPE_BUCKETS - 1)

    table = [bucket(n) for n in range(4 * RPE_MAX_DIST)]
    return [min(n for n in range(len(table)) if table[n] >= b)
            for b in range(max_exact + 1, RPE_BUCKETS)]


RPE_THRESHOLDS = _rpe_thresholds()
assert RPE_THRESHOLDS[-1] <= A_BLOCK


VEC_FIELDS = (("norm1_g", D_MODEL), ("norm2_g", D_MODEL), ("sg_ln_g", C_WIDTH), ("sg_ln_b", C_WIDTH),
              ("gla_lr_b", B_KWIDTH), ("gla_out_g", B_DV), ("q_norm_g", HEAD_DIM), ("k_norm_g", HEAD_DIM))
VEC_OFFSET = {}
VEC_WIDTH = 0
for _name, _width in VEC_FIELDS:
    assert VEC_WIDTH % _width == 0
    VEC_OFFSET[_name] = VEC_WIDTH
    VEC_WIDTH += _width


def _vec_spec(name, layer, width=None, block_of=None):
    width = width or dict(VEC_FIELDS)[name]
    first = VEC_OFFSET[name] // width
    if block_of is None:
        return pl.BlockSpec((None, 1, width), lambda *idx: (layer, 0, first))
    return pl.BlockSpec((None, 1, width), lambda *idx: (layer, 0, first + block_of(*idx)))


def _cparams(*sem):
    return pltpu.CompilerParams(dimension_semantics=sem, vmem_limit_bytes=VMEM_LIMIT)


def _rpe_bias_kernel(*refs):
    tbl_ref, lrw_ref = refs[:2]
    fields = refs[2:2 + len(VEC_FIELDS)]
    bias_ref, vec_ref, lrw_pad_ref = refs[2 + len(VEC_FIELDS):]
    h = pl.program_id(0)

    @pl.when(h == 0)
    def _():
        for (name, width), src in zip(VEC_FIELDS, fields):
            for l in range(src.shape[0]):
                vec_ref[l, :, VEC_OFFSET[name]:VEC_OFFSET[name] + width] = src[l:l + 1, :]
        lrw_pad_ref[...] = jnp.zeros(lrw_pad_ref.shape, lrw_pad_ref.dtype)
        lrw_pad_ref[:, LR_LANE:LR_LANE + B_LOWRANK, :] = lrw_ref[...]

    t = lax.broadcasted_iota(jnp.int32, (A_BLOCK, 2 * A_BLOCK), 0)
    c = lax.broadcasted_iota(jnp.int32, (A_BLOCK, 2 * A_BLOCK), 1)
    n = jnp.maximum(t - c + A_BLOCK, 0)
    max_exact = RPE_BUCKETS // 2
    large = jnp.full(n.shape, max_exact, jnp.int32)
    for thr in RPE_THRESHOLDS:
        large = large + (n >= thr).astype(jnp.int32)
    bucket = jnp.where(n < max_exact, n, large)
    bias = jnp.zeros(n.shape, F32)
    for b in range(RPE_BUCKETS):
        bias = jnp.where(bucket == b, tbl_ref[b, h], bias)
    bias = (bias - tbl_ref[RPE_BUCKETS - 1, h]) * LOG2E
    bias_ref[0] = jnp.where(c - A_BLOCK > t, NEG_INF, bias)


def rpe_bias(rpe_table, lr_w, fields):
    layers = lr_w.shape[0]
    whole = lambda a: pl.BlockSpec(a.shape, lambda h: (0,) * a.ndim)
    return pl.pallas_call(
        _rpe_bias_kernel,
        grid=(A_HEADS,),
        in_specs=[pl.BlockSpec(memory_space=pltpu.SMEM), whole(lr_w)] + [whole(f) for f in fields],
        out_specs=[pl.BlockSpec((1, A_BLOCK, 2 * A_BLOCK), lambda h: (h, 0, 0)),
                   pl.BlockSpec((layers, 1, VEC_WIDTH), lambda h: (0, 0, 0)),
                   pl.BlockSpec((layers, LANES, B_KWIDTH), lambda h: (0, 0, 0))],
        out_shape=[jax.ShapeDtypeStruct((A_HEADS, A_BLOCK, 2 * A_BLOCK), F32),
                   jax.ShapeDtypeStruct((layers, 1, VEC_WIDTH), F32),
                   jax.ShapeDtypeStruct((layers, LANES, B_KWIDTH), F32)],
        compiler_params=_cparams("arbitrary"),
        name="rpe_bias",
    )(rpe_table, lr_w, *fields)


def _cast_specs(to_cast, layer, n_steps, step_of):
    in_specs, out_specs, shapes = [], [], []
    for w in to_cast:
        _, rows, cols = w.shape
        slice_rows = next(r for r in range(BF16_ROWS, rows + 1, BF16_ROWS)
                          if rows % r == 0 and rows // r <= n_steps)
        n_slices = rows // slice_rows

        def slice_of(*idx, n_slices=n_slices):
            return jnp.minimum(step_of(*idx), n_slices - 1)

        in_specs.append(pl.BlockSpec((None, slice_rows, cols), lambda *idx, f=slice_of: (layer, f(*idx), 0)))
        out_specs.append(pl.BlockSpec((slice_rows, cols), lambda *idx, f=slice_of: (f(*idx), 0)))
        shapes.append(jax.ShapeDtypeStruct((rows, cols), BF16))
    return in_specs, out_specs, shapes


def _cast_all(srcs, dsts):
    for src, dst in zip(srcs, dsts):
        dst[...] = src[...].astype(dst.dtype)


def _proj_kernel(*refs, n_cast):
    xn_ref, w_ref = refs[:2]
    o_ref = refs[2 + n_cast]
    o_ref[...] = lax.dot_general(xn_ref[...], w_ref[...].astype(BF16), (((1,), (1,)), ((), ())),
                                 preferred_element_type=F32).astype(o_ref.dtype)
    _cast_all(refs[2:2 + n_cast], refs[3 + n_cast:])


def _proj_src_row(j):
    u = BF16_ROWS
    return u * jnp.where(j < N_AB_TILES, j * (PROJ_TN // u),
                         G_SRC // u + (j - N_AB_TILES) * (PROJ_TN // u))


def _proj_out_tile(j):
    return jnp.where(j < N_AB_TILES, j + N_G_TILES, j - N_AB_TILES)


def proj(xn, w_t, layer, to_cast, *, tm):
    m, k = xn.shape
    tn = PROJ_TN
    n_tiles = N_AB_TILES + N_G_TILES
    cast_in_specs, cast_out_specs, cast_shapes = _cast_specs(
        to_cast, layer, (m // tm) * n_tiles, lambda i, j: i * n_tiles + j)
    return pl.pallas_call(
        functools.partial(_proj_kernel, n_cast=len(to_cast)),
        grid=(m // tm, n_tiles),
        in_specs=[pl.BlockSpec((tm, k), lambda i, j: (i, 0)),
                  pl.BlockSpec((None, pl.Element(tn), pl.Element(k)),
                               lambda i, j: (layer, _proj_src_row(j), 0))] + cast_in_specs,
        out_specs=[pl.BlockSpec((tm, tn), lambda i, j: (i, _proj_out_tile(j)))] + cast_out_specs,
        out_shape=[jax.ShapeDtypeStruct((m, PROJ_WIDTH), BF16)] + cast_shapes,
        compiler_params=_cparams("parallel", "arbitrary"),
        name="proj",
    )(xn, w_t, *to_cast)


def _head_rms(x, g):
    ms = jnp.mean(x * x, axis=-1, keepdims=True)
    return x * lax.rsqrt(ms + EPS) * g


def _moba_stages(q_ref, k_ref, v_ref, qg_ref, kg_ref, bias_ref, o_ref,
                 qn_ref, qa_ref, ka_ref, va_ref, kmean_ref, s_ref, p_ref):
    seq, hd = q_ref.shape
    nb = seq // A_BLOCK
    rc = BF16_ROWS

    def off(i):
        return A_BLOCK * (i * (i + 1) // 2)

    def prologue():
        lane = lax.broadcasted_iota(jnp.int32, (A_BLOCK, hd), 1)
        for j in range(nb):
            rows = slice(j * A_BLOCK, (j + 1) * A_BLOCK)
            qn = _head_rms(q_ref[rows, :].astype(F32), qg_ref[...])
            qn_ref[rows, :] = qn
            qa_ref[rows, :hd] = (qn * (hd ** -0.5 * LOG2E)).astype(BF16)
            qa_ref[rows, hd:] = jnp.zeros((A_BLOCK, hd), BF16)
            kn = _head_rms(k_ref[rows, :].astype(F32), kg_ref[...])
            ka_ref[rows, :hd] = kn.astype(BF16)
            ka_ref[rows, hd:] = jnp.where(lane == j, 1.0, 0.0).astype(BF16)
            va_ref[rows, :hd] = v_ref[rows, :]
            va_ref[rows, hd:] = jnp.ones((A_BLOCK, hd), BF16)
            kmean_ref[j:j + 1, :] = jnp.mean(kn, axis=0, keepdims=True)

    def scores(i):
        rows = slice(i * A_BLOCK, (i + 1) * A_BLOCK)
        nk = (i + 1) * A_BLOCK
        if i > A_TOPK:
            blk = lax.broadcasted_iota(jnp.int32, (nb, A_BLOCK), 0)
            eye = jnp.where(lax.broadcasted_iota(jnp.int32, (nb, hd), 0)
                            == lax.broadcasted_iota(jnp.int32, (nb, hd), 1), 1.0, 0.0)
            g = lax.dot_general(kmean_ref[...], qn_ref[rows, :], (((1,), (1,)), ((), ())),
                                precision=lax.Precision.HIGHEST, preferred_element_type=F32)
            rank = jnp.zeros((nb, A_BLOCK), jnp.int32)
            for jp in range(i):
                gj = g[jp:jp + 1, :]
                beats = (gj > g) | ((gj == g) & (jp < blk))
                rank = rank + beats.astype(jnp.int32)
            neg = jnp.where((blk < i) & (rank >= A_TOPK), NEG_INF, 0.0)
            qa_ref[rows, hd:] = lax.dot_general(neg, eye, (((0,), (0,)), ((), ())),
                                                preferred_element_type=F32).astype(BF16)
        s_ref[:, off(i):off(i) + nk] = lax.dot_general(qa_ref[rows, :], ka_ref[:nk, :], (((1,), (1,)), ((), ())),
                                               preferred_element_type=F32)

    def softmax(i):
        for c in range(A_BLOCK // rc):
            r = slice(c * rc, (c + 1) * rc)
            tiles = []
            for j in range(i + 1):
                sj = s_ref[r, off(i) + j * A_BLOCK:off(i) + (j + 1) * A_BLOCK]
                if j == i:
                    sj = sj + bias_ref[0, r, A_BLOCK:]
                elif j == i - 1:
                    sj = sj + bias_ref[0, r, :A_BLOCK]
                tiles.append(sj)
            mt = tiles[0]
            for t in tiles[1:]:
                mt = jnp.maximum(mt, t)
            m = jnp.max(mt, axis=-1, keepdims=True)
            for j, t in enumerate(tiles):
                p_ref[r, off(i) + j * A_BLOCK:off(i) + (j + 1) * A_BLOCK] = jnp.exp2(t - m).astype(BF16)

    def weighted_sum(i):
        rows = slice(i * A_BLOCK, (i + 1) * A_BLOCK)
        nk = (i + 1) * A_BLOCK
        o = jnp.dot(p_ref[:, off(i):off(i) + nk], va_ref[:nk, :], preferred_element_type=F32)
        o_ref[rows, :] = (o[:, :hd] / o[:, hd:]).astype(o_ref.dtype)

    return ([prologue] + [functools.partial(scores, i) for i in range(nb)]
            + [functools.partial(softmax, i) for i in range(nb)]
            + [functools.partial(weighted_sum, i) for i in range(nb)])


def _gla_stages(q_ref, k_ref, v_ref, r_ref, lr_ref, lrw_ref, lrb_ref, og_ref, o_ref,
                hml_ref, bc_ref, qd_ref, ki_ref, ke_ref, dec_ref, oi_ref, ss_ref, sb_ref):
    seq = q_ref.shape[0]
    ck = B_CHUNK
    nc = seq // ck
    grp = 4 * ck
    ng = seq // grp
    cpg = grp // ck
    qscale = B_DK ** -0.5
    nt = (((1,), (1,)), ((), ()))
    tn = (((0,), (0,)), ((), ()))

    def same_chunk_causal():
        ti = lax.broadcasted_iota(jnp.int32, (grp, grp), 0)
        si = lax.broadcasted_iota(jnp.int32, (grp, grp), 1)
        return (si <= ti) & (si // ck == ti // ck)

    def log_decay(g):
        rows = slice(g * grp, (g + 1) * grp)
        z = jnp.dot(lr_ref[rows, :], lrw_ref[...].astype(BF16), preferred_element_type=F32) + lrb_ref[...]
        la = (jnp.minimum(z, 0.0) - jnp.log(1.0 + jnp.exp(-jnp.abs(z)))) * (1.0 / B_GATE_NORM)
        hi = la.astype(BF16)
        r1 = la - hi.astype(F32)
        mid = r1.astype(BF16)
        hml_ref[rows, :B_DK] = hi
        hml_ref[rows, B_DK:2 * B_DK] = mid
        hml_ref[rows, 2 * B_DK:] = (r1 - mid.astype(F32)).astype(BF16)

    def cumsum(g):
        rows = slice(g * grp, (g + 1) * grp)
        tril_grp = jnp.where(same_chunk_causal(), 1.0, 0.0).astype(BF16)
        parts = jnp.dot(tril_grp, hml_ref[rows, :], preferred_element_type=F32)
        bc_ref[rows, :] = parts[:, :B_DK] + parts[:, B_DK:2 * B_DK] + parts[:, 2 * B_DK:]

    def decayed_qk(g):
        for c in range(g * cpg, (g + 1) * cpg):
            rows = slice(c * ck, (c + 1) * ck)
            bc = bc_ref[rows, :]
            bl = bc_ref[(c + 1) * ck - 1:(c + 1) * ck, :]
            q = q_ref[rows, :].astype(F32)
            k = k_ref[rows, :].astype(F32)
            qd_ref[rows, :] = (q * jnp.exp(bc) * qscale).astype(BF16)
            ki_ref[rows, :] = (k * jnp.exp(-bc)).astype(BF16)
            ke_ref[rows, :] = (k * jnp.exp(bl - bc)).astype(BF16)
            dec_ref[c:c + 1, :] = jnp.exp(bl)

    def intra(g):
        rows = slice(g * grp, (g + 1) * grp)
        a = lax.dot_general(qd_ref[rows, :], ki_ref[rows, :], nt, preferred_element_type=F32)
        a = jnp.where(same_chunk_causal(), a, 0.0).astype(BF16)
        oi_ref[rows, :] = jnp.dot(a, v_ref[rows, :], preferred_element_type=F32)

    def state_terms(g):
        for c in range(g * cpg, (g + 1) * cpg):
            rows = slice(c * ck, (c + 1) * ck)
            ss_ref[c] = lax.dot_general(v_ref[rows, :], ke_ref[rows, :], tn, preferred_element_type=F32)

    def scan():
        slab = B_DV // 4
        st = [jnp.zeros((slab, B_DK), F32) for _ in range(4)]
        for c in range(nc):
            dec = dec_ref[c:c + 1, :]
            for s in range(4):
                r = slice(s * slab, (s + 1) * slab)
                sb_ref[c, r, :] = st[s].astype(BF16)
                st[s] = st[s] * dec + ss_ref[c, r, :]

    def outputs(g):
        for c in range(g * cpg, (g + 1) * cpg):
            rows = slice(c * ck, (c + 1) * ck)
            o = oi_ref[rows, :] + lax.dot_general(qd_ref[rows, :], sb_ref[c], nt,
                                                  preferred_element_type=F32)
            ms = jnp.mean(o * o, axis=-1, keepdims=True)
            y = o * lax.rsqrt(ms + EPS) * og_ref[...]
            r = r_ref[rows, :]
            half_r = 0.5 * r
            o_ref[rows, :] = y.astype(o_ref.dtype) * (half_r + half_r * jnp.tanh(half_r))

    groups = [[functools.partial(fn, g) for g in range(ng)]
              for fn in (log_decay, cumsum, decayed_qk, intra, state_terms)]
    return groups + [[scan], [functools.partial(outputs, g) for g in range(ng)]]


N_MOBA_IN, N_MOBA_SCRATCH = 6, 7
N_GLA_IN = 8


def _mixer_ab_kernel(*refs, n_cast):
    n_in = N_MOBA_IN + N_GLA_IN
    ins, cast_in = refs[:n_in], refs[n_in:n_in + n_cast]
    oa_ref, ob_ref = refs[n_in + n_cast:n_in + n_cast + 2]
    cast_out = refs[n_in + n_cast + 2:n_in + 2 * n_cast + 2]
    scratch = refs[n_in + 2 * n_cast + 2:]
    moba_stages = _moba_stages(*ins[:N_MOBA_IN], oa_ref, *scratch[:N_MOBA_SCRATCH])
    gla_steps = [fn for group in _gla_stages(*ins[N_MOBA_IN:], ob_ref, *scratch[N_MOBA_SCRATCH:])
                 for fn in group]
    per_slot = -(-len(gla_steps) // len(moba_stages))
    for n, stage in enumerate(moba_stages):
        stage()
        for fn in gla_steps[n * per_slot:(n + 1) * per_slot]:
            fn()
    _cast_all(cast_in, cast_out)


def mixer_ab(proj_out, lr_blk, vec, bias, lr_w, layer, to_cast, *, bsz, seq):
    hd = HEAD_DIM
    assert A_HEADS == B_HEADS
    qa0 = AB0 // hd
    qb0 = (AB0 + 3 * A_WIDTH) // B_DK
    kb0 = qb0 + B_HEADS
    vb0 = (AB0 + 3 * A_WIDTH + 2 * B_KWIDTH) // B_DV
    rb0 = vb0 + B_HEADS
    nc = seq // B_CHUNK
    nb = seq // A_BLOCK
    tri = nb * (nb + 1) // 2
    cast_in_specs, cast_out_specs, cast_shapes = _cast_specs(
        to_cast, layer, bsz * A_HEADS, lambda b, h: b * A_HEADS + h)
    return pl.pallas_call(
        functools.partial(_mixer_ab_kernel, n_cast=len(to_cast)),
        grid=(bsz, A_HEADS),
        in_specs=[pl.BlockSpec((seq, hd), lambda b, h: (b, qa0 + h)),
                  pl.BlockSpec((seq, hd), lambda b, h: (b, qa0 + A_HEADS + h)),
                  pl.BlockSpec((seq, hd), lambda b, h: (b, qa0 + 2 * A_HEADS + h)),
                  _vec_spec("q_norm_g", layer),
                  _vec_spec("k_norm_g", layer),
                  pl.BlockSpec((1, A_BLOCK, 2 * A_BLOCK), lambda b, h: (h, 0, 0)),
                  pl.BlockSpec((seq, B_DK), lambda b, h: (b, qb0 + h)),
                  pl.BlockSpec((seq, B_DK), lambda b, h: (b, kb0 + h)),
                  pl.BlockSpec((seq, B_DV), lambda b, h: (b, vb0 + h)),
                  pl.BlockSpec((seq, B_DV), lambda b, h: (b, rb0 + h)),
                  pl.BlockSpec((seq, LANES), lambda b, h: (b, 0)),
                  pl.BlockSpec((None, LANES, B_DK), lambda b, h: (layer, 0, h)),
                  _vec_spec("gla_lr_b", layer, B_DK, lambda b, h: h),
                  _vec_spec("gla_out_g", layer)] + cast_in_specs,
        out_specs=[pl.BlockSpec((seq, hd), lambda b, h: (b, h)),
                   pl.BlockSpec((seq, B_DV), lambda b, h: (b, h))] + cast_out_specs,
        out_shape=[jax.ShapeDtypeStruct((bsz * seq, A_WIDTH), BF16),
                   jax.ShapeDtypeStruct((bsz * seq, B_VWIDTH), BF16)] + cast_shapes,
        scratch_shapes=[pltpu.VMEM((seq, hd), F32),
                        pltpu.VMEM((seq, 2 * hd), BF16),
                        pltpu.VMEM((seq, 2 * hd), BF16),
                        pltpu.VMEM((seq, 2 * hd), BF16),
                        pltpu.VMEM((seq // A_BLOCK, hd), F32),
                        pltpu.VMEM((A_BLOCK, tri * A_BLOCK), F32),
                        pltpu.VMEM((A_BLOCK, tri * A_BLOCK), BF16),
                        pltpu.VMEM((seq, 3 * B_DK), BF16),
                        pltpu.VMEM((seq, B_DK), F32),
                        pltpu.VMEM((seq, B_DK), BF16),
                        pltpu.VMEM((seq, B_DK), BF16),
                        pltpu.VMEM((seq, B_DK), BF16),
                        pltpu.VMEM((nc, B_DK), F32),
                        pltpu.VMEM((seq, B_DV), F32),
                        pltpu.VMEM((nc, B_DV, B_DK), F32),
                        pltpu.VMEM((nc, B_DV, B_DK), BF16)],
        compiler_params=_cparams("parallel", "parallel"),
        name="mixer_ab",
    )(proj_out, proj_out, proj_out, vec, vec, bias,
      proj_out, proj_out, proj_out, proj_out, lr_blk, lr_w, vec, vec, *to_cast)


def _gelu(x):
    c = math.sqrt(2.0 / math.pi)
    half_x = 0.5 * x
    return half_x + half_x * jnp.tanh(x * (c + (c * 0.044715) * (x * x)))


SGU_SUB = 2 * C_CHUNK


def _sgu_kernel(*refs, n_cast):
    x_ref, g1_ref, wsrc_ref, lg_ref, lb_ref, w_ref, b_ref = refs[:7]
    o_ref, lr_ref, xn_ref = refs[7 + n_cast:10 + n_cast]
    wc_ref, luv_ref = refs[-2:]

    @pl.when(pl.program_id(0) == 0)
    def _():
        wc_ref[...] = wsrc_ref[...].astype(BF16)

    _cast_all(refs[7:7 + n_cast], refs[10 + n_cast:10 + 2 * n_cast])
    ti = lax.broadcasted_iota(jnp.int32, (C_CHUNK, C_CHUNK), 0)
    si = lax.broadcasted_iota(jnp.int32, (C_CHUNK, C_CHUNK), 1)
    causal = si <= ti
    ws = [jnp.where(causal, w_ref[g], 0.0).astype(BF16) for g in range(C_GROUPS)]
    n_sub = x_ref.shape[0] // SGU_SUB

    def project(s):
        rs = slice(s * SGU_SUB, (s + 1) * SGU_SUB)
        x = x_ref[rs, :]
        ms = jnp.mean(x * x, axis=-1, keepdims=True)
        xn = (x * lax.rsqrt(ms + EPS) * g1_ref[...]).astype(BF16)
        xn_ref[rs, :] = xn
        luv = lax.dot_general(xn, wc_ref[...], (((1,), (1,)), ((), ())), preferred_element_type=F32)
        luv_ref[s % 2] = luv
        lr_ref[rs, :] = luv[:, :LANES].astype(lr_ref.dtype)

    def gate(s):
        for n in range(SGU_SUB // C_CHUNK):
            rl = slice(n * C_CHUNK, (n + 1) * C_CHUNK)
            ro = slice(s * SGU_SUB + n * C_CHUNK, s * SGU_SUB + (n + 1) * C_CHUNK)
            v = _gelu(luv_ref[s % 2, rl, LANES + C_WIDTH:])
            mu = jnp.mean(v, axis=-1, keepdims=True)
            vc = v - mu
            vn = vc * lax.rsqrt(jnp.mean(vc * vc, axis=-1, keepdims=True) + EPS)
            vn = (vn * lg_ref[...] + lb_ref[...]).astype(BF16)
            u = _gelu(luv_ref[s % 2, rl, LANES:LANES + C_WIDTH])
            for g in range(C_GROUPS):
                cols = slice(g * C_GROUP_DIM, (g + 1) * C_GROUP_DIM)
                mixed = jnp.dot(ws[g], vn[:, cols], preferred_element_type=F32) + b_ref[g]
                o_ref[ro, cols] = (u[:, cols] * mixed).astype(o_ref.dtype)

    project(0)
    for s in range(n_sub):
        if s + 1 < n_sub:
            project(s + 1)
        gate(s)


def sgu(x, vec, w_t, w_s, b_s, layer, to_cast, *, tm):
    m, d = x.shape
    cast_in_specs, cast_out_specs, cast_shapes = _cast_specs(to_cast, layer, m // tm, lambda i: i)
    return pl.pallas_call(
        functools.partial(_sgu_kernel, n_cast=len(to_cast)),
        grid=(m // tm,),
        in_specs=[pl.BlockSpec((tm, d), lambda i: (i, 0)),
                  _vec_spec("norm1_g", layer),
                  pl.BlockSpec((None, pl.Element(C_WIN), pl.Element(d)), lambda i: (layer, C_SRC, 0),
                               pipeline_mode=pl.Buffered(1)),
                  _vec_spec("sg_ln_g", layer),
                  _vec_spec("sg_ln_b", layer),
                  pl.BlockSpec((None, C_GROUPS, C_CHUNK, C_CHUNK), lambda i: (layer, 0, 0, 0)),
                  pl.BlockSpec((None, C_GROUPS, C_CHUNK, 1), lambda i: (layer, 0, 0, 0))] + cast_in_specs,
        out_specs=[pl.BlockSpec((tm, C_WIDTH), lambda i: (i, 0)),
                   pl.BlockSpec((tm, LANES), lambda i: (i, 0)),
                   pl.BlockSpec((tm, d), lambda i: (i, 0))] + cast_out_specs,
        out_shape=[jax.ShapeDtypeStruct((m, C_WIDTH), BF16),
                   jax.ShapeDtypeStruct((m, LANES), BF16),
                   jax.ShapeDtypeStruct((m, d), BF16)] + cast_shapes,
        scratch_shapes=[pltpu.VMEM((C_WIN, d), BF16),
                        pltpu.VMEM((2, SGU_SUB, C_WIN), F32)],
        compiler_params=_cparams("arbitrary"),
        name="sgu",
    )(x, vec, w_t, vec, vec, w_s, b_s, *to_cast)


def _merge_kernel(ya_ref, yb_ref, yc_ref, ga_ref, gb_ref, gc_ref, x_ref,
                  wa_ref, wb_ref, wc_ref, wo_ref, g2_ref, o_ref, on_ref):
    sub = 256
    for s in range(x_ref.shape[0] // sub):
        r = slice(s * sub, (s + 1) * sub)

        def gated(g_ref, y_ref, w_ref):
            gate = 0.5 * jnp.tanh(0.5 * g_ref[r, :].astype(F32)) + 0.5
            return gate * jnp.dot(y_ref[r, :], w_ref[...], preferred_element_type=F32)

        merged = gated(ga_ref, ya_ref, wa_ref) + gated(gb_ref, yb_ref, wb_ref) + gated(gc_ref, yc_ref, wc_ref)
        h = x_ref[r, :] + jnp.dot(merged.astype(BF16), wo_ref[...], preferred_element_type=F32)
        o_ref[r, :] = h
        ms = jnp.mean(h * h, axis=-1, keepdims=True)
        on_ref[r, :] = (h * lax.rsqrt(ms + EPS) * g2_ref[...]).astype(on_ref.dtype)


def merge(ya, yb, yc, proj_out, x, wa, wb, wc, wo, vec, layer, *, tm):
    m, d = x.shape
    resident = functools.partial(pl.BlockSpec, pipeline_mode=pl.Buffered(1))
    return pl.pallas_call(
        _merge_kernel,
        grid=(m // tm,),
        in_specs=[pl.BlockSpec((tm, A_WIDTH), lambda i: (i, 0)),
                  pl.BlockSpec((tm, B_VWIDTH), lambda i: (i, 0)),
                  pl.BlockSpec((tm, C_WIDTH), lambda i: (i, 0)),
                  pl.BlockSpec((tm, d), lambda i: (i, 0)),
                  pl.BlockSpec((tm, d), lambda i: (i, 1)),
                  pl.BlockSpec((tm, d), lambda i: (i, 2)),
                  pl.BlockSpec((tm, d), lambda i: (i, 0)),
                  resident((A_WIDTH, d), lambda i: (0, 0)),
                  resident((B_VWIDTH, d), lambda i: (0, 0)),
                  resident((C_WIDTH, d), lambda i: (0, 0)),
                  resident((d, d), lambda i: (0, 0)),
                  _vec_spec("norm2_g", layer)],
        out_specs=[pl.BlockSpec((tm, d), lambda i: (i, 0)),
                   pl.BlockSpec((tm, d), lambda i: (i, 0))],
        out_shape=[jax.ShapeDtypeStruct((m, d), F32),
                   jax.ShapeDtypeStruct((m, d), BF16)],
        compiler_params=_cparams("parallel"),
        name="merge",
    )(ya, yb, yc, proj_out, proj_out, proj_out, x, wa, wb, wc, wo, vec)


def _mlp_kernel(h_hbm, hn_ref, w1_ref, w2_ref, o_ref, h_ref, h_sem):
    i, j = pl.program_id(0), pl.program_id(1)
    last = pl.num_programs(1) - 1
    tm = h_ref.shape[0]
    h_copy = pltpu.make_async_copy(h_hbm.at[pl.ds(i * tm, tm), :], h_ref, h_sem)

    def update():
        a = jnp.dot(hn_ref[...], w1_ref[...], preferred_element_type=F32)
        a = jnp.square(jnp.maximum(a, 0.0)).astype(BF16)
        return jnp.dot(a, w2_ref[...], preferred_element_type=F32)

    @pl.when(j == 0)
    def _():
        h_copy.start()
        o_ref[...] = update()

    @pl.when((j > 0) & (j < last))
    def _():
        o_ref[...] += update()

    @pl.when(j == last)
    def _():
        h_copy.wait()
        o_ref[...] = (o_ref[...] + h_ref[...]) + update()


def mlp(h, hn, w1, w2, *, tm, tf):
    m, d = h.shape
    f = w1.shape[1]
    return pl.pallas_call(
        _mlp_kernel,
        grid=(m // tm, f // tf),
        in_specs=[pl.BlockSpec(memory_space=pl.ANY),
                  pl.BlockSpec((tm, d), lambda i, j: (i, 0)),
                  pl.BlockSpec((d, tf), lambda i, j: (0, j)),
                  pl.BlockSpec((tf, d), lambda i, j: (j, 0))],
        out_specs=pl.BlockSpec((tm, d), lambda i, j: (i, 0)),
        out_shape=jax.ShapeDtypeStruct((m, d), F32),
        scratch_shapes=[pltpu.VMEM((tm, d), F32),
                        pltpu.SemaphoreType.DMA(())],
        compiler_params=_cparams("parallel", "arbitrary"),
        name="mlp",
    )(h, hn, w1, w2)


def kernel(x, rpe_table, norm1_g, w_in, q_norm_g, k_norm_g, gla_lr_w, gla_lr_b, gla_out_g,
           sg_ln_g, sg_ln_b, sg_w, sg_b, w_br_a, w_br_b, w_br_c, w_o, norm2_g, w_ff1, w_ff2):
    bsz, seq, d = x.shape
    assert seq % A_BLOCK == 0 and seq % B_CHUNK == 0 and seq % C_CHUNK == 0
    assert w_in.shape[2] == IN_WIDTH
    h = x.reshape(bsz * seq, d)
    fields = dict(norm1_g=norm1_g, norm2_g=norm2_g, sg_ln_g=sg_ln_g, sg_ln_b=sg_ln_b, gla_lr_b=gla_lr_b,
                  gla_out_g=gla_out_g, q_norm_g=q_norm_g, k_norm_g=k_norm_g)
    bias, vec, lr_w = rpe_bias(rpe_table, gla_lr_w, [fields[name] for name, _ in VEC_FIELDS])
    w_t = jnp.swapaxes(w_in, 1, 2)
    sg_b_col = sg_b[..., None]
    for l in range(DEPTH):
        y_c, lr_blk, xn, wa, wb, wc, wo = sgu(h, vec, w_t, sg_w, sg_b_col, l,
                                              (w_br_a, w_br_b, w_br_c, w_o), tm=SGU_TM)
        p, w1 = proj(xn, w_t, l, (w_ff1,), tm=PROJ_TM)
        y_a, y_b, w2 = mixer_ab(p, lr_blk, vec, bias, lr_w, l, (w_ff2,), bsz=bsz, seq=seq)
        h, hn = merge(y_a, y_b, y_c, p, h, wa, wb, wc, wo, vec, l, tm=MERGE_TM)
        h = mlp(h, hn, w1, w2, tm=MLP_TM, tf=MLP_TF)
    return h.reshape(bsz, seq, d)
```

```python
import functools
import math

import jax
import jax.numpy as jnp
from jax import lax
from jax.experimental import pallas as pl
from jax.experimental.pallas import tpu as pltpu

D_MODEL = 2048
DEPTH = 2
HEAD_DIM = 128
A_HEADS = 4
A_WIDTH = A_HEADS * HEAD_DIM
A_BLOCK = 256
A_TOPK = 3
RPE_BUCKETS = 32
RPE_MAX_DIST = 128
B_HEADS = 4
B_DK = 128
B_DV = 256
B_KWIDTH = B_HEADS * B_DK
B_VWIDTH = B_HEADS * B_DV
B_LOWRANK = 16
B_GATE_NORM = 16.0
B_CHUNK = 64
C_GROUPS = 4
C_GROUP_DIM = 128
C_WIDTH = C_GROUPS * C_GROUP_DIM
C_CHUNK = 128
D_FF = 4 * D_MODEL
EPS = 1e-6
NEG_INF = -1e30
LOG2E = 1.4426950408889634

LANES = 128
BF16_ROWS = 16
VMEM_LIMIT = 56 * 1024 * 1024

F32 = jnp.float32
BF16 = jnp.bfloat16

PROJ_TM = 1024
MLP_TM = 1024
MLP_TF = 1024
MERGE_TM = 512
SGU_TM = 1024

PROJ_TN = 1536
AB_WIDTH = 3 * A_WIDTH + 2 * B_KWIDTH + 2 * B_VWIDTH
G_WIDTH = 3 * D_MODEL
IN_WIDTH = AB_WIDTH + B_LOWRANK + 2 * C_WIDTH + G_WIDTH
AB0 = G_WIDTH
PROJ_WIDTH = G_WIDTH + AB_WIDTH
N_AB_TILES = AB_WIDTH // PROJ_TN
N_G_TILES = G_WIDTH // PROJ_TN
G_SRC = IN_WIDTH - G_WIDTH
C_PAD = LANES - B_LOWRANK
C_SRC = AB_WIDTH - C_PAD
C_WIN = LANES + 2 * C_WIDTH
LR_LANE = C_PAD
assert AB_WIDTH % PROJ_TN == 0 and G_WIDTH % PROJ_TN == 0
assert C_SRC % BF16_ROWS == 0 and G_SRC % BF16_ROWS == 0


def _rpe_thresholds():
    max_exact = RPE_BUCKETS // 2

    def bucket(n):
        if n < max_exact:
            return n
        v = math.log(n / max_exact) / math.log(RPE_MAX_DIST / max_exact) * (RPE_BUCKETS - max_exact)
        return min(max_exact + int(v), RPE_BUCKETS - 1)

    table = [bucket(n) for n in range(4 * RPE_MAX_DIST)]
    return [min(n for n in range(len(table)) if table[n] >= b)
            for b in range(max_exact + 1, RPE_BUCKETS)]


RPE_THRESHOLDS = _rpe_thresholds()
assert RPE_THRESHOLDS[-1] <= A_BLOCK


VEC_FIELDS = (("norm1_g", D_MODEL), ("norm2_g", D_MODEL), ("sg_ln_g", C_WIDTH), ("sg_ln_b", C_WIDTH),
              ("gla_lr_b", B_KWIDTH), ("gla_out_g", B_DV), ("q_norm_g", HEAD_DIM), ("k_norm_g", HEAD_DIM))
VEC_OFFSET = {}
VEC_WIDTH = 0
for _name, _width in VEC_FIELDS:
    assert VEC_WIDTH % _width == 0
    VEC_OFFSET[_name] = VEC_WIDTH
    VEC_WIDTH += _width


def _vec_spec(name, layer, width=None, block_of=None):
    width = width or dict(VEC_FIELDS)[name]
    first = VEC_OFFSET[name] // width
    if block_of is None:
        return pl.BlockSpec((None, 1, width), lambda *idx: (layer, 0, first))
    return pl.BlockSpec((None, 1, width), lambda *idx: (layer, 0, first + block_of(*idx)))


def _cparams(*sem):
    return pltpu.CompilerParams(dimension_semantics=sem, vmem_limit_bytes=VMEM_LIMIT)


def _rpe_bias_kernel(*refs):
    tbl_ref, lrw_ref = refs[:2]
    fields = refs[2:2 + len(VEC_FIELDS)]
    bias_ref, vec_ref, lrw_pad_ref = refs[2 + len(VEC_FIELDS):]
    h = pl.program_id(0)

    @pl.when(h == 0)
    def _():
        for (name, width), src in zip(VEC_FIELDS, fields):
            for l in range(src.shape[0]):
                vec_ref[l, :, VEC_OFFSET[name]:VEC_OFFSET[name] + width] = src[l:l + 1, :]
        lrw_pad_ref[...] = jnp.zeros(lrw_pad_ref.shape, lrw_pad_ref.dtype)
        lrw_pad_ref[:, LR_LANE:LR_LANE + B_LOWRANK, :] = lrw_ref[...]

    t = lax.broadcasted_iota(jnp.int32, (A_BLOCK, 2 * A_BLOCK), 0)
    c = lax.broadcasted_iota(jnp.int32, (A_BLOCK, 2 * A_BLOCK), 1)
    n = jnp.maximum(t - c + A_BLOCK, 0)
    max_exact = RPE_BUCKETS // 2
    large = jnp.full(n.shape, max_exact, jnp.int32)
    for thr in RPE_THRESHOLDS:
        large = large + (n >= thr).astype(jnp.int32)
    bucket = jnp.where(n < max_exact, n, large)
    bias = jnp.zeros(n.shape, F32)
    for b in range(RPE_BUCKETS):
        bias = jnp.where(bucket == b, tbl_ref[b, h], bias)
    bias = (bias - tbl_ref[RPE_BUCKETS - 1, h]) * LOG2E
    bias_ref[0] = jnp.where(c - A_BLOCK > t, NEG_INF, bias)


def rpe_bias(rpe_table, lr_w, fields):
    layers = lr_w.shape[0]
    whole = lambda a: pl.BlockSpec(a.shape, lambda h: (0,) * a.ndim)
    return pl.pallas_call(
        _rpe_bias_kernel,
        grid=(A_HEADS,),
        in_specs=[pl.BlockSpec(memory_space=pltpu.SMEM), whole(lr_w)] + [whole(f) for f in fields],
        out_specs=[pl.BlockSpec((1, A_BLOCK, 2 * A_BLOCK), lambda h: (h, 0, 0)),
                   pl.BlockSpec((layers, 1, VEC_WIDTH), lambda h: (0, 0, 0)),
                   pl.BlockSpec((layers, LANES, B_KWIDTH), lambda h: (0, 0, 0))],
        out_shape=[jax.ShapeDtypeStruct((A_HEADS, A_BLOCK, 2 * A_BLOCK), F32),
                   jax.ShapeDtypeStruct((layers, 1, VEC_WIDTH), F32),
                   jax.ShapeDtypeStruct((layers, LANES, B_KWIDTH), F32)],
        compiler_params=_cparams("arbitrary"),
        name="rpe_bias",
    )(rpe_table, lr_w, *fields)


def _cast_specs(to_cast, layer, n_steps, step_of):
    in_specs, out_specs, shapes = [], [], []
    for w in to_cast:
        _, rows, cols = w.shape
        slice_rows = next(r for r in range(BF16_ROWS, rows + 1, BF16_ROWS)
                          if rows % r == 0 and rows // r <= n_steps)
        n_slices = rows // slice_rows

        def slice_of(*idx, n_slices=n_slices):
            return jnp.minimum(step_of(*idx), n_slices - 1)

        in_specs.append(pl.BlockSpec((None, slice_rows, cols), lambda *idx, f=slice_of: (layer, f(*idx), 0)))
        out_specs.append(pl.BlockSpec((slice_rows, cols), lambda *idx, f=slice_of: (f(*idx), 0)))
        shapes.append(jax.ShapeDtypeStruct((rows, cols), BF16))
    return in_specs, out_specs, shapes


def _cast_all(srcs, dsts):
    for src, dst in zip(srcs, dsts):
        dst[...] = src[...].astype(dst.dtype)


def _proj_kernel(*refs, n_cast):
    xn_ref, w_ref = refs[:2]
    o_ref = refs[2 + n_cast]
    o_ref[...] = lax.dot_general(xn_ref[...], w_ref[...].astype(BF16), (((1,), (1,)), ((), ())),
                                 preferred_element_type=F32).astype(o_ref.dtype)
    _cast_all(refs[2:2 + n_cast], refs[3 + n_cast:])


def _proj_src_row(j):
    u = BF16_ROWS
    return u * jnp.where(j < N_AB_TILES, j * (PROJ_TN // u),
                         G_SRC // u + (j - N_AB_TILES) * (PROJ_TN // u))


def _proj_out_tile(j):
    return jnp.where(j < N_AB_TILES, j + N_G_TILES, j - N_AB_TILES)


def proj(xn, w_t, layer, to_cast, *, tm):
    m, k = xn.shape
    tn = PROJ_TN
    n_tiles = N_AB_TILES + N_G_TILES
    cast_in_specs, cast_out_specs, cast_shapes = _cast_specs(
        to_cast, layer, (m // tm) * n_tiles, lambda i, j: i * n_tiles + j)
    return pl.pallas_call(
        functools.partial(_proj_kernel, n_cast=len(to_cast)),
        grid=(m // tm, n_tiles),
        in_specs=[pl.BlockSpec((tm, k), lambda i, j: (i, 0)),
                  pl.BlockSpec((None, pl.Element(tn), pl.Element(k)),
                               lambda i, j: (layer, _proj_src_row(j), 0))] + cast_in_specs,
        out_specs=[pl.BlockSpec((tm, tn), lambda i, j: (i, _proj_out_tile(j)))] + cast_out_specs,
        out_shape=[jax.ShapeDtypeStruct((m, PROJ_WIDTH), BF16)] + cast_shapes,
        compiler_params=_cparams("parallel", "arbitrary"),
        name="proj",
    )(xn, w_t, *to_cast)


def _head_rms(x, g):
    ms = jnp.mean(x * x, axis=-1, keepdims=True)
    return x * lax.rsqrt(ms + EPS) * g


def _moba_stages(q_ref, k_ref, v_ref, qg_ref, kg_ref, bias_ref, o_ref,
                 qn_ref, qa_ref, ka_ref, va_ref, kmean_ref, s_ref, p_ref):
    seq, hd = q_ref.shape
    nb = seq // A_BLOCK
    rc = BF16_ROWS

    def off(i):
        return A_BLOCK * (i * (i + 1) // 2)

    def prologue():
        lane = lax.broadcasted_iota(jnp.int32, (A_BLOCK, hd), 1)
        for j in range(nb):
            rows = slice(j * A_BLOCK, (j + 1) * A_BLOCK)
            qn = _head_rms(q_ref[rows, :].astype(F32), qg_ref[...])
            qn_ref[rows, :] = qn
            qa_ref[rows, :hd] = (qn * (hd ** -0.5 * LOG2E)).astype(BF16)
            qa_ref[rows, hd:] = jnp.zeros((A_BLOCK, hd), BF16)
            kn = _head_rms(k_ref[rows, :].astype(F32), kg_ref[...])
            ka_ref[rows, :hd] = kn.astype(BF16)
            ka_ref[rows, hd:] = jnp.where(lane == j, 1.0, 0.0).astype(BF16)
            va_ref[rows, :hd] = v_ref[rows, :]
            va_ref[rows, hd:] = jnp.ones((A_BLOCK, hd), BF16)
            kmean_ref[j:j + 1, :] = jnp.mean(kn, axis=0, keepdims=True)

    def scores(i):
        rows = slice(i * A_BLOCK, (i + 1) * A_BLOCK)
        nk = (i + 1) * A_BLOCK
        if i > A_TOPK:
            blk = lax.broadcasted_iota(jnp.int32, (nb, A_BLOCK), 0)
            g = lax.dot_general(kmean_ref[...], qn_ref[rows, :], (((1,), (1,)), ((), ())),
                                precision=lax.Precision.HIGHEST, preferred_element_type=F32)
            rank = jnp.zeros((nb, A_BLOCK), jnp.int32)
            for jp in range(i):
                gj = g[jp:jp + 1, :]
                beats = (gj > g) | ((gj == g) & (jp < blk))
                rank = rank + beats.astype(jnp.int32)
            neg = jnp.where((blk < i) & (rank >= A_TOPK), NEG_INF, 0.0)
            neg_pad = jnp.concatenate([neg, jnp.zeros((hd - nb, A_BLOCK), F32)], axis=0).astype(BF16)
            eye = jnp.where(lax.broadcasted_iota(jnp.int32, (A_BLOCK, A_BLOCK), 0)
                            == lax.broadcasted_iota(jnp.int32, (A_BLOCK, A_BLOCK), 1), 1.0, 0.0).astype(BF16)
            qa_ref[rows, hd:] = lax.dot_general(eye, neg_pad, (((1,), (1,)), ((), ())),
                                                preferred_element_type=F32).astype(BF16)
        s_ref[:, off(i):off(i) + nk] = lax.dot_general(qa_ref[rows, :], ka_ref[:nk, :], (((1,), (1,)), ((), ())),
                                               preferred_element_type=F32)

    def softmax(i):
        for c in range(A_BLOCK // rc):
            r = slice(c * rc, (c + 1) * rc)
            tiles = []
            for j in range(i + 1):
                sj = s_ref[r, off(i) + j * A_BLOCK:off(i) + (j + 1) * A_BLOCK]
                if j == i:
                    sj = sj + bias_ref[0, r, A_BLOCK:]
                elif j == i - 1:
                    sj = sj + bias_ref[0, r, :A_BLOCK]
                tiles.append(sj)
            mt = tiles[0]
            for t in tiles[1:]:
                mt = jnp.maximum(mt, t)
            m = jnp.max(mt, axis=-1, keepdims=True)
            for j, t in enumerate(tiles):
                p_ref[r, off(i) + j * A_BLOCK:off(i) + (j + 1) * A_BLOCK] = jnp.exp2(t - m).astype(BF16)

    def weighted_sum(i):
        rows = slice(i * A_BLOCK, (i + 1) * A_BLOCK)
        nk = (i + 1) * A_BLOCK
        o = jnp.dot(p_ref[:, off(i):off(i) + nk], va_ref[:nk, :], preferred_element_type=F32)
        o_ref[rows, :] = (o[:, :hd] / o[:, hd:]).astype(o_ref.dtype)

    return ([prologue] + [functools.partial(scores, i) for i in range(nb)]
            + [functools.partial(softmax, i) for i in range(nb)]
            + [functools.partial(weighted_sum, i) for i in range(nb)])


def _gla_stages(q_ref, k_ref, v_ref, r_ref, lr_ref, lrw_ref, lrb_ref, og_ref, o_ref,
                hml_ref, bc_ref, qd_ref, ki_ref, ke_ref, dec_ref, oi_ref, ss_ref, sb_ref):
    seq = q_ref.shape[0]
    ck = B_CHUNK
    nc = seq // ck
    grp = 4 * ck
    ng = seq // grp
    cpg = grp // ck
    qscale = B_DK ** -0.5
    nt = (((1,), (1,)), ((), ()))
    tn = (((0,), (0,)), ((), ()))

    def same_chunk_causal():
        ti = lax.broadcasted_iota(jnp.int32, (grp, grp), 0)
        si = lax.broadcasted_iota(jnp.int32, (grp, grp), 1)
        return (si <= ti) & (si // ck == ti // ck)

    def log_decay(g):
        rows = slice(g * grp, (g + 1) * grp)
        z = jnp.dot(lr_ref[rows, :], lrw_ref[...].astype(BF16), preferred_element_type=F32) + lrb_ref[...]
        la = (jnp.minimum(z, 0.0) - jnp.log(1.0 + jnp.exp(-jnp.abs(z)))) * (1.0 / B_GATE_NORM)
        hi = la.astype(BF16)
        r1 = la - hi.astype(F32)
        mid = r1.astype(BF16)
        hml_ref[rows, :B_DK] = hi
        hml_ref[rows, B_DK:2 * B_DK] = mid
        hml_ref[rows, 2 * B_DK:] = (r1 - mid.astype(F32)).astype(BF16)

    def cumsum(g):
        rows = slice(g * grp, (g + 1) * grp)
        tril_grp = jnp.where(same_chunk_causal(), 1.0, 0.0).astype(BF16)
        parts = jnp.dot(tril_grp, hml_ref[rows, :], preferred_element_type=F32)
        bc_ref[rows, :] = parts[:, :B_DK] + parts[:, B_DK:2 * B_DK] + parts[:, 2 * B_DK:]

    def decayed_qk(g):
        for c in range(g * cpg, (g + 1) * cpg):
            rows = slice(c * ck, (c + 1) * ck)
            bc = bc_ref[rows, :]
            bl = bc_ref[(c + 1) * ck - 1:(c + 1) * ck, :]
            q = q_ref[rows, :].astype(F32)
            k = k_ref[rows, :].astype(F32)
            qd_ref[rows, :] = (q * jnp.exp(bc) * qscale).astype(BF16)
            ki_ref[rows, :] = (k * jnp.exp(-bc)).astype(BF16)
            ke_ref[rows, :] = (k * jnp.exp(bl - bc)).astype(BF16)
            dec_ref[c:c + 1, :] = jnp.exp(bl)

    def intra(g):
        rows = slice(g * grp, (g + 1) * grp)
        a = lax.dot_general(qd_ref[rows, :], ki_ref[rows, :], nt, preferred_element_type=F32)
        a = jnp.where(same_chunk_causal(), a, 0.0).astype(BF16)
        oi_ref[rows, :] = jnp.dot(a, v_ref[rows, :], preferred_element_type=F32)

    def state_terms(g):
        for c in range(g * cpg, (g + 1) * cpg):
            rows = slice(c * ck, (c + 1) * ck)
            ss_ref[c] = lax.dot_general(v_ref[rows, :], ke_ref[rows, :], tn, preferred_element_type=F32)

    def scan():
        slab = B_DV // 4
        st = [jnp.zeros((slab, B_DK), F32) for _ in range(4)]
        for c in range(nc):
            dec = dec_ref[c:c + 1, :]
            for s in range(4):
                r = slice(s * slab, (s + 1) * slab)
                sb_ref[c, r, :] = st[s].astype(BF16)
                st[s] = st[s] * dec + ss_ref[c, r, :]

    def outputs(g):
        for c in range(g * cpg, (g + 1) * cpg):
            rows = slice(c * ck, (c + 1) * ck)
            o = oi_ref[rows, :] + lax.dot_general(qd_ref[rows, :], sb_ref[c], nt,
                                                  preferred_element_type=F32)
            ms = jnp.mean(o * o, axis=-1, keepdims=True)
            y = o * lax.rsqrt(ms + EPS) * og_ref[...]
            r = r_ref[rows, :]
            half_r = 0.5 * r
            o_ref[rows, :] = y.astype(o_ref.dtype) * (half_r + half_r * jnp.tanh(half_r))

    groups = [[functools.partial(fn, g) for g in range(ng)]
              for fn in (log_decay, cumsum, decayed_qk, intra, state_terms)]
    return groups + [[scan], [functools.partial(outputs, g) for g in range(ng)]]


N_MOBA_IN, N_MOBA_SCRATCH = 6, 7
N_GLA_IN = 8


def _mixer_ab_kernel(*refs, n_cast):
    n_in = N_MOBA_IN + N_GLA_IN
    ins, cast_in = refs[:n_in], refs[n_in:n_in + n_cast]
    oa_ref, ob_ref = refs[n_in + n_cast:n_in + n_cast + 2]
    cast_out = refs[n_in + n_cast + 2:n_in + 2 * n_cast + 2]
    scratch = refs[n_in + 2 * n_cast + 2:]
    moba_stages = _moba_stages(*ins[:N_MOBA_IN], oa_ref, *scratch[:N_MOBA_SCRATCH])
    gla_steps = [fn for group in _gla_stages(*ins[N_MOBA_IN:], ob_ref, *scratch[N_MOBA_SCRATCH:])
                 for fn in group]
    per_slot = -(-len(gla_steps) // len(moba_stages))
    for n, stage in enumerate(moba_stages):
        stage()
        for fn in gla_steps[n * per_slot:(n + 1) * per_slot]:
            fn()
    _cast_all(cast_in, cast_out)


def mixer_ab(proj_out, lr_blk, vec, bias, lr_w, layer, to_cast, *, bsz, seq):
    hd = HEAD_DIM
    assert A_HEADS == B_HEADS
    qa0 = AB0 // hd
    qb0 = (AB0 + 3 * A_WIDTH) // B_DK
    kb0 = qb0 + B_HEADS
    vb0 = (AB0 + 3 * A_WIDTH + 2 * B_KWIDTH) // B_DV
    rb0 = vb0 + B_HEADS
    nc = seq // B_CHUNK
    nb = seq // A_BLOCK
    tri = nb * (nb + 1) // 2
    cast_in_specs, cast_out_specs, cast_shapes = _cast_specs(
        to_cast, layer, bsz * A_HEADS, lambda b, h: b * A_HEADS + h)
    return pl.pallas_call(
        functools.partial(_mixer_ab_kernel, n_cast=len(to_cast)),
        grid=(bsz, A_HEADS),
        in_specs=[pl.BlockSpec((seq, hd), lambda b, h: (b, qa0 + h)),
                  pl.BlockSpec((seq, hd), lambda b, h: (b, qa0 + A_HEADS + h)),
                  pl.BlockSpec((seq, hd), lambda b, h: (b, qa0 + 2 * A_HEADS + h)),
                  _vec_spec("q_norm_g", layer),
                  _vec_spec("k_norm_g", layer),
                  pl.BlockSpec((1, A_BLOCK, 2 * A_BLOCK), lambda b, h: (h, 0, 0)),
                  pl.BlockSpec((seq, B_DK), lambda b, h: (b, qb0 + h)),
                  pl.BlockSpec((seq, B_DK), lambda b, h: (b, kb0 + h)),
                  pl.BlockSpec((seq, B_DV), lambda b, h: (b, vb0 + h)),
                  pl.BlockSpec((seq, B_DV), lambda b, h: (b, rb0 + h)),
                  pl.BlockSpec((seq, LANES), lambda b, h: (b, 0)),
                  pl.BlockSpec((None, LANES, B_DK), lambda b, h: (layer, 0, h)),
                  _vec_spec("gla_lr_b", layer, B_DK, lambda b, h: h),
                  _vec_spec("gla_out_g", layer)] + cast_in_specs,
        out_specs=[pl.BlockSpec((seq, hd), lambda b, h: (b, h)),
                   pl.BlockSpec((seq, B_DV), lambda b, h: (b, h))] + cast_out_specs,
        out_shape=[jax.ShapeDtypeStruct((bsz * seq, A_WIDTH), BF16),
                   jax.ShapeDtypeStruct((bsz * seq, B_VWIDTH), BF16)] + cast_shapes,
        scratch_shapes=[pltpu.VMEM((seq, hd), F32),
                        pltpu.VMEM((seq, 2 * hd), BF16),
                        pltpu.VMEM((seq, 2 * hd), BF16),
                        pltpu.VMEM((seq, 2 * hd), BF16),
                        pltpu.VMEM((seq // A_BLOCK, hd), F32),
                        pltpu.VMEM((A_BLOCK, tri * A_BLOCK), F32),
                        pltpu.VMEM((A_BLOCK, tri * A_BLOCK), BF16),
                        pltpu.VMEM((seq, 3 * B_DK), BF16),
                        pltpu.VMEM((seq, B_DK), F32),
                        pltpu.VMEM((seq, B_DK), BF16),
                        pltpu.VMEM((seq, B_DK), BF16),
                        pltpu.VMEM((seq, B_DK), BF16),
                        pltpu.VMEM((nc, B_DK), F32),
                        pltpu.VMEM((seq, B_DV), F32),
                        pltpu.VMEM((nc, B_DV, B_DK), F32),
                        pltpu.VMEM((nc, B_DV, B_DK), BF16)],
        compiler_params=_cparams("parallel", "parallel"),
        name="mixer_ab",
    )(proj_out, proj_out, proj_out, vec, vec, bias,
      proj_out, proj_out, proj_out, proj_out, lr_blk, lr_w, vec, vec, *to_cast)


def _gelu(x):
    c = math.sqrt(2.0 / math.pi)
    half_x = 0.5 * x
    return half_x + half_x * jnp.tanh(x * (c + (c * 0.044715) * (x * x)))


SGU_SUB = 2 * C_CHUNK


def _sgu_kernel(*refs, n_cast):
    x_ref, g1_ref, wsrc_ref, lg_ref, lb_ref, w_ref, b_ref = refs[:7]
    o_ref, lr_ref, xn_ref = refs[7 + n_cast:10 + n_cast]
    wc_ref, luv_ref = refs[-2:]

    @pl.when(pl.program_id(0) == 0)
    def _():
        wc_ref[...] = wsrc_ref[...].astype(BF16)

    _cast_all(refs[7:7 + n_cast], refs[10 + n_cast:10 + 2 * n_cast])
    ti = lax.broadcasted_iota(jnp.int32, (C_CHUNK, C_CHUNK), 0)
    si = lax.broadcasted_iota(jnp.int32, (C_CHUNK, C_CHUNK), 1)
    causal = si <= ti
    ws = [jnp.where(causal, w_ref[g], 0.0).astype(BF16) for g in range(C_GROUPS)]
    n_sub = x_ref.shape[0] // SGU_SUB

    def project(s):
        rs = slice(s * SGU_SUB, (s + 1) * SGU_SUB)
        x = x_ref[rs, :]
        ms = jnp.mean(x * x, axis=-1, keepdims=True)
        xn = (x * lax.rsqrt(ms + EPS) * g1_ref[...]).astype(BF16)
        xn_ref[rs, :] = xn
        luv = lax.dot_general(xn, wc_ref[...], (((1,), (1,)), ((), ())), preferred_element_type=F32)
        luv_ref[s % 2] = luv
        lr_ref[rs, :] = luv[:, :LANES].astype(lr_ref.dtype)

    def gate(s):
        for n in range(SGU_SUB // C_CHUNK):
            rl = slice(n * C_CHUNK, (n + 1) * C_CHUNK)
            ro = slice(s * SGU_SUB + n * C_CHUNK, s * SGU_SUB + (n + 1) * C_CHUNK)
            v = _gelu(luv_ref[s % 2, rl, LANES + C_WIDTH:])
            mu = jnp.mean(v, axis=-1, keepdims=True)
            vc = v - mu
            vn = vc * lax.rsqrt(jnp.mean(vc * vc, axis=-1, keepdims=True) + EPS)
            vn = (vn * lg_ref[...] + lb_ref[...]).astype(BF16)
            u = _gelu(luv_ref[s % 2, rl, LANES:LANES + C_WIDTH])
            for g in range(C_GROUPS):
                cols = slice(g * C_GROUP_DIM, (g + 1) * C_GROUP_DIM)
                mixed = jnp.dot(ws[g], vn[:, cols], preferred_element_type=F32) + b_ref[g]
                o_ref[ro, cols] = (u[:, cols] * mixed).astype(o_ref.dtype)

    project(0)
    for s in range(n_sub):
        if s + 1 < n_sub:
            project(s + 1)
        gate(s)


def sgu(x, vec, w_t, w_s, b_s, layer, to_cast, *, tm):
    m, d = x.shape
    cast_in_specs, cast_out_specs, cast_shapes = _cast_specs(to_cast, layer, m // tm, lambda i: i)
    return pl.pallas_call(
        functools.partial(_sgu_kernel, n_cast=len(to_cast)),
        grid=(m // tm,),
        in_specs=[pl.BlockSpec((tm, d), lambda i: (i, 0)),
                  _vec_spec("norm1_g", layer),
                  pl.BlockSpec((None, pl.Element(C_WIN), pl.Element(d)), lambda i: (layer, C_SRC, 0),
                               pipeline_mode=pl.Buffered(1)),
                  _vec_spec("sg_ln_g", layer),
                  _vec_spec("sg_ln_b", layer),
                  pl.BlockSpec((None, C_GROUPS, C_CHUNK, C_CHUNK), lambda i: (layer, 0, 0, 0)),
                  pl.BlockSpec((None, C_GROUPS, C_CHUNK, 1), lambda i: (layer, 0, 0, 0))] + cast_in_specs,
        out_specs=[pl.BlockSpec((tm, C_WIDTH), lambda i: (i, 0)),
                   pl.BlockSpec((tm, LANES), lambda i: (i, 0)),
                   pl.BlockSpec((tm, d), lambda i: (i, 0))] + cast_out_specs,
        out_shape=[jax.ShapeDtypeStruct((m, C_WIDTH), BF16),
                   jax.ShapeDtypeStruct((m, LANES), BF16),
                   jax.ShapeDtypeStruct((m, d), BF16)] + cast_shapes,
        scratch_shapes=[pltpu.VMEM((C_WIN, d), BF16),
                        pltpu.VMEM((2, SGU_SUB, C_WIN), F32)],
        compiler_params=_cparams("arbitrary"),
        name="sgu",
    )(x, vec, w_t, vec, vec, w_s, b_s, *to_cast)


def _merge_kernel(ya_ref, yb_ref, yc_ref, ga_ref, gb_ref, gc_ref, x_ref,
                  wa_ref, wb_ref, wc_ref, wo_ref, g2_ref, o_ref, on_ref):
    sub = 256
    for s in range(x_ref.shape[0] // sub):
        r = slice(s * sub, (s + 1) * sub)

        def gated(g_ref, y_ref, w_ref):
            gate = 0.5 * jnp.tanh(0.5 * g_ref[r, :].astype(F32)) + 0.5
            return gate * jnp.dot(y_ref[r, :], w_ref[...], preferred_element_type=F32)

        merged = gated(ga_ref, ya_ref, wa_ref) + gated(gb_ref, yb_ref, wb_ref) + gated(gc_ref, yc_ref, wc_ref)
        h = x_ref[r, :] + jnp.dot(merged.astype(BF16), wo_ref[...], preferred_element_type=F32)
        o_ref[r, :] = h
        ms = jnp.mean(h * h, axis=-1, keepdims=True)
        on_ref[r, :] = (h * lax.rsqrt(ms + EPS) * g2_ref[...]).astype(on_ref.dtype)


def merge(ya, yb, yc, proj_out, x, wa, wb, wc, wo, vec, layer, *, tm):
    m, d = x.shape
    resident = functools.partial(pl.BlockSpec, pipeline_mode=pl.Buffered(1))
    return pl.pallas_call(
        _merge_kernel,
        grid=(m // tm,),
        in_specs=[pl.BlockSpec((tm, A_WIDTH), lambda i: (i, 0)),
                  pl.BlockSpec((tm, B_VWIDTH), lambda i: (i, 0)),
                  pl.BlockSpec((tm, C_WIDTH), lambda i: (i, 0)),
                  pl.BlockSpec((tm, d), lambda i: (i, 0)),
                  pl.BlockSpec((tm, d), lambda i: (i, 1)),
                  pl.BlockSpec((tm, d), lambda i: (i, 2)),
                  pl.BlockSpec((tm, d), lambda i: (i, 0)),
                  resident((A_WIDTH, d), lambda i: (0, 0)),
                  resident((B_VWIDTH, d), lambda i: (0, 0)),
                  resident((C_WIDTH, d), lambda i: (0, 0)),
                  resident((d, d), lambda i: (0, 0)),
                  _vec_spec("norm2_g", layer)],
        out_specs=[pl.BlockSpec((tm, d), lambda i: (i, 0)),
                   pl.BlockSpec((tm, d), lambda i: (i, 0))],
        out_shape=[jax.ShapeDtypeStruct((m, d), F32),
                   jax.ShapeDtypeStruct((m, d), BF16)],
        compiler_params=_cparams("parallel"),
        name="merge",
    )(ya, yb, yc, proj_out, proj_out, proj_out, x, wa, wb, wc, wo, vec)


def _mlp_kernel(h_hbm, hn_ref, w1_ref, w2_ref, o_ref, h_ref, h_sem):
    i, j = pl.program_id(0), pl.program_id(1)
    last = pl.num_programs(1) - 1
    tm = h_ref.shape[0]
    h_copy = pltpu.make_async_copy(h_hbm.at[pl.ds(i * tm, tm), :], h_ref, h_sem)

    def update():
        a = jnp.dot(hn_ref[...], w1_ref[...], preferred_element_type=F32)
        a = jnp.square(jnp.maximum(a, 0.0)).astype(BF16)
        return jnp.dot(a, w2_ref[...], preferred_element_type=F32)

    @pl.when(j == 0)
    def _():
        h_copy.start()
        o_ref[...] = update()

    @pl.when((j > 0) & (j < last))
    def _():
        o_ref[...] += update()

    @pl.when(j == last)
    def _():
        h_copy.wait()
        o_ref[...] = (o_ref[...] + h_ref[...]) + update()


def mlp(h, hn, w1, w2, *, tm, tf):
    m, d = h.shape
    f = w1.shape[1]
    return pl.pallas_call(
        _mlp_kernel,
        grid=(m // tm, f // tf),
        in_specs=[pl.BlockSpec(memory_space=pl.ANY),
                  pl.BlockSpec((tm, d), lambda i, j: (i, 0)),
                  pl.BlockSpec((d, tf), lambda i, j: (0, j)),
                  pl.BlockSpec((tf, d), lambda i, j: (j, 0))],
        out_specs=pl.BlockSpec((tm, d), lambda i, j: (i, 0)),
        out_shape=jax.ShapeDtypeStruct((m, d), F32),
        scratch_shapes=[pltpu.VMEM((tm, d), F32),
                        pltpu.SemaphoreType.DMA(())],
        compiler_params=_cparams("parallel", "arbitrary"),
        name="mlp",
    )(h, hn, w1, w2)


def kernel(x, rpe_table, norm1_g, w_in, q_norm_g, k_norm_g, gla_lr_w, gla_lr_b, gla_out_g,
           sg_ln_g, sg_ln_b, sg_w, sg_b, w_br_a, w_br_b, w_br_c, w_o, norm2_g, w_ff1, w_ff2):
    bsz, seq, d = x.shape
    assert seq % A_BLOCK == 0 and seq % B_CHUNK == 0 and seq % C_CHUNK == 0
    assert w_in.shape[2] == IN_WIDTH
    h = x.reshape(bsz * seq, d)
    fields = dict(norm1_g=norm1_g, norm2_g=norm2_g, sg_ln_g=sg_ln_g, sg_ln_b=sg_ln_b, gla_lr_b=gla_lr_b,
                  gla_out_g=gla_out_g, q_norm_g=q_norm_g, k_norm_g=k_norm_g)
    bias, vec, lr_w = rpe_bias(rpe_table, gla_lr_w, [fields[name] for name, _ in VEC_FIELDS])
    w_t = jnp.swapaxes(w_in, 1, 2)
    sg_b_col = sg_b[..., None]
    for l in range(DEPTH):
        y_c, lr_blk, xn, wa, wb, wc, wo = sgu(h, vec, w_t, sg_w, sg_b_col, l,
                                              (w_br_a, w_br_b, w_br_c, w_o), tm=SGU_TM)
        p, w1 = proj(xn, w_t, l, (w_ff1,), tm=PROJ_TM)
        y_a, y_b, w2 = mixer_ab(p, lr_blk, vec, bias, lr_w, l, (w_ff2,), bsz=bsz, seq=seq)
        h, hn = merge(y_a, y_b, y_c, p, h, wa, wb, wc, wo, vec, l, tm=MERGE_TM)
        h = mlp(h, hn, w1, w2, tm=MLP_TM, tf=MLP_TF)
    return h.reshape(bsz, seq, d)
```

```python
import functools
import math

import jax
import jax.numpy as jnp
from jax import lax
from jax.experimental import pallas as pl
from jax.experimental.pallas import tpu as pltpu

D_MODEL = 2048
DEPTH = 2
HEAD_DIM = 128
A_HEADS = 4
A_WIDTH = A_HEADS * HEAD_DIM
A_BLOCK = 256
A_TOPK = 3
RPE_BUCKETS = 32
RPE_MAX_DIST = 128
B_HEADS = 4
B_DK = 128
B_DV = 256
B_KWIDTH = B_HEADS * B_DK
B_VWIDTH = B_HEADS * B_DV
B_LOWRANK = 16
B_GATE_NORM = 16.0
B_CHUNK = 64
C_GROUPS = 4
C_GROUP_DIM = 128
C_WIDTH = C_GROUPS * C_GROUP_DIM
C_CHUNK = 128
D_FF = 4 * D_MODEL
EPS = 1e-6
NEG_INF = -1e30
LOG2E = 1.4426950408889634

LANES = 128
BF16_ROWS = 16
VMEM_LIMIT = 56 * 1024 * 1024

F32 = jnp.float32
BF16 = jnp.bfloat16

PROJ_TM = 1024
MLP_TM = 1024
MLP_TF = 1024
MERGE_TM = 512
SGU_TM = 1024

PROJ_TN = 1536
AB_WIDTH = 3 * A_WIDTH + 2 * B_KWIDTH + 2 * B_VWIDTH
G_WIDTH = 3 * D_MODEL
IN_WIDTH = AB_WIDTH + B_LOWRANK + 2 * C_WIDTH + G_WIDTH
AB0 = G_WIDTH
PROJ_WIDTH = G_WIDTH + AB_WIDTH
N_AB_TILES = AB_WIDTH // PROJ_TN
N_G_TILES = G_WIDTH // PROJ_TN
G_SRC = IN_WIDTH - G_WIDTH
C_PAD = LANES - B_LOWRANK
C_SRC = AB_WIDTH - C_PAD
C_WIN = LANES + 2 * C_WIDTH
LR_LANE = C_PAD
assert AB_WIDTH % PROJ_TN == 0 and G_WIDTH % PROJ_TN == 0
assert C_SRC % BF16_ROWS == 0 and G_SRC % BF16_ROWS == 0


def _rpe_thresholds():
    max_exact = RPE_BUCKETS // 2

    def bucket(n):
        if n < max_exact:
            return n
        v = math.log(n / max_exact) / math.log(RPE_MAX_DIST / max_exact) * (RPE_BUCKETS - max_exact)
        return min(max_exact + int(v), RPE_BUCKETS - 1)

    table = [bucket(n) for n in range(4 * RPE_MAX_DIST)]
    return [min(n for n in range(len(table)) if table[n] >= b)
            for b in range(max_exact + 1, RPE_BUCKETS)]


RPE_THRESHOLDS = _rpe_thresholds()
assert RPE_THRESHOLDS[-1] <= A_BLOCK


VEC_FIELDS = (("norm1_g", D_MODEL), ("norm2_g", D_MODEL), ("sg_ln_g", C_WIDTH), ("sg_ln_b", C_WIDTH),
              ("gla_lr_b", B_KWIDTH), ("gla_out_g", B_DV), ("q_norm_g", HEAD_DIM), ("k_norm_g", HEAD_DIM))
VEC_OFFSET = {}
VEC_WIDTH = 0
for _name, _width in VEC_FIELDS:
    assert VEC_WIDTH % _width == 0
    VEC_OFFSET[_name] = VEC_WIDTH
    VEC_WIDTH += _width


def _vec_spec(name, layer, width=None, block_of=None):
    width = width or dict(VEC_FIELDS)[name]
    first = VEC_OFFSET[name] // width
    if block_of is None:
        return pl.BlockSpec((None, 1, width), lambda *idx: (layer, 0, first))
    return pl.BlockSpec((None, 1, width), lambda *idx: (layer, 0, first + block_of(*idx)))


def _cparams(*sem):
    return pltpu.CompilerParams(dimension_semantics=sem, vmem_limit_bytes=VMEM_LIMIT)


def _rpe_bias_kernel(*refs):
    tbl_ref, lrw_ref = refs[:2]
    fields = refs[2:2 + len(VEC_FIELDS)]
    bias_ref, vec_ref, lrw_pad_ref = refs[2 + len(VEC_FIELDS):]
    h = pl.program_id(0)

    @pl.when(h == 0)
    def _():
        for (name, width), src in zip(VEC_FIELDS, fields):
            for l in range(src.shape[0]):
                vec_ref[l, :, VEC_OFFSET[name]:VEC_OFFSET[name] + width] = src[l:l + 1, :]
        lrw_pad_ref[...] = jnp.zeros(lrw_pad_ref.shape, lrw_pad_ref.dtype)
        lrw_pad_ref[:, LR_LANE:LR_LANE + B_LOWRANK, :] = lrw_ref[...]

    t = lax.broadcasted_iota(jnp.int32, (A_BLOCK, 2 * A_BLOCK), 0)
    c = lax.broadcasted_iota(jnp.int32, (A_BLOCK, 2 * A_BLOCK), 1)
    n = jnp.maximum(t - c + A_BLOCK, 0)
    max_exact = RPE_BUCKETS // 2
    large = jnp.full(n.shape, max_exact, jnp.int32)
    for thr in RPE_THRESHOLDS:
        large = large + (n >= thr).astype(jnp.int32)
    bucket = jnp.where(n < max_exact, n, large)
    bias = jnp.zeros(n.shape, F32)
    for b in range(RPE_BUCKETS):
        bias = jnp.where(bucket == b, tbl_ref[b, h], bias)
    bias = (bias - tbl_ref[RPE_BUCKETS - 1, h]) * LOG2E
    bias_ref[0] = jnp.where(c - A_BLOCK > t, NEG_INF, bias)


def rpe_bias(rpe_table, lr_w, fields):
    layers = lr_w.shape[0]
    whole = lambda a: pl.BlockSpec(a.shape, lambda h: (0,) * a.ndim)
    return pl.pallas_call(
        _rpe_bias_kernel,
        grid=(A_HEADS,),
        in_specs=[pl.BlockSpec(memory_space=pltpu.SMEM), whole(lr_w)] + [whole(f) for f in fields],
        out_specs=[pl.BlockSpec((1, A_BLOCK, 2 * A_BLOCK), lambda h: (h, 0, 0)),
                   pl.BlockSpec((layers, 1, VEC_WIDTH), lambda h: (0, 0, 0)),
                   pl.BlockSpec((layers, LANES, B_KWIDTH), lambda h: (0, 0, 0))],
        out_shape=[jax.ShapeDtypeStruct((A_HEADS, A_BLOCK, 2 * A_BLOCK), F32),
                   jax.ShapeDtypeStruct((layers, 1, VEC_WIDTH), F32),
                   jax.ShapeDtypeStruct((layers, LANES, B_KWIDTH), F32)],
        compiler_params=_cparams("arbitrary"),
        name="rpe_bias",
    )(rpe_table, lr_w, *fields)


def _cast_specs(to_cast, layer, n_steps, step_of):
    in_specs, out_specs, shapes = [], [], []
    for w in to_cast:
        _, rows, cols = w.shape
        slice_rows = next(r for r in range(BF16_ROWS, rows + 1, BF16_ROWS)
                          if rows % r == 0 and rows // r <= n_steps)
        n_slices = rows // slice_rows

        def slice_of(*idx, n_slices=n_slices):
            return jnp.minimum(step_of(*idx), n_slices - 1)

        in_specs.append(pl.BlockSpec((None, slice_rows, cols), lambda *idx, f=slice_of: (layer, f(*idx), 0)))
        out_specs.append(pl.BlockSpec((slice_rows, cols), lambda *idx, f=slice_of: (f(*idx), 0)))
        shapes.append(jax.ShapeDtypeStruct((rows, cols), BF16))
    return in_specs, out_specs, shapes


def _cast_all(srcs, dsts):
    for src, dst in zip(srcs, dsts):
        dst[...] = src[...].astype(dst.dtype)


def _proj_kernel(*refs, n_cast):
    xn_ref, w_ref = refs[:2]
    o_ref = refs[2 + n_cast]
    o_ref[...] = lax.dot_general(xn_ref[...], w_ref[...].astype(BF16), (((1,), (1,)), ((), ())),
                                 preferred_element_type=F32).astype(o_ref.dtype)
    _cast_all(refs[2:2 + n_cast], refs[3 + n_cast:])


def _proj_src_row(j):
    u = BF16_ROWS
    return u * jnp.where(j < N_AB_TILES, j * (PROJ_TN // u),
                         G_SRC // u + (j - N_AB_TILES) * (PROJ_TN // u))


def _proj_out_tile(j):
    return jnp.where(j < N_AB_TILES, j + N_G_TILES, j - N_AB_TILES)


def proj(xn, w_t, layer, to_cast, *, tm):
    m, k = xn.shape
    tn = PROJ_TN
    n_tiles = N_AB_TILES + N_G_TILES
    cast_in_specs, cast_out_specs, cast_shapes = _cast_specs(
        to_cast, layer, (m // tm) * n_tiles, lambda i, j: i * n_tiles + j)
    return pl.pallas_call(
        functools.partial(_proj_kernel, n_cast=len(to_cast)),
        grid=(m // tm, n_tiles),
        in_specs=[pl.BlockSpec((tm, k), lambda i, j: (i, 0)),
                  pl.BlockSpec((None, pl.Element(tn), pl.Element(k)),
                               lambda i, j: (layer, _proj_src_row(j), 0))] + cast_in_specs,
        out_specs=[pl.BlockSpec((tm, tn), lambda i, j: (i, _proj_out_tile(j)))] + cast_out_specs,
        out_shape=[jax.ShapeDtypeStruct((m, PROJ_WIDTH), BF16)] + cast_shapes,
        compiler_params=_cparams("parallel", "arbitrary"),
        name="proj",
    )(xn, w_t, *to_cast)


def _head_rms(x, g):
    ms = jnp.mean(x * x, axis=-1, keepdims=True)
    return x * lax.rsqrt(ms + EPS) * g


def _moba_stages(q_ref, k_ref, v_ref, qg_ref, kg_ref, bias_ref, o_ref,
                 qn_ref, qa_ref, ka_ref, va_ref, kmean_ref, s_ref, p_ref):
    seq, hd = q_ref.shape
    nb = seq // A_BLOCK
    rc = BF16_ROWS

    def off(i):
        return A_BLOCK * (i * (i + 1) // 2)

    def prologue():
        lane = lax.broadcasted_iota(jnp.int32, (A_BLOCK, hd), 1)
        for j in range(nb):
            rows = slice(j * A_BLOCK, (j + 1) * A_BLOCK)
            qn = _head_rms(q_ref[rows, :].astype(F32), qg_ref[...])
            qn_ref[rows, :] = qn
            qa_ref[rows, :hd] = (qn * (hd ** -0.5 * LOG2E)).astype(BF16)
            qa_ref[rows, hd:] = jnp.zeros((A_BLOCK, hd), BF16)
            kn = _head_rms(k_ref[rows, :].astype(F32), kg_ref[...])
            ka_ref[rows, :hd] = kn.astype(BF16)
            ka_ref[rows, hd:] = jnp.where(lane == j, 1.0, 0.0).astype(BF16)
            va_ref[rows, :hd] = v_ref[rows, :]
            va_ref[rows, hd:] = jnp.ones((A_BLOCK, hd), BF16)
            kmean_ref[j:j + 1, :] = jnp.mean(kn, axis=0, keepdims=True)

    def scores(i):
        rows = slice(i * A_BLOCK, (i + 1) * A_BLOCK)
        nk = (i + 1) * A_BLOCK
        if i > A_TOPK:
            blk = lax.broadcasted_iota(jnp.int32, (nb, A_BLOCK), 0)
            g = lax.dot_general(kmean_ref[...], qn_ref[rows, :], (((1,), (1,)), ((), ())),
                                precision=lax.Precision.HIGHEST, preferred_element_type=F32)
            rank = jnp.zeros((nb, A_BLOCK), jnp.int32)
            for jp in range(i):
                gj = g[jp:jp + 1, :]
                beats = (gj > g) | ((gj == g) & (jp < blk))
                rank = rank + beats.astype(jnp.int32)
            neg = jnp.where((blk < i) & (rank >= A_TOPK), NEG_INF, 0.0)
            neg_pad = jnp.concatenate([neg, jnp.zeros((hd - nb, A_BLOCK), F32)], axis=0).astype(BF16)
            eye = jnp.where(lax.broadcasted_iota(jnp.int32, (A_BLOCK, A_BLOCK), 0)
                            == lax.broadcasted_iota(jnp.int32, (A_BLOCK, A_BLOCK), 1), 1.0, 0.0).astype(BF16)
            qa_ref[rows, hd:] = lax.dot_general(eye, neg_pad, (((1,), (1,)), ((), ())),
                                                preferred_element_type=F32).astype(BF16)
        s_ref[:, off(i):off(i) + nk] = lax.dot_general(qa_ref[rows, :], ka_ref[:nk, :], (((1,), (1,)), ((), ())),
                                               preferred_element_type=F32)

    def softmax(i):
        for c in range(A_BLOCK // rc):
            r = slice(c * rc, (c + 1) * rc)
            tiles = []
            for j in range(i + 1):
                sj = s_ref[r, off(i) + j * A_BLOCK:off(i) + (j + 1) * A_BLOCK]
                if j == i:
                    sj = sj + bias_ref[0, r, A_BLOCK:]
                elif j == i - 1:
                    sj = sj + bias_ref[0, r, :A_BLOCK]
                tiles.append(sj)
            mt = tiles[0]
            for t in tiles[1:]:
                mt = jnp.maximum(mt, t)
            m = jnp.max(mt, axis=-1, keepdims=True)
            for j, t in enumerate(tiles):
                p_ref[r, off(i) + j * A_BLOCK:off(i) + (j + 1) * A_BLOCK] = jnp.exp2(t - m).astype(BF16)

    def weighted_sum(i):
        rows = slice(i * A_BLOCK, (i + 1) * A_BLOCK)
        nk = (i + 1) * A_BLOCK
        o = jnp.dot(p_ref[:, off(i):off(i) + nk], va_ref[:nk, :], preferred_element_type=F32)
        o_ref[rows, :] = (o[:, :hd] / o[:, hd:]).astype(o_ref.dtype)

    return ([prologue] + [functools.partial(scores, i) for i in range(nb)]
            + [functools.partial(softmax, i) for i in range(nb)]
            + [functools.partial(weighted_sum, i) for i in range(nb)])


def _gla_stages(q_ref, k_ref, v_ref, r_ref, lr_ref, lrw_ref, lrb_ref, og_ref, o_ref,
                hml_ref, bc_ref, qd_ref, ki_ref, ke_ref, dec_ref, oi_ref, ss_ref, sb_ref):
    seq = q_ref.shape[0]
    ck = B_CHUNK
    nc = seq // ck
    grp = 4 * ck
    ng = seq // grp
    cpg = grp // ck
    qscale = B_DK ** -0.5
    nt = (((1,), (1,)), ((), ()))
    tn = (((0,), (0,)), ((), ()))

    def same_chunk_causal():
        ti = lax.broadcasted_iota(jnp.int32, (grp, grp), 0)
        si = lax.broadcasted_iota(jnp.int32, (grp, grp), 1)
        return (si <= ti) & (si // ck == ti // ck)

    def log_decay(g):
        rows = slice(g * grp, (g + 1) * grp)
        z = jnp.dot(lr_ref[rows, :], lrw_ref[...].astype(BF16), preferred_element_type=F32) + lrb_ref[...]
        la = (jnp.minimum(z, 0.0) - jnp.log(1.0 + jnp.exp(-jnp.abs(z)))) * (1.0 / B_GATE_NORM)
        hi = la.astype(BF16)
        r1 = la - hi.astype(F32)
        mid = r1.astype(BF16)
        hml_ref[rows, :B_DK] = hi
        hml_ref[rows, B_DK:2 * B_DK] = mid
        hml_ref[rows, 2 * B_DK:] = (r1 - mid.astype(F32)).astype(BF16)

    def cumsum(g):
        rows = slice(g * grp, (g + 1) * grp)
        tril_grp = jnp.where(same_chunk_causal(), 1.0, 0.0).astype(BF16)
        parts = jnp.dot(tril_grp, hml_ref[rows, :], preferred_element_type=F32)
        bc_ref[rows, :] = parts[:, :B_DK] + parts[:, B_DK:2 * B_DK] + parts[:, 2 * B_DK:]

    def decayed_qk(g):
        for c in range(g * cpg, (g + 1) * cpg):
            rows = slice(c * ck, (c + 1) * ck)
            bc = bc_ref[rows, :]
            bl = bc_ref[(c + 1) * ck - 1:(c + 1) * ck, :]
            q = q_ref[rows, :].astype(F32)
            k = k_ref[rows, :].astype(F32)
            qd_ref[rows, :] = (q * jnp.exp(bc) * qscale).astype(BF16)
            ki_ref[rows, :] = (k * jnp.exp(-bc)).astype(BF16)
            ke_ref[rows, :] = (k * jnp.exp(bl - bc)).astype(BF16)
            dec_ref[c:c + 1, :] = jnp.exp(bl)

    def intra(g):
        rows = slice(g * grp, (g + 1) * grp)
        a = lax.dot_general(qd_ref[rows, :], ki_ref[rows, :], nt, preferred_element_type=F32)
        a = jnp.where(same_chunk_causal(), a, 0.0).astype(BF16)
        oi_ref[rows, :] = jnp.dot(a, v_ref[rows, :], preferred_element_type=F32)

    def state_terms(g):
        for c in range(g * cpg, (g + 1) * cpg):
            rows = slice(c * ck, (c + 1) * ck)
            ss_ref[c] = lax.dot_general(v_ref[rows, :], ke_ref[rows, :], tn, preferred_element_type=F32)

    def scan():
        slab = B_DV // 4
        st = [jnp.zeros((slab, B_DK), F32) for _ in range(4)]
        for c in range(nc):
            dec = dec_ref[c:c + 1, :]
            for s in range(4):
                r = slice(s * slab, (s + 1) * slab)
                sb_ref[c, r, :] = st[s].astype(BF16)
                st[s] = st[s] * dec + ss_ref[c, r, :]

    def outputs(g):
        for c in range(g * cpg, (g + 1) * cpg):
            rows = slice(c * ck, (c + 1) * ck)
            o = oi_ref[rows, :] + lax.dot_general(qd_ref[rows, :], sb_ref[c], nt,
                                                  preferred_element_type=F32)
            ms = jnp.mean(o * o, axis=-1, keepdims=True)
            y = o * lax.rsqrt(ms + EPS) * og_ref[...]
            r = r_ref[rows, :]
            half_r = 0.5 * r
            o_ref[rows, :] = y.astype(o_ref.dtype) * (half_r + half_r * jnp.tanh(half_r))

    groups = [[functools.partial(fn, g) for g in range(ng)]
              for fn in (log_decay, cumsum, decayed_qk, intra, state_terms)]
    return groups + [[scan], [functools.partial(outputs, g) for g in range(ng)]]


N_MOBA_IN, N_MOBA_SCRATCH = 6, 7
N_GLA_IN = 8


def _mixer_ab_kernel(*refs, n_cast):
    n_in = N_MOBA_IN + N_GLA_IN
    ins, cast_in = refs[:n_in], refs[n_in:n_in + n_cast]
    oa_ref, ob_ref = refs[n_in + n_cast:n_in + n_cast + 2]
    cast_out = refs[n_in + n_cast + 2:n_in + 2 * n_cast + 2]
    scratch = refs[n_in + 2 * n_cast + 2:]
    moba_stages = _moba_stages(*ins[:N_MOBA_IN], oa_ref, *scratch[:N_MOBA_SCRATCH])
    gla_steps = [fn for group in _gla_stages(*ins[N_MOBA_IN:], ob_ref, *scratch[N_MOBA_SCRATCH:])
                 for fn in group]
    first = 1 + ins[0].shape[0] // A_BLOCK
    per_slot = -(-len(gla_steps) // (len(moba_stages) - first))
    for n, stage in enumerate(moba_stages):
        stage()
        if n >= first:
            for fn in gla_steps[(n - first) * per_slot:(n - first + 1) * per_slot]:
                fn()
    _cast_all(cast_in, cast_out)


def mixer_ab(proj_out, lr_blk, vec, bias, lr_w, layer, to_cast, *, bsz, seq):
    hd = HEAD_DIM
    assert A_HEADS == B_HEADS
    qa0 = AB0 // hd
    qb0 = (AB0 + 3 * A_WIDTH) // B_DK
    kb0 = qb0 + B_HEADS
    vb0 = (AB0 + 3 * A_WIDTH + 2 * B_KWIDTH) // B_DV
    rb0 = vb0 + B_HEADS
    nc = seq // B_CHUNK
    nb = seq // A_BLOCK
    tri = nb * (nb + 1) // 2
    cast_in_specs, cast_out_specs, cast_shapes = _cast_specs(
        to_cast, layer, bsz * A_HEADS, lambda b, h: b * A_HEADS + h)
    return pl.pallas_call(
        functools.partial(_mixer_ab_kernel, n_cast=len(to_cast)),
        grid=(bsz, A_HEADS),
        in_specs=[pl.BlockSpec((seq, hd), lambda b, h: (b, qa0 + h)),
                  pl.BlockSpec((seq, hd), lambda b, h: (b, qa0 + A_HEADS + h)),
                  pl.BlockSpec((seq, hd), lambda b, h: (b, qa0 + 2 * A_HEADS + h)),
                  _vec_spec("q_norm_g", layer),
                  _vec_spec("k_norm_g", layer),
                  pl.BlockSpec((1, A_BLOCK, 2 * A_BLOCK), lambda b, h: (h, 0, 0)),
                  pl.BlockSpec((seq, B_DK), lambda b, h: (b, qb0 + h)),
                  pl.BlockSpec((seq, B_DK), lambda b, h: (b, kb0 + h)),
                  pl.BlockSpec((seq, B_DV), lambda b, h: (b, vb0 + h)),
                  pl.BlockSpec((seq, B_DV), lambda b, h: (b, rb0 + h)),
                  pl.BlockSpec((seq, LANES), lambda b, h: (b, 0)),
                  pl.BlockSpec((None, LANES, B_DK), lambda b, h: (layer, 0, h)),
                  _vec_spec("gla_lr_b", layer, B_DK, lambda b, h: h),
                  _vec_spec("gla_out_g", layer)] + cast_in_specs,
        out_specs=[pl.BlockSpec((seq, hd), lambda b, h: (b, h)),
                   pl.BlockSpec((seq, B_DV), lambda b, h: (b, h))] + cast_out_specs,
        out_shape=[jax.ShapeDtypeStruct((bsz * seq, A_WIDTH), BF16),
                   jax.ShapeDtypeStruct((bsz * seq, B_VWIDTH), BF16)] + cast_shapes,
        scratch_shapes=[pltpu.VMEM((seq, hd), F32),
                        pltpu.VMEM((seq, 2 * hd), BF16),
                        pltpu.VMEM((seq, 2 * hd), BF16),
                        pltpu.VMEM((seq, 2 * hd), BF16),
                        pltpu.VMEM((seq // A_BLOCK, hd), F32),
                        pltpu.VMEM((A_BLOCK, tri * A_BLOCK), F32),
                        pltpu.VMEM((A_BLOCK, tri * A_BLOCK), BF16),
                        pltpu.VMEM((seq, 3 * B_DK), BF16),
                        pltpu.VMEM((seq, B_DK), F32),
                        pltpu.VMEM((seq, B_DK), BF16),
                        pltpu.VMEM((seq, B_DK), BF16),
                        pltpu.VMEM((seq, B_DK), BF16),
                        pltpu.VMEM((nc, B_DK), F32),
                        pltpu.VMEM((seq, B_DV), F32),
                        pltpu.VMEM((nc, B_DV, B_DK), F32),
                        pltpu.VMEM((nc, B_DV, B_DK), BF16)],
        compiler_params=_cparams("parallel", "parallel"),
        name="mixer_ab",
    )(proj_out, proj_out, proj_out, vec, vec, bias,
      proj_out, proj_out, proj_out, proj_out, lr_blk, lr_w, vec, vec, *to_cast)


def _gelu(x):
    c = math.sqrt(2.0 / math.pi)
    half_x = 0.5 * x
    return half_x + half_x * jnp.tanh(x * (c + (c * 0.044715) * (x * x)))


SGU_SUB = 2 * C_CHUNK


def _sgu_kernel(*refs, n_cast):
    x_ref, g1_ref, wsrc_ref, lg_ref, lb_ref, w_ref, b_ref = refs[:7]
    o_ref, lr_ref, xn_ref = refs[7 + n_cast:10 + n_cast]
    wc_ref, luv_ref = refs[-2:]

    @pl.when(pl.program_id(0) == 0)
    def _():
        wc_ref[...] = wsrc_ref[...].astype(BF16)

    _cast_all(refs[7:7 + n_cast], refs[10 + n_cast:10 + 2 * n_cast])
    ti = lax.broadcasted_iota(jnp.int32, (C_CHUNK, C_CHUNK), 0)
    si = lax.broadcasted_iota(jnp.int32, (C_CHUNK, C_CHUNK), 1)
    causal = si <= ti
    ws = [jnp.where(causal, w_ref[g], 0.0).astype(BF16) for g in range(C_GROUPS)]
    n_sub = x_ref.shape[0] // SGU_SUB

    def project(s):
        rs = slice(s * SGU_SUB, (s + 1) * SGU_SUB)
        x = x_ref[rs, :]
        ms = jnp.mean(x * x, axis=-1, keepdims=True)
        xn = (x * lax.rsqrt(ms + EPS) * g1_ref[...]).astype(BF16)
        xn_ref[rs, :] = xn
        luv = lax.dot_general(xn, wc_ref[...], (((1,), (1,)), ((), ())), preferred_element_type=F32)
        luv_ref[s % 2] = luv
        lr_ref[rs, :] = luv[:, :LANES].astype(lr_ref.dtype)

    def gate(s):
        for n in range(SGU_SUB // C_CHUNK):
            rl = slice(n * C_CHUNK, (n + 1) * C_CHUNK)
            ro = slice(s * SGU_SUB + n * C_CHUNK, s * SGU_SUB + (n + 1) * C_CHUNK)
            v = _gelu(luv_ref[s % 2, rl, LANES + C_WIDTH:])
            mu = jnp.mean(v, axis=-1, keepdims=True)
            vc = v - mu
            vn = vc * lax.rsqrt(jnp.mean(vc * vc, axis=-1, keepdims=True) + EPS)
            vn = (vn * lg_ref[...] + lb_ref[...]).astype(BF16)
            u = _gelu(luv_ref[s % 2, rl, LANES:LANES + C_WIDTH])
            for g in range(C_GROUPS):
                cols = slice(g * C_GROUP_DIM, (g + 1) * C_GROUP_DIM)
                mixed = jnp.dot(ws[g], vn[:, cols], preferred_element_type=F32) + b_ref[g]
                o_ref[ro, cols] = (u[:, cols] * mixed).astype(o_ref.dtype)

    project(0)
    for s in range(n_sub):
        if s + 1 < n_sub:
            project(s + 1)
        gate(s)


def sgu(x, vec, w_t, w_s, b_s, layer, to_cast, *, tm):
    m, d = x.shape
    cast_in_specs, cast_out_specs, cast_shapes = _cast_specs(to_cast, layer, m // tm, lambda i: i)
    return pl.pallas_call(
        functools.partial(_sgu_kernel, n_cast=len(to_cast)),
        grid=(m // tm,),
        in_specs=[pl.BlockSpec((tm, d), lambda i: (i, 0)),
                  _vec_spec("norm1_g", layer),
                  pl.BlockSpec((None, pl.Element(C_WIN), pl.Element(d)), lambda i: (layer, C_SRC, 0),
                               pipeline_mode=pl.Buffered(1)),
                  _vec_spec("sg_ln_g", layer),
                  _vec_spec("sg_ln_b", layer),
                  pl.BlockSpec((None, C_GROUPS, C_CHUNK, C_CHUNK), lambda i: (layer, 0, 0, 0)),
                  pl.BlockSpec((None, C_GROUPS, C_CHUNK, 1), lambda i: (layer, 0, 0, 0))] + cast_in_specs,
        out_specs=[pl.BlockSpec((tm, C_WIDTH), lambda i: (i, 0)),
                   pl.BlockSpec((tm, LANES), lambda i: (i, 0)),
                   pl.BlockSpec((tm, d), lambda i: (i, 0))] + cast_out_specs,
        out_shape=[jax.ShapeDtypeStruct((m, C_WIDTH), BF16),
                   jax.ShapeDtypeStruct((m, LANES), BF16),
                   jax.ShapeDtypeStruct((m, d), BF16)] + cast_shapes,
        scratch_shapes=[pltpu.VMEM((C_WIN, d), BF16),
                        pltpu.VMEM((2, SGU_SUB, C_WIN), F32)],
        compiler_params=_cparams("arbitrary"),
        name="sgu",
    )(x, vec, w_t, vec, vec, w_s, b_s, *to_cast)


def _merge_kernel(ya_ref, yb_ref, yc_ref, ga_ref, gb_ref, gc_ref, x_ref,
                  wa_ref, wb_ref, wc_ref, wo_ref, g2_ref, o_ref, on_ref):
    sub = 256
    for s in range(x_ref.shape[0] // sub):
        r = slice(s * sub, (s + 1) * sub)

        def gated(g_ref, y_ref, w_ref):
            gate = 0.5 * jnp.tanh(0.5 * g_ref[r, :].astype(F32)) + 0.5
            return gate * jnp.dot(y_ref[r, :], w_ref[...], preferred_element_type=F32)

        merged = gated(ga_ref, ya_ref, wa_ref) + gated(gb_ref, yb_ref, wb_ref) + gated(gc_ref, yc_ref, wc_ref)
        h = x_ref[r, :] + jnp.dot(merged.astype(BF16), wo_ref[...], preferred_element_type=F32)
        o_ref[r, :] = h
        ms = jnp.mean(h * h, axis=-1, keepdims=True)
        on_ref[r, :] = (h * lax.rsqrt(ms + EPS) * g2_ref[...]).astype(on_ref.dtype)


def merge(ya, yb, yc, proj_out, x, wa, wb, wc, wo, vec, layer, *, tm):
    m, d = x.shape
    resident = functools.partial(pl.BlockSpec, pipeline_mode=pl.Buffered(1))
    return pl.pallas_call(
        _merge_kernel,
        grid=(m // tm,),
        in_specs=[pl.BlockSpec((tm, A_WIDTH), lambda i: (i, 0)),
                  pl.BlockSpec((tm, B_VWIDTH), lambda i: (i, 0)),
                  pl.BlockSpec((tm, C_WIDTH), lambda i: (i, 0)),
                  pl.BlockSpec((tm, d), lambda i: (i, 0)),
                  pl.BlockSpec((tm, d), lambda i: (i, 1)),
                  pl.BlockSpec((tm, d), lambda i: (i, 2)),
                  pl.BlockSpec((tm, d), lambda i: (i, 0)),
                  resident((A_WIDTH, d), lambda i: (0, 0)),
                  resident((B_VWIDTH, d), lambda i: (0, 0)),
                  resident((C_WIDTH, d), lambda i: (0, 0)),
                  resident((d, d), lambda i: (0, 0)),
                  _vec_spec("norm2_g", layer)],
        out_specs=[pl.BlockSpec((tm, d), lambda i: (i, 0)),
                   pl.BlockSpec((tm, d), lambda i: (i, 0))],
        out_shape=[jax.ShapeDtypeStruct((m, d), F32),
                   jax.ShapeDtypeStruct((m, d), BF16)],
        compiler_params=_cparams("parallel"),
        name="merge",
    )(ya, yb, yc, proj_out, proj_out, proj_out, x, wa, wb, wc, wo, vec)


def _mlp_kernel(h_hbm, hn_ref, w1_ref, w2_ref, o_ref, h_ref, h_sem):
    i, j = pl.program_id(0), pl.program_id(1)
    last = pl.num_programs(1) - 1
    tm = h_ref.shape[0]
    h_copy = pltpu.make_async_copy(h_hbm.at[pl.ds(i * tm, tm), :], h_ref, h_sem)

    def update():
        a = jnp.dot(hn_ref[...], w1_ref[...], preferred_element_type=F32)
        a = jnp.square(jnp.maximum(a, 0.0)).astype(BF16)
        return jnp.dot(a, w2_ref[...], preferred_element_type=F32)

    @pl.when(j == 0)
    def _():
        h_copy.start()
        o_ref[...] = update()

    @pl.when((j > 0) & (j < last))
    def _():
        o_ref[...] += update()

    @pl.when(j == last)
    def _():
        h_copy.wait()
        o_ref[...] = (o_ref[...] + h_ref[...]) + update()


def mlp(h, hn, w1, w2, *, tm, tf):
    m, d = h.shape
    f = w1.shape[1]
    return pl.pallas_call(
        _mlp_kernel,
        grid=(m // tm, f // tf),
        in_specs=[pl.BlockSpec(memory_space=pl.ANY),
                  pl.BlockSpec((tm, d), lambda i, j: (i, 0)),
                  pl.BlockSpec((d, tf), lambda i, j: (0, j)),
                  pl.BlockSpec((tf, d), lambda i, j: (j, 0))],
        out_specs=pl.BlockSpec((tm, d), lambda i, j: (i, 0)),
        out_shape=jax.ShapeDtypeStruct((m, d), F32),
        scratch_shapes=[pltpu.VMEM((tm, d), F32),
                        pltpu.SemaphoreType.DMA(())],
        compiler_params=_cparams("parallel", "arbitrary"),
        name="mlp",
    )(h, hn, w1, w2)


def kernel(x, rpe_table, norm1_g, w_in, q_norm_g, k_norm_g, gla_lr_w, gla_lr_b, gla_out_g,
           sg_ln_g, sg_ln_b, sg_w, sg_b, w_br_a, w_br_b, w_br_c, w_o, norm2_g, w_ff1, w_ff2):
    bsz, seq, d = x.shape
    assert seq % A_BLOCK == 0 and seq % B_CHUNK == 0 and seq % C_CHUNK == 0
    assert w_in.shape[2] == IN_WIDTH
    h = x.reshape(bsz * seq, d)
    fields = dict(norm1_g=norm1_g, norm2_g=norm2_g, sg_ln_g=sg_ln_g, sg_ln_b=sg_ln_b, gla_lr_b=gla_lr_b,
                  gla_out_g=gla_out_g, q_norm_g=q_norm_g, k_norm_g=k_norm_g)
    bias, vec, lr_w = rpe_bias(rpe_table, gla_lr_w, [fields[name] for name, _ in VEC_FIELDS])
    w_t = jnp.swapaxes(w_in, 1, 2)
    sg_b_col = sg_b[..., None]
    for l in range(DEPTH):
        y_c, lr_blk, xn, wa, wb, wc, wo = sgu(h, vec, w_t, sg_w, sg_b_col, l,
                                              (w_br_a, w_br_b, w_br_c, w_o), tm=SGU_TM)
        p, w1 = proj(xn, w_t, l, (w_ff1,), tm=PROJ_TM)
        y_a, y_b, w2 = mixer_ab(p, lr_blk, vec, bias, lr_w, l, (w_ff2,), bsz=bsz, seq=seq)
        h, hn = merge(y_a, y_b, y_c, p, h, wa, wb, wc, wo, vec, l, tm=MERGE_TM)
        h = mlp(h, hn, w1, w2, tm=MLP_TM, tf=MLP_TF)
    return h.reshape(bsz, seq, d)
```

```python
import functools
import math

import jax
import jax.numpy as jnp
from jax import lax
from jax.experimental import pallas as pl
from jax.experimental.pallas import tpu as pltpu

D_MODEL = 2048
DEPTH = 2
HEAD_DIM = 128
A_HEADS = 4
A_WIDTH = A_HEADS * HEAD_DIM
A_BLOCK = 256
A_TOPK = 3
RPE_BUCKETS = 32
RPE_MAX_DIST = 128
B_HEADS = 4
B_DK = 128
B_DV = 256
B_KWIDTH = B_HEADS * B_DK
B_VWIDTH = B_HEADS * B_DV
B_LOWRANK = 16
B_GATE_NORM = 16.0
B_CHUNK = 64
C_GROUPS = 4
C_GROUP_DIM = 128
C_WIDTH = C_GROUPS * C_GROUP_DIM
C_CHUNK = 128
D_FF = 4 * D_MODEL
EPS = 1e-6
NEG_INF = -1e30
LOG2E = 1.4426950408889634

LANES = 128
BF16_ROWS = 16
VMEM_LIMIT = 56 * 1024 * 1024

F32 = jnp.float32
BF16 = jnp.bfloat16

PROJ_TM = 1024
MLP_TM = 1024
MLP_TF = 1024
MERGE_TM = 512
SGU_TM = 1024

PROJ_TN = 1536
AB_WIDTH = 3 * A_WIDTH + 2 * B_KWIDTH + 2 * B_VWIDTH
G_WIDTH = 3 * D_MODEL
IN_WIDTH = AB_WIDTH + B_LOWRANK + 2 * C_WIDTH + G_WIDTH
AB0 = G_WIDTH
PROJ_WIDTH = G_WIDTH + AB_WIDTH
N_AB_TILES = AB_WIDTH // PROJ_TN
N_G_TILES = G_WIDTH // PROJ_TN
G_SRC = IN_WIDTH - G_WIDTH
C_PAD = LANES - B_LOWRANK
C_SRC = AB_WIDTH - C_PAD
C_WIN = LANES + 2 * C_WIDTH
LR_LANE = C_PAD
assert AB_WIDTH % PROJ_TN == 0 and G_WIDTH % PROJ_TN == 0
assert C_SRC % BF16_ROWS == 0 and G_SRC % BF16_ROWS == 0


def _rpe_thresholds():
    max_exact = RPE_BUCKETS // 2

    def bucket(n):
        if n < max_exact:
            return n
        v = math.log(n / max_exact) / math.log(RPE_MAX_DIST / max_exact) * (RPE_BUCKETS - max_exact)
        return min(max_exact + int(v), RPE_BUCKETS - 1)

    table = [bucket(n) for n in range(4 * RPE_MAX_DIST)]
    return [min(n for n in range(len(table)) if table[n] >= b)
            for b in range(max_exact + 1, RPE_BUCKETS)]


RPE_THRESHOLDS = _rpe_thresholds()
assert RPE_THRESHOLDS[-1] <= A_BLOCK


VEC_FIELDS = (("norm1_g", D_MODEL), ("norm2_g", D_MODEL), ("sg_ln_g", C_WIDTH), ("sg_ln_b", C_WIDTH),
              ("gla_lr_b", B_KWIDTH), ("gla_out_g", B_DV), ("q_norm_g", HEAD_DIM), ("k_norm_g", HEAD_DIM))
VEC_OFFSET = {}
VEC_WIDTH = 0
for _name, _width in VEC_FIELDS:
    assert VEC_WIDTH % _width == 0
    VEC_OFFSET[_name] = VEC_WIDTH
    VEC_WIDTH += _width


def _vec_spec(name, layer, width=None, block_of=None):
    width = width or dict(VEC_FIELDS)[name]
    first = VEC_OFFSET[name] // width
    if block_of is None:
        return pl.BlockSpec((None, 1, width), lambda *idx: (layer, 0, first))
    return pl.BlockSpec((None, 1, width), lambda *idx: (layer, 0, first + block_of(*idx)))


def _cparams(*sem):
    return pltpu.CompilerParams(dimension_semantics=sem, vmem_limit_bytes=VMEM_LIMIT)


def _rpe_bias_kernel(*refs):
    tbl_ref, lrw_ref = refs[:2]
    fields = refs[2:2 + len(VEC_FIELDS)]
    bias_ref, vec_ref, lrw_pad_ref = refs[2 + len(VEC_FIELDS):]
    h = pl.program_id(0)

    @pl.when(h == 0)
    def _():
        for (name, width), src in zip(VEC_FIELDS, fields):
            for l in range(src.shape[0]):
                vec_ref[l, :, VEC_OFFSET[name]:VEC_OFFSET[name] + width] = src[l:l + 1, :]
        lrw_pad_ref[...] = jnp.zeros(lrw_pad_ref.shape, lrw_pad_ref.dtype)
        lrw_pad_ref[:, LR_LANE:LR_LANE + B_LOWRANK, :] = lrw_ref[...]

    t = lax.broadcasted_iota(jnp.int32, (A_BLOCK, 2 * A_BLOCK), 0)
    c = lax.broadcasted_iota(jnp.int32, (A_BLOCK, 2 * A_BLOCK), 1)
    n = jnp.maximum(t - c + A_BLOCK, 0)
    max_exact = RPE_BUCKETS // 2
    large = jnp.full(n.shape, max_exact, jnp.int32)
    for thr in RPE_THRESHOLDS:
        large = large + (n >= thr).astype(jnp.int32)
    bucket = jnp.where(n < max_exact, n, large)
    bias = jnp.zeros(n.shape, F32)
    for b in range(RPE_BUCKETS):
        bias = jnp.where(bucket == b, tbl_ref[b, h], bias)
    bias = (bias - tbl_ref[RPE_BUCKETS - 1, h]) * LOG2E
    bias_ref[0] = jnp.where(c - A_BLOCK > t, NEG_INF, bias)


def rpe_bias(rpe_table, lr_w, fields):
    layers = lr_w.shape[0]
    whole = lambda a: pl.BlockSpec(a.shape, lambda h: (0,) * a.ndim)
    return pl.pallas_call(
        _rpe_bias_kernel,
        grid=(A_HEADS,),
        in_specs=[pl.BlockSpec(memory_space=pltpu.SMEM), whole(lr_w)] + [whole(f) for f in fields],
        out_specs=[pl.BlockSpec((1, A_BLOCK, 2 * A_BLOCK), lambda h: (h, 0, 0)),
                   pl.BlockSpec((layers, 1, VEC_WIDTH), lambda h: (0, 0, 0)),
                   pl.BlockSpec((layers, LANES, B_KWIDTH), lambda h: (0, 0, 0))],
        out_shape=[jax.ShapeDtypeStruct((A_HEADS, A_BLOCK, 2 * A_BLOCK), F32),
                   jax.ShapeDtypeStruct((layers, 1, VEC_WIDTH), F32),
                   jax.ShapeDtypeStruct((layers, LANES, B_KWIDTH), F32)],
        compiler_params=_cparams("arbitrary"),
        name="rpe_bias",
    )(rpe_table, lr_w, *fields)


def _cast_specs(to_cast, layer, n_steps, step_of):
    in_specs, out_specs, shapes = [], [], []
    for w in to_cast:
        _, rows, cols = w.shape
        slice_rows = next(r for r in range(BF16_ROWS, rows + 1, BF16_ROWS)
                          if rows % r == 0 and rows // r <= n_steps)
        n_slices = rows // slice_rows

        def slice_of(*idx, n_slices=n_slices):
            return jnp.minimum(step_of(*idx), n_slices - 1)

        in_specs.append(pl.BlockSpec((None, slice_rows, cols), lambda *idx, f=slice_of: (layer, f(*idx), 0)))
        out_specs.append(pl.BlockSpec((slice_rows, cols), lambda *idx, f=slice_of: (f(*idx), 0)))
        shapes.append(jax.ShapeDtypeStruct((rows, cols), BF16))
    return in_specs, out_specs, shapes


def _cast_all(srcs, dsts):
    for src, dst in zip(srcs, dsts):
        dst[...] = src[...].astype(dst.dtype)


def _proj_kernel(*refs, n_cast):
    xn_ref, w_ref = refs[:2]
    o_ref = refs[2 + n_cast]
    o_ref[...] = lax.dot_general(xn_ref[...], w_ref[...].astype(BF16), (((1,), (1,)), ((), ())),
                                 preferred_element_type=F32).astype(o_ref.dtype)
    _cast_all(refs[2:2 + n_cast], refs[3 + n_cast:])


def _proj_src_row(j):
    u = BF16_ROWS
    return u * jnp.where(j < N_AB_TILES, j * (PROJ_TN // u),
                         G_SRC // u + (j - N_AB_TILES) * (PROJ_TN // u))


def _proj_out_tile(j):
    return jnp.where(j < N_AB_TILES, j + N_G_TILES, j - N_AB_TILES)


def proj(xn, w_t, layer, to_cast, *, tm):
    m, k = xn.shape
    tn = PROJ_TN
    n_tiles = N_AB_TILES + N_G_TILES
    cast_in_specs, cast_out_specs, cast_shapes = _cast_specs(
        to_cast, layer, (m // tm) * n_tiles, lambda i, j: i * n_tiles + j)
    return pl.pallas_call(
        functools.partial(_proj_kernel, n_cast=len(to_cast)),
        grid=(m // tm, n_tiles),
        in_specs=[pl.BlockSpec((tm, k), lambda i, j: (i, 0)),
                  pl.BlockSpec((None, pl.Element(tn), pl.Element(k)),
                               lambda i, j: (layer, _proj_src_row(j), 0))] + cast_in_specs,
        out_specs=[pl.BlockSpec((tm, tn), lambda i, j: (i, _proj_out_tile(j)))] + cast_out_specs,
        out_shape=[jax.ShapeDtypeStruct((m, PROJ_WIDTH), BF16)] + cast_shapes,
        compiler_params=_cparams("parallel", "arbitrary"),
        name="proj",
    )(xn, w_t, *to_cast)


def _head_rms(x, g):
    ms = jnp.mean(x * x, axis=-1, keepdims=True)
    return x * lax.rsqrt(ms + EPS) * g


def _moba_stages(q_ref, k_ref, v_ref, qg_ref, kg_ref, bias_ref, qx_ref, kx_ref, o_ref,
                 qn_ref, qa_ref, ka_ref, va_ref, kmean_ref, s_ref, p_ref):
    seq, hd = q_ref.shape
    nb = seq // A_BLOCK
    rc = BF16_ROWS

    def off(i):
        return A_BLOCK * (i * (i + 1) // 2)

    def prep_qk(qs_ref, ks_ref, j):
        lane = lax.broadcasted_iota(jnp.int32, (A_BLOCK, hd), 1)
        rows = slice(j * A_BLOCK, (j + 1) * A_BLOCK)
        qn = _head_rms(qs_ref[rows, :].astype(F32), qg_ref[...])
        qn_ref[rows, :] = qn
        qa_ref[rows, :hd] = (qn * (hd ** -0.5 * LOG2E)).astype(BF16)
        qa_ref[rows, hd:] = jnp.zeros((A_BLOCK, hd), BF16)
        kn = _head_rms(ks_ref[rows, :].astype(F32), kg_ref[...])
        ka_ref[rows, :hd] = kn.astype(BF16)
        ka_ref[rows, hd:] = jnp.where(lane == j, 1.0, 0.0).astype(BF16)
        kmean_ref[j:j + 1, :] = jnp.mean(kn, axis=0, keepdims=True)

    def prep_v():
        va_ref[:, :hd] = v_ref[...]
        va_ref[:, hd:] = jnp.ones((seq, hd), BF16)

    def scores(i):
        rows = slice(i * A_BLOCK, (i + 1) * A_BLOCK)
        nk = (i + 1) * A_BLOCK
        if i > A_TOPK:
            blk = lax.broadcasted_iota(jnp.int32, (nb, A_BLOCK), 0)
            g = lax.dot_general(kmean_ref[...], qn_ref[rows, :], (((1,), (1,)), ((), ())),
                                precision=lax.Precision.HIGHEST, preferred_element_type=F32)
            rank = jnp.zeros((nb, A_BLOCK), jnp.int32)
            for jp in range(i):
                gj = g[jp:jp + 1, :]
                beats = (gj > g) | ((gj == g) & (jp < blk))
                rank = rank + beats.astype(jnp.int32)
            neg = jnp.where((blk < i) & (rank >= A_TOPK), NEG_INF, 0.0)
            neg_pad = jnp.concatenate([neg, jnp.zeros((hd - nb, A_BLOCK), F32)], axis=0).astype(BF16)
            eye = jnp.where(lax.broadcasted_iota(jnp.int32, (A_BLOCK, A_BLOCK), 0)
                            == lax.broadcasted_iota(jnp.int32, (A_BLOCK, A_BLOCK), 1), 1.0, 0.0).astype(BF16)
            qa_ref[rows, hd:] = lax.dot_general(eye, neg_pad, (((1,), (1,)), ((), ())),
                                                preferred_element_type=F32).astype(BF16)
        s_ref[:, off(i):off(i) + nk] = lax.dot_general(qa_ref[rows, :], ka_ref[:nk, :], (((1,), (1,)), ((), ())),
                                               preferred_element_type=F32)

    def softmax(i):
        for c in range(A_BLOCK // rc):
            r = slice(c * rc, (c + 1) * rc)
            tiles = []
            for j in range(i + 1):
                sj = s_ref[r, off(i) + j * A_BLOCK:off(i) + (j + 1) * A_BLOCK]
                if j == i:
                    sj = sj + bias_ref[0, r, A_BLOCK:]
                elif j == i - 1:
                    sj = sj + bias_ref[0, r, :A_BLOCK]
                tiles.append(sj)
            mt = tiles[0]
            for t in tiles[1:]:
                mt = jnp.maximum(mt, t)
            m = jnp.max(mt, axis=-1, keepdims=True)
            for j, t in enumerate(tiles):
                p_ref[r, off(i) + j * A_BLOCK:off(i) + (j + 1) * A_BLOCK] = jnp.exp2(t - m).astype(BF16)

    def weighted_sum(i):
        rows = slice(i * A_BLOCK, (i + 1) * A_BLOCK)
        nk = (i + 1) * A_BLOCK
        o = jnp.dot(p_ref[:, off(i):off(i) + nk], va_ref[:nk, :], preferred_element_type=F32)
        o_ref[rows, :] = (o[:, :hd] / o[:, hd:]).astype(o_ref.dtype)

    own_prep = [functools.partial(prep_qk, q_ref, k_ref, j) for j in range(nb)]
    next_prep = [functools.partial(prep_qk, qx_ref, kx_ref, j) for j in range(nb)]
    stages = ([prep_v] + [functools.partial(scores, i) for i in range(nb)]
              + [functools.partial(softmax, i) for i in range(nb)]
              + [functools.partial(weighted_sum, i) for i in range(nb)])
    return own_prep, stages, next_prep


def _gla_stages(q_ref, k_ref, v_ref, r_ref, lr_ref, lrw_ref, lrb_ref, og_ref, o_ref,
                hml_ref, bc_ref, qd_ref, ki_ref, ke_ref, dec_ref, oi_ref, ss_ref, sb_ref):
    seq = q_ref.shape[0]
    ck = B_CHUNK
    nc = seq // ck
    grp = 4 * ck
    ng = seq // grp
    cpg = grp // ck
    qscale = B_DK ** -0.5
    nt = (((1,), (1,)), ((), ()))
    tn = (((0,), (0,)), ((), ()))

    def same_chunk_causal():
        ti = lax.broadcasted_iota(jnp.int32, (grp, grp), 0)
        si = lax.broadcasted_iota(jnp.int32, (grp, grp), 1)
        return (si <= ti) & (si // ck == ti // ck)

    def log_decay(g):
        rows = slice(g * grp, (g + 1) * grp)
        z = jnp.dot(lr_ref[rows, :], lrw_ref[...].astype(BF16), preferred_element_type=F32) + lrb_ref[...]
        la = (jnp.minimum(z, 0.0) - jnp.log(1.0 + jnp.exp(-jnp.abs(z)))) * (1.0 / B_GATE_NORM)
        hi = la.astype(BF16)
        r1 = la - hi.astype(F32)
        mid = r1.astype(BF16)
        hml_ref[rows, :B_DK] = hi
        hml_ref[rows, B_DK:2 * B_DK] = mid
        hml_ref[rows, 2 * B_DK:] = (r1 - mid.astype(F32)).astype(BF16)

    def cumsum(g):
        rows = slice(g * grp, (g + 1) * grp)
        tril_grp = jnp.where(same_chunk_causal(), 1.0, 0.0).astype(BF16)
        parts = jnp.dot(tril_grp, hml_ref[rows, :], preferred_element_type=F32)
        bc_ref[rows, :] = parts[:, :B_DK] + parts[:, B_DK:2 * B_DK] + parts[:, 2 * B_DK:]

    def decayed_qk(g):
        for c in range(g * cpg, (g + 1) * cpg):
            rows = slice(c * ck, (c + 1) * ck)
            bc = bc_ref[rows, :]
            bl = bc_ref[(c + 1) * ck - 1:(c + 1) * ck, :]
            q = q_ref[rows, :].astype(F32)
            k = k_ref[rows, :].astype(F32)
            qd_ref[rows, :] = (q * jnp.exp(bc) * qscale).astype(BF16)
            ki_ref[rows, :] = (k * jnp.exp(-bc)).astype(BF16)
            ke_ref[rows, :] = (k * jnp.exp(bl - bc)).astype(BF16)
            dec_ref[c:c + 1, :] = jnp.exp(bl)

    def intra(g):
        rows = slice(g * grp, (g + 1) * grp)
        a = lax.dot_general(qd_ref[rows, :], ki_ref[rows, :], nt, preferred_element_type=F32)
        a = jnp.where(same_chunk_causal(), a, 0.0).astype(BF16)
        oi_ref[rows, :] = jnp.dot(a, v_ref[rows, :], preferred_element_type=F32)

    def state_terms(g):
        for c in range(g * cpg, (g + 1) * cpg):
            rows = slice(c * ck, (c + 1) * ck)
            ss_ref[c] = lax.dot_general(v_ref[rows, :], ke_ref[rows, :], tn, preferred_element_type=F32)

    def scan():
        slab = B_DV // 4
        st = [jnp.zeros((slab, B_DK), F32) for _ in range(4)]
        for c in range(nc):
            dec = dec_ref[c:c + 1, :]
            for s in range(4):
                r = slice(s * slab, (s + 1) * slab)
                sb_ref[c, r, :] = st[s].astype(BF16)
                st[s] = st[s] * dec + ss_ref[c, r, :]

    def outputs(g):
        for c in range(g * cpg, (g + 1) * cpg):
            rows = slice(c * ck, (c + 1) * ck)
            o = oi_ref[rows, :] + lax.dot_general(qd_ref[rows, :], sb_ref[c], nt,
                                                  preferred_element_type=F32)
            ms = jnp.mean(o * o, axis=-1, keepdims=True)
            y = o * lax.rsqrt(ms + EPS) * og_ref[...]
            r = r_ref[rows, :]
            half_r = 0.5 * r
            o_ref[rows, :] = y.astype(o_ref.dtype) * (half_r + half_r * jnp.tanh(half_r))

    groups = [[functools.partial(fn, g) for g in range(ng)]
              for fn in (log_decay, cumsum, decayed_qk, intra, state_terms)]
    return groups + [[scan], [functools.partial(outputs, g) for g in range(ng)]]


N_MOBA_IN, N_MOBA_SCRATCH = 8, 7
N_GLA_IN = 8


def _mixer_ab_kernel(*refs, n_cast):
    n_in = N_MOBA_IN + N_GLA_IN
    ins, cast_in = refs[:n_in], refs[n_in:n_in + n_cast]
    oa_ref, ob_ref = refs[n_in + n_cast:n_in + n_cast + 2]
    cast_out = refs[n_in + n_cast + 2:n_in + 2 * n_cast + 2]
    scratch = refs[n_in + 2 * n_cast + 2:]
    own_prep, moba_stages, next_prep = _moba_stages(*ins[:N_MOBA_IN], oa_ref, *scratch[:N_MOBA_SCRATCH])
    gla_steps = [fn for group in _gla_stages(*ins[N_MOBA_IN:], ob_ref, *scratch[N_MOBA_SCRATCH:])
                 for fn in group]

    @pl.when((pl.program_id(0) == 0) & (pl.program_id(1) == 0))
    def _():
        for fn in own_prep:
            fn()

    first = 1 + ins[0].shape[0] // A_BLOCK
    per_slot = -(-len(gla_steps) // (len(moba_stages) - first))
    n_slots = len(moba_stages) - first
    for n, stage in enumerate(moba_stages):
        stage()
        if n >= first:
            for fn in gla_steps[(n - first) * per_slot:(n - first + 1) * per_slot]:
                fn()
            if n - first < len(next_prep):
                next_prep[n - first]()
            for src, dst in zip(cast_in, cast_out):
                piece = src.shape[0] // n_slots
                rows = pl.ds((n - first) * piece, piece)
                dst[rows, :] = src[rows, :].astype(dst.dtype)


def mixer_ab(proj_out, lr_blk, vec, bias, lr_w, layer, to_cast, *, bsz, seq):
    hd = HEAD_DIM
    assert A_HEADS == B_HEADS
    qa0 = AB0 // hd
    qb0 = (AB0 + 3 * A_WIDTH) // B_DK
    kb0 = qb0 + B_HEADS
    vb0 = (AB0 + 3 * A_WIDTH + 2 * B_KWIDTH) // B_DV
    rb0 = vb0 + B_HEADS
    nc = seq // B_CHUNK
    nb = seq // A_BLOCK
    tri = nb * (nb + 1) // 2
    cast_in_specs, cast_out_specs, cast_shapes = _cast_specs(
        to_cast, layer, bsz * A_HEADS, lambda b, h: b * A_HEADS + h)

    def nxt(b, h):
        return jnp.minimum(b * A_HEADS + h + 1, bsz * A_HEADS - 1)

    return pl.pallas_call(
        functools.partial(_mixer_ab_kernel, n_cast=len(to_cast)),
        grid=(bsz, A_HEADS),
        in_specs=[pl.BlockSpec((seq, hd), lambda b, h: (b, qa0 + h)),
                  pl.BlockSpec((seq, hd), lambda b, h: (b, qa0 + A_HEADS + h)),
                  pl.BlockSpec((seq, hd), lambda b, h: (b, qa0 + 2 * A_HEADS + h)),
                  _vec_spec("q_norm_g", layer),
                  _vec_spec("k_norm_g", layer),
                  pl.BlockSpec((1, A_BLOCK, 2 * A_BLOCK), lambda b, h: (h, 0, 0)),
                  pl.BlockSpec((seq, hd), lambda b, h: (nxt(b, h) // A_HEADS, qa0 + nxt(b, h) % A_HEADS)),
                  pl.BlockSpec((seq, hd),
                               lambda b, h: (nxt(b, h) // A_HEADS, qa0 + A_HEADS + nxt(b, h) % A_HEADS)),
                  pl.BlockSpec((seq, B_DK), lambda b, h: (b, qb0 + h)),
                  pl.BlockSpec((seq, B_DK), lambda b, h: (b, kb0 + h)),
                  pl.BlockSpec((seq, B_DV), lambda b, h: (b, vb0 + h)),
                  pl.BlockSpec((seq, B_DV), lambda b, h: (b, rb0 + h)),
                  pl.BlockSpec((seq, LANES), lambda b, h: (b, 0)),
                  pl.BlockSpec((None, LANES, B_DK), lambda b, h: (layer, 0, h)),
                  _vec_spec("gla_lr_b", layer, B_DK, lambda b, h: h),
                  _vec_spec("gla_out_g", layer)] + cast_in_specs,
        out_specs=[pl.BlockSpec((seq, hd), lambda b, h: (b, h)),
                   pl.BlockSpec((seq, B_DV), lambda b, h: (b, h))] + cast_out_specs,
        out_shape=[jax.ShapeDtypeStruct((bsz * seq, A_WIDTH), BF16),
                   jax.ShapeDtypeStruct((bsz * seq, B_VWIDTH), BF16)] + cast_shapes,
        scratch_shapes=[pltpu.VMEM((seq, hd), F32),
                        pltpu.VMEM((seq, 2 * hd), BF16),
                        pltpu.VMEM((seq, 2 * hd), BF16),
                        pltpu.VMEM((seq, 2 * hd), BF16),
                        pltpu.VMEM((seq // A_BLOCK, hd), F32),
                        pltpu.VMEM((A_BLOCK, tri * A_BLOCK), F32),
                        pltpu.VMEM((A_BLOCK, tri * A_BLOCK), BF16),
                        pltpu.VMEM((seq, 3 * B_DK), BF16),
                        pltpu.VMEM((seq, B_DK), F32),
                        pltpu.VMEM((seq, B_DK), BF16),
                        pltpu.VMEM((seq, B_DK), BF16),
                        pltpu.VMEM((seq, B_DK), BF16),
                        pltpu.VMEM((nc, B_DK), F32),
                        pltpu.VMEM((seq, B_DV), F32),
                        pltpu.VMEM((nc, B_DV, B_DK), F32),
                        pltpu.VMEM((nc, B_DV, B_DK), BF16)],
        compiler_params=_cparams("arbitrary", "arbitrary"),
        name="mixer_ab",
    )(proj_out, proj_out, proj_out, vec, vec, bias, proj_out, proj_out,
      proj_out, proj_out, proj_out, proj_out, lr_blk, lr_w, vec, vec, *to_cast)


def _gelu(x):
    c = math.sqrt(2.0 / math.pi)
    half_x = 0.5 * x
    return half_x + half_x * jnp.tanh(x * (c + (c * 0.044715) * (x * x)))


SGU_SUB = 2 * C_CHUNK


def _sgu_kernel(*refs, n_cast):
    x_ref, g1_ref, wsrc_ref, lg_ref, lb_ref, w_ref, b_ref = refs[:7]
    o_ref, lr_ref, xn_ref = refs[7 + n_cast:10 + n_cast]
    wc_ref, luv_ref = refs[-2:]

    @pl.when(pl.program_id(0) == 0)
    def _():
        wc_ref[...] = wsrc_ref[...].astype(BF16)

    _cast_all(refs[7:7 + n_cast], refs[10 + n_cast:10 + 2 * n_cast])
    ti = lax.broadcasted_iota(jnp.int32, (C_CHUNK, C_CHUNK), 0)
    si = lax.broadcasted_iota(jnp.int32, (C_CHUNK, C_CHUNK), 1)
    causal = si <= ti
    ws = [jnp.where(causal, w_ref[g], 0.0).astype(BF16) for g in range(C_GROUPS)]
    n_sub = x_ref.shape[0] // SGU_SUB

    def project(s):
        rs = slice(s * SGU_SUB, (s + 1) * SGU_SUB)
        x = x_ref[rs, :]
        ms = jnp.mean(x * x, axis=-1, keepdims=True)
        xn = (x * lax.rsqrt(ms + EPS) * g1_ref[...]).astype(BF16)
        xn_ref[rs, :] = xn
        luv = lax.dot_general(xn, wc_ref[...], (((1,), (1,)), ((), ())), preferred_element_type=F32)
        luv_ref[s % 2] = luv
        lr_ref[rs, :] = luv[:, :LANES].astype(lr_ref.dtype)

    def gate(s):
        for n in range(SGU_SUB // C_CHUNK):
            rl = slice(n * C_CHUNK, (n + 1) * C_CHUNK)
            ro = slice(s * SGU_SUB + n * C_CHUNK, s * SGU_SUB + (n + 1) * C_CHUNK)
            v = _gelu(luv_ref[s % 2, rl, LANES + C_WIDTH:])
            mu = jnp.mean(v, axis=-1, keepdims=True)
            vc = v - mu
            vn = vc * lax.rsqrt(jnp.mean(vc * vc, axis=-1, keepdims=True) + EPS)
            vn = (vn * lg_ref[...] + lb_ref[...]).astype(BF16)
            u = _gelu(luv_ref[s % 2, rl, LANES:LANES + C_WIDTH])
            for g in range(C_GROUPS):
                cols = slice(g * C_GROUP_DIM, (g + 1) * C_GROUP_DIM)
                mixed = jnp.dot(ws[g], vn[:, cols], preferred_element_type=F32) + b_ref[g]
                o_ref[ro, cols] = (u[:, cols] * mixed).astype(o_ref.dtype)

    project(0)
    for s in range(n_sub):
        if s + 1 < n_sub:
            project(s + 1)
        gate(s)


def sgu(x, vec, w_t, w_s, b_s, layer, to_cast, *, tm):
    m, d = x.shape
    cast_in_specs, cast_out_specs, cast_shapes = _cast_specs(to_cast, layer, m // tm, lambda i: i)
    return pl.pallas_call(
        functools.partial(_sgu_kernel, n_cast=len(to_cast)),
        grid=(m // tm,),
        in_specs=[pl.BlockSpec((tm, d), lambda i: (i, 0)),
                  _vec_spec("norm1_g", layer),
                  pl.BlockSpec((None, pl.Element(C_WIN), pl.Element(d)), lambda i: (layer, C_SRC, 0),
                               pipeline_mode=pl.Buffered(1)),
                  _vec_spec("sg_ln_g", layer),
                  _vec_spec("sg_ln_b", layer),
                  pl.BlockSpec((None, C_GROUPS, C_CHUNK, C_CHUNK), lambda i: (layer, 0, 0, 0)),
                  pl.BlockSpec((None, C_GROUPS, C_CHUNK, 1), lambda i: (layer, 0, 0, 0))] + cast_in_specs,
        out_specs=[pl.BlockSpec((tm, C_WIDTH), lambda i: (i, 0)),
                   pl.BlockSpec((tm, LANES), lambda i: (i, 0)),
                   pl.BlockSpec((tm, d), lambda i: (i, 0))] + cast_out_specs,
        out_shape=[jax.ShapeDtypeStruct((m, C_WIDTH), BF16),
                   jax.ShapeDtypeStruct((m, LANES), BF16),
                   jax.ShapeDtypeStruct((m, d), BF16)] + cast_shapes,
        scratch_shapes=[pltpu.VMEM((C_WIN, d), BF16),
                        pltpu.VMEM((2, SGU_SUB, C_WIN), F32)],
        compiler_params=_cparams("arbitrary"),
        name="sgu",
    )(x, vec, w_t, vec, vec, w_s, b_s, *to_cast)


def _merge_kernel(ya_ref, yb_ref, yc_ref, ga_ref, gb_ref, gc_ref, x_ref,
                  wa_ref, wb_ref, wc_ref, wo_ref, g2_ref, o_ref, on_ref):
    sub = 256
    for s in range(x_ref.shape[0] // sub):
        r = slice(s * sub, (s + 1) * sub)

        def gated(g_ref, y_ref, w_ref):
            gate = 0.5 * jnp.tanh(0.5 * g_ref[r, :].astype(F32)) + 0.5
            return gate * jnp.dot(y_ref[r, :], w_ref[...], preferred_element_type=F32)

        merged = gated(ga_ref, ya_ref, wa_ref) + gated(gb_ref, yb_ref, wb_ref) + gated(gc_ref, yc_ref, wc_ref)
        h = x_ref[r, :] + jnp.dot(merged.astype(BF16), wo_ref[...], preferred_element_type=F32)
        o_ref[r, :] = h
        ms = jnp.mean(h * h, axis=-1, keepdims=True)
        on_ref[r, :] = (h * lax.rsqrt(ms + EPS) * g2_ref[...]).astype(on_ref.dtype)


def merge(ya, yb, yc, proj_out, x, wa, wb, wc, wo, vec, layer, *, tm):
    m, d = x.shape
    resident = functools.partial(pl.BlockSpec, pipeline_mode=pl.Buffered(1))
    return pl.pallas_call(
        _merge_kernel,
        grid=(m // tm,),
        in_specs=[pl.BlockSpec((tm, A_WIDTH), lambda i: (i, 0)),
                  pl.BlockSpec((tm, B_VWIDTH), lambda i: (i, 0)),
                  pl.BlockSpec((tm, C_WIDTH), lambda i: (i, 0)),
                  pl.BlockSpec((tm, d), lambda i: (i, 0)),
                  pl.BlockSpec((tm, d), lambda i: (i, 1)),
                  pl.BlockSpec((tm, d), lambda i: (i, 2)),
                  pl.BlockSpec((tm, d), lambda i: (i, 0)),
                  resident((A_WIDTH, d), lambda i: (0, 0)),
                  resident((B_VWIDTH, d), lambda i: (0, 0)),
                  resident((C_WIDTH, d), lambda i: (0, 0)),
                  resident((d, d), lambda i: (0, 0)),
                  _vec_spec("norm2_g", layer)],
        out_specs=[pl.BlockSpec((tm, d), lambda i: (i, 0)),
                   pl.BlockSpec((tm, d), lambda i: (i, 0))],
        out_shape=[jax.ShapeDtypeStruct((m, d), F32),
                   jax.ShapeDtypeStruct((m, d), BF16)],
        compiler_params=_cparams("parallel"),
        name="merge",
    )(ya, yb, yc, proj_out, proj_out, proj_out, x, wa, wb, wc, wo, vec)


def _mlp_kernel(h_hbm, hn_ref, w1_ref, w2_ref, o_ref, h_ref, h_sem):
    i, j = pl.program_id(0), pl.program_id(1)
    last = pl.num_programs(1) - 1
    tm = h_ref.shape[0]
    h_copy = pltpu.make_async_copy(h_hbm.at[pl.ds(i * tm, tm), :], h_ref, h_sem)

    def update():
        a = jnp.dot(hn_ref[...], w1_ref[...], preferred_element_type=F32)
        a = jnp.square(jnp.maximum(a, 0.0)).astype(BF16)
        return jnp.dot(a, w2_ref[...], preferred_element_type=F32)

    @pl.when(j == 0)
    def _():
        h_copy.start()
        o_ref[...] = update()

    @pl.when((j > 0) & (j < last))
    def _():
        o_ref[...] += update()

    @pl.when(j == last)
    def _():
        h_copy.wait()
        o_ref[...] = (o_ref[...] + h_ref[...]) + update()


def mlp(h, hn, w1, w2, *, tm, tf):
    m, d = h.shape
    f = w1.shape[1]
    return pl.pallas_call(
        _mlp_kernel,
        grid=(m // tm, f // tf),
        in_specs=[pl.BlockSpec(memory_space=pl.ANY),
                  pl.BlockSpec((tm, d), lambda i, j: (i, 0)),
                  pl.BlockSpec((d, tf), lambda i, j: (0, j)),
                  pl.BlockSpec((tf, d), lambda i, j: (j, 0))],
        out_specs=pl.BlockSpec((tm, d), lambda i, j: (i, 0)),
        out_shape=jax.ShapeDtypeStruct((m, d), F32),
        scratch_shapes=[pltpu.VMEM((tm, d), F32),
                        pltpu.SemaphoreType.DMA(())],
        compiler_params=_cparams("parallel", "arbitrary"),
        name="mlp",
    )(h, hn, w1, w2)


def kernel(x, rpe_table, norm1_g, w_in, q_norm_g, k_norm_g, gla_lr_w, gla_lr_b, gla_out_g,
           sg_ln_g, sg_ln_b, sg_w, sg_b, w_br_a, w_br_b, w_br_c, w_o, norm2_g, w_ff1, w_ff2):
    bsz, seq, d = x.shape
    assert seq % A_BLOCK == 0 and seq % B_CHUNK == 0 and seq % C_CHUNK == 0
    assert w_in.shape[2] == IN_WIDTH
    h = x.reshape(bsz * seq, d)
    fields = dict(norm1_g=norm1_g, norm2_g=norm2_g, sg_ln_g=sg_ln_g, sg_ln_b=sg_ln_b, gla_lr_b=gla_lr_b,
                  gla_out_g=gla_out_g, q_norm_g=q_norm_g, k_norm_g=k_norm_g)
    bias, vec, lr_w = rpe_bias(rpe_table, gla_lr_w, [fields[name] for name, _ in VEC_FIELDS])
    w_t = jnp.swapaxes(w_in, 1, 2)
    sg_b_col = sg_b[..., None]
    for l in range(DEPTH):
        y_c, lr_blk, xn, wa, wb, wc, wo = sgu(h, vec, w_t, sg_w, sg_b_col, l,
                                              (w_br_a, w_br_b, w_br_c, w_o), tm=SGU_TM)
        p, w1 = proj(xn, w_t, l, (w_ff1,), tm=PROJ_TM)
        y_a, y_b, w2 = mixer_ab(p, lr_blk, vec, bias, lr_w, l, (w_ff2,), bsz=bsz, seq=seq)
        h, hn = merge(y_a, y_b, y_c, p, h, wa, wb, wc, wo, vec, l, tm=MERGE_TM)
        h = mlp(h, hn, w1, w2, tm=MLP_TM, tf=MLP_TF)
    return h.reshape(bsz, seq, d)
```
